```python
import jax, jax.numpy as jnp
from jax import lax
import numpy as np

D_MODEL = 2048
BATCH = 4
SEQ = 2048
DEPTH = 2
DEC_BATCH = 128
DEC_SEQ = 8
PAST_LEN = 16384
PAGE_SIZE = 128

N_AB_LAYERS = (DEPTH + 1) // 2
N_C_LAYERS = DEPTH // 2
CONV_W = 4
HEAD_A = 64
D_INNER_A = D_MODEL
H_A = D_INNER_A // HEAD_A
G_A = 4
N_A = 128
CHUNK_A = 128
CONV_DIM_A = D_INNER_A + 2 * G_A * N_A
IN_A = D_INNER_A + CONV_DIM_A + H_A
HEAD_B = 64
D_B = D_MODEL
H_B = D_B // HEAD_B
R_W = 64
R_A = 64
R_G = 128
SHIFT_DIM_B = 3 * D_B + R_W + R_A + R_G
IN_B = SHIFT_DIM_B
IN_AB = IN_A + IN_B
MIX_AB = D_INNER_A + D_B
LRU_WIDTH = D_MODEL
H_C = 8
BLK_C = LRU_WIDTH // H_C
LRU_C = 8.0
D_FF = 5504
EPS = 1e-6
GN_EPS = 64e-5

kernel_name = 'hybrid_ssd_rwkv7_rglru_macaron_step'


def rmsnorm(x, g):
    xf = x.astype(jnp.float32)
    y = xf * lax.rsqrt(jnp.mean(xf * xf, axis=-1, keepdims=True) + EPS)
    return (y * g.astype(jnp.float32)).astype(x.dtype)


def swiglu(x, w_in, w_out):
    gate, up = jnp.split(x @ w_in, 2, axis=-1)
    return (jax.nn.silu(gate) * up) @ w_out


def causal_conv(u, buf, w, b):
    L = u.shape[1]
    full = jnp.concatenate([buf.astype(u.dtype), u], axis=1)
    y = b
    for k in range(CONV_W):
        y = y + full[:, k:k + L] * w[k]
    return y, full[:, L:]


def ssd_scan(xs, dt, a_head, bm, cm, h0):
    b, l = xs.shape[:2]
    q = min(CHUNK_A, l)
    nc = l // q
    e = H_A // G_A
    dtype = xs.dtype
    cum = jnp.cumsum((dt * a_head).reshape(b, nc, q, G_A, e), axis=2)
    xdt = (xs * dt[..., None].astype(dtype)).reshape(b, nc, q, G_A, e, HEAD_A)
    bm = bm.reshape(b, nc, q, G_A, N_A)
    cm = cm.reshape(b, nc, q, G_A, N_A)
    seg = cum[:, :, :, None] - cum[:, :, None, :]
    causal = jnp.tril(jnp.ones((q, q), bool))[None, None, :, :, None, None]
    decay_in = jnp.exp(jnp.where(causal, seg, -jnp.inf)).astype(dtype)
    cb = jnp.einsum('bcign,bcjgn->bcijg', cm, bm)
    y_diag = jnp.einsum('bcijge,bcjgep->bcigep', cb[..., None] * decay_in, xdt)
    tail = jnp.exp(cum[:, :, -1:] - cum).astype(dtype)
    st = jnp.einsum('bcjgn,bcjgep->bcgepn', bm, xdt * tail[..., None])
    chunk_decay = jnp.exp(cum[:, :, -1]).astype(dtype)

    def step(h, inp):
        dcy, s = inp
        return h * dcy[..., None, None] + s, h

    h_init = h0.astype(dtype).reshape(b, G_A, e, HEAD_A, N_A)
    h_last, h_starts = lax.scan(step, h_init, (jnp.moveaxis(chunk_decay, 1, 0), jnp.moveaxis(st, 1, 0)))
    h_starts = jnp.moveaxis(h_starts, 0, 1)
    y_off = jnp.einsum('bcign,bcgepn->bcigep', cm, h_starts) * jnp.exp(cum).astype(dtype)[..., None]
    y = (y_diag + y_off).reshape(b, l, H_A, HEAD_A)
    return y, h_last.reshape(b, H_A, HEAD_A, N_A)


def mamba2_mixer(p_a, conv_buf, h0, conv_w, conv_b, dt_bias, a_log, d_skip, gnorm):
    b, l, _ = p_a.shape
    z, xbc, dt_raw = jnp.split(p_a, [D_INNER_A, D_INNER_A + CONV_DIM_A], axis=-1)
    xbc, new_buf = causal_conv(xbc, conv_buf, conv_w, conv_b)
    xbc = jax.nn.silu(xbc)
    xs, bm, cm = jnp.split(xbc, [D_INNER_A, D_INNER_A + G_A * N_A], axis=-1)
    dt = jax.nn.softplus((dt_raw + dt_bias).astype(jnp.float32))
    a_head = -jnp.exp(a_log.astype(jnp.float32))
    xs = xs.reshape(b, l, H_A, HEAD_A)
    y, h_last = ssd_scan(xs, dt, a_head, bm.reshape(b, l, G_A, N_A), cm.reshape(b, l, G_A, N_A), h0)
    y = y + xs * d_skip[:, None]
    y = y.reshape(b, l, D_INNER_A) * jax.nn.silu(z)
    yg = y.reshape(b, l, G_A, D_INNER_A // G_A).astype(jnp.float32)
    yg = yg * lax.rsqrt(jnp.mean(yg * yg, axis=-1, keepdims=True) + EPS)
    y = (yg.reshape(b, l, D_INNER_A) * gnorm.astype(jnp.float32)).astype(p_a.dtype)
    return y, new_buf, h_last


def wkv7_scan(r, decay, k, v, kk, a, s0):
    def step(S, inp):
        r_t, w_t, k_t, v_t, kk_t, a_t = inp
        sa = jnp.einsum('bhvk,bhk->bhv', S, kk_t)
        S = S * w_t[:, :, None, :] - sa[..., None] * (kk_t * a_t)[:, :, None, :] + v_t[..., None] * k_t[:, :, None, :]
        return S, jnp.einsum('bhvk,bhk->bhv', S, r_t)

    xs = (jnp.moveaxis(r, 1, 0), jnp.moveaxis(decay, 1, 0), jnp.moveaxis(k, 1, 0),
          jnp.moveaxis(v, 1, 0), jnp.moveaxis(kk, 1, 0), jnp.moveaxis(a, 1, 0))
    s_last, o = lax.scan(step, s0.astype(r.dtype), xs)
    return jnp.moveaxis(o, 0, 1), s_last


def rwkv7_mixer(p_b, shift_buf, s0, mu, w0, w2, a0, a2, g2, k_k, k_a, r_k, ln_w, ln_b):
    b, l, _ = p_b.shape
    dtype = p_b.dtype
    prev = jnp.concatenate([shift_buf.astype(dtype), p_b[:, :-1]], axis=1)
    ps = p_b + mu * (prev - p_b)
    new_buf = p_b[:, -1:]
    r, k, v, xw, xa, xg = jnp.split(ps, [D_B, 2 * D_B, 3 * D_B, 3 * D_B + R_W, 3 * D_B + R_W + R_A], axis=-1)
    wlog = -jax.nn.softplus(-(w0 + jnp.tanh(xw) @ w2).astype(jnp.float32)) - 0.5
    decay = jnp.exp(-jnp.exp(wlog)).astype(dtype)
    a = jax.nn.sigmoid(a0 + xa @ a2)
    g = jax.nn.sigmoid(xg) @ g2
    kkf = (k * k_k).reshape(b, l, H_B, HEAD_B).astype(jnp.float32)
    kk = (kkf / jnp.maximum(jnp.sqrt(jnp.sum(kkf * kkf, axis=-1, keepdims=True)), 1e-12)).astype(dtype)
    k = k * (1 + (a - 1) * k_a)
    sh = (b, l, H_B, HEAD_B)
    r, k, v, a, decay = r.reshape(sh), k.reshape(sh), v.reshape(sh), a.reshape(sh), decay.reshape(sh)
    o, s_last = wkv7_scan(r, decay, k, v, kk, a, s0)
    of = o.astype(jnp.float32)
    mean = jnp.mean(of, axis=-1, keepdims=True)
    var = jnp.mean(jnp.square(of - mean), axis=-1, keepdims=True)
    on = ((of - mean) * lax.rsqrt(var + GN_EPS)).reshape(b, l, D_B)
    on = (on * ln_w.astype(jnp.float32) + ln_b.astype(jnp.float32)).astype(dtype)
    bonus = (jnp.sum(r * k * r_k, axis=-1, keepdims=True) * v).reshape(b, l, D_B)
    return (on + bonus) * g, new_buf, s_last


def linear_scan(a, bterm, h0):
    bterm = bterm.at[:, 0].add(a[:, 0] * h0)

    def comb(lhs, rhs):
        a1, b1 = lhs
        a2, b2 = rhs
        return a1 * a2, a2 * b1 + b2

    _, h = lax.associative_scan(comb, (a, bterm), axis=1)
    return h


def rglru_mixer(p_c, conv_buf, h0, conv_w, conv_b, w_ga, b_ga, w_gx, b_gx, lam):
    b, l, _ = p_c.shape
    gate_branch, xr = jnp.split(p_c, 2, axis=-1)
    xc, new_buf = causal_conv(xr, conv_buf, conv_w, conv_b)
    xh = xc.reshape(b, l, H_C, BLK_C)
    rg = jax.nn.sigmoid(jnp.einsum('blhi,hij->blhj', xh, w_ga).reshape(b, l, LRU_WIDTH) + b_ga)
    ig = jax.nn.sigmoid(jnp.einsum('blhi,hij->blhj', xh, w_gx).reshape(b, l, LRU_WIDTH) + b_gx)
    log_a = -LRU_C * rg.astype(jnp.float32) * jax.nn.softplus(-lam.astype(jnp.float32))
    a = jnp.exp(log_a)
    mult = jnp.sqrt(-jnp.expm1(2.0 * log_a))
    bterm = mult * (ig * xc).astype(jnp.float32)
    h = linear_scan(a, bterm, h0.astype(jnp.float32))
    y = h.astype(p_c.dtype) * jax.nn.gelu(gate_branch)
    return y, new_buf, h[:, -1].astype(p_c.dtype)


def setup_inputs(seed: int = 0) -> dict:
    key = jax.random.key(seed)
    ks = iter(jax.random.split(key, 64))
    f32 = jnp.float32

    def nrm(shape, scale):
        return jax.random.normal(next(ks), shape, f32) * scale

    def uni(shape, lo, hi):
        return jax.random.uniform(next(ks), shape, f32, lo, hi)

    dt0 = jnp.exp(uni((N_AB_LAYERS, H_A), float(np.log(1e-3)), float(np.log(1e-1))))
    a_tgt = uni((N_C_LAYERS, LRU_WIDTH), 0.9, 0.999)
    s_tgt = a_tgt ** (1.0 / LRU_C)
    return {
        'x_prompt': nrm((BATCH, SEQ, D_MODEL), 1.0),
        'x_sample': nrm((DEC_BATCH, DEC_SEQ, D_MODEL), 1.0),
        'state_ssm_a': nrm((N_AB_LAYERS, DEC_BATCH, H_A, HEAD_A, N_A), 0.1),
        'state_conv_a': nrm((N_AB_LAYERS, DEC_BATCH, CONV_W - 1, CONV_DIM_A), 1.0),
        'state_wkv_b': nrm((N_AB_LAYERS, DEC_BATCH, H_B, HEAD_B, HEAD_B), 1.0),
        'state_shift_b': nrm((N_AB_LAYERS, DEC_BATCH, 1, SHIFT_DIM_B), 1.0),
        'state_lru_c': nrm((N_C_LAYERS, DEC_BATCH, LRU_WIDTH), 0.5),
        'state_conv_c': nrm((N_C_LAYERS, DEC_BATCH, CONV_W - 1, LRU_WIDTH), 1.0),
        'norm_gain': 1.0 + nrm((DEPTH, 3, D_MODEL), 0.02),
        'w_ffn_in': nrm((DEPTH, 2, D_MODEL, 2 * D_FF), D_MODEL ** -0.5),
        'w_ffn_out': nrm((DEPTH, 2, D_FF, D_MODEL), D_FF ** -0.5),
        'w_in_ab': nrm((N_AB_LAYERS, D_MODEL, IN_AB), D_MODEL ** -0.5),
        'conv_w_a': nrm((N_AB_LAYERS, CONV_W, CONV_DIM_A), CONV_W ** -0.5),
        'conv_b_a': nrm((N_AB_LAYERS, CONV_DIM_A), 0.02),
        'dt_bias_a': dt0 + jnp.log(-jnp.expm1(-dt0)),
        'a_log_a': jnp.log(uni((N_AB_LAYERS, H_A), 1.0, 16.0)),
        'd_skip_a': 1.0 + nrm((N_AB_LAYERS, H_A), 0.1),
        'gnorm_a': 1.0 + nrm((N_AB_LAYERS, D_INNER_A), 0.02),
        'mu_b': uni((N_AB_LAYERS, SHIFT_DIM_B), 0.0, 1.0),
        'w0_b': uni((N_AB_LAYERS, D_B), -6.0, -1.0),
        'w2_b': nrm((N_AB_LAYERS, R_W, D_B), 0.1 * R_W ** -0.5),
        'a0_b': nrm((N_AB_LAYERS, D_B), 0.1),
        'a2_b': nrm((N_AB_LAYERS, R_A, D_B), 0.1 * R_A ** -0.5),
        'g2_b': nrm((N_AB_LAYERS, R_G, D_B), R_G ** -0.5),
        'k_k_b': 0.85 + nrm((N_AB_LAYERS, D_B), 0.02),
        'k_a_b': 1.0 + nrm((N_AB_LAYERS, D_B), 0.02),
        'r_k_b': nrm((N_AB_LAYERS, H_B, HEAD_B), 0.1),
        'ln_w_b': 1.0 + nrm((N_AB_LAYERS, D_B), 0.02),
        'ln_b_b': nrm((N_AB_LAYERS, D_B), 0.02),
        'w_out_ab': nrm((N_AB_LAYERS, MIX_AB, D_MODEL), MIX_AB ** -0.5),
        'w_in_c': nrm((N_C_LAYERS, D_MODEL, 2 * LRU_WIDTH), D_MODEL ** -0.5),
        'conv_w_c': nrm((N_C_LAYERS, CONV_W, LRU_WIDTH), CONV_W ** -0.5),
        'conv_b_c': nrm((N_C_LAYERS, LRU_WIDTH), 0.02),
        'w_gate_a_c': nrm((N_C_LAYERS, H_C, BLK_C, BLK_C), BLK_C ** -0.5),
        'b_gate_a_c': nrm((N_C_LAYERS, LRU_WIDTH), 0.02),
        'w_gate_x_c': nrm((N_C_LAYERS, H_C, BLK_C, BLK_C), BLK_C ** -0.5),
        'b_gate_x_c': nrm((N_C_LAYERS, LRU_WIDTH), 0.02),
        'lambda_c': jnp.log(s_tgt) - jnp.log1p(-s_tgt),
        'w_out_c': nrm((N_C_LAYERS, LRU_WIDTH, D_MODEL), LRU_WIDTH ** -0.5),
        'final_norm_gain': 1.0 + nrm((D_MODEL,), 0.02),
    }


def reference(x_prompt, x_sample, state_ssm_a, state_conv_a, state_wkv_b, state_shift_b, state_lru_c, state_conv_c,
              norm_gain, w_ffn_in, w_ffn_out, w_in_ab, conv_w_a, conv_b_a, dt_bias_a, a_log_a, d_skip_a, gnorm_a,
              mu_b, w0_b, w2_b, a0_b, a2_b, g2_b, k_k_b, k_a_b, r_k_b, ln_w_b, ln_b_b, w_out_ab,
              w_in_c, conv_w_c, conv_b_c, w_gate_a_c, b_gate_a_c, w_gate_x_c, b_gate_x_c, lambda_c, w_out_c,
              final_norm_gain):
    def trunk(x, ssm0, conva0, wkv0, shift0, lru0, convc0):
        ssm_n, conva_n, wkv_n, shift_n, lru_n, convc_n = [], [], [], [], [], []
        for i in range(DEPTH):
            j = i // 2
            x = x + 0.5 * swiglu(rmsnorm(x, norm_gain[i, 0]), w_ffn_in[i, 0], w_ffn_out[i, 0])
            h = rmsnorm(x, norm_gain[i, 1])
            if i % 2 == 0:
                p = h @ w_in_ab[j]
                ya, buf_a, hs = mamba2_mixer(p[..., :IN_A], conva0[j], ssm0[j], conv_w_a[j], conv_b_a[j],
                                             dt_bias_a[j], a_log_a[j], d_skip_a[j], gnorm_a[j])
                yb, buf_b, sb = rwkv7_mixer(p[..., IN_A:], shift0[j], wkv0[j], mu_b[j], w0_b[j], w2_b[j],
                                            a0_b[j], a2_b[j], g2_b[j], k_k_b[j], k_a_b[j], r_k_b[j],
                                            ln_w_b[j], ln_b_b[j])
                x = x + jnp.concatenate([ya, yb], axis=-1) @ w_out_ab[j]
                ssm_n.append(hs)
                conva_n.append(buf_a)
                wkv_n.append(sb)
                shift_n.append(buf_b)
            else:
                p = h @ w_in_c[j]
                yc, buf_c, hc = rglru_mixer(p, convc0[j], lru0[j], conv_w_c[j], conv_b_c[j], w_gate_a_c[j],
                                            b_gate_a_c[j], w_gate_x_c[j], b_gate_x_c[j], lambda_c[j])
                x = x + yc @ w_out_c[j]
                lru_n.append(hc)
                convc_n.append(buf_c)
            x = x + 0.5 * swiglu(rmsnorm(x, norm_gain[i, 2]), w_ffn_in[i, 1], w_ffn_out[i, 1])
        y = rmsnorm(x, final_norm_gain)
        return (y, jnp.stack(ssm_n), jnp.stack(conva_n), jnp.stack(wkv_n), jnp.stack(shift_n),
                jnp.stack(lru_n), jnp.stack(convc_n))

    bp = x_prompt.shape[0]
    dtp = x_prompt.dtype

    def zeros_like_state(s):
        return jnp.zeros((s.shape[0], bp) + s.shape[2:], dtp)

    y_prompt, ssm_p, conva_p, wkv_p, shift_p, lru_p, convc_p = trunk(
        x_prompt, zeros_like_state(state_ssm_a), zeros_like_state(state_conv_a), zeros_like_state(state_wkv_b),
        zeros_like_state(state_shift_b), zeros_like_state(state_lru_c), zeros_like_state(state_conv_c))
    y_sample, ssm_s, conva_s, wkv_s, shift_s, lru_s, convc_s = trunk(
        x_sample, state_ssm_a, state_conv_a, state_wkv_b, state_shift_b, state_lru_c, state_conv_c)
    return (y_prompt, y_sample, ssm_p, conva_p, wkv_p, shift_p, lru_p, convc_p,
            ssm_s, conva_s, wkv_s, shift_s, lru_s, convc_s)
```

```python
import functools

import jax
import jax.numpy as jnp
from jax import lax
from jax.experimental import pallas as pl
from jax.experimental.pallas import tpu as pltpu

F32 = jnp.float32
BF16 = jnp.bfloat16

D = 2048
D_FF = 5504
D_FF_PAD = 5632
HEAD = 64
N_HEADS = 32
N_GROUPS = 4
GROUP_W = D // N_GROUPS
N_STATE = 128
CONV_W = 4
LORA_W = 256
LRU_BLOCKS = 8
LRU_BLK = D // LRU_BLOCKS
LRU_C = 8.0
EPS = 1e-6
GN_EPS = 64e-5
SUBLANES = 8
LANES = 128
CHUNK = 128

COL_RKV = 0
COL_Z = 3 * D
COL_X = 4 * D
COL_BC = 5 * D
COL_LORA = 5 * D + 2 * N_GROUPS * N_STATE
COL_DT = COL_LORA + LORA_W
PROJ_W = COL_DT + 256

TM_FFN = 512
TF_FFN = 512
TM_PROJ = 1024
TN_PROJ = 512
TM_OUT = 512
TN_OUT = 1024
TM_POST = 256


def _cparams(sem, vmem_mib):
    return pltpu.CompilerParams(dimension_semantics=sem, vmem_limit_bytes=vmem_mib * 1024 * 1024)


def _softplus(x):
    return jnp.maximum(x, 0.0) + jnp.log(1.0 + jnp.exp(-jnp.abs(x)))


def _silu(x):
    return x * jax.nn.sigmoid(x)


def _rms(x, gain):
    ms = jnp.mean(x * x, axis=-1, keepdims=True)
    return x * lax.rsqrt(ms + EPS) * gain


def _ffn_body(x_ref, g_ref, wg_ref, wu_ref, wo_ref, fg_ref, o_ref, xn_ref, acc_ref, *, nf, final):
    f = pl.program_id(1)

    @pl.when(f == 0)
    def _():
        xn_ref[...] = _rms(x_ref[...], g_ref[...]).astype(BF16)
        acc_ref[...] = jnp.zeros_like(acc_ref)

    xn = xn_ref[...]
    gate = jnp.dot(xn, wg_ref[...], preferred_element_type=F32)
    up = jnp.dot(xn, wu_ref[...], preferred_element_type=F32)
    h = (_silu(gate) * up).astype(BF16)
    acc_ref[...] += jnp.dot(h, wo_ref[...], preferred_element_type=F32)

    @pl.when(f == nf - 1)
    def _():
        y = x_ref[...] + 0.5 * acc_ref[...]
        if final:
            y = _rms(y, fg_ref[...])
        o_ref[...] = y


def _ffn(x, gain, wg, wu, wo, final_gain=None):
    m = x.shape[0]
    nf = D_FF_PAD // TF_FFN
    final = final_gain is not None
    fg = final_gain if final else gain
    return pl.pallas_call(
        functools.partial(_ffn_body, nf=nf, final=final),
        out_shape=jax.ShapeDtypeStruct((m, D), F32),
        grid=(m // TM_FFN, nf),
        in_specs=[
            pl.BlockSpec((TM_FFN, D), lambda i, f: (i, 0)),
            pl.BlockSpec((1, D), lambda i, f: (0, 0)),
            pl.BlockSpec((D, TF_FFN), lambda i, f: (0, f)),
            pl.BlockSpec((D, TF_FFN), lambda i, f: (0, f)),
            pl.BlockSpec((TF_FFN, D), lambda i, f: (f, 0)),
            pl.BlockSpec((1, D), lambda i, f: (0, 0)),
        ],
        out_specs=pl.BlockSpec((TM_FFN, D), lambda i, f: (i, 0)),
        scratch_shapes=[pltpu.VMEM((TM_FFN, D), BF16), pltpu.VMEM((TM_FFN, D), F32)],
        compiler_params=_cparams(("parallel", "arbitrary"), 48),
        name="ffn",
    )(x, gain, wg, wu, wo, fg)


def _proj_body(x_ref, g_ref, w_ref, o_ref, xn_ref):
    @pl.when(pl.program_id(1) == 0)
    def _():
        xn_ref[...] = _rms(x_ref[...], g_ref[...]).astype(BF16)

    o_ref[...] = jnp.dot(xn_ref[...], w_ref[...], preferred_element_type=F32)


def _proj(x, gain, w):
    m = x.shape[0]
    n = w.shape[1]
    return pl.pallas_call(
        _proj_body,
        out_shape=jax.ShapeDtypeStruct((m, n), F32),
        grid=(m // TM_PROJ, n // TN_PROJ),
        in_specs=[
            pl.BlockSpec((TM_PROJ, D), lambda i, j: (i, 0)),
            pl.BlockSpec((1, D), lambda i, j: (0, 0)),
            pl.BlockSpec((D, TN_PROJ), lambda i, j: (0, j)),
        ],
        out_specs=pl.BlockSpec((TM_PROJ, TN_PROJ), lambda i, j: (i, j)),
        scratch_shapes=[pltpu.VMEM((TM_PROJ, D), BF16)],
        compiler_params=_cparams(("parallel", "arbitrary"), 40),
        name="proj",
    )(x, gain, w)


def _out2_body(res_ref, ya_ref, yb_ref, wa_ref, wb_ref, o_ref):
    acc = jnp.dot(ya_ref[...].astype(BF16), wa_ref[...], preferred_element_type=F32)
    acc = acc + jnp.dot(yb_ref[...].astype(BF16), wb_ref[...], preferred_element_type=F32)
    o_ref[...] = res_ref[...] + acc


def _out1_body(res_ref, y_ref, w_ref, o_ref):
    o_ref[...] = res_ref[...] + jnp.dot(y_ref[...].astype(BF16), w_ref[...], preferred_element_type=F32)


def _out_proj(res, ys, ws):
    m = res.shape[0]
    body = _out2_body if len(ys) == 2 else _out1_body
    y_spec = pl.BlockSpec((TM_OUT, D), lambda j, i: (i, 0))
    w_spec = pl.BlockSpec((D, TN_OUT), lambda j, i: (0, j))
    r_spec = pl.BlockSpec((TM_OUT, TN_OUT), lambda j, i: (i, j))
    return pl.pallas_call(
        body,
        out_shape=jax.ShapeDtypeStruct((m, D), F32),
        grid=(D // TN_OUT, m // TM_OUT),
        in_specs=[r_spec] + [y_spec] * len(ys) + [w_spec] * len(ws),
        out_specs=r_spec,
        compiler_params=_cparams(("arbitrary", "arbitrary"), 48),
        name="out_proj",
    )(res, *ys, *ws)


def _carried_window(buf, cur_ref, init_ref, first, q):
    @pl.when(first)
    def _():
        buf[0:SUBLANES, :] = init_ref[0]

    buf[SUBLANES:SUBLANES + q, :] = cur_ref[...]


def _advance_window(buf, q):
    tail = buf[q:q + SUBLANES, :]
    buf[0:SUBLANES, :] = tail


def _causal_conv(buf, cw_ref, cb_ref, q):
    acc = cb_ref[...] + cw_ref[0:1, :] * buf[pl.ds(SUBLANES - CONV_W + 1, q), :]
    for k in range(1, CONV_W):
        acc = acc + cw_ref[k:k + 1, :] * buf[pl.ds(SUBLANES - CONV_W + 1 + k, q), :]
    return acc


def _cumsum_rows(x):
    n = x.shape[0]
    row = lax.broadcasted_iota(jnp.int32, x.shape, 0)
    s = 1
    while s < n:
        x = x + jnp.where(row >= s, pltpu.roll(x, s, axis=0), 0.0)
        s *= 2
    return x


def _expand_heads(a, rows):
    lane = lax.broadcasted_iota(jnp.int32, (rows, LANES), 1)
    low = lane < HEAD
    pieces = []
    for j in range(N_HEADS // 2):
        e0 = jnp.broadcast_to(a[:, 2 * j:2 * j + 1], (rows, LANES))
        e1 = jnp.broadcast_to(a[:, 2 * j + 1:2 * j + 2], (rows, LANES))
        pieces.append(jnp.where(low, e0, e1))
    return jnp.concatenate(pieces, axis=1)


def _head_allsum(x):
    w = x.shape[-1]
    lane = lax.broadcasted_iota(jnp.int32, x.shape, 1)
    s = 1
    while s < HEAD:
        up = pltpu.roll(x, w - s, axis=1)
        down = pltpu.roll(x, s, axis=1)
        x = x + jnp.where((lane & s) == 0, up, down)
        s *= 2
    return x


def _ssd_body(z_ref, x_ref, bc_ref, dt_ref, cix_ref, cibc_ref, h0_ref,
              cwx_ref, cwbc_ref, cbx_ref, cbbc_ref, dtb_ref, alog_ref, dsk_ref, gn_ref,
              y_ref, ho_ref, xbuf, bcbuf, zbuf, dtbuf, h_s, *, q, nc):
    c = pl.program_id(1)
    padded = q < CHUNK
    if padded:
        xbuf[...] = jnp.zeros_like(xbuf)
        bcbuf[...] = jnp.zeros_like(bcbuf)
        zbuf[...] = jnp.zeros_like(zbuf)
        dtbuf[...] = jnp.zeros_like(dtbuf)
    _carried_window(xbuf, x_ref, cix_ref, c == 0, q)
    _carried_window(bcbuf, bc_ref, cibc_ref, c == 0, q)

    @pl.when(c == 0)
    def _():
        h_s[...] = h0_ref[0]

    row = lax.broadcasted_iota(jnp.int32, (CHUNK, LANES), 0)
    xs = _silu(_causal_conv(xbuf, cwx_ref, cbx_ref, CHUNK))
    bcv = _silu(_causal_conv(bcbuf, cwbc_ref, cbbc_ref, CHUNK))
    if padded:
        zbuf[0:q, :] = z_ref[...]
        dtbuf[0:q, :] = dt_ref[...]
        z = zbuf[...]
        dt_raw = dtbuf[...]
    else:
        _advance_window(xbuf, q)
        _advance_window(bcbuf, q)
        z = z_ref[...]
        dt_raw = dt_ref[...]

    dt = _softplus(dt_raw + dtb_ref[...])
    if padded:
        dt = jnp.where(row < q, dt, 0.0)
    a_head = -jnp.exp(alog_ref[...])
    cum = _cumsum_rows(dt * a_head)
    cum_last = cum[CHUNK - 1:CHUNK, :]
    cum_t = cum.T
    dtx = _expand_heads(dt, CHUNK)
    ecx = _expand_heads(jnp.exp(cum), CHUNK)
    tlx = _expand_heads(jnp.exp(cum_last - cum), CHUNK)
    xdt = xs * dtx
    xdtw = (xdt * tlx).astype(BF16)

    ti = lax.broadcasted_iota(jnp.int32, (CHUNK, CHUNK), 0)
    tj = lax.broadcasted_iota(jnp.int32, (CHUNK, CHUNK), 1)
    causal = ti >= tj
    low = tj < HEAD
    nt = (((1,), (1,)), ((), ()))
    tn = (((0,), (0,)), ((), ()))

    for g in range(N_GROUPS):
        gsl = slice(g * GROUP_W, (g + 1) * GROUP_W)
        bg = bcv[:, g * N_STATE:(g + 1) * N_STATE].astype(BF16)
        cg = bcv[:, (N_GROUPS + g) * N_STATE:(N_GROUPS + g + 1) * N_STATE].astype(BF16)
        cb = lax.dot_general(cg, bg, nt, preferred_element_type=F32)
        hg = h_s[gsl, :]
        y_off = lax.dot_general(cg, hg.astype(BF16), nt, preferred_element_type=F32) * ecx[:, gsl]
        st = lax.dot_general(xdtw[:, gsl], bg, tn, preferred_element_type=F32)
        y_pairs = []
        for j in range(GROUP_W // LANES):
            h0 = g * (GROUP_W // HEAD) + 2 * j
            ms = []
            for h in (h0, h0 + 1):
                seg = cum[:, h:h + 1] - cum_t[h:h + 1, :]
                decay = jnp.exp(jnp.where(causal, seg, -jnp.inf))
                ms.append((cb * decay).astype(BF16))
            psl = slice(h0 * HEAD, (h0 + 2) * HEAD)
            slab = xdt[:, psl]
            rhs = jnp.concatenate([jnp.where(low, slab, 0.0), jnp.where(low, 0.0, slab)], axis=0).astype(BF16)
            y_pairs.append(jnp.dot(jnp.concatenate(ms, axis=1), rhs, preferred_element_type=F32))
            for h in (h0, h0 + 1):
                hsl = slice(h * HEAD, (h + 1) * HEAD)
                dec = jnp.exp(jnp.broadcast_to(cum_last[:, h:h + 1], (HEAD, N_STATE)))
                h_s[hsl, :] = h_s[hsl, :] * dec + st[(h % 8) * HEAD:(h % 8 + 1) * HEAD, :]
        y = jnp.concatenate(y_pairs, axis=1) + y_off + xs[:, gsl] * dsk_ref[:, gsl]
        y = y * _silu(z[:, gsl])
        y = y * lax.rsqrt(jnp.mean(y * y, axis=-1, keepdims=True) + EPS) * gn_ref[:, gsl]
        if padded:
            y_ref[:, gsl] = y[0:q, :]
        else:
            y_ref[:, gsl] = y

    @pl.when(c == nc - 1)
    def _():
        ho_ref[0] = h_s[...]


def _ssd(p, conv_x, conv_bc, h0, prm, nseq, seq_len):
    q = min(CHUNK, seq_len)
    nc = seq_len // q
    rows = lambda w, col: pl.BlockSpec((q, w), lambda s, c: (s * nc + c, col))
    per_seq = lambda shape: pl.BlockSpec((1,) + shape, lambda s, c: (s,) + (0,) * len(shape))
    const = lambda shape: pl.BlockSpec(shape, lambda s, c: (0,) * len(shape))
    y, h_last = pl.pallas_call(
        functools.partial(_ssd_body, q=q, nc=nc),
        out_shape=(jax.ShapeDtypeStruct((nseq * seq_len, D), F32),
                   jax.ShapeDtypeStruct((nseq, D, N_STATE), F32)),
        grid=(nseq, nc),
        in_specs=[
            rows(D, COL_Z // D), rows(D, COL_X // D), rows(2 * N_GROUPS * N_STATE, COL_BC // 1024),
            rows(LANES, COL_DT // LANES),
            per_seq((SUBLANES, D)), per_seq((SUBLANES, 1024)), per_seq((D, N_STATE)),
            const((CONV_W, D)), const((CONV_W, 1024)), const((1, D)), const((1, 1024)),
            const((1, LANES)), const((1, LANES)), const((1, D)), const((1, D)),
        ],
        out_specs=(pl.BlockSpec((q, D), lambda s, c: (s * nc + c, 0)), per_seq((D, N_STATE))),
        scratch_shapes=[
            pltpu.VMEM((CHUNK + SUBLANES, D), F32), pltpu.VMEM((CHUNK + SUBLANES, 1024), F32),
            pltpu.VMEM((CHUNK, D), F32), pltpu.VMEM((CHUNK, LANES), F32),
            pltpu.VMEM((D, N_STATE), F32),
        ],
        compiler_params=_cparams(("parallel", "arbitrary"), 40),
        name="ssd",
    )(p, p, p, p, conv_x, conv_bc, h0, *prm)
    return y, h_last


def _rwkv_pre_body(rkv_ref, lora_ref, si_rkv_ref, si_lora_ref, mu_rkv_ref, mu_lora_ref,
                   w0_ref, w2_ref, a0_ref, a2_ref, g2_ref, kk_ref, ka_ref, rk_ref,
                   r_out, w_out, k_out, v_out, kk_out, b_out, bonus_out, g_out,
                   rkvbuf, lorabuf, *, q):
    c = pl.program_id(1)
    _carried_window(rkvbuf, rkv_ref, si_rkv_ref, c == 0, q)
    _carried_window(lorabuf, lora_ref, si_lora_ref, c == 0, q)

    def shifted(buf, cur_ref, mu_ref):
        cur = cur_ref[...]
        prev = buf[pl.ds(SUBLANES - 1, q), :]
        return cur + mu_ref[...] * (prev - cur)

    ps = shifted(rkvbuf, rkv_ref, mu_rkv_ref)
    lo_in = shifted(lorabuf, lora_ref, mu_lora_ref)
    _advance_window(rkvbuf, q)
    _advance_window(lorabuf, q)
    r = ps[:, 0:D]
    k = ps[:, D:2 * D]
    v = ps[:, 2 * D:3 * D]

    lw = jnp.dot(jnp.tanh(lo_in).astype(BF16), w2_ref[...], preferred_element_type=F32)
    la = jnp.dot(lo_in.astype(BF16), a2_ref[...], preferred_element_type=F32)
    g = jnp.dot(jax.nn.sigmoid(lo_in).astype(BF16), g2_ref[...], preferred_element_type=F32)
    wlog = -_softplus(-(w0_ref[...] + lw)) - 0.5
    decay = jnp.exp(-jnp.exp(wlog))
    a = jax.nn.sigmoid(a0_ref[...] + la)
    kkf = k * kk_ref[...]
    norm = jnp.maximum(jnp.sqrt(_head_allsum(kkf * kkf)), 1e-12)
    kk = kkf / norm
    k2 = k * (1.0 + (a - 1.0) * ka_ref[...])
    bonus = _head_allsum(r * k2 * rk_ref[...]) * v
    r_out[...] = r
    w_out[...] = decay
    k_out[...] = k2
    v_out[...] = v
    kk_out[...] = kk
    b_out[...] = kk * a
    bonus_out[...] = bonus
    g_out[...] = g


def _rwkv_pre(p, shift_rkv, shift_lora, prm, nseq, seq_len):
    q = min(CHUNK, seq_len)
    nc = seq_len // q
    per_seq = lambda shape: pl.BlockSpec((1,) + shape, lambda s, c: (s,) + (0,) * len(shape))
    const = lambda shape: pl.BlockSpec(shape, lambda s, c: (0,) * len(shape))
    tile = pl.BlockSpec((q, D), lambda s, c: (s * nc + c, 0))
    sds = jax.ShapeDtypeStruct((nseq * seq_len, D), F32)
    return pl.pallas_call(
        functools.partial(_rwkv_pre_body, q=q),
        out_shape=(sds,) * 8,
        grid=(nseq, nc),
        in_specs=[
            pl.BlockSpec((q, 3 * D), lambda s, c: (s * nc + c, 0)),
            pl.BlockSpec((q, LORA_W), lambda s, c: (s * nc + c, COL_LORA // LORA_W)),
            per_seq((SUBLANES, 3 * D)), per_seq((SUBLANES, LORA_W)),
            const((1, 3 * D)), const((1, LORA_W)),
            const((1, D)), const((LORA_W, D)), const((1, D)), const((LORA_W, D)), const((LORA_W, D)),
            const((1, D)), const((1, D)), const((1, D)),
        ],
        out_specs=(tile,) * 8,
        scratch_shapes=[pltpu.VMEM((CHUNK + SUBLANES, 3 * D), F32), pltpu.VMEM((CHUNK + SUBLANES, LORA_W), F32)],
        compiler_params=_cparams(("parallel", "arbitrary"), 48),
        name="rwkv_pre",
    )(p, p, shift_rkv, shift_lora, *prm)


def _wkv_long_body(r_ref, w_ref, k_ref, v_ref, kk_ref, b_ref, o_ref, so_ref, s_s, xt_s, *, nc):
    c = pl.program_id(2)
    half = CHUNK // 2

    @pl.when(c == 0)
    def _():
        s_s[...] = jnp.zeros_like(s_s)

    lane = lax.broadcasted_iota(jnp.int32, (HEAD, LANES), 1)
    low = lane < HEAD
    for i, ref in enumerate((w_ref, kk_ref, b_ref, k_ref, r_ref)):
        xt = ref[...].T
        top = xt[0:HEAD, :]
        bot = xt[HEAD:2 * HEAD, :]
        xt_s[i, 0] = jnp.where(low, top, pltpu.roll(bot, half, axis=1))
        xt_s[i, 1] = jnp.where(low, pltpu.roll(top, half, axis=1), bot)
    lane_base = jnp.where(low, 0, half)

    def step(sub, tl, s):
        t = sub * half + tl
        idx = lane_base + tl
        wc, kkc, bc, kc, rc = [jnp.take_along_axis(xt_s[i, sub], idx, axis=1) for i in range(5)]
        vrow = v_ref[pl.ds(t, 1), :]
        sa = jnp.sum(s * kkc, axis=0, keepdims=True)
        s = s * wc - bc * sa + kc * vrow
        o_ref[pl.ds(t, 1), :] = jnp.sum(s * rc, axis=0, keepdims=True)
        return s

    s = s_s[...]
    for sub in range(2):
        s = lax.fori_loop(0, half, functools.partial(step, sub), s)
    s_s[...] = s

    @pl.when(c == nc - 1)
    def _():
        full = jnp.concatenate([s, jnp.zeros_like(s)], axis=0)
        so_ref[0] = full.T[:, 0:HEAD]


def _wkv_long(r, w, k, v, kk, b, nseq, seq_len):
    nc = seq_len // CHUNK
    tile = pl.BlockSpec((CHUNK, LANES), lambda s, hp, c: (s * nc + c, hp))
    o, s_last = pl.pallas_call(
        functools.partial(_wkv_long_body, nc=nc),
        out_shape=(jax.ShapeDtypeStruct((nseq * seq_len, D), F32),
                   jax.ShapeDtypeStruct((nseq, D, HEAD), F32)),
        grid=(nseq, D // LANES, nc),
        in_specs=[tile] * 6,
        out_specs=(tile, pl.BlockSpec((1, LANES, HEAD), lambda s, hp, c: (s, hp, 0))),
        scratch_shapes=[pltpu.VMEM((HEAD, LANES), F32), pltpu.VMEM((5, 2, HEAD, LANES), F32)],
        compiler_params=_cparams(("parallel", "parallel", "arbitrary"), 32),
        name="wkv_long",
    )(r, w, k, v, kk, b)
    return o, s_last


def _wkv_short_body(r_ref, w_ref, k_ref, v_ref, kk_ref, b_ref, s0_ref, o_ref, so_ref, *, steps):
    def per_v(vi, carry):
        s = s0_ref[0, vi]
        for t in range(steps):
            vrow = v_ref[t, 0, pl.ds(vi, 1), :]
            sa = jnp.sum(s * kk_ref[t, 0], axis=0, keepdims=True)
            s = s * w_ref[t, 0] - b_ref[t, 0] * sa + k_ref[t, 0] * vrow
            o_ref[t, 0, pl.ds(vi, 1), :] = jnp.sum(s * r_ref[t, 0], axis=0, keepdims=True)
        so_ref[0, vi] = s
        return carry

    lax.fori_loop(0, HEAD, per_v, 0)


def _wkv_short(r, w, k, v, kk, b, s0, nseq, seq_len):
    vec = pl.BlockSpec((seq_len, 1, HEAD, nseq), lambda h: (0, h, 0, 0))
    st = pl.BlockSpec((1, HEAD, HEAD, nseq), lambda h: (h, 0, 0, 0))
    return pl.pallas_call(
        functools.partial(_wkv_short_body, steps=seq_len),
        out_shape=(jax.ShapeDtypeStruct((seq_len, N_HEADS, HEAD, nseq), F32),
                   jax.ShapeDtypeStruct((N_HEADS, HEAD, HEAD, nseq), F32)),
        grid=(N_HEADS,),
        in_specs=[vec] * 6 + [st],
        out_specs=(vec, st),
        compiler_params=_cparams(("parallel",), 32),
        name="wkv_short",
    )(r, w, k, v, kk, b, s0)


def _rwkv_post_body(o_ref, bonus_ref, g_ref, lnw_ref, lnb_ref, y_ref):
    o = o_ref[...]
    mean = _head_allsum(o) * (1.0 / HEAD)
    cen = o - mean
    var = _head_allsum(cen * cen) * (1.0 / HEAD)
    on = cen * lax.rsqrt(var + GN_EPS) * lnw_ref[...] + lnb_ref[...]
    y_ref[...] = (on + bonus_ref[...]) * g_ref[...]


def _rwkv_post(o, bonus, g, ln_w, ln_b):
    m = o.shape[0]
    tile = pl.BlockSpec((TM_POST, D), lambda i: (i, 0))
    const = pl.BlockSpec((1, D), lambda i: (0, 0))
    return pl.pallas_call(
        _rwkv_post_body,
        out_shape=jax.ShapeDtypeStruct((m, D), F32),
        grid=(m // TM_POST,),
        in_specs=[tile, tile, tile, const, const],
        out_specs=tile,
        compiler_params=_cparams(("parallel",), 32),
        name="rwkv_post",
    )(o, bonus, g, ln_w, ln_b)


def _lru_body(gate_ref, x_ref, ci_ref, h0_ref, cw_ref, cb_ref, wga_ref, bga_ref, wgx_ref, bgx_ref, lam_ref,
              y_ref, ho_ref, xbuf, a_s, b_s, h_s, hc_s, *, q, nc):
    c = pl.program_id(1)
    _carried_window(xbuf, x_ref, ci_ref, c == 0, q)

    @pl.when(c == 0)
    def _():
        hc_s[...] = h0_ref[0]

    xc = _causal_conv(xbuf, cw_ref, cb_ref, q)
    _advance_window(xbuf, q)
    ra, rx = [], []
    for blk in range(LRU_BLOCKS):
        xh = xc[:, blk * LRU_BLK:(blk + 1) * LRU_BLK].astype(BF16)
        ra.append(jnp.dot(xh, wga_ref[blk], preferred_element_type=F32))
        rx.append(jnp.dot(xh, wgx_ref[blk], preferred_element_type=F32))
    rg = jax.nn.sigmoid(jnp.concatenate(ra, axis=1) + bga_ref[...])
    ig = jax.nn.sigmoid(jnp.concatenate(rx, axis=1) + bgx_ref[...])
    log_a = -LRU_C * rg * _softplus(-lam_ref[...])
    a_s[...] = jnp.exp(log_a)
    b_s[...] = jnp.sqrt(1.0 - jnp.exp(2.0 * log_a)) * (ig * xc)

    def step(t, h):
        h = a_s[pl.ds(t, 1), :] * h + b_s[pl.ds(t, 1), :]
        h_s[pl.ds(t, 1), :] = h
        return h

    h = lax.fori_loop(0, q, step, hc_s[...])
    hc_s[...] = h
    y_ref[...] = h_s[...] * jax.nn.gelu(gate_ref[...])

    @pl.when(c == nc - 1)
    def _():
        ho_ref[0] = h


def _lru(p, conv_init, h0, prm, nseq, seq_len):
    q = min(CHUNK, seq_len)
    nc = seq_len // q
    per_seq = lambda shape: pl.BlockSpec((1,) + shape, lambda s, c: (s,) + (0,) * len(shape))
    const = lambda shape: pl.BlockSpec(shape, lambda s, c: (0,) * len(shape))
    return pl.pallas_call(
        functools.partial(_lru_body, q=q, nc=nc),
        out_shape=(jax.ShapeDtypeStruct((nseq * seq_len, D), F32),
                   jax.ShapeDtypeStruct((nseq, 1, D), F32)),
        grid=(nseq, nc),
        in_specs=[
            pl.BlockSpec((q, D), lambda s, c: (s * nc + c, 0)),
            pl.BlockSpec((q, D), lambda s, c: (s * nc + c, 1)),
            per_seq((SUBLANES, D)), per_seq((1, D)),
            const((CONV_W, D)), const((1, D)),
            const((LRU_BLOCKS, LRU_BLK, LRU_BLK)), const((1, D)),
            const((LRU_BLOCKS, LRU_BLK, LRU_BLK)), const((1, D)), const((1, D)),
        ],
        out_specs=(pl.BlockSpec((q, D), lambda s, c: (s * nc + c, 0)), per_seq((1, D))),
        scratch_shapes=[pltpu.VMEM((CHUNK + SUBLANES, D), F32), pltpu.VMEM((q, D), F32), pltpu.VMEM((q, D), F32),
                        pltpu.VMEM((q, D), F32), pltpu.VMEM((1, D), F32)],
        compiler_params=_cparams(("parallel", "arbitrary"), 32),
        name="lru",
    )(p, p, conv_init, h0, *prm)


def _pad_front_rows(buf):
    return jnp.pad(buf, ((0, 0), (SUBLANES - buf.shape[1], 0), (0, 0)))


def _row2(v):
    return v.reshape(1, -1)


def _prep_ffn(w_in, w_out):
    pad = D_FF_PAD - D_FF
    wg = jnp.pad(w_in[:, :D_FF], ((0, 0), (0, pad))).astype(BF16)
    wu = jnp.pad(w_in[:, D_FF:], ((0, 0), (0, pad))).astype(BF16)
    wo = jnp.pad(w_out, ((0, pad), (0, 0))).astype(BF16)
    return wg, wu, wo


def kernel(x_prompt, x_sample, state_ssm_a, state_conv_a, state_wkv_b, state_shift_b, state_lru_c, state_conv_c, norm_gain, w_ffn_in, w_ffn_out, w_in_ab, conv_w_a, conv_b_a, dt_bias_a, a_log_a, d_skip_a, gnorm_a, mu_b, w0_b, w2_b, a0_b, a2_b, g2_b, k_k_b, k_a_b, r_k_b, ln_w_b, ln_b_b, w_out_ab, w_in_c, conv_w_c, conv_b_c, w_gate_a_c, b_gate_a_c, w_gate_x_c, b_gate_x_c, lambda_c, w_out_c, final_norm_gain):
    ffn_w = [[_prep_ffn(w_ffn_in[i, s], w_ffn_out[i, s]) for s in range(2)] for i in range(2)]

    in_a = D + (D + 2 * N_GROUPS * N_STATE) + N_HEADS
    wab = w_in_ab[0]
    bc_w = 2 * N_GROUPS * N_STATE
    w_proj0 = jnp.concatenate([
        wab[:, in_a:in_a + 3 * D],
        wab[:, 0:D],
        wab[:, D:2 * D],
        wab[:, 2 * D:2 * D + bc_w],
        wab[:, in_a + 3 * D:in_a + 3 * D + LORA_W],
        wab[:, in_a - N_HEADS:in_a],
        jnp.zeros((D, PROJ_W - COL_DT - N_HEADS), F32),
    ], axis=1).astype(BF16)
    w_proj1 = w_in_c[0].astype(BF16)
    w_out0 = w_out_ab[0].astype(BF16)
    w_out1 = w_out_c[0].astype(BF16)

    pad_lanes = lambda v: jnp.pad(v.reshape(1, -1), ((0, 0), (0, LANES - v.shape[-1])))
    rep_head = lambda v: jnp.repeat(v, HEAD).reshape(1, D)
    ssd_prm = (conv_w_a[0][:, :D], conv_w_a[0][:, D:], _row2(conv_b_a[0][:D]), _row2(conv_b_a[0][D:]),
               pad_lanes(dt_bias_a[0]), pad_lanes(a_log_a[0]), rep_head(d_skip_a[0]), _row2(gnorm_a[0]))
    lora_rows = lambda w, lo: jnp.pad(w, ((lo, LORA_W - lo - w.shape[0]), (0, 0))).astype(BF16)
    mu = mu_b[0]
    rwkv_prm = (_row2(mu[:3 * D]), _row2(mu[3 * D:]),
                _row2(w0_b[0]), lora_rows(w2_b[0], 0), _row2(a0_b[0]), lora_rows(a2_b[0], 64),
                lora_rows(g2_b[0], 128), _row2(k_k_b[0]), _row2(k_a_b[0]), _row2(r_k_b[0]))
    lru_prm = (conv_w_c[0], _row2(conv_b_c[0]), w_gate_a_c[0].astype(BF16), _row2(b_gate_a_c[0]),
               w_gate_x_c[0].astype(BF16), _row2(b_gate_x_c[0]), _row2(lambda_c[0]))

    def trunk(x3, ssm0, conva0, wkv0, shift0, lru0, convc0):
        nseq, seq_len, _ = x3.shape
        x = x3.reshape(nseq * seq_len, D)
        last = lambda arr, n: arr.reshape(nseq, seq_len, -1)[:, seq_len - n:, :]

        x = _ffn(x, _row2(norm_gain[0, 0]), *ffn_w[0][0])
        p = _proj(x, _row2(norm_gain[0, 1]), w_proj0)
        conv_x = _pad_front_rows(conva0[:, :, :D])
        conv_bc = _pad_front_rows(conva0[:, :, D:])
        ya, ssm_n = _ssd(p, conv_x, conv_bc, ssm0.reshape(nseq, D, N_STATE), ssd_prm, nseq, seq_len)
        conva_n = jnp.concatenate([last(p[:, COL_X:COL_X + D], 3), last(p[:, COL_BC:COL_BC + bc_w], 3)], axis=-1)
        shift_n = jnp.concatenate([last(p[:, :3 * D], 1), last(p[:, COL_LORA:COL_LORA + LORA_W], 1)], axis=-1)

        r, w, k, v, kk, b, bonus, g = _rwkv_pre(p, _pad_front_rows(shift0[:, :, :3 * D]),
                                                _pad_front_rows(shift0[:, :, 3 * D:]), rwkv_prm, nseq, seq_len)
        if wkv0 is None:
            o, wkv_n = _wkv_long(r, w, k, v, kk, b, nseq, seq_len)
            wkv_n = wkv_n.reshape(nseq, N_HEADS, HEAD, HEAD)
        else:
            to_lanes = lambda a: a.reshape(nseq, seq_len, N_HEADS, HEAD).transpose(1, 2, 3, 0)
            o, wkv_n = _wkv_short(*[to_lanes(a) for a in (r, w, k, v, kk, b)], wkv0.transpose(1, 2, 3, 0),
                                  nseq, seq_len)
            o = o.transpose(3, 0, 1, 2).reshape(nseq * seq_len, D)
            wkv_n = wkv_n.transpose(3, 0, 1, 2)
        yb = _rwkv_post(o, bonus, g, _row2(ln_w_b[0]), _row2(ln_b_b[0]))
        x = _out_proj(x, (ya, yb), (w_out0[:D], w_out0[D:]))
        x = _ffn(x, _row2(norm_gain[0, 2]), *ffn_w[0][1])

        x = _ffn(x, _row2(norm_gain[1, 0]), *ffn_w[1][0])
        pc = _proj(x, _row2(norm_gain[1, 1]), w_proj1)
        yc, lru_n = _lru(pc, _pad_front_rows(convc0), lru0.reshape(nseq, 1, D), lru_prm, nseq, seq_len)
        convc_n = last(pc[:, D:], 3)
        x = _out_proj(x, (yc,), (w_out1,))
        y = _ffn(x, _row2(norm_gain[1, 2]), *ffn_w[1][1], final_gain=_row2(final_norm_gain))

        return (y.reshape(nseq, seq_len, D), ssm_n.reshape(1, nseq, N_HEADS, HEAD, N_STATE), conva_n[None],
                wkv_n[None], shift_n[None], lru_n.reshape(1, nseq, D), convc_n[None])

    bp = x_prompt.shape[0]
    zeros = lambda s: jnp.zeros((bp,) + s.shape[2:], F32)
    outs_p = trunk(x_prompt, zeros(state_ssm_a), zeros(state_conv_a), None, zeros(state_shift_b),
                   zeros(state_lru_c), zeros(state_conv_c))
    outs_s = trunk(x_sample, state_ssm_a[0], state_conv_a[0], state_wkv_b[0], state_shift_b[0],
                   state_lru_c[0], state_conv_c[0])
    return (outs_p[0], outs_s[0]) + outs_p[1:] + outs_s[1:]
```

```python
import functools

import jax
import jax.numpy as jnp
from jax import lax
from jax.experimental import pallas as pl
from jax.experimental.pallas import tpu as pltpu

F32 = jnp.float32
BF16 = jnp.bfloat16

D = 2048
D_FF = 5504
D_FF_PAD = 5632
HEAD = 64
N_HEADS = 32
N_GROUPS = 4
GROUP_W = D // N_GROUPS
N_STATE = 128
CONV_W = 4
LORA_W = 256
LRU_BLOCKS = 8
LRU_BLK = D // LRU_BLOCKS
LRU_C = 8.0
EPS = 1e-6
GN_EPS = 64e-5
SUBLANES = 8
LANES = 128
CHUNK = 128
WKV_CHUNK = 64
WKV_LANES = 256

COL_RKV = 0
COL_Z = 3 * D
COL_X = 4 * D
COL_BC = 5 * D
COL_LORA = 5 * D + 2 * N_GROUPS * N_STATE
COL_DT = COL_LORA + LORA_W
PROJ_W = COL_DT + 256

TM_FFN = 512
TF_FFN = 512
TM_PROJ = 1024
TN_PROJ = 512
TM_OUT = 512
TN_OUT = 1024
TM_POST = 256


def _cparams(sem, vmem_mib):
    return pltpu.CompilerParams(dimension_semantics=sem, vmem_limit_bytes=vmem_mib * 1024 * 1024)


def _softplus(x):
    return jnp.maximum(x, 0.0) + jnp.log(1.0 + jnp.exp(-jnp.abs(x)))


def _silu(x):
    return x * jax.nn.sigmoid(x)


def _rms(x, gain):
    ms = jnp.mean(x * x, axis=-1, keepdims=True)
    return x * lax.rsqrt(ms + EPS) * gain


def _ffn_body(x_ref, g_ref, wg_ref, wu_ref, wo_ref, fg_ref, o_ref, xn_ref, acc_ref, *, nf, final):
    f = pl.program_id(1)

    @pl.when(f == 0)
    def _():
        xn_ref[...] = _rms(x_ref[...], g_ref[...]).astype(BF16)
        acc_ref[...] = jnp.zeros_like(acc_ref)

    xn = xn_ref[...]
    gate = jnp.dot(xn, wg_ref[...], preferred_element_type=F32)
    up = jnp.dot(xn, wu_ref[...], preferred_element_type=F32)
    h = (_silu(gate) * up).astype(BF16)
    acc_ref[...] += jnp.dot(h, wo_ref[...], preferred_element_type=F32)

    @pl.when(f == nf - 1)
    def _():
        y = x_ref[...] + 0.5 * acc_ref[...]
        if final:
            y = _rms(y, fg_ref[...])
        o_ref[...] = y


def _ffn(x, gain, wg, wu, wo, final_gain=None):
    m = x.shape[0]
    nf = D_FF_PAD // TF_FFN
    final = final_gain is not None
    fg = final_gain if final else gain
    return pl.pallas_call(
        functools.partial(_ffn_body, nf=nf, final=final),
        out_shape=jax.ShapeDtypeStruct((m, D), F32),
        grid=(m // TM_FFN, nf),
        in_specs=[
            pl.BlockSpec((TM_FFN, D), lambda i, f: (i, 0)),
            pl.BlockSpec((1, D), lambda i, f: (0, 0)),
            pl.BlockSpec((D, TF_FFN), lambda i, f: (0, f)),
            pl.BlockSpec((D, TF_FFN), lambda i, f: (0, f)),
            pl.BlockSpec((TF_FFN, D), lambda i, f: (f, 0)),
            pl.BlockSpec((1, D), lambda i, f: (0, 0)),
        ],
        out_specs=pl.BlockSpec((TM_FFN, D), lambda i, f: (i, 0)),
        scratch_shapes=[pltpu.VMEM((TM_FFN, D), BF16), pltpu.VMEM((TM_FFN, D), F32)],
        compiler_params=_cparams(("parallel", "arbitrary"), 48),
        name="ffn",
    )(x, gain, wg, wu, wo, fg)


def _proj_body(x_ref, g_ref, w_ref, o_ref, xn_ref):
    @pl.when(pl.program_id(1) == 0)
    def _():
        xn_ref[...] = _rms(x_ref[...], g_ref[...]).astype(BF16)

    o_ref[...] = jnp.dot(xn_ref[...], w_ref[...], preferred_element_type=F32)


def _proj(x, gain, w):
    m = x.shape[0]
    n = w.shape[1]
    return pl.pallas_call(
        _proj_body,
        out_shape=jax.ShapeDtypeStruct((m, n), F32),
        grid=(m // TM_PROJ, n // TN_PROJ),
        in_specs=[
            pl.BlockSpec((TM_PROJ, D), lambda i, j: (i, 0)),
            pl.BlockSpec((1, D), lambda i, j: (0, 0)),
            pl.BlockSpec((D, TN_PROJ), lambda i, j: (0, j)),
        ],
        out_specs=pl.BlockSpec((TM_PROJ, TN_PROJ), lambda i, j: (i, j)),
        scratch_shapes=[pltpu.VMEM((TM_PROJ, D), BF16)],
        compiler_params=_cparams(("parallel", "arbitrary"), 40),
        name="proj",
    )(x, gain, w)


def _out2_body(res_ref, ya_ref, yb_ref, wa_ref, wb_ref, o_ref):
    acc = jnp.dot(ya_ref[...].astype(BF16), wa_ref[...], preferred_element_type=F32)
    acc = acc + jnp.dot(yb_ref[...].astype(BF16), wb_ref[...], preferred_element_type=F32)
    o_ref[...] = res_ref[...] + acc


def _out1_body(res_ref, y_ref, w_ref, o_ref):
    o_ref[...] = res_ref[...] + jnp.dot(y_ref[...].astype(BF16), w_ref[...], preferred_element_type=F32)


def _out_proj(res, ys, ws):
    m = res.shape[0]
    body = _out2_body if len(ys) == 2 else _out1_body
    y_spec = pl.BlockSpec((TM_OUT, D), lambda j, i: (i, 0))
    w_spec = pl.BlockSpec((D, TN_OUT), lambda j, i: (0, j))
    r_spec = pl.BlockSpec((TM_OUT, TN_OUT), lambda j, i: (i, j))
    return pl.pallas_call(
        body,
        out_shape=jax.ShapeDtypeStruct((m, D), F32),
        grid=(D // TN_OUT, m // TM_OUT),
        in_specs=[r_spec] + [y_spec] * len(ys) + [w_spec] * len(ws),
        out_specs=r_spec,
        compiler_params=_cparams(("arbitrary", "arbitrary"), 48),
        name="out_proj",
    )(res, *ys, *ws)


def _carried_window(buf, cur_ref, init_ref, first, q):
    @pl.when(first)
    def _():
        buf[0:SUBLANES, :] = init_ref[0]

    buf[SUBLANES:SUBLANES + q, :] = cur_ref[...]


def _advance_window(buf, q):
    tail = buf[q:q + SUBLANES, :]
    buf[0:SUBLANES, :] = tail


def _causal_conv(buf, cw_ref, cb_ref, q):
    acc = cb_ref[...] + cw_ref[0:1, :] * buf[pl.ds(SUBLANES - CONV_W + 1, q), :]
    for k in range(1, CONV_W):
        acc = acc + cw_ref[k:k + 1, :] * buf[pl.ds(SUBLANES - CONV_W + 1 + k, q), :]
    return acc


def _cumsum_rows(x):
    n = x.shape[0]
    row = lax.broadcasted_iota(jnp.int32, x.shape, 0)
    s = 1
    while s < n:
        x = x + jnp.where(row >= s, pltpu.roll(x, s, axis=0), 0.0)
        s *= 2
    return x


def _expand_heads(a, rows):
    lane = lax.broadcasted_iota(jnp.int32, (rows, LANES), 1)
    low = lane < HEAD
    pieces = []
    for j in range(N_HEADS // 2):
        e0 = jnp.broadcast_to(a[:, 2 * j:2 * j + 1], (rows, LANES))
        e1 = jnp.broadcast_to(a[:, 2 * j + 1:2 * j + 2], (rows, LANES))
        pieces.append(jnp.where(low, e0, e1))
    return jnp.concatenate(pieces, axis=1)


def _head_allsum(x):
    w = x.shape[-1]
    lane = lax.broadcasted_iota(jnp.int32, x.shape, 1)
    s = 1
    while s < HEAD:
        up = pltpu.roll(x, w - s, axis=1)
        down = pltpu.roll(x, s, axis=1)
        x = x + jnp.where((lane & s) == 0, up, down)
        s *= 2
    return x


def _ssd_body(z_ref, x_ref, bc_ref, dt_ref, cix_ref, cibc_ref, h0_ref,
              cwx_ref, cwbc_ref, cbx_ref, cbbc_ref, dtb_ref, alog_ref, dsk_ref, gn_ref,
              y_ref, ho_ref, xbuf, bcbuf, zbuf, dtbuf, h_s, *, q, nc):
    c = pl.program_id(1)
    padded = q < CHUNK
    if padded:
        xbuf[...] = jnp.zeros_like(xbuf)
        bcbuf[...] = jnp.zeros_like(bcbuf)
        zbuf[...] = jnp.zeros_like(zbuf)
        dtbuf[...] = jnp.zeros_like(dtbuf)
    _carried_window(xbuf, x_ref, cix_ref, c == 0, q)
    _carried_window(bcbuf, bc_ref, cibc_ref, c == 0, q)

    @pl.when(c == 0)
    def _():
        h_s[...] = h0_ref[0]

    row = lax.broadcasted_iota(jnp.int32, (CHUNK, LANES), 0)
    xs = _silu(_causal_conv(xbuf, cwx_ref, cbx_ref, CHUNK))
    bcv = _silu(_causal_conv(bcbuf, cwbc_ref, cbbc_ref, CHUNK))
    if padded:
        zbuf[0:q, :] = z_ref[...]
        dtbuf[0:q, :] = dt_ref[...]
        z = zbuf[...]
        dt_raw = dtbuf[...]
    else:
        _advance_window(xbuf, q)
        _advance_window(bcbuf, q)
        z = z_ref[...]
        dt_raw = dt_ref[...]

    dt = _softplus(dt_raw + dtb_ref[...])
    if padded:
        dt = jnp.where(row < q, dt, 0.0)
    a_head = -jnp.exp(alog_ref[...])
    cum = _cumsum_rows(dt * a_head)
    cum_last = cum[CHUNK - 1:CHUNK, :]
    cum_t = cum.T
    dtx = _expand_heads(dt, CHUNK)
    ecx = _expand_heads(jnp.exp(cum), CHUNK)
    tlx = _expand_heads(jnp.exp(cum_last - cum), CHUNK)
    xdt = xs * dtx
    xdtw = (xdt * tlx).astype(BF16)

    ti = lax.broadcasted_iota(jnp.int32, (CHUNK, CHUNK), 0)
    tj = lax.broadcasted_iota(jnp.int32, (CHUNK, CHUNK), 1)
    causal = ti >= tj
    low = tj < HEAD
    nt = (((1,), (1,)), ((), ()))
    tn = (((0,), (0,)), ((), ()))

    for g in range(N_GROUPS):
        gsl = slice(g * GROUP_W, (g + 1) * GROUP_W)
        bg = bcv[:, g * N_STATE:(g + 1) * N_STATE].astype(BF16)
        cg = bcv[:, (N_GROUPS + g) * N_STATE:(N_GROUPS + g + 1) * N_STATE].astype(BF16)
        cb = lax.dot_general(cg, bg, nt, preferred_element_type=F32)
        hg = h_s[gsl, :]
        y_off = lax.dot_general(cg, hg.astype(BF16), nt, preferred_element_type=F32) * ecx[:, gsl]
        st = lax.dot_general(xdtw[:, gsl], bg, tn, preferred_element_type=F32)
        y_pairs = []
        for j in range(GROUP_W // LANES):
            h0 = g * (GROUP_W // HEAD) + 2 * j
            ms = []
            for h in (h0, h0 + 1):
                seg = cum[:, h:h + 1] - cum_t[h:h + 1, :]
                decay = jnp.exp(jnp.where(causal, seg, -jnp.inf))
                ms.append((cb * decay).astype(BF16))
            psl = slice(h0 * HEAD, (h0 + 2) * HEAD)
            slab = xdt[:, psl]
            rhs = jnp.concatenate([jnp.where(low, slab, 0.0), jnp.where(low, 0.0, slab)], axis=0).astype(BF16)
            y_pairs.append(jnp.dot(jnp.concatenate(ms, axis=1), rhs, preferred_element_type=F32))
            for h in (h0, h0 + 1):
                hsl = slice(h * HEAD, (h + 1) * HEAD)
                dec = jnp.exp(jnp.broadcast_to(cum_last[:, h:h + 1], (HEAD, N_STATE)))
                h_s[hsl, :] = h_s[hsl, :] * dec + st[(h % 8) * HEAD:(h % 8 + 1) * HEAD, :]
        y = jnp.concatenate(y_pairs, axis=1) + y_off + xs[:, gsl] * dsk_ref[:, gsl]
        y = y * _silu(z[:, gsl])
        y = y * lax.rsqrt(jnp.mean(y * y, axis=-1, keepdims=True) + EPS) * gn_ref[:, gsl]
        if padded:
            y_ref[:, gsl] = y[0:q, :]
        else:
            y_ref[:, gsl] = y

    @pl.when(c == nc - 1)
    def _():
        ho_ref[0] = h_s[...]


def _ssd(p, conv_x, conv_bc, h0, prm, nseq, seq_len):
    q = min(CHUNK, seq_len)
    nc = seq_len // q
    rows = lambda w, col: pl.BlockSpec((q, w), lambda s, c: (s * nc + c, col))
    per_seq = lambda shape: pl.BlockSpec((1,) + shape, lambda s, c: (s,) + (0,) * len(shape))
    const = lambda shape: pl.BlockSpec(shape, lambda s, c: (0,) * len(shape))
    y, h_last = pl.pallas_call(
        functools.partial(_ssd_body, q=q, nc=nc),
        out_shape=(jax.ShapeDtypeStruct((nseq * seq_len, D), F32),
                   jax.ShapeDtypeStruct((nseq, D, N_STATE), F32)),
        grid=(nseq, nc),
        in_specs=[
            rows(D, COL_Z // D), rows(D, COL_X // D), rows(2 * N_GROUPS * N_STATE, COL_BC // 1024),
            rows(LANES, COL_DT // LANES),
            per_seq((SUBLANES, D)), per_seq((SUBLANES, 1024)), per_seq((D, N_STATE)),
            const((CONV_W, D)), const((CONV_W, 1024)), const((1, D)), const((1, 1024)),
            const((1, LANES)), const((1, LANES)), const((1, D)), const((1, D)),
        ],
        out_specs=(pl.BlockSpec((q, D), lambda s, c: (s * nc + c, 0)), per_seq((D, N_STATE))),
        scratch_shapes=[
            pltpu.VMEM((CHUNK + SUBLANES, D), F32), pltpu.VMEM((CHUNK + SUBLANES, 1024), F32),
            pltpu.VMEM((CHUNK, D), F32), pltpu.VMEM((CHUNK, LANES), F32),
            pltpu.VMEM((D, N_STATE), F32),
        ],
        compiler_params=_cparams(("parallel", "arbitrary"), 40),
        name="ssd",
    )(p, p, p, p, conv_x, conv_bc, h0, *prm)
    return y, h_last


def _rwkv_pre_body(rkv_ref, lora_ref, si_rkv_ref, si_lora_ref, mu_rkv_ref, mu_lora_ref,
                   w0_ref, w2_ref, a0_ref, a2_ref, g2_ref, kk_ref, ka_ref, rk_ref,
                   r_out, w_out, k_out, v_out, kk_out, b_out, bonus_out, g_out,
                   rkvbuf, lorabuf, *, q, log_decay):
    c = pl.program_id(1)
    _carried_window(rkvbuf, rkv_ref, si_rkv_ref, c == 0, q)
    _carried_window(lorabuf, lora_ref, si_lora_ref, c == 0, q)

    def shifted(buf, cur_ref, mu_ref):
        cur = cur_ref[...]
        prev = buf[pl.ds(SUBLANES - 1, q), :]
        return cur + mu_ref[...] * (prev - cur)

    ps = shifted(rkvbuf, rkv_ref, mu_rkv_ref)
    lo_in = shifted(lorabuf, lora_ref, mu_lora_ref)
    _advance_window(rkvbuf, q)
    _advance_window(lorabuf, q)
    r = ps[:, 0:D]
    k = ps[:, D:2 * D]
    v = ps[:, 2 * D:3 * D]

    lw = jnp.dot(jnp.tanh(lo_in).astype(BF16), w2_ref[...], preferred_element_type=F32)
    la = jnp.dot(lo_in.astype(BF16), a2_ref[...], preferred_element_type=F32)
    g = jnp.dot(jax.nn.sigmoid(lo_in).astype(BF16), g2_ref[...], preferred_element_type=F32)
    wlog = -_softplus(-(w0_ref[...] + lw)) - 0.5
    log_w = -jnp.exp(wlog)
    a = jax.nn.sigmoid(a0_ref[...] + la)
    kkf = k * kk_ref[...]
    norm = jnp.maximum(jnp.sqrt(_head_allsum(kkf * kkf)), 1e-12)
    kk = kkf / norm
    k2 = k * (1.0 + (a - 1.0) * ka_ref[...])
    bonus = _head_allsum(r * k2 * rk_ref[...]) * v
    r_out[...] = r
    w_out[...] = log_w if log_decay else jnp.exp(log_w)
    k_out[...] = k2
    v_out[...] = v
    kk_out[...] = kk
    b_out[...] = kk * a
    bonus_out[...] = bonus
    g_out[...] = g


def _rwkv_pre(p, shift_rkv, shift_lora, prm, nseq, seq_len, log_decay):
    q = min(CHUNK, seq_len)
    nc = seq_len // q
    per_seq = lambda shape: pl.BlockSpec((1,) + shape, lambda s, c: (s,) + (0,) * len(shape))
    const = lambda shape: pl.BlockSpec(shape, lambda s, c: (0,) * len(shape))
    tile = pl.BlockSpec((q, D), lambda s, c: (s * nc + c, 0))
    sds = jax.ShapeDtypeStruct((nseq * seq_len, D), F32)
    return pl.pallas_call(
        functools.partial(_rwkv_pre_body, q=q, log_decay=log_decay),
        out_shape=(sds,) * 8,
        grid=(nseq, nc),
        in_specs=[
            pl.BlockSpec((q, 3 * D), lambda s, c: (s * nc + c, 0)),
            pl.BlockSpec((q, LORA_W), lambda s, c: (s * nc + c, COL_LORA // LORA_W)),
            per_seq((SUBLANES, 3 * D)), per_seq((SUBLANES, LORA_W)),
            const((1, 3 * D)), const((1, LORA_W)),
            const((1, D)), const((LORA_W, D)), const((1, D)), const((LORA_W, D)), const((LORA_W, D)),
            const((1, D)), const((1, D)), const((1, D)),
        ],
        out_specs=(tile,) * 8,
        scratch_shapes=[pltpu.VMEM((CHUNK + SUBLANES, 3 * D), F32), pltpu.VMEM((CHUNK + SUBLANES, LORA_W), F32)],
        compiler_params=_cparams(("parallel", "arbitrary"), 48),
        name="rwkv_pre",
    )(p, p, shift_rkv, shift_lora, *prm)


def _split_bf16(x):
    hi = x.astype(BF16)
    return hi, (x - hi.astype(F32)).astype(BF16)


def _dot3(a, b, dims):
    ah, al = _split_bf16(a)
    bh, bl = _split_bf16(b)
    dg = lambda x, y: lax.dot_general(x, y, dims, preferred_element_type=F32)
    return dg(ah, bh) + (dg(ah, bl) + dg(al, bh))


_NN = (((1,), (0,)), ((), ()))
_NT = (((1,), (1,)), ((), ()))
_TN = (((0,), (0,)), ((), ()))


def _solve_unit_lower(n, rhs):
    rows, width = rhs.shape
    nblk, ncol = rows // SUBLANES, width // LANES
    nb = [[n[SUBLANES * i:SUBLANES * (i + 1), LANES * j:LANES * (j + 1)] for j in range(ncol)] for i in range(nblk)]
    xb = [[rhs[SUBLANES * i:SUBLANES * (i + 1), LANES * j:LANES * (j + 1)] for j in range(ncol)] for i in range(nblk)]
    low = lax.broadcasted_iota(jnp.int32, (SUBLANES, LANES), 1) < HEAD
    for s in range(rows - 1):
        i0, r0 = divmod(s, SUBLANES)
        idx = jnp.where(low, s, HEAD + s)
        for j in range(ncol):
            row = xb[i0][j][r0:r0 + 1, :]
            for i in range(i0 if r0 < SUBLANES - 1 else i0 + 1, nblk):
                xb[i][j] = xb[i][j] - jnp.take_along_axis(nb[i][j], idx, axis=1) * row
    return jnp.concatenate([jnp.concatenate(xr, axis=1) for xr in xb], axis=0)


def _wkv_long_body(r_ref, lw_ref, k_ref, v_ref, kk_ref, b_ref, o_ref, so_ref, st_s, *, nc):
    c = pl.program_id(2)
    rows, width = r_ref.shape

    @pl.when(c == 0)
    def _():
        st_s[...] = jnp.zeros_like(st_s)

    lw = lw_ref[...]
    v = v_ref[...]
    cl = _cumsum_rows(lw)
    cl_last = cl[rows - 1:rows, :]
    p_inv = jnp.exp(-cl)
    p_end = jnp.exp(cl_last - cl)
    x2 = jnp.concatenate([kk_ref[...] * jnp.exp(cl - lw), r_ref[...] * jnp.exp(cl)], axis=0)
    k = k_ref[...]
    b = b_ref[...]

    nh = width // HEAD
    bd_r = lax.broadcasted_iota(jnp.int32, (nh * rows, width), 0) // rows
    bd_c = lax.broadcasted_iota(jnp.int32, (nh * rows, width), 1) // HEAD
    block_diag = bd_r == bd_c

    def per_head_rows(y):
        return jnp.where(block_diag, jnp.concatenate([y] * nh, axis=0), 0.0)

    t_i = lax.broadcasted_iota(jnp.int32, (rows, width), 0)
    lane = lax.broadcasted_iota(jnp.int32, (rows, width), 1)
    s_i = lane & (HEAD - 1)
    strict = t_i > s_i
    incl = t_i >= s_i

    ak = _dot3(x2, per_head_rows(k * p_inv), _NT)
    ab = _dot3(x2, per_head_rows(b * p_inv), _NT)
    a_kk = jnp.where(strict, ak[0:rows], 0.0)
    a_rk = jnp.where(incl, ak[rows:], 0.0)
    a_kb = jnp.where(strict, ab[0:rows], 0.0)
    a_rb = jnp.where(incl, ab[rows:], 0.0)
    st = st_s[...]
    xs = _dot3(x2, per_head_rows(st), _NN)
    av = _dot3(jnp.concatenate([a_kk, a_rk], axis=0), per_head_rows(v), _NN)
    sa = _solve_unit_lower(a_kb, xs[0:rows] + av[0:rows])
    o_ref[...] = xs[rows:] + av[rows:] - _dot3(a_rb, per_head_rows(sa), _NN)

    eye = jnp.where(t_i == s_i, jnp.exp(cl_last), 0.0)
    lhs = jnp.concatenate([k * p_end, -(b * p_end), eye], axis=0)
    full = _dot3(lhs, jnp.concatenate([v, sa, st], axis=0), _TN)
    head_of_lane = lane // HEAD
    st_new = jnp.zeros_like(st)
    for h in range(nh):
        st_new = st_new + jnp.where(head_of_lane == h, full[h * HEAD:(h + 1) * HEAD, :], 0.0)
    st_s[...] = st_new

    @pl.when(c == nc - 1)
    def _():
        for j in range(width // LANES):
            sq = jnp.concatenate([st_new[:, j * LANES:(j + 1) * LANES], jnp.zeros((LANES - HEAD, LANES), F32)], axis=0)
            so_ref[0, j * LANES:(j + 1) * LANES, :] = sq.T[:, 0:HEAD]


def _wkv_long(r, lw, k, v, kk, b, nseq, seq_len):
    nc = seq_len // WKV_CHUNK
    tile = pl.BlockSpec((WKV_CHUNK, WKV_LANES), lambda s, hg, c: (s * nc + c, hg))
    o, s_last = pl.pallas_call(
        functools.partial(_wkv_long_body, nc=nc),
        out_shape=(jax.ShapeDtypeStruct((nseq * seq_len, D), F32),
                   jax.ShapeDtypeStruct((nseq, D, HEAD), F32)),
        grid=(nseq, D // WKV_LANES, nc),
        in_specs=[tile] * 6,
        out_specs=(tile, pl.BlockSpec((1, WKV_LANES, HEAD), lambda s, hg, c: (s, hg, 0))),
        scratch_shapes=[pltpu.VMEM((HEAD, WKV_LANES), F32)],
        compiler_params=_cparams(("parallel", "parallel", "arbitrary"), 32),
        name="wkv_long",
    )(r, lw, k, v, kk, b)
    return o, s_last


def _wkv_short_body(r_ref, w_ref, k_ref, v_ref, kk_ref, b_ref, s0_ref, o_ref, so_ref, *, steps):
    def per_v(vi, carry):
        s = s0_ref[0, vi]
        for t in range(steps):
            vrow = v_ref[t, 0, pl.ds(vi, 1), :]
            sa = jnp.sum(s * kk_ref[t, 0], axis=0, keepdims=True)
            s = s * w_ref[t, 0] - b_ref[t, 0] * sa + k_ref[t, 0] * vrow
            o_ref[t, 0, pl.ds(vi, 1), :] = jnp.sum(s * r_ref[t, 0], axis=0, keepdims=True)
        so_ref[0, vi] = s
        return carry

    lax.fori_loop(0, HEAD, per_v, 0)


def _wkv_short(r, w, k, v, kk, b, s0, nseq, seq_len):
    vec = pl.BlockSpec((seq_len, 1, HEAD, nseq), lambda h: (0, h, 0, 0))
    st = pl.BlockSpec((1, HEAD, HEAD, nseq), lambda h: (h, 0, 0, 0))
    return pl.pallas_call(
        functools.partial(_wkv_short_body, steps=seq_len),
        out_shape=(jax.ShapeDtypeStruct((seq_len, N_HEADS, HEAD, nseq), F32),
                   jax.ShapeDtypeStruct((N_HEADS, HEAD, HEAD, nseq), F32)),
        grid=(N_HEADS,),
        in_specs=[vec] * 6 + [st],
        out_specs=(vec, st),
        compiler_params=_cparams(("parallel",), 32),
        name="wkv_short",
    )(r, w, k, v, kk, b, s0)


def _rwkv_post_body(o_ref, bonus_ref, g_ref, lnw_ref, lnb_ref, y_ref):
    o = o_ref[...]
    mean = _head_allsum(o) * (1.0 / HEAD)
    cen = o - mean
    var = _head_allsum(cen * cen) * (1.0 / HEAD)
    on = cen * lax.rsqrt(var + GN_EPS) * lnw_ref[...] + lnb_ref[...]
    y_ref[...] = (on + bonus_ref[...]) * g_ref[...]


def _rwkv_post(o, bonus, g, ln_w, ln_b):
    m = o.shape[0]
    tile = pl.BlockSpec((TM_POST, D), lambda i: (i, 0))
    const = pl.BlockSpec((1, D), lambda i: (0, 0))
    return pl.pallas_call(
        _rwkv_post_body,
        out_shape=jax.ShapeDtypeStruct((m, D), F32),
        grid=(m // TM_POST,),
        in_specs=[tile, tile, tile, const, const],
        out_specs=tile,
        compiler_params=_cparams(("parallel",), 32),
        name="rwkv_post",
    )(o, bonus, g, ln_w, ln_b)


def _lru_body(gate_ref, x_ref, ci_ref, h0_ref, cw_ref, cb_ref, wga_ref, bga_ref, wgx_ref, bgx_ref, lam_ref,
              y_ref, ho_ref, xbuf, a_s, b_s, h_s, hc_s, *, q, nc):
    c = pl.program_id(1)
    _carried_window(xbuf, x_ref, ci_ref, c == 0, q)

    @pl.when(c == 0)
    def _():
        hc_s[...] = h0_ref[0]

    xc = _causal_conv(xbuf, cw_ref, cb_ref, q)
    _advance_window(xbuf, q)
    ra, rx = [], []
    for blk in range(LRU_BLOCKS):
        xh = xc[:, blk * LRU_BLK:(blk + 1) * LRU_BLK].astype(BF16)
        ra.append(jnp.dot(xh, wga_ref[blk], preferred_element_type=F32))
        rx.append(jnp.dot(xh, wgx_ref[blk], preferred_element_type=F32))
    rg = jax.nn.sigmoid(jnp.concatenate(ra, axis=1) + bga_ref[...])
    ig = jax.nn.sigmoid(jnp.concatenate(rx, axis=1) + bgx_ref[...])
    log_a = -LRU_C * rg * _softplus(-lam_ref[...])
    a_s[...] = jnp.exp(log_a)
    b_s[...] = jnp.sqrt(1.0 - jnp.exp(2.0 * log_a)) * (ig * xc)

    def step(t, h):
        h = a_s[pl.ds(t, 1), :] * h + b_s[pl.ds(t, 1), :]
        h_s[pl.ds(t, 1), :] = h
        return h

    h = lax.fori_loop(0, q, step, hc_s[...])
    hc_s[...] = h
    y_ref[...] = h_s[...] * jax.nn.gelu(gate_ref[...])

    @pl.when(c == nc - 1)
    def _():
        ho_ref[0] = h


def _lru(p, conv_init, h0, prm, nseq, seq_len):
    q = min(CHUNK, seq_len)
    nc = seq_len // q
    per_seq = lambda shape: pl.BlockSpec((1,) + shape, lambda s, c: (s,) + (0,) * len(shape))
    const = lambda shape: pl.BlockSpec(shape, lambda s, c: (0,) * len(shape))
    return pl.pallas_call(
        functools.partial(_lru_body, q=q, nc=nc),
        out_shape=(jax.ShapeDtypeStruct((nseq * seq_len, D), F32),
                   jax.ShapeDtypeStruct((nseq, 1, D), F32)),
        grid=(nseq, nc),
        in_specs=[
            pl.BlockSpec((q, D), lambda s, c: (s * nc + c, 0)),
            pl.BlockSpec((q, D), lambda s, c: (s * nc + c, 1)),
            per_seq((SUBLANES, D)), per_seq((1, D)),
            const((CONV_W, D)), const((1, D)),
            const((LRU_BLOCKS, LRU_BLK, LRU_BLK)), const((1, D)),
            const((LRU_BLOCKS, LRU_BLK, LRU_BLK)), const((1, D)), const((1, D)),
        ],
        out_specs=(pl.BlockSpec((q, D), lambda s, c: (s * nc + c, 0)), per_seq((1, D))),
        scratch_shapes=[pltpu.VMEM((CHUNK + SUBLANES, D), F32), pltpu.VMEM((q, D), F32), pltpu.VMEM((q, D), F32),
                        pltpu.VMEM((q, D), F32), pltpu.VMEM((1, D), F32)],
        compiler_params=_cparams(("parallel", "arbitrary"), 32),
        name="lru",
    )(p, p, conv_init, h0, *prm)


def _pad_front_rows(buf):
    return jnp.pad(buf, ((0, 0), (SUBLANES - buf.shape[1], 0), (0, 0)))


def _row2(v):
    return v.reshape(1, -1)


def _prep_ffn(w_in, w_out):
    pad = D_FF_PAD - D_FF
    wg = jnp.pad(w_in[:, :D_FF], ((0, 0), (0, pad))).astype(BF16)
    wu = jnp.pad(w_in[:, D_FF:], ((0, 0), (0, pad))).astype(BF16)
    wo = jnp.pad(w_out, ((0, pad), (0, 0))).astype(BF16)
    return wg, wu, wo


def kernel(x_prompt, x_sample, state_ssm_a, state_conv_a, state_wkv_b, state_shift_b, state_lru_c, state_conv_c, norm_gain, w_ffn_in, w_ffn_out, w_in_ab, conv_w_a, conv_b_a, dt_bias_a, a_log_a, d_skip_a, gnorm_a, mu_b, w0_b, w2_b, a0_b, a2_b, g2_b, k_k_b, k_a_b, r_k_b, ln_w_b, ln_b_b, w_out_ab, w_in_c, conv_w_c, conv_b_c, w_gate_a_c, b_gate_a_c, w_gate_x_c, b_gate_x_c, lambda_c, w_out_c, final_norm_gain):
    ffn_w = [[_prep_ffn(w_ffn_in[i, s], w_ffn_out[i, s]) for s in range(2)] for i in range(2)]

    in_a = D + (D + 2 * N_GROUPS * N_STATE) + N_HEADS
    wab = w_in_ab[0]
    bc_w = 2 * N_GROUPS * N_STATE
    w_proj0 = jnp.concatenate([
        wab[:, in_a:in_a + 3 * D],
        wab[:, 0:D],
        wab[:, D:2 * D],
        wab[:, 2 * D:2 * D + bc_w],
        wab[:, in_a + 3 * D:in_a + 3 * D + LORA_W],
        wab[:, in_a - N_HEADS:in_a],
        jnp.zeros((D, PROJ_W - COL_DT - N_HEADS), F32),
    ], axis=1).astype(BF16)
    w_proj1 = w_in_c[0].astype(BF16)
    w_out0 = w_out_ab[0].astype(BF16)
    w_out1 = w_out_c[0].astype(BF16)

    pad_lanes = lambda v: jnp.pad(v.reshape(1, -1), ((0, 0), (0, LANES - v.shape[-1])))
    rep_head = lambda v: jnp.repeat(v, HEAD).reshape(1, D)
    ssd_prm = (conv_w_a[0][:, :D], conv_w_a[0][:, D:], _row2(conv_b_a[0][:D]), _row2(conv_b_a[0][D:]),
               pad_lanes(dt_bias_a[0]), pad_lanes(a_log_a[0]), rep_head(d_skip_a[0]), _row2(gnorm_a[0]))
    lora_rows = lambda w, lo: jnp.pad(w, ((lo, LORA_W - lo - w.shape[0]), (0, 0))).astype(BF16)
    mu = mu_b[0]
    rwkv_prm = (_row2(mu[:3 * D]), _row2(mu[3 * D:]),
                _row2(w0_b[0]), lora_rows(w2_b[0], 0), _row2(a0_b[0]), lora_rows(a2_b[0], 64),
                lora_rows(g2_b[0], 128), _row2(k_k_b[0]), _row2(k_a_b[0]), _row2(r_k_b[0]))
    lru_prm = (conv_w_c[0], _row2(conv_b_c[0]), w_gate_a_c[0].astype(BF16), _row2(b_gate_a_c[0]),
               w_gate_x_c[0].astype(BF16), _row2(b_gate_x_c[0]), _row2(lambda_c[0]))

    def trunk(x3, ssm0, conva0, wkv0, shift0, lru0, convc0):
        nseq, seq_len, _ = x3.shape
        x = x3.reshape(nseq * seq_len, D)
        last = lambda arr, n: arr.reshape(nseq, seq_len, -1)[:, seq_len - n:, :]

        x = _ffn(x, _row2(norm_gain[0, 0]), *ffn_w[0][0])
        p = _proj(x, _row2(norm_gain[0, 1]), w_proj0)
        conv_x = _pad_front_rows(conva0[:, :, :D])
        conv_bc = _pad_front_rows(conva0[:, :, D:])
        ya, ssm_n = _ssd(p, conv_x, conv_bc, ssm0.reshape(nseq, D, N_STATE), ssd_prm, nseq, seq_len)
        conva_n = jnp.concatenate([last(p[:, COL_X:COL_X + D], 3), last(p[:, COL_BC:COL_BC + bc_w], 3)], axis=-1)
        shift_n = jnp.concatenate([last(p[:, :3 * D], 1), last(p[:, COL_LORA:COL_LORA + LORA_W], 1)], axis=-1)

        r, w, k, v, kk, b, bonus, g = _rwkv_pre(p, _pad_front_rows(shift0[:, :, :3 * D]),
                                                _pad_front_rows(shift0[:, :, 3 * D:]), rwkv_prm, nseq, seq_len,
                                                log_decay=wkv0 is None)
        if wkv0 is None:
            o, wkv_n = _wkv_long(r, w, k, v, kk, b, nseq, seq_len)
            wkv_n = wkv_n.reshape(nseq, N_HEADS, HEAD, HEAD)
        else:
            to_lanes = lambda a: a.reshape(nseq, seq_len, N_HEADS, HEAD).transpose(1, 2, 3, 0)
            o, wkv_n = _wkv_short(*[to_lanes(a) for a in (r, w, k, v, kk, b)], wkv0.transpose(1, 2, 3, 0),
                                  nseq, seq_len)
            o = o.transpose(3, 0, 1, 2).reshape(nseq * seq_len, D)
            wkv_n = wkv_n.transpose(3, 0, 1, 2)
        yb = _rwkv_post(o, bonus, g, _row2(ln_w_b[0]), _row2(ln_b_b[0]))
        x = _out_proj(x, (ya, yb), (w_out0[:D], w_out0[D:]))
        x = _ffn(x, _row2(norm_gain[0, 2]), *ffn_w[0][1])

        x = _ffn(x, _row2(norm_gain[1, 0]), *ffn_w[1][0])
        pc = _proj(x, _row2(norm_gain[1, 1]), w_proj1)
        yc, lru_n = _lru(pc, _pad_front_rows(convc0), lru0.reshape(nseq, 1, D), lru_prm, nseq, seq_len)
        convc_n = last(pc[:, D:], 3)
        x = _out_proj(x, (yc,), (w_out1,))
        y = _ffn(x, _row2(norm_gain[1, 2]), *ffn_w[1][1], final_gain=_row2(final_norm_gain))

        return (y.reshape(nseq, seq_len, D), ssm_n.reshape(1, nseq, N_HEADS, HEAD, N_STATE), conva_n[None],
                wkv_n[None], shift_n[None], lru_n.reshape(1, nseq, D), convc_n[None])

    bp = x_prompt.shape[0]
    zeros = lambda s: jnp.zeros((bp,) + s.shape[2:], F32)
    outs_p = trunk(x_prompt, zeros(state_ssm_a), zeros(state_conv_a), None, zeros(state_shift_b),
                   zeros(state_lru_c), zeros(state_conv_c))
    outs_s = trunk(x_sample, state_ssm_a[0], state_conv_a[0], state_wkv_b[0], state_shift_b[0],
                   state_lru_c[0], state_conv_c[0])
    return (outs_p[0], outs_s[0]) + outs_p[1:] + outs_s[1:]
```

```python
import functools

import jax
import jax.numpy as jnp
from jax import lax
from jax.experimental import pallas as pl
from jax.experimental.pallas import tpu as pltpu

F32 = jnp.float32
BF16 = jnp.bfloat16

D = 2048
D_FF = 5504
D_FF_PAD = 5632
HEAD = 64
N_HEADS = 32
N_GROUPS = 4
GROUP_W = D // N_GROUPS
N_STATE = 128
CONV_W = 4
LORA_W = 256
LRU_BLOCKS = 8
LRU_BLK = D // LRU_BLOCKS
LRU_C = 8.0
EPS = 1e-6
GN_EPS = 64e-5
SUBLANES = 8
LANES = 128
CHUNK = 128
WKV_CHUNK = 64
WKV_LANES = 256
WKV_GROUPS = 2

COL_RKV = 0
COL_Z = 3 * D
COL_X = 4 * D
COL_BC = 5 * D
COL_LORA = 5 * D + 2 * N_GROUPS * N_STATE
COL_DT = COL_LORA + LORA_W
PROJ_W = COL_DT + 256

TM_FFN = 512
TF_FFN = 512
TM_PROJ = 1024
TN_PROJ = 512
TM_OUT = 512
TN_OUT = 1024
TM_POST = 256


def _cparams(sem, vmem_mib):
    return pltpu.CompilerParams(dimension_semantics=sem, vmem_limit_bytes=vmem_mib * 1024 * 1024)


def _softplus(x):
    return jnp.maximum(x, 0.0) + jnp.log(1.0 + jnp.exp(-jnp.abs(x)))


def _silu(x):
    return x * jax.nn.sigmoid(x)


def _rms(x, gain):
    ms = jnp.mean(x * x, axis=-1, keepdims=True)
    return x * lax.rsqrt(ms + EPS) * gain


def _ffn_body(x_ref, g_ref, wg_ref, wu_ref, wo_ref, fg_ref, o_ref, xn_ref, acc_ref, *, nf, final):
    f = pl.program_id(1)

    @pl.when(f == 0)
    def _():
        xn_ref[...] = _rms(x_ref[...], g_ref[...]).astype(BF16)
        acc_ref[...] = jnp.zeros_like(acc_ref)

    xn = xn_ref[...]
    gate = jnp.dot(xn, wg_ref[...], preferred_element_type=F32)
    up = jnp.dot(xn, wu_ref[...], preferred_element_type=F32)
    h = (_silu(gate) * up).astype(BF16)
    acc_ref[...] += jnp.dot(h, wo_ref[...], preferred_element_type=F32)

    @pl.when(f == nf - 1)
    def _():
        y = x_ref[...] + 0.5 * acc_ref[...]
        if final:
            y = _rms(y, fg_ref[...])
        o_ref[...] = y


def _ffn(x, gain, w_gu, w_dn, which, final_gain=None):
    m = x.shape[0]
    li, si = which
    nf = D_FF_PAD // TF_FFN
    final = final_gain is not None
    fg = final_gain if final else gain
    return pl.pallas_call(
        functools.partial(_ffn_body, nf=nf, final=final),
        out_shape=jax.ShapeDtypeStruct((m, D), F32),
        grid=(m // TM_FFN, nf),
        in_specs=[
            pl.BlockSpec((TM_FFN, D), lambda i, f: (i, 0)),
            pl.BlockSpec((1, D), lambda i, f: (0, 0)),
            pl.BlockSpec((None, None, D, TF_FFN), lambda i, f: (li, si, 0, f)),
            pl.BlockSpec((None, None, D, TF_FFN), lambda i, f: (li, si, 0, f + nf)),
            pl.BlockSpec((None, None, TF_FFN, D), lambda i, f: (li, si, f, 0)),
            pl.BlockSpec((1, D), lambda i, f: (0, 0)),
        ],
        out_specs=pl.BlockSpec((TM_FFN, D), lambda i, f: (i, 0)),
        scratch_shapes=[pltpu.VMEM((TM_FFN, D), BF16), pltpu.VMEM((TM_FFN, D), F32)],
        compiler_params=_cparams(("parallel", "arbitrary"), 48),
        name="ffn",
    )(x, gain, w_gu, w_gu, w_dn, fg)


def _proj_body(x_ref, g_ref, w_ref, o_ref, xn_ref):
    @pl.when(pl.program_id(1) == 0)
    def _():
        xn_ref[...] = _rms(x_ref[...], g_ref[...]).astype(BF16)

    o_ref[...] = jnp.dot(xn_ref[...], w_ref[...], preferred_element_type=F32)


def _proj(x, gain, w):
    m = x.shape[0]
    n = w.shape[1]
    return pl.pallas_call(
        _proj_body,
        out_shape=jax.ShapeDtypeStruct((m, n), F32),
        grid=(m // TM_PROJ, n // TN_PROJ),
        in_specs=[
            pl.BlockSpec((TM_PROJ, D), lambda i, j: (i, 0)),
            pl.BlockSpec((1, D), lambda i, j: (0, 0)),
            pl.BlockSpec((D, TN_PROJ), lambda i, j: (0, j)),
        ],
        out_specs=pl.BlockSpec((TM_PROJ, TN_PROJ), lambda i, j: (i, j)),
        scratch_shapes=[pltpu.VMEM((TM_PROJ, D), BF16)],
        compiler_params=_cparams(("parallel", "arbitrary"), 40),
        name="proj",
    )(x, gain, w)


def _out2_body(res_ref, ya_ref, yb_ref, wa_ref, wb_ref, o_ref):
    acc = jnp.dot(ya_ref[...].astype(BF16), wa_ref[...], preferred_element_type=F32)
    acc = acc + jnp.dot(yb_ref[...].astype(BF16), wb_ref[...], preferred_element_type=F32)
    o_ref[...] = res_ref[...] + acc


def _out1_body(res_ref, y_ref, w_ref, o_ref):
    o_ref[...] = res_ref[...] + jnp.dot(y_ref[...].astype(BF16), w_ref[...], preferred_element_type=F32)


def _out_proj(res, ys, ws):
    m = res.shape[0]
    body = _out2_body if len(ys) == 2 else _out1_body
    y_spec = pl.BlockSpec((TM_OUT, D), lambda j, i: (i, 0))
    w_spec = pl.BlockSpec((D, TN_OUT), lambda j, i: (0, j))
    r_spec = pl.BlockSpec((TM_OUT, TN_OUT), lambda j, i: (i, j))
    return pl.pallas_call(
        body,
        out_shape=jax.ShapeDtypeStruct((m, D), F32),
        grid=(D // TN_OUT, m // TM_OUT),
        in_specs=[r_spec] + [y_spec] * len(ys) + [w_spec] * len(ws),
        out_specs=r_spec,
        compiler_params=_cparams(("arbitrary", "arbitrary"), 48),
        name="out_proj",
    )(res, *ys, *ws)


def _carried_window(buf, cur_ref, init_ref, first, q):
    @pl.when(first)
    def _():
        buf[0:SUBLANES, :] = init_ref[0]

    buf[SUBLANES:SUBLANES + q, :] = cur_ref[...]


def _advance_window(buf, q):
    tail = buf[q:q + SUBLANES, :]
    buf[0:SUBLANES, :] = tail


def _causal_conv(buf, cw_ref, cb_ref, q):
    acc = cb_ref[...] + cw_ref[0:1, :] * buf[pl.ds(SUBLANES - CONV_W + 1, q), :]
    for k in range(1, CONV_W):
        acc = acc + cw_ref[k:k + 1, :] * buf[pl.ds(SUBLANES - CONV_W + 1 + k, q), :]
    return acc


def _cumsum_rows(x):
    n = x.shape[0]
    row = lax.broadcasted_iota(jnp.int32, x.shape, 0)
    s = 1
    while s < n:
        x = x + jnp.where(row >= s, pltpu.roll(x, s, axis=0), 0.0)
        s *= 2
    return x


def _expand_heads(a, rows):
    lane = lax.broadcasted_iota(jnp.int32, (rows, LANES), 1)
    low = lane < HEAD
    pieces = []
    for j in range(N_HEADS // 2):
        e0 = jnp.broadcast_to(a[:, 2 * j:2 * j + 1], (rows, LANES))
        e1 = jnp.broadcast_to(a[:, 2 * j + 1:2 * j + 2], (rows, LANES))
        pieces.append(jnp.where(low, e0, e1))
    return jnp.concatenate(pieces, axis=1)


def _head_allsum(x):
    width = x.shape[-1]
    blk = 4 * HEAD
    r = lax.broadcasted_iota(jnp.int32, (blk, blk), 0) // HEAD
    c = lax.broadcasted_iota(jnp.int32, (blk, blk), 1) // HEAD
    ones_bd = jnp.where(r == c, 1.0, 0.0).astype(BF16)
    hi = x.astype(BF16)
    rem = x - hi.astype(F32)
    mid = rem.astype(BF16)
    lo = (rem - mid.astype(F32)).astype(BF16)
    dot = lambda p: jnp.dot(p, ones_bd, preferred_element_type=F32)
    cols = []
    for j in range(width // blk):
        sl = slice(j * blk, (j + 1) * blk)
        cols.append(dot(hi[:, sl]) + (dot(mid[:, sl]) + dot(lo[:, sl])))
    return jnp.concatenate(cols, axis=1)


def _ssd_body(z_ref, x_ref, bc_ref, dt_ref, cix_ref, cibc_ref, h0_ref,
              cwx_ref, cwbc_ref, cbx_ref, cbbc_ref, dtb_ref, alog_ref, dsk_ref, gn_ref,
              y_ref, ho_ref, xbuf, bcbuf, zbuf, dtbuf, h_s, *, q, nc):
    c = pl.program_id(1)
    padded = q < CHUNK
    if padded:
        xbuf[...] = jnp.zeros_like(xbuf)
        bcbuf[...] = jnp.zeros_like(bcbuf)
        zbuf[...] = jnp.zeros_like(zbuf)
        dtbuf[...] = jnp.zeros_like(dtbuf)
    _carried_window(xbuf, x_ref, cix_ref, c == 0, q)
    _carried_window(bcbuf, bc_ref, cibc_ref, c == 0, q)

    @pl.when(c == 0)
    def _():
        h_s[...] = h0_ref[0]

    row = lax.broadcasted_iota(jnp.int32, (CHUNK, LANES), 0)
    xs = _silu(_causal_conv(xbuf, cwx_ref, cbx_ref, CHUNK))
    bcv = _silu(_causal_conv(bcbuf, cwbc_ref, cbbc_ref, CHUNK))
    if padded:
        zbuf[0:q, :] = z_ref[...]
        dtbuf[0:q, :] = dt_ref[...]
        z = zbuf[...]
        dt_raw = dtbuf[...]
    else:
        _advance_window(xbuf, q)
        _advance_window(bcbuf, q)
        z = z_ref[...]
        dt_raw = dt_ref[...]

    dt = _softplus(dt_raw + dtb_ref[...])
    if padded:
        dt = jnp.where(row < q, dt, 0.0)
    a_head = -jnp.exp(alog_ref[...])
    cum = _cumsum_rows(dt * a_head)
    cum_last = cum[CHUNK - 1:CHUNK, :]
    cum_t = cum.T
    dtx = _expand_heads(dt, CHUNK)
    ecx = _expand_heads(jnp.exp(cum), CHUNK)
    tlx = _expand_heads(jnp.exp(cum_last - cum), CHUNK)
    xdt = xs * dtx
    xdtw = (xdt * tlx).astype(BF16)

    ti = lax.broadcasted_iota(jnp.int32, (CHUNK, CHUNK), 0)
    tj = lax.broadcasted_iota(jnp.int32, (CHUNK, CHUNK), 1)
    causal = ti >= tj
    low = tj < HEAD
    nt = (((1,), (1,)), ((), ()))
    tn = (((0,), (0,)), ((), ()))

    for g in range(N_GROUPS):
        gsl = slice(g * GROUP_W, (g + 1) * GROUP_W)
        bg = bcv[:, g * N_STATE:(g + 1) * N_STATE].astype(BF16)
        cg = bcv[:, (N_GROUPS + g) * N_STATE:(N_GROUPS + g + 1) * N_STATE].astype(BF16)
        cb = lax.dot_general(cg, bg, nt, preferred_element_type=F32)
        hg = h_s[gsl, :]
        y_off = lax.dot_general(cg, hg.astype(BF16), nt, preferred_element_type=F32) * ecx[:, gsl]
        st = lax.dot_general(xdtw[:, gsl], bg, tn, preferred_element_type=F32)
        y_pairs = []
        for j in range(GROUP_W // LANES):
            h0 = g * (GROUP_W // HEAD) + 2 * j
            ms = []
            for h in (h0, h0 + 1):
                seg = cum[:, h:h + 1] - cum_t[h:h + 1, :]
                decay = jnp.exp(jnp.where(causal, seg, -jnp.inf))
                ms.append((cb * decay).astype(BF16))
            psl = slice(h0 * HEAD, (h0 + 2) * HEAD)
            slab = xdt[:, psl]
            rhs = jnp.concatenate([jnp.where(low, slab, 0.0), jnp.where(low, 0.0, slab)], axis=0).astype(BF16)
            y_pairs.append(jnp.dot(jnp.concatenate(ms, axis=1), rhs, preferred_element_type=F32))
            for h in (h0, h0 + 1):
                hsl = slice(h * HEAD, (h + 1) * HEAD)
                dec = jnp.exp(jnp.broadcast_to(cum_last[:, h:h + 1], (HEAD, N_STATE)))
                h_s[hsl, :] = h_s[hsl, :] * dec + st[(h % 8) * HEAD:(h % 8 + 1) * HEAD, :]
        y = jnp.concatenate(y_pairs, axis=1) + y_off + xs[:, gsl] * dsk_ref[:, gsl]
        y = y * _silu(z[:, gsl])
        y = y * lax.rsqrt(jnp.mean(y * y, axis=-1, keepdims=True) + EPS) * gn_ref[:, gsl]
        if padded:
            y_ref[:, gsl] = y[0:q, :]
        else:
            y_ref[:, gsl] = y

    @pl.when(c == nc - 1)
    def _():
        ho_ref[0] = h_s[...]


def _ssd(p, conv_x, conv_bc, h0, prm, nseq, seq_len):
    q = min(CHUNK, seq_len)
    nc = seq_len // q
    rows = lambda w, col: pl.BlockSpec((q, w), lambda s, c: (s * nc + c, col))
    per_seq = lambda shape: pl.BlockSpec((1,) + shape, lambda s, c: (s,) + (0,) * len(shape))
    const = lambda shape: pl.BlockSpec(shape, lambda s, c: (0,) * len(shape))
    y, h_last = pl.pallas_call(
        functools.partial(_ssd_body, q=q, nc=nc),
        out_shape=(jax.ShapeDtypeStruct((nseq * seq_len, D), F32),
                   jax.ShapeDtypeStruct((nseq, D, N_STATE), F32)),
        grid=(nseq, nc),
        in_specs=[
            rows(D, COL_Z // D), rows(D, COL_X // D), rows(2 * N_GROUPS * N_STATE, COL_BC // 1024),
            rows(LANES, COL_DT // LANES),
            per_seq((SUBLANES, D)), per_seq((SUBLANES, 1024)), per_seq((D, N_STATE)),
            const((CONV_W, D)), const((CONV_W, 1024)), const((1, D)), const((1, 1024)),
            const((1, LANES)), const((1, LANES)), const((1, D)), const((1, D)),
        ],
        out_specs=(pl.BlockSpec((q, D), lambda s, c: (s * nc + c, 0)), per_seq((D, N_STATE))),
        scratch_shapes=[
            pltpu.VMEM((CHUNK + SUBLANES, D), F32), pltpu.VMEM((CHUNK + SUBLANES, 1024), F32),
            pltpu.VMEM((CHUNK, D), F32), pltpu.VMEM((CHUNK, LANES), F32),
            pltpu.VMEM((D, N_STATE), F32),
        ],
        compiler_params=_cparams(("parallel", "arbitrary"), 40),
        name="ssd",
    )(p, p, p, p, conv_x, conv_bc, h0, *prm)
    return y, h_last


def _rwkv_pre_body(rkv_ref, lora_ref, si_rkv_ref, si_lora_ref, mu_rkv_ref, mu_lora_ref,
                   w0_ref, w2_ref, a0_ref, a2_ref, g2_ref, kk_ref, ka_ref, rk_ref,
                   r_out, w_out, k_out, v_out, kk_out, b_out, bonus_out, g_out,
                   rkvbuf, lorabuf, *, q, log_decay):
    c = pl.program_id(1)
    _carried_window(rkvbuf, rkv_ref, si_rkv_ref, c == 0, q)
    _carried_window(lorabuf, lora_ref, si_lora_ref, c == 0, q)

    def shifted(buf, cur_ref, mu_ref):
        cur = cur_ref[...]
        prev = buf[pl.ds(SUBLANES - 1, q), :]
        return cur + mu_ref[...] * (prev - cur)

    ps = shifted(rkvbuf, rkv_ref, mu_rkv_ref)
    lo_in = shifted(lorabuf, lora_ref, mu_lora_ref)
    _advance_window(rkvbuf, q)
    _advance_window(lorabuf, q)
    r = ps[:, 0:D]
    k = ps[:, D:2 * D]
    v = ps[:, 2 * D:3 * D]

    lw = jnp.dot(jnp.tanh(lo_in).astype(BF16), w2_ref[...], preferred_element_type=F32)
    la = jnp.dot(lo_in.astype(BF16), a2_ref[...], preferred_element_type=F32)
    g = jnp.dot(jax.nn.sigmoid(lo_in).astype(BF16), g2_ref[...], preferred_element_type=F32)
    wlog = -_softplus(-(w0_ref[...] + lw)) - 0.5
    log_w = -jnp.exp(wlog)
    a = jax.nn.sigmoid(a0_ref[...] + la)
    kkf = k * kk_ref[...]
    norm = jnp.maximum(jnp.sqrt(_head_allsum(kkf * kkf)), 1e-12)
    kk = kkf / norm
    k2 = k * (1.0 + (a - 1.0) * ka_ref[...])
    bonus = _head_allsum(r * k2 * rk_ref[...]) * v
    r_out[...] = r
    w_out[...] = log_w if log_decay else jnp.exp(log_w)
    k_out[...] = k2
    v_out[...] = v
    kk_out[...] = kk
    b_out[...] = kk * a
    bonus_out[...] = bonus
    g_out[...] = g


def _rwkv_pre(p, shift_rkv, shift_lora, prm, nseq, seq_len, log_decay):
    q = min(CHUNK, seq_len)
    nc = seq_len // q
    per_seq = lambda shape: pl.BlockSpec((1,) + shape, lambda s, c: (s,) + (0,) * len(shape))
    const = lambda shape: pl.BlockSpec(shape, lambda s, c: (0,) * len(shape))
    tile = pl.BlockSpec((q, D), lambda s, c: (s * nc + c, 0))
    sds = jax.ShapeDtypeStruct((nseq * seq_len, D), F32)
    return pl.pallas_call(
        functools.partial(_rwkv_pre_body, q=q, log_decay=log_decay),
        out_shape=(sds,) * 8,
        grid=(nseq, nc),
        in_specs=[
            pl.BlockSpec((q, 3 * D), lambda s, c: (s * nc + c, 0)),
            pl.BlockSpec((q, LORA_W), lambda s, c: (s * nc + c, COL_LORA // LORA_W)),
            per_seq((SUBLANES, 3 * D)), per_seq((SUBLANES, LORA_W)),
            const((1, 3 * D)), const((1, LORA_W)),
            const((1, D)), const((LORA_W, D)), const((1, D)), const((LORA_W, D)), const((LORA_W, D)),
            const((1, D)), const((1, D)), const((1, D)),
        ],
        out_specs=(tile,) * 8,
        scratch_shapes=[pltpu.VMEM((CHUNK + SUBLANES, 3 * D), F32), pltpu.VMEM((CHUNK + SUBLANES, LORA_W), F32)],
        compiler_params=_cparams(("parallel", "arbitrary"), 48),
        name="rwkv_pre",
    )(p, p, shift_rkv, shift_lora, *prm)


def _split_bf16(x):
    hi = x.astype(BF16)
    return hi, (x - hi.astype(F32)).astype(BF16)


_NN = (((1,), (0,)), ((), ()))
_NT = (((1,), (1,)), ((), ()))
_TN = (((0,), (0,)), ((), ()))


def _solve_unit_lower(n, rhs):
    rows, width = rhs.shape
    nblk, ncol = rows // SUBLANES, width // LANES
    nb = [[n[SUBLANES * i:SUBLANES * (i + 1), LANES * j:LANES * (j + 1)] for j in range(ncol)] for i in range(nblk)]
    xb = [[rhs[SUBLANES * i:SUBLANES * (i + 1), LANES * j:LANES * (j + 1)] for j in range(ncol)] for i in range(nblk)]
    low = lax.broadcasted_iota(jnp.int32, (SUBLANES, LANES), 1) < HEAD
    for s in range(rows - 1):
        i0, r0 = divmod(s, SUBLANES)
        idx = jnp.where(low, s, HEAD + s)
        for j in range(ncol):
            row = xb[i0][j][r0:r0 + 1, :]
            for i in range(i0 if r0 < SUBLANES - 1 else i0 + 1, nblk):
                xb[i][j] = xb[i][j] - jnp.take_along_axis(nb[i][j], idx, axis=1) * row
    return jnp.concatenate([jnp.concatenate(xr, axis=1) for xr in xb], axis=0)


def _wkv_tile(r, lw, k, v, kk, b, st):
    rows, width = r.shape
    nh = WKV_LANES // HEAD
    groups = [slice(g * WKV_LANES, (g + 1) * WKV_LANES) for g in range(width // WKV_LANES)]
    cl = _cumsum_rows(lw)
    cl_last = cl[rows - 1:rows, :]
    p_inv = jnp.exp(-cl)
    p_end = jnp.exp(cl_last - cl)
    x2h, x2l = _split_bf16(jnp.concatenate([kk * jnp.exp(cl - lw), r * jnp.exp(cl)], axis=0))
    k_hat = k * p_inv
    b_hat = b * p_inv
    k_end = k * p_end
    b_end = -(b * p_end)

    bd_r = lax.broadcasted_iota(jnp.int32, (nh * rows, WKV_LANES), 0) // rows
    bd_c = lax.broadcasted_iota(jnp.int32, (nh * rows, WKV_LANES), 1) // HEAD
    bd_mask = jnp.where(bd_r == bd_c, 1.0, 0.0).astype(BF16)

    def per_head_rows(y):
        return [jnp.concatenate([part] * nh, axis=0) * bd_mask for part in _split_bf16(y)]

    def dot3(ah, al, bh, bl, dims):
        dg = lambda x, y: lax.dot_general(x, y, dims, preferred_element_type=F32)
        return dg(ah, bh) + (dg(ah, bl) + dg(al, bh))

    t_i = lax.broadcasted_iota(jnp.int32, (rows, WKV_LANES), 0)
    lane = lax.broadcasted_iota(jnp.int32, (rows, WKV_LANES), 1)
    s_i = lane & (HEAD - 1)
    strict = t_i > s_i
    incl = t_i >= s_i
    head_of_lane = lane // HEAD

    a_kb, a_rb, base, o_part = [], [], [], []
    for gs in groups:
        ak = dot3(x2h[:, gs], x2l[:, gs], *per_head_rows(k_hat[:, gs]), _NT)
        ab = dot3(x2h[:, gs], x2l[:, gs], *per_head_rows(b_hat[:, gs]), _NT)
        a_k = jnp.concatenate([jnp.where(strict, ak[0:rows], 0.0), jnp.where(incl, ak[rows:], 0.0)], axis=0)
        xs = dot3(x2h[:, gs], x2l[:, gs], *per_head_rows(st[:, gs]), _NN)
        av = dot3(*_split_bf16(a_k), *per_head_rows(v[:, gs]), _NN)
        a_kb.append(jnp.where(strict, ab[0:rows], 0.0))
        a_rb.append(jnp.where(incl, ab[rows:], 0.0))
        base.append(xs[0:rows] + av[0:rows])
        o_part.append(xs[rows:] + av[rows:])
    sa = _solve_unit_lower(jnp.concatenate(a_kb, axis=1), jnp.concatenate(base, axis=1))

    o, st_new = [], []
    for g, gs in enumerate(groups):
        o.append(o_part[g] - dot3(*_split_bf16(a_rb[g]), *per_head_rows(sa[:, gs]), _NN))
        eye = jnp.where(t_i == s_i, jnp.exp(cl_last[:, gs]), 0.0)
        lhs = jnp.concatenate([k_end[:, gs], b_end[:, gs], eye], axis=0)
        rhs = jnp.concatenate([v[:, gs], sa[:, gs], st[:, gs]], axis=0)
        full = dot3(*_split_bf16(lhs), *_split_bf16(rhs), _TN)
        acc = jnp.where(head_of_lane == 0, full[0:HEAD, :], 0.0)
        for h in range(1, nh):
            acc = acc + jnp.where(head_of_lane == h, full[h * HEAD:(h + 1) * HEAD, :], 0.0)
        st_new.append(acc)
    return jnp.concatenate(o, axis=1), jnp.concatenate(st_new, axis=1)


def _wkv_long_body(r_ref, lw_ref, k_ref, v_ref, kk_ref, b_ref, o_ref, so_ref, st_s, *, nc):
    c = pl.program_id(2)

    @pl.when(c == 0)
    def _():
        st_s[...] = jnp.zeros_like(st_s)

    o, st_new = _wkv_tile(r_ref[...], lw_ref[...], k_ref[...], v_ref[...], kk_ref[...], b_ref[...], st_s[...])
    o_ref[...] = o
    st_s[...] = st_new

    @pl.when(c == nc - 1)
    def _():
        for j in range(WKV_GROUPS * WKV_LANES // LANES):
            js = slice(j * LANES, (j + 1) * LANES)
            sq = jnp.concatenate([st_s[:, js], jnp.zeros((LANES - HEAD, LANES), F32)], axis=0)
            so_ref[0, js, :] = sq.T[:, 0:HEAD]


def _wkv_long(r, lw, k, v, kk, b, nseq, seq_len):
    nc = seq_len // WKV_CHUNK
    width = WKV_GROUPS * WKV_LANES
    tile = pl.BlockSpec((WKV_CHUNK, width), lambda s, hg, c: (s * nc + c, hg))
    o, s_last = pl.pallas_call(
        functools.partial(_wkv_long_body, nc=nc),
        out_shape=(jax.ShapeDtypeStruct((nseq * seq_len, D), F32),
                   jax.ShapeDtypeStruct((nseq, D, HEAD), F32)),
        grid=(nseq, D // width, nc),
        in_specs=[tile] * 6,
        out_specs=(tile, pl.BlockSpec((1, width, HEAD), lambda s, hg, c: (s, hg, 0))),
        scratch_shapes=[pltpu.VMEM((HEAD, width), F32)],
        compiler_params=_cparams(("parallel", "parallel", "arbitrary"), 32),
        name="wkv_long",
    )(r, lw, k, v, kk, b)
    return o, s_last


def _wkv_short_body(r_ref, w_ref, k_ref, v_ref, kk_ref, b_ref, s0_ref, o_ref, so_ref, *, steps):
    def per_v(vi, carry):
        s = s0_ref[0, vi]
        for t in range(steps):
            vrow = v_ref[t, 0, pl.ds(vi, 1), :]
            sa = jnp.sum(s * kk_ref[t, 0], axis=0, keepdims=True)
            s = s * w_ref[t, 0] - b_ref[t, 0] * sa + k_ref[t, 0] * vrow
            o_ref[t, 0, pl.ds(vi, 1), :] = jnp.sum(s * r_ref[t, 0], axis=0, keepdims=True)
        so_ref[0, vi] = s
        return carry

    lax.fori_loop(0, HEAD, per_v, 0)


def _wkv_short(r, w, k, v, kk, b, s0, nseq, seq_len):
    vec = pl.BlockSpec((seq_len, 1, HEAD, nseq), lambda h: (0, h, 0, 0))
    st = pl.BlockSpec((1, HEAD, HEAD, nseq), lambda h: (h, 0, 0, 0))
    return pl.pallas_call(
        functools.partial(_wkv_short_body, steps=seq_len),
        out_shape=(jax.ShapeDtypeStruct((seq_len, N_HEADS, HEAD, nseq), F32),
                   jax.ShapeDtypeStruct((N_HEADS, HEAD, HEAD, nseq), F32)),
        grid=(N_HEADS,),
        in_specs=[vec] * 6 + [st],
        out_specs=(vec, st),
        compiler_params=_cparams(("parallel",), 32),
        name="wkv_short",
    )(r, w, k, v, kk, b, s0)


def _rwkv_post_body(o_ref, bonus_ref, g_ref, lnw_ref, lnb_ref, y_ref):
    o = o_ref[...]
    mean = _head_allsum(o) * (1.0 / HEAD)
    cen = o - mean
    var = _head_allsum(cen * cen) * (1.0 / HEAD)
    on = cen * lax.rsqrt(var + GN_EPS) * lnw_ref[...] + lnb_ref[...]
    y_ref[...] = (on + bonus_ref[...]) * g_ref[...]


def _rwkv_post(o, bonus, g, ln_w, ln_b):
    m = o.shape[0]
    tile = pl.BlockSpec((TM_POST, D), lambda i: (i, 0))
    const = pl.BlockSpec((1, D), lambda i: (0, 0))
    return pl.pallas_call(
        _rwkv_post_body,
        out_shape=jax.ShapeDtypeStruct((m, D), F32),
        grid=(m // TM_POST,),
        in_specs=[tile, tile, tile, const, const],
        out_specs=tile,
        compiler_params=_cparams(("parallel",), 32),
        name="rwkv_post",
    )(o, bonus, g, ln_w, ln_b)


def _lru_body(gate_ref, x_ref, ci_ref, h0_ref, cw_ref, cb_ref, wga_ref, bga_ref, wgx_ref, bgx_ref, lam_ref,
              y_ref, ho_ref, xbuf, a_s, b_s, h_s, hc_s, *, q, nc):
    c = pl.program_id(1)
    _carried_window(xbuf, x_ref, ci_ref, c == 0, q)

    @pl.when(c == 0)
    def _():
        hc_s[...] = h0_ref[0]

    xc = _causal_conv(xbuf, cw_ref, cb_ref, q)
    _advance_window(xbuf, q)
    ra, rx = [], []
    for blk in range(LRU_BLOCKS):
        xh = xc[:, blk * LRU_BLK:(blk + 1) * LRU_BLK].astype(BF16)
        ra.append(jnp.dot(xh, wga_ref[blk], preferred_element_type=F32))
        rx.append(jnp.dot(xh, wgx_ref[blk], preferred_element_type=F32))
    rg = jax.nn.sigmoid(jnp.concatenate(ra, axis=1) + bga_ref[...])
    ig = jax.nn.sigmoid(jnp.concatenate(rx, axis=1) + bgx_ref[...])
    log_a = -LRU_C * rg * _softplus(-lam_ref[...])
    a_s[...] = jnp.exp(log_a)
    b_s[...] = jnp.sqrt(1.0 - jnp.exp(2.0 * log_a)) * (ig * xc)

    def step(t, h):
        h = a_s[pl.ds(t, 1), :] * h + b_s[pl.ds(t, 1), :]
        h_s[pl.ds(t, 1), :] = h
        return h

    h = lax.fori_loop(0, q, step, hc_s[...])
    hc_s[...] = h
    y_ref[...] = h_s[...] * jax.nn.gelu(gate_ref[...])

    @pl.when(c == nc - 1)
    def _():
        ho_ref[0] = h


def _lru(p, conv_init, h0, prm, nseq, seq_len):
    q = min(CHUNK, seq_len)
    nc = seq_len // q
    per_seq = lambda shape: pl.BlockSpec((1,) + shape, lambda s, c: (s,) + (0,) * len(shape))
    const = lambda shape: pl.BlockSpec(shape, lambda s, c: (0,) * len(shape))
    return pl.pallas_call(
        functools.partial(_lru_body, q=q, nc=nc),
        out_shape=(jax.ShapeDtypeStruct((nseq * seq_len, D), F32),
                   jax.ShapeDtypeStruct((nseq, 1, D), F32)),
        grid=(nseq, nc),
        in_specs=[
            pl.BlockSpec((q, D), lambda s, c: (s * nc + c, 0)),
            pl.BlockSpec((q, D), lambda s, c: (s * nc + c, 1)),
            per_seq((SUBLANES, D)), per_seq((1, D)),
            const((CONV_W, D)), const((1, D)),
            const((LRU_BLOCKS, LRU_BLK, LRU_BLK)), const((1, D)),
            const((LRU_BLOCKS, LRU_BLK, LRU_BLK)), const((1, D)), const((1, D)),
        ],
        out_specs=(pl.BlockSpec((q, D), lambda s, c: (s * nc + c, 0)), per_seq((1, D))),
        scratch_shapes=[pltpu.VMEM((CHUNK + SUBLANES, D), F32), pltpu.VMEM((q, D), F32), pltpu.VMEM((q, D), F32),
                        pltpu.VMEM((q, D), F32), pltpu.VMEM((1, D), F32)],
        compiler_params=_cparams(("parallel", "arbitrary"), 32),
        name="lru",
    )(p, p, conv_init, h0, *prm)


def _pad_front_rows(buf):
    return jnp.pad(buf, ((0, 0), (SUBLANES - buf.shape[1], 0), (0, 0)))


def _row2(v):
    return v.reshape(1, -1)


def kernel(x_prompt, x_sample, state_ssm_a, state_conv_a, state_wkv_b, state_shift_b, state_lru_c, state_conv_c, norm_gain, w_ffn_in, w_ffn_out, w_in_ab, conv_w_a, conv_b_a, dt_bias_a, a_log_a, d_skip_a, gnorm_a, mu_b, w0_b, w2_b, a0_b, a2_b, g2_b, k_k_b, k_a_b, r_k_b, ln_w_b, ln_b_b, w_out_ab, w_in_c, conv_w_c, conv_b_c, w_gate_a_c, b_gate_a_c, w_gate_x_c, b_gate_x_c, lambda_c, w_out_c, final_norm_gain):
    pad_ff = D_FF_PAD - D_FF
    w_gu = jnp.pad(w_ffn_in.reshape(2, 2, D, 2, D_FF), ((0, 0),) * 4 + ((0, pad_ff),)).astype(BF16)
    w_gu = w_gu.reshape(2, 2, D, 2 * D_FF_PAD)
    w_dn = jnp.pad(w_ffn_out, ((0, 0), (0, 0), (0, pad_ff), (0, 0))).astype(BF16)

    in_a = D + (D + 2 * N_GROUPS * N_STATE) + N_HEADS
    wab = w_in_ab[0]
    bc_w = 2 * N_GROUPS * N_STATE
    w_proj0 = jnp.concatenate([
        wab[:, in_a:in_a + 3 * D],
        wab[:, 0:D],
        wab[:, D:2 * D],
        wab[:, 2 * D:2 * D + bc_w],
        wab[:, in_a + 3 * D:in_a + 3 * D + LORA_W],
        wab[:, in_a - N_HEADS:in_a],
        jnp.zeros((D, PROJ_W - COL_DT - N_HEADS), F32),
    ], axis=1).astype(BF16)
    w_proj1 = w_in_c[0].astype(BF16)
    w_out0 = w_out_ab[0].astype(BF16)
    w_out1 = w_out_c[0].astype(BF16)

    pad_lanes = lambda v: jnp.pad(v.reshape(1, -1), ((0, 0), (0, LANES - v.shape[-1])))
    rep_head = lambda v: jnp.repeat(v, HEAD).reshape(1, D)
    ssd_prm = (conv_w_a[0][:, :D], conv_w_a[0][:, D:], _row2(conv_b_a[0][:D]), _row2(conv_b_a[0][D:]),
               pad_lanes(dt_bias_a[0]), pad_lanes(a_log_a[0]), rep_head(d_skip_a[0]), _row2(gnorm_a[0]))
    lora_rows = lambda w, lo: jnp.pad(w, ((lo, LORA_W - lo - w.shape[0]), (0, 0))).astype(BF16)
    mu = mu_b[0]
    rwkv_prm = (_row2(mu[:3 * D]), _row2(mu[3 * D:]),
                _row2(w0_b[0]), lora_rows(w2_b[0], 0), _row2(a0_b[0]), lora_rows(a2_b[0], 64),
                lora_rows(g2_b[0], 128), _row2(k_k_b[0]), _row2(k_a_b[0]), _row2(r_k_b[0]))
    lru_prm = (conv_w_c[0], _row2(conv_b_c[0]), w_gate_a_c[0].astype(BF16), _row2(b_gate_a_c[0]),
               w_gate_x_c[0].astype(BF16), _row2(b_gate_x_c[0]), _row2(lambda_c[0]))

    def trunk(x3, ssm0, conva0, wkv0, shift0, lru0, convc0):
        nseq, seq_len, _ = x3.shape
        x = x3.reshape(nseq * seq_len, D)
        tail = lambda arr, n: arr.reshape(nseq, seq_len, arr.shape[-1])[:, seq_len - n:, :]

        x = _ffn(x, _row2(norm_gain[0, 0]), w_gu, w_dn, (0, 0))
        p = _proj(x, _row2(norm_gain[0, 1]), w_proj0)
        conv_x = _pad_front_rows(conva0[:, :, :D])
        conv_bc = _pad_front_rows(conva0[:, :, D:])
        ya, ssm_n = _ssd(p, conv_x, conv_bc, ssm0.reshape(nseq, D, N_STATE), ssd_prm, nseq, seq_len)
        p_tail = tail(p, CONV_W - 1)
        conva_n = jnp.concatenate([p_tail[:, :, COL_X:COL_X + D], p_tail[:, :, COL_BC:COL_BC + bc_w]], axis=-1)
        shift_n = jnp.concatenate([p_tail[:, -1:, :3 * D], p_tail[:, -1:, COL_LORA:COL_LORA + LORA_W]], axis=-1)

        r, w, k, v, kk, b, bonus, g = _rwkv_pre(p, _pad_front_rows(shift0[:, :, :3 * D]),
                                                _pad_front_rows(shift0[:, :, 3 * D:]), rwkv_prm, nseq, seq_len,
                                                log_decay=wkv0 is None)
        if wkv0 is None:
            o, wkv_n = _wkv_long(r, w, k, v, kk, b, nseq, seq_len)
            wkv_n = wkv_n.reshape(nseq, N_HEADS, HEAD, HEAD)
        else:
            to_lanes = lambda a: a.reshape(nseq, seq_len, N_HEADS, HEAD).transpose(1, 2, 3, 0)
            o, wkv_n = _wkv_short(*[to_lanes(a) for a in (r, w, k, v, kk, b)], wkv0.transpose(1, 2, 3, 0),
                                  nseq, seq_len)
            o = o.transpose(3, 0, 1, 2).reshape(nseq * seq_len, D)
            wkv_n = wkv_n.transpose(3, 0, 1, 2)
        yb = _rwkv_post(o, bonus, g, _row2(ln_w_b[0]), _row2(ln_b_b[0]))
        x = _out_proj(x, (ya, yb), (w_out0[:D], w_out0[D:]))
        x = _ffn(x, _row2(norm_gain[0, 2]), w_gu, w_dn, (0, 1))

        x = _ffn(x, _row2(norm_gain[1, 0]), w_gu, w_dn, (1, 0))
        pc = _proj(x, _row2(norm_gain[1, 1]), w_proj1)
        yc, lru_n = _lru(pc, _pad_front_rows(convc0), lru0.reshape(nseq, 1, D), lru_prm, nseq, seq_len)
        convc_n = tail(pc, CONV_W - 1)[:, :, D:]
        x = _out_proj(x, (yc,), (w_out1,))
        y = _ffn(x, _row2(norm_gain[1, 2]), w_gu, w_dn, (1, 1), final_gain=_row2(final_norm_gain))

        return (y.reshape(nseq, seq_len, D), ssm_n.reshape(1, nseq, N_HEADS, HEAD, N_STATE), conva_n[None],
                wkv_n[None], shift_n[None], lru_n.reshape(1, nseq, D), convc_n[None])

    bp = x_prompt.shape[0]
    zeros = lambda s: jnp.zeros((bp,) + s.shape[2:], F32)
    outs_p = trunk(x_prompt, zeros(state_ssm_a), zeros(state_conv_a), None, zeros(state_shift_b),
                   zeros(state_lru_c), zeros(state_conv_c))
    outs_s = trunk(x_sample, state_ssm_a[0], state_conv_a[0], state_wkv_b[0], state_shift_b[0],
                   state_lru_c[0], state_conv_c[0])
    return (outs_p[0], outs_s[0]) + outs_p[1:] + outs_s[1:]
```

```python
import functools

import jax
import jax.numpy as jnp
from jax import lax
from jax.experimental import pallas as pl
from jax.experimental.pallas import tpu as pltpu

F32 = jnp.float32
BF16 = jnp.bfloat16

D = 2048
D_FF = 5504
D_FF_PAD = 5632
HEAD = 64
N_HEADS = 32
N_GROUPS = 4
GROUP_W = D // N_GROUPS
N_STATE = 128
CONV_W = 4
LORA_W = 256
LRU_BLOCKS = 8
LRU_BLK = D // LRU_BLOCKS
LRU_C = 8.0
EPS = 1e-6
GN_EPS = 64e-5
SUBLANES = 8
LANES = 128
CHUNK = 128
WKV_CHUNK = 64
WKV_LANES = 256
WKV_GROUPS = 2

COL_RKV = 0
COL_Z = 3 * D
COL_X = 4 * D
COL_BC = 5 * D
COL_LORA = 5 * D + 2 * N_GROUPS * N_STATE
COL_DT = COL_LORA + LORA_W
PROJ_W = COL_DT + 256

TM_FFN = 512
TF_FFN = 512
TM_PROJ = 1024
TN_PROJ = 512
TM_OUT = 512
TN_OUT = 1024
TM_POST = 256


def _cparams(sem, vmem_mib):
    return pltpu.CompilerParams(dimension_semantics=sem, vmem_limit_bytes=vmem_mib * 1024 * 1024)


def _softplus(x):
    return jnp.maximum(x, 0.0) + jnp.log(1.0 + jnp.exp(-jnp.abs(x)))


def _silu(x):
    return x * jax.nn.sigmoid(x)


def _rms(x, gain):
    ms = jnp.mean(x * x, axis=-1, keepdims=True)
    return x * lax.rsqrt(ms + EPS) * gain


def _ffn_body(x_ref, g_ref, wg_ref, wu_ref, wo_ref, fg_ref, o_ref, xn_ref, acc_ref, *, nf, final):
    f = pl.program_id(1)

    @pl.when(f == 0)
    def _():
        xn_ref[...] = _rms(x_ref[...], g_ref[...]).astype(BF16)
        acc_ref[...] = jnp.zeros_like(acc_ref)

    xn = xn_ref[...]
    gate = jnp.dot(xn, wg_ref[...], preferred_element_type=F32)
    up = jnp.dot(xn, wu_ref[...], preferred_element_type=F32)
    h = (_silu(gate) * up).astype(BF16)
    acc_ref[...] += jnp.dot(h, wo_ref[...], preferred_element_type=F32)

    @pl.when(f == nf - 1)
    def _():
        y = x_ref[...] + 0.5 * acc_ref[...]
        if final:
            y = _rms(y, fg_ref[...])
        o_ref[...] = y


def _ffn(x, gain, w_gu, w_dn, which, final_gain=None):
    m = x.shape[0]
    li, si = which
    nf = D_FF_PAD // TF_FFN
    final = final_gain is not None
    fg = final_gain if final else gain
    return pl.pallas_call(
        functools.partial(_ffn_body, nf=nf, final=final),
        out_shape=jax.ShapeDtypeStruct((m, D), F32),
        grid=(m // TM_FFN, nf),
        in_specs=[
            pl.BlockSpec((TM_FFN, D), lambda i, f: (i, 0)),
            pl.BlockSpec((1, D), lambda i, f: (0, 0)),
            pl.BlockSpec((None, None, D, TF_FFN), lambda i, f: (li, si, 0, f)),
            pl.BlockSpec((None, None, D, TF_FFN), lambda i, f: (li, si, 0, f + nf)),
            pl.BlockSpec((None, None, TF_FFN, D), lambda i, f: (li, si, f, 0)),
            pl.BlockSpec((1, D), lambda i, f: (0, 0)),
        ],
        out_specs=pl.BlockSpec((TM_FFN, D), lambda i, f: (i, 0)),
        scratch_shapes=[pltpu.VMEM((TM_FFN, D), BF16), pltpu.VMEM((TM_FFN, D), F32)],
        compiler_params=_cparams(("parallel", "arbitrary"), 48),
        name="ffn",
    )(x, gain, w_gu, w_gu, w_dn, fg)


def _proj_body(x_ref, g_ref, w_ref, o_ref, xn_ref):
    @pl.when(pl.program_id(1) == 0)
    def _():
        xn_ref[...] = _rms(x_ref[...], g_ref[...]).astype(BF16)

    o_ref[...] = jnp.dot(xn_ref[...], w_ref[...], preferred_element_type=F32)


def _proj(x, gain, w):
    m = x.shape[0]
    n = w.shape[1]
    return pl.pallas_call(
        _proj_body,
        out_shape=jax.ShapeDtypeStruct((m, n), F32),
        grid=(m // TM_PROJ, n // TN_PROJ),
        in_specs=[
            pl.BlockSpec((TM_PROJ, D), lambda i, j: (i, 0)),
            pl.BlockSpec((1, D), lambda i, j: (0, 0)),
            pl.BlockSpec((D, TN_PROJ), lambda i, j: (0, j)),
        ],
        out_specs=pl.BlockSpec((TM_PROJ, TN_PROJ), lambda i, j: (i, j)),
        scratch_shapes=[pltpu.VMEM((TM_PROJ, D), BF16)],
        compiler_params=_cparams(("parallel", "arbitrary"), 40),
        name="proj",
    )(x, gain, w)


def _out2_body(res_ref, ya_ref, yb_ref, wa_ref, wb_ref, o_ref):
    acc = jnp.dot(ya_ref[...].astype(BF16), wa_ref[...], preferred_element_type=F32)
    acc = acc + jnp.dot(yb_ref[...].astype(BF16), wb_ref[...], preferred_element_type=F32)
    o_ref[...] = res_ref[...] + acc


def _out1_body(res_ref, y_ref, w_ref, o_ref):
    o_ref[...] = res_ref[...] + jnp.dot(y_ref[...].astype(BF16), w_ref[...], preferred_element_type=F32)


def _out_proj(res, ys, ws):
    m = res.shape[0]
    body = _out2_body if len(ys) == 2 else _out1_body
    y_spec = pl.BlockSpec((TM_OUT, D), lambda j, i: (i, 0))
    w_spec = pl.BlockSpec((D, TN_OUT), lambda j, i: (0, j))
    r_spec = pl.BlockSpec((TM_OUT, TN_OUT), lambda j, i: (i, j))
    return pl.pallas_call(
        body,
        out_shape=jax.ShapeDtypeStruct((m, D), F32),
        grid=(D // TN_OUT, m // TM_OUT),
        in_specs=[r_spec] + [y_spec] * len(ys) + [w_spec] * len(ws),
        out_specs=r_spec,
        compiler_params=_cparams(("arbitrary", "arbitrary"), 48),
        name="out_proj",
    )(res, *ys, *ws)


def _cast_pad_body(w_ref, o_ref, *, axis, zero_blocks):
    c = pl.program_id(axis)
    is_pad = functools.reduce(jnp.logical_or, [c == z for z in zero_blocks])
    o_ref[...] = jnp.where(is_pad, 0.0, w_ref[...]).astype(BF16)


def _ffn_weights(w_in, w_out):
    n_layers, n_slots = w_in.shape[:2]
    nb = D_FF // LANES
    nbp = D_FF_PAD // LANES
    src_col = lambda c: jnp.where(c < nbp, jnp.minimum(c, nb - 1), jnp.minimum(c - (nbp - nb), 2 * nb - 1))
    w_gu = pl.pallas_call(
        functools.partial(_cast_pad_body, axis=1, zero_blocks=tuple(range(nb, nbp)) + tuple(range(nbp + nb, 2 * nbp))),
        out_shape=jax.ShapeDtypeStruct((n_layers, n_slots, D, 2 * D_FF_PAD), BF16),
        grid=(n_layers * n_slots, 2 * nbp),
        in_specs=[pl.BlockSpec((None, None, D, LANES), lambda q, c: (q // n_slots, q % n_slots, 0, src_col(c)))],
        out_specs=pl.BlockSpec((None, None, D, LANES), lambda q, c: (q // n_slots, q % n_slots, 0, c)),
        compiler_params=_cparams(("parallel", "parallel"), 32),
        name="ffn_w_in",
    )(w_in)
    w_dn = pl.pallas_call(
        functools.partial(_cast_pad_body, axis=1, zero_blocks=tuple(range(nb, nbp))),
        out_shape=jax.ShapeDtypeStruct((n_layers, n_slots, D_FF_PAD, D), BF16),
        grid=(n_layers * n_slots, nbp),
        in_specs=[pl.BlockSpec((None, None, LANES, D),
                               lambda q, r: (q // n_slots, q % n_slots, jnp.minimum(r, nb - 1), 0))],
        out_specs=pl.BlockSpec((None, None, LANES, D), lambda q, r: (q // n_slots, q % n_slots, r, 0)),
        compiler_params=_cparams(("parallel", "parallel"), 32),
        name="ffn_w_out",
    )(w_out)
    return w_gu, w_dn


def _carried_window(buf, cur_ref, init_ref, first, q):
    @pl.when(first)
    def _():
        buf[0:SUBLANES, :] = init_ref[0]

    buf[SUBLANES:SUBLANES + q, :] = cur_ref[...]


def _advance_window(buf, q):
    tail = buf[q:q + SUBLANES, :]
    buf[0:SUBLANES, :] = tail


def _causal_conv(buf, cw_ref, cb_ref, q):
    acc = cb_ref[...] + cw_ref[0:1, :] * buf[pl.ds(SUBLANES - CONV_W + 1, q), :]
    for k in range(1, CONV_W):
        acc = acc + cw_ref[k:k + 1, :] * buf[pl.ds(SUBLANES - CONV_W + 1 + k, q), :]
    return acc


def _cumsum_rows(x):
    n = x.shape[0]
    row = lax.broadcasted_iota(jnp.int32, x.shape, 0)
    s = 1
    while s < n:
        x = x + jnp.where(row >= s, pltpu.roll(x, s, axis=0), 0.0)
        s *= 2
    return x


def _expand_heads(a, rows):
    lane = lax.broadcasted_iota(jnp.int32, (rows, LANES), 1)
    low = lane < HEAD
    pieces = []
    for j in range(N_HEADS // 2):
        e0 = jnp.broadcast_to(a[:, 2 * j:2 * j + 1], (rows, LANES))
        e1 = jnp.broadcast_to(a[:, 2 * j + 1:2 * j + 2], (rows, LANES))
        pieces.append(jnp.where(low, e0, e1))
    return jnp.concatenate(pieces, axis=1)


def _head_allsum(x):
    width = x.shape[-1]
    blk = 4 * HEAD
    r = lax.broadcasted_iota(jnp.int32, (blk, blk), 0) // HEAD
    c = lax.broadcasted_iota(jnp.int32, (blk, blk), 1) // HEAD
    ones_bd = jnp.where(r == c, 1.0, 0.0).astype(BF16)
    hi = x.astype(BF16)
    rem = x - hi.astype(F32)
    mid = rem.astype(BF16)
    lo = (rem - mid.astype(F32)).astype(BF16)
    dot = lambda p: jnp.dot(p, ones_bd, preferred_element_type=F32)
    cols = []
    for j in range(width // blk):
        sl = slice(j * blk, (j + 1) * blk)
        cols.append(dot(hi[:, sl]) + (dot(mid[:, sl]) + dot(lo[:, sl])))
    return jnp.concatenate(cols, axis=1)


def _ssd_body(z_ref, x_ref, bc_ref, dt_ref, cix_ref, cibc_ref, h0_ref,
              cwx_ref, cwbc_ref, cbx_ref, cbbc_ref, dtb_ref, alog_ref, dsk_ref, gn_ref,
              y_ref, ho_ref, xbuf, bcbuf, zbuf, dtbuf, h_s, *, q, nc):
    c = pl.program_id(1)
    padded = q < CHUNK
    if padded:
        xbuf[...] = jnp.zeros_like(xbuf)
        bcbuf[...] = jnp.zeros_like(bcbuf)
        zbuf[...] = jnp.zeros_like(zbuf)
        dtbuf[...] = jnp.zeros_like(dtbuf)
    _carried_window(xbuf, x_ref, cix_ref, c == 0, q)
    _carried_window(bcbuf, bc_ref, cibc_ref, c == 0, q)

    @pl.when(c == 0)
    def _():
        h_s[...] = h0_ref[0]

    row = lax.broadcasted_iota(jnp.int32, (CHUNK, LANES), 0)
    xs = _silu(_causal_conv(xbuf, cwx_ref, cbx_ref, CHUNK))
    bcv = _silu(_causal_conv(bcbuf, cwbc_ref, cbbc_ref, CHUNK))
    if padded:
        zbuf[0:q, :] = z_ref[...]
        dtbuf[0:q, :] = dt_ref[...]
        z = zbuf[...]
        dt_raw = dtbuf[...]
    else:
        _advance_window(xbuf, q)
        _advance_window(bcbuf, q)
        z = z_ref[...]
        dt_raw = dt_ref[...]

    dt = _softplus(dt_raw + dtb_ref[...])
    if padded:
        dt = jnp.where(row < q, dt, 0.0)
    a_head = -jnp.exp(alog_ref[...])
    cum = _cumsum_rows(dt * a_head)
    cum_last = cum[CHUNK - 1:CHUNK, :]
    cum_t = cum.T
    dtx = _expand_heads(dt, CHUNK)
    ecx = _expand_heads(jnp.exp(cum), CHUNK)
    tlx = _expand_heads(jnp.exp(cum_last - cum), CHUNK)
    xdt = xs * dtx
    xdtw = (xdt * tlx).astype(BF16)

    ti = lax.broadcasted_iota(jnp.int32, (CHUNK, CHUNK), 0)
    tj = lax.broadcasted_iota(jnp.int32, (CHUNK, CHUNK), 1)
    causal = ti >= tj
    low = tj < HEAD
    nt = (((1,), (1,)), ((), ()))
    tn = (((0,), (0,)), ((), ()))

    for g in range(N_GROUPS):
        gsl = slice(g * GROUP_W, (g + 1) * GROUP_W)
        bg = bcv[:, g * N_STATE:(g + 1) * N_STATE].astype(BF16)
        cg = bcv[:, (N_GROUPS + g) * N_STATE:(N_GROUPS + g + 1) * N_STATE].astype(BF16)
        cb = lax.dot_general(cg, bg, nt, preferred_element_type=F32)
        hg = h_s[gsl, :]
        y_off = lax.dot_general(cg, hg.astype(BF16), nt, preferred_element_type=F32) * ecx[:, gsl]
        st = lax.dot_general(xdtw[:, gsl], bg, tn, preferred_element_type=F32)
        y_pairs = []
        for j in range(GROUP_W // LANES):
            h0 = g * (GROUP_W // HEAD) + 2 * j
            ms = []
            for h in (h0, h0 + 1):
                seg = cum[:, h:h + 1] - cum_t[h:h + 1, :]
                decay = jnp.exp(jnp.where(causal, seg, -jnp.inf))
                ms.append((cb * decay).astype(BF16))
            psl = slice(h0 * HEAD, (h0 + 2) * HEAD)
            slab = xdt[:, psl]
            rhs = jnp.concatenate([jnp.where(low, slab, 0.0), jnp.where(low, 0.0, slab)], axis=0).astype(BF16)
            y_pairs.append(jnp.dot(jnp.concatenate(ms, axis=1), rhs, preferred_element_type=F32))
            for h in (h0, h0 + 1):
                hsl = slice(h * HEAD, (h + 1) * HEAD)
                dec = jnp.exp(jnp.broadcast_to(cum_last[:, h:h + 1], (HEAD, N_STATE)))
                h_s[hsl, :] = h_s[hsl, :] * dec + st[(h % 8) * HEAD:(h % 8 + 1) * HEAD, :]
        y = jnp.concatenate(y_pairs, axis=1) + y_off + xs[:, gsl] * dsk_ref[:, gsl]
        y = y * _silu(z[:, gsl])
        y = y * lax.rsqrt(jnp.mean(y * y, axis=-1, keepdims=True) + EPS) * gn_ref[:, gsl]
        if padded:
            y_ref[:, gsl] = y[0:q, :]
        else:
            y_ref[:, gsl] = y

    @pl.when(c == nc - 1)
    def _():
        ho_ref[0] = h_s[...]


def _ssd(p, conv_x, conv_bc, h0, prm, nseq, seq_len):
    q = min(CHUNK, seq_len)
    nc = seq_len // q
    rows = lambda w, col: pl.BlockSpec((q, w), lambda s, c: (s * nc + c, col))
    per_seq = lambda shape: pl.BlockSpec((1,) + shape, lambda s, c: (s,) + (0,) * len(shape))
    const = lambda shape: pl.BlockSpec(shape, lambda s, c: (0,) * len(shape))
    y, h_last = pl.pallas_call(
        functools.partial(_ssd_body, q=q, nc=nc),
        out_shape=(jax.ShapeDtypeStruct((nseq * seq_len, D), F32),
                   jax.ShapeDtypeStruct((nseq, D, N_STATE), F32)),
        grid=(nseq, nc),
        in_specs=[
            rows(D, COL_Z // D), rows(D, COL_X // D), rows(2 * N_GROUPS * N_STATE, COL_BC // 1024),
            rows(LANES, COL_DT // LANES),
            per_seq((SUBLANES, D)), per_seq((SUBLANES, 1024)), per_seq((D, N_STATE)),
            const((CONV_W, D)), const((CONV_W, 1024)), const((1, D)), const((1, 1024)),
            const((1, LANES)), const((1, LANES)), const((1, D)), const((1, D)),
        ],
        out_specs=(pl.BlockSpec((q, D), lambda s, c: (s * nc + c, 0)), per_seq((D, N_STATE))),
        scratch_shapes=[
            pltpu.VMEM((CHUNK + SUBLANES, D), F32), pltpu.VMEM((CHUNK + SUBLANES, 1024), F32),
            pltpu.VMEM((CHUNK, D), F32), pltpu.VMEM((CHUNK, LANES), F32),
            pltpu.VMEM((D, N_STATE), F32),
        ],
        compiler_params=_cparams(("parallel", "arbitrary"), 40),
        name="ssd",
    )(p, p, p, p, conv_x, conv_bc, h0, *prm)
    return y, h_last


def _rwkv_pre_body(rkv_ref, lora_ref, si_rkv_ref, si_lora_ref, mu_rkv_ref, mu_lora_ref,
                   w0_ref, w2_ref, a0_ref, a2_ref, g2_ref, kk_ref, ka_ref, rk_ref,
                   r_out, w_out, k_out, v_out, kk_out, b_out, bonus_out, g_out,
                   rkvbuf, lorabuf, *, q, log_decay):
    c = pl.program_id(1)
    _carried_window(rkvbuf, rkv_ref, si_rkv_ref, c == 0, q)
    _carried_window(lorabuf, lora_ref, si_lora_ref, c == 0, q)

    def shifted(buf, cur_ref, mu_ref):
        cur = cur_ref[...]
        prev = buf[pl.ds(SUBLANES - 1, q), :]
        return cur + mu_ref[...] * (prev - cur)

    ps = shifted(rkvbuf, rkv_ref, mu_rkv_ref)
    lo_in = shifted(lorabuf, lora_ref, mu_lora_ref)
    _advance_window(rkvbuf, q)
    _advance_window(lorabuf, q)
    r = ps[:, 0:D]
    k = ps[:, D:2 * D]
    v = ps[:, 2 * D:3 * D]

    lw = jnp.dot(jnp.tanh(lo_in).astype(BF16), w2_ref[...], preferred_element_type=F32)
    la = jnp.dot(lo_in.astype(BF16), a2_ref[...], preferred_element_type=F32)
    g = jnp.dot(jax.nn.sigmoid(lo_in).astype(BF16), g2_ref[...], preferred_element_type=F32)
    wlog = -_softplus(-(w0_ref[...] + lw)) - 0.5
    log_w = -jnp.exp(wlog)
    a = jax.nn.sigmoid(a0_ref[...] + la)
    kkf = k * kk_ref[...]
    norm = jnp.maximum(jnp.sqrt(_head_allsum(kkf * kkf)), 1e-12)
    kk = kkf / norm
    k2 = k * (1.0 + (a - 1.0) * ka_ref[...])
    bonus = _head_allsum(r * k2 * rk_ref[...]) * v
    r_out[...] = r
    w_out[...] = log_w if log_decay else jnp.exp(log_w)
    k_out[...] = k2
    v_out[...] = v
    kk_out[...] = kk
    b_out[...] = kk * a
    bonus_out[...] = bonus
    g_out[...] = g


def _rwkv_pre(p, shift_rkv, shift_lora, prm, nseq, seq_len, log_decay):
    q = min(CHUNK, seq_len)
    nc = seq_len // q
    per_seq = lambda shape: pl.BlockSpec((1,) + shape, lambda s, c: (s,) + (0,) * len(shape))
    const = lambda shape: pl.BlockSpec(shape, lambda s, c: (0,) * len(shape))
    tile = pl.BlockSpec((q, D), lambda s, c: (s * nc + c, 0))
    sds = jax.ShapeDtypeStruct((nseq * seq_len, D), F32)
    return pl.pallas_call(
        functools.partial(_rwkv_pre_body, q=q, log_decay=log_decay),
        out_shape=(sds,) * 8,
        grid=(nseq, nc),
        in_specs=[
            pl.BlockSpec((q, 3 * D), lambda s, c: (s * nc + c, 0)),
            pl.BlockSpec((q, LORA_W), lambda s, c: (s * nc + c, COL_LORA // LORA_W)),
            per_seq((SUBLANES, 3 * D)), per_seq((SUBLANES, LORA_W)),
            const((1, 3 * D)), const((1, LORA_W)),
            const((1, D)), const((LORA_W, D)), const((1, D)), const((LORA_W, D)), const((LORA_W, D)),
            const((1, D)), const((1, D)), const((1, D)),
        ],
        out_specs=(tile,) * 8,
        scratch_shapes=[pltpu.VMEM((CHUNK + SUBLANES, 3 * D), F32), pltpu.VMEM((CHUNK + SUBLANES, LORA_W), F32)],
        compiler_params=_cparams(("parallel", "arbitrary"), 48),
        name="rwkv_pre",
    )(p, p, shift_rkv, shift_lora, *prm)


def _split_bf16(x):
    hi = x.astype(BF16)
    return hi, (x - hi.astype(F32)).astype(BF16)


_NN = (((1,), (0,)), ((), ()))
_NT = (((1,), (1,)), ((), ()))
_TN = (((0,), (0,)), ((), ()))


def _solve_unit_lower(n, rhs):
    rows, width = rhs.shape
    nblk, ncol = rows // SUBLANES, width // LANES
    nb = [[n[SUBLANES * i:SUBLANES * (i + 1), LANES * j:LANES * (j + 1)] for j in range(ncol)] for i in range(nblk)]
    xb = [[rhs[SUBLANES * i:SUBLANES * (i + 1), LANES * j:LANES * (j + 1)] for j in range(ncol)] for i in range(nblk)]
    low = lax.broadcasted_iota(jnp.int32, (SUBLANES, LANES), 1) < HEAD
    for s in range(rows - 1):
        i0, r0 = divmod(s, SUBLANES)
        idx = jnp.where(low, s, HEAD + s)
        for j in range(ncol):
            row = xb[i0][j][r0:r0 + 1, :]
            for i in range(i0 if r0 < SUBLANES - 1 else i0 + 1, nblk):
                xb[i][j] = xb[i][j] - jnp.take_along_axis(nb[i][j], idx, axis=1) * row
    return jnp.concatenate([jnp.concatenate(xr, axis=1) for xr in xb], axis=0)


def _wkv_tile(r, lw, k, v, kk, b, st):
    rows, width = r.shape
    nh = WKV_LANES // HEAD
    groups = [slice(g * WKV_LANES, (g + 1) * WKV_LANES) for g in range(width // WKV_LANES)]
    cl = _cumsum_rows(lw)
    cl_last = cl[rows - 1:rows, :]
    p_inv = jnp.exp(-cl)
    p_end = jnp.exp(cl_last - cl)
    x2h, x2l = _split_bf16(jnp.concatenate([kk * jnp.exp(cl - lw), r * jnp.exp(cl)], axis=0))
    k_hat = k * p_inv
    b_hat = b * p_inv
    k_end = k * p_end
    b_end = -(b * p_end)

    bd_r = lax.broadcasted_iota(jnp.int32, (nh * rows, WKV_LANES), 0) // rows
    bd_c = lax.broadcasted_iota(jnp.int32, (nh * rows, WKV_LANES), 1) // HEAD
    bd_mask = jnp.where(bd_r == bd_c, 1.0, 0.0).astype(BF16)

    def per_head_rows(y):
        return [jnp.concatenate([part] * nh, axis=0) * bd_mask for part in _split_bf16(y)]

    def dot3(ah, al, bh, bl, dims):
        dg = lambda x, y: lax.dot_general(x, y, dims, preferred_element_type=F32)
        return dg(ah, bh) + (dg(ah, bl) + dg(al, bh))

    t_i = lax.broadcasted_iota(jnp.int32, (rows, WKV_LANES), 0)
    lane = lax.broadcasted_iota(jnp.int32, (rows, WKV_LANES), 1)
    s_i = lane & (HEAD - 1)
    strict = t_i > s_i
    incl = t_i >= s_i
    head_of_lane = lane // HEAD

    a_kb, a_rb, base, o_part = [], [], [], []
    for gs in groups:
        ak = dot3(x2h[:, gs], x2l[:, gs], *per_head_rows(k_hat[:, gs]), _NT)
        ab = dot3(x2h[:, gs], x2l[:, gs], *per_head_rows(b_hat[:, gs]), _NT)
        a_k = jnp.concatenate([jnp.where(strict, ak[0:rows], 0.0), jnp.where(incl, ak[rows:], 0.0)], axis=0)
        xs = dot3(x2h[:, gs], x2l[:, gs], *per_head_rows(st[:, gs]), _NN)
        av = dot3(*_split_bf16(a_k), *per_head_rows(v[:, gs]), _NN)
        a_kb.append(jnp.where(strict, ab[0:rows], 0.0))
        a_rb.append(jnp.where(incl, ab[rows:], 0.0))
        base.append(xs[0:rows] + av[0:rows])
        o_part.append(xs[rows:] + av[rows:])
    sa = _solve_unit_lower(jnp.concatenate(a_kb, axis=1), jnp.concatenate(base, axis=1))

    o, st_new = [], []
    for g, gs in enumerate(groups):
        o.append(o_part[g] - dot3(*_split_bf16(a_rb[g]), *per_head_rows(sa[:, gs]), _NN))
        eye = jnp.where(t_i == s_i, jnp.exp(cl_last[:, gs]), 0.0)
        lhs = jnp.concatenate([k_end[:, gs], b_end[:, gs], eye], axis=0)
        rhs = jnp.concatenate([v[:, gs], sa[:, gs], st[:, gs]], axis=0)
        full = dot3(*_split_bf16(lhs), *_split_bf16(rhs), _TN)
        acc = jnp.where(head_of_lane == 0, full[0:HEAD, :], 0.0)
        for h in range(1, nh):
            acc = acc + jnp.where(head_of_lane == h, full[h * HEAD:(h + 1) * HEAD, :], 0.0)
        st_new.append(acc)
    return jnp.concatenate(o, axis=1), jnp.concatenate(st_new, axis=1)


def _wkv_long_body(r_ref, lw_ref, k_ref, v_ref, kk_ref, b_ref, o_ref, so_ref, st_s, *, nc):
    c = pl.program_id(2)

    @pl.when(c == 0)
    def _():
        st_s[...] = jnp.zeros_like(st_s)

    o, st_new = _wkv_tile(r_ref[...], lw_ref[...], k_ref[...], v_ref[...], kk_ref[...], b_ref[...], st_s[...])
    o_ref[...] = o
    st_s[...] = st_new

    @pl.when(c == nc - 1)
    def _():
        for j in range(WKV_GROUPS * WKV_LANES // LANES):
            js = slice(j * LANES, (j + 1) * LANES)
            sq = jnp.concatenate([st_s[:, js], jnp.zeros((LANES - HEAD, LANES), F32)], axis=0)
            so_ref[0, js, :] = sq.T[:, 0:HEAD]


def _wkv_long(r, lw, k, v, kk, b, nseq, seq_len):
    nc = seq_len // WKV_CHUNK
    width = WKV_GROUPS * WKV_LANES
    tile = pl.BlockSpec((WKV_CHUNK, width), lambda s, hg, c: (s * nc + c, hg))
    o, s_last = pl.pallas_call(
        functools.partial(_wkv_long_body, nc=nc),
        out_shape=(jax.ShapeDtypeStruct((nseq * seq_len, D), F32),
                   jax.ShapeDtypeStruct((nseq, D, HEAD), F32)),
        grid=(nseq, D // width, nc),
        in_specs=[tile] * 6,
        out_specs=(tile, pl.BlockSpec((1, width, HEAD), lambda s, hg, c: (s, hg, 0))),
        scratch_shapes=[pltpu.VMEM((HEAD, width), F32)],
        compiler_params=_cparams(("parallel", "parallel", "arbitrary"), 32),
        name="wkv_long",
    )(r, lw, k, v, kk, b)
    return o, s_last


def _wkv_short_body(r_ref, w_ref, k_ref, v_ref, kk_ref, b_ref, s0_ref, o_ref, so_ref, *, steps):
    def per_v(vi, carry):
        s = s0_ref[0, vi]
        for t in range(steps):
            vrow = v_ref[t, 0, pl.ds(vi, 1), :]
            sa = jnp.sum(s * kk_ref[t, 0], axis=0, keepdims=True)
            s = s * w_ref[t, 0] - b_ref[t, 0] * sa + k_ref[t, 0] * vrow
            o_ref[t, 0, pl.ds(vi, 1), :] = jnp.sum(s * r_ref[t, 0], axis=0, keepdims=True)
        so_ref[0, vi] = s
        return carry

    lax.fori_loop(0, HEAD, per_v, 0)


def _wkv_short(r, w, k, v, kk, b, s0, nseq, seq_len):
    vec = pl.BlockSpec((seq_len, 1, HEAD, nseq), lambda h: (0, h, 0, 0))
    st = pl.BlockSpec((1, HEAD, HEAD, nseq), lambda h: (h, 0, 0, 0))
    return pl.pallas_call(
        functools.partial(_wkv_short_body, steps=seq_len),
        out_shape=(jax.ShapeDtypeStruct((seq_len, N_HEADS, HEAD, nseq), F32),
                   jax.ShapeDtypeStruct((N_HEADS, HEAD, HEAD, nseq), F32)),
        grid=(N_HEADS,),
        in_specs=[vec] * 6 + [st],
        out_specs=(vec, st),
        compiler_params=_cparams(("parallel",), 32),
        name="wkv_short",
    )(r, w, k, v, kk, b, s0)


def _rwkv_post_body(o_ref, bonus_ref, g_ref, lnw_ref, lnb_ref, y_ref):
    o = o_ref[...]
    mean = _head_allsum(o) * (1.0 / HEAD)
    cen = o - mean
    var = _head_allsum(cen * cen) * (1.0 / HEAD)
    on = cen * lax.rsqrt(var + GN_EPS) * lnw_ref[...] + lnb_ref[...]
    y_ref[...] = (on + bonus_ref[...]) * g_ref[...]


def _rwkv_post(o, bonus, g, ln_w, ln_b):
    m = o.shape[0]
    tile = pl.BlockSpec((TM_POST, D), lambda i: (i, 0))
    const = pl.BlockSpec((1, D), lambda i: (0, 0))
    return pl.pallas_call(
        _rwkv_post_body,
        out_shape=jax.ShapeDtypeStruct((m, D), F32),
        grid=(m // TM_POST,),
        in_specs=[tile, tile, tile, const, const],
        out_specs=tile,
        compiler_params=_cparams(("parallel",), 32),
        name="rwkv_post",
    )(o, bonus, g, ln_w, ln_b)


def _lru_body(gate_ref, x_ref, ci_ref, h0_ref, cw_ref, cb_ref, wga_ref, bga_ref, wgx_ref, bgx_ref, lam_ref,
              y_ref, ho_ref, xbuf, a_s, b_s, h_s, hc_s, *, q, nc):
    c = pl.program_id(1)
    _carried_window(xbuf, x_ref, ci_ref, c == 0, q)

    @pl.when(c == 0)
    def _():
        hc_s[...] = h0_ref[0]

    xc = _causal_conv(xbuf, cw_ref, cb_ref, q)
    _advance_window(xbuf, q)
    ra, rx = [], []
    for blk in range(LRU_BLOCKS):
        xh = xc[:, blk * LRU_BLK:(blk + 1) * LRU_BLK].astype(BF16)
        ra.append(jnp.dot(xh, wga_ref[blk], preferred_element_type=F32))
        rx.append(jnp.dot(xh, wgx_ref[blk], preferred_element_type=F32))
    rg = jax.nn.sigmoid(jnp.concatenate(ra, axis=1) + bga_ref[...])
    ig = jax.nn.sigmoid(jnp.concatenate(rx, axis=1) + bgx_ref[...])
    log_a = -LRU_C * rg * _softplus(-lam_ref[...])
    a_s[...] = jnp.exp(log_a)
    b_s[...] = jnp.sqrt(1.0 - jnp.exp(2.0 * log_a)) * (ig * xc)

    def step(t, h):
        h = a_s[pl.ds(t, 1), :] * h + b_s[pl.ds(t, 1), :]
        h_s[pl.ds(t, 1), :] = h
        return h

    h = lax.fori_loop(0, q, step, hc_s[...])
    hc_s[...] = h
    y_ref[...] = h_s[...] * jax.nn.gelu(gate_ref[...])

    @pl.when(c == nc - 1)
    def _():
        ho_ref[0] = h


def _lru(p, conv_init, h0, prm, nseq, seq_len):
    q = min(CHUNK, seq_len)
    nc = seq_len // q
    per_seq = lambda shape: pl.BlockSpec((1,) + shape, lambda s, c: (s,) + (0,) * len(shape))
    const = lambda shape: pl.BlockSpec(shape, lambda s, c: (0,) * len(shape))
    return pl.pallas_call(
        functools.partial(_lru_body, q=q, nc=nc),
        out_shape=(jax.ShapeDtypeStruct((nseq * seq_len, D), F32),
                   jax.ShapeDtypeStruct((nseq, 1, D), F32)),
        grid=(nseq, nc),
        in_specs=[
            pl.BlockSpec((q, D), lambda s, c: (s * nc + c, 0)),
            pl.BlockSpec((q, D), lambda s, c: (s * nc + c, 1)),
            per_seq((SUBLANES, D)), per_seq((1, D)),
            const((CONV_W, D)), const((1, D)),
            const((LRU_BLOCKS, LRU_BLK, LRU_BLK)), const((1, D)),
            const((LRU_BLOCKS, LRU_BLK, LRU_BLK)), const((1, D)), const((1, D)),
        ],
        out_specs=(pl.BlockSpec((q, D), lambda s, c: (s * nc + c, 0)), per_seq((1, D))),
        scratch_shapes=[pltpu.VMEM((CHUNK + SUBLANES, D), F32), pltpu.VMEM((q, D), F32), pltpu.VMEM((q, D), F32),
                        pltpu.VMEM((q, D), F32), pltpu.VMEM((1, D), F32)],
        compiler_params=_cparams(("parallel", "arbitrary"), 32),
        name="lru",
    )(p, p, conv_init, h0, *prm)


def _pad_front_rows(buf):
    return jnp.pad(buf, ((0, 0), (SUBLANES - buf.shape[1], 0), (0, 0)))


def _row2(v):
    return v.reshape(1, -1)


def kernel(x_prompt, x_sample, state_ssm_a, state_conv_a, state_wkv_b, state_shift_b, state_lru_c, state_conv_c, norm_gain, w_ffn_in, w_ffn_out, w_in_ab, conv_w_a, conv_b_a, dt_bias_a, a_log_a, d_skip_a, gnorm_a, mu_b, w0_b, w2_b, a0_b, a2_b, g2_b, k_k_b, k_a_b, r_k_b, ln_w_b, ln_b_b, w_out_ab, w_in_c, conv_w_c, conv_b_c, w_gate_a_c, b_gate_a_c, w_gate_x_c, b_gate_x_c, lambda_c, w_out_c, final_norm_gain):
    w_gu, w_dn = _ffn_weights(w_ffn_in, w_ffn_out)

    in_a = D + (D + 2 * N_GROUPS * N_STATE) + N_HEADS
    wab = w_in_ab[0]
    bc_w = 2 * N_GROUPS * N_STATE
    w_proj0 = jnp.concatenate([
        wab[:, in_a:in_a + 3 * D],
        wab[:, 0:D],
        wab[:, D:2 * D],
        wab[:, 2 * D:2 * D + bc_w],
        wab[:, in_a + 3 * D:in_a + 3 * D + LORA_W],
        wab[:, in_a - N_HEADS:in_a],
        jnp.zeros((D, PROJ_W - COL_DT - N_HEADS), F32),
    ], axis=1).astype(BF16)
    w_proj1 = w_in_c[0].astype(BF16)
    w_out0 = w_out_ab[0].astype(BF16)
    w_out1 = w_out_c[0].astype(BF16)

    pad_lanes = lambda v: jnp.pad(v.reshape(1, -1), ((0, 0), (0, LANES - v.shape[-1])))
    rep_head = lambda v: jnp.repeat(v, HEAD).reshape(1, D)
    ssd_prm = (conv_w_a[0][:, :D], conv_w_a[0][:, D:], _row2(conv_b_a[0][:D]), _row2(conv_b_a[0][D:]),
               pad_lanes(dt_bias_a[0]), pad_lanes(a_log_a[0]), rep_head(d_skip_a[0]), _row2(gnorm_a[0]))
    lora_rows = lambda w, lo: jnp.pad(w, ((lo, LORA_W - lo - w.shape[0]), (0, 0))).astype(BF16)
    mu = mu_b[0]
    rwkv_prm = (_row2(mu[:3 * D]), _row2(mu[3 * D:]),
                _row2(w0_b[0]), lora_rows(w2_b[0], 0), _row2(a0_b[0]), lora_rows(a2_b[0], 64),
                lora_rows(g2_b[0], 128), _row2(k_k_b[0]), _row2(k_a_b[0]), _row2(r_k_b[0]))
    lru_prm = (conv_w_c[0], _row2(conv_b_c[0]), w_gate_a_c[0].astype(BF16), _row2(b_gate_a_c[0]),
               w_gate_x_c[0].astype(BF16), _row2(b_gate_x_c[0]), _row2(lambda_c[0]))

    def trunk(x3, ssm0, conva0, wkv0, shift0, lru0, convc0):
        nseq, seq_len, _ = x3.shape
        x = x3.reshape(nseq * seq_len, D)
        tail = lambda arr, n: arr.reshape(nseq, seq_len, arr.shape[-1])[:, seq_len - n:, :]

        x = _ffn(x, _row2(norm_gain[0, 0]), w_gu, w_dn, (0, 0))
        p = _proj(x, _row2(norm_gain[0, 1]), w_proj0)
        conv_x = _pad_front_rows(conva0[:, :, :D])
        conv_bc = _pad_front_rows(conva0[:, :, D:])
        ya, ssm_n = _ssd(p, conv_x, conv_bc, ssm0.reshape(nseq, D, N_STATE), ssd_prm, nseq, seq_len)
        p_tail = tail(p, CONV_W - 1)
        conva_n = jnp.concatenate([p_tail[:, :, COL_X:COL_X + D], p_tail[:, :, COL_BC:COL_BC + bc_w]], axis=-1)
        shift_n = jnp.concatenate([p_tail[:, -1:, :3 * D], p_tail[:, -1:, COL_LORA:COL_LORA + LORA_W]], axis=-1)

        r, w, k, v, kk, b, bonus, g = _rwkv_pre(p, _pad_front_rows(shift0[:, :, :3 * D]),
                                                _pad_front_rows(shift0[:, :, 3 * D:]), rwkv_prm, nseq, seq_len,
                                                log_decay=wkv0 is None)
        if wkv0 is None:
            o, wkv_n = _wkv_long(r, w, k, v, kk, b, nseq, seq_len)
            wkv_n = wkv_n.reshape(nseq, N_HEADS, HEAD, HEAD)
        else:
            to_lanes = lambda a: a.reshape(nseq, seq_len, N_HEADS, HEAD).transpose(1, 2, 3, 0)
            o, wkv_n = _wkv_short(*[to_lanes(a) for a in (r, w, k, v, kk, b)], wkv0.transpose(1, 2, 3, 0),
                                  nseq, seq_len)
            o = o.transpose(3, 0, 1, 2).reshape(nseq * seq_len, D)
            wkv_n = wkv_n.transpose(3, 0, 1, 2)
        yb = _rwkv_post(o, bonus, g, _row2(ln_w_b[0]), _row2(ln_b_b[0]))
        x = _out_proj(x, (ya, yb), (w_out0[:D], w_out0[D:]))
        x = _ffn(x, _row2(norm_gain[0, 2]), w_gu, w_dn, (0, 1))

        x = _ffn(x, _row2(norm_gain[1, 0]), w_gu, w_dn, (1, 0))
        pc = _proj(x, _row2(norm_gain[1, 1]), w_proj1)
        yc, lru_n = _lru(pc, _pad_front_rows(convc0), lru0.reshape(nseq, 1, D), lru_prm, nseq, seq_len)
        convc_n = tail(pc, CONV_W - 1)[:, :, D:]
        x = _out_proj(x, (yc,), (w_out1,))
        y = _ffn(x, _row2(norm_gain[1, 2]), w_gu, w_dn, (1, 1), final_gain=_row2(final_norm_gain))

        return (y.reshape(nseq, seq_len, D), ssm_n.reshape(1, nseq, N_HEADS, HEAD, N_STATE), conva_n[None],
                wkv_n[None], shift_n[None], lru_n.reshape(1, nseq, D), convc_n[None])

    bp = x_prompt.shape[0]
    zeros = lambda s: jnp.zeros((bp,) + s.shape[2:], F32)
    outs_p = trunk(x_prompt, zeros(state_ssm_a), zeros(state_conv_a), None, zeros(state_shift_b),
                   zeros(state_lru_c), zeros(state_conv_c))
    outs_s = trunk(x_sample, state_ssm_a[0], state_conv_a[0], state_wkv_b[0], state_shift_b[0],
                   state_lru_c[0], state_conv_c[0])
    return (outs_p[0], outs_s[0]) + outs_p[1:] + outs_s[1:]
```

```python
import functools

import jax
import jax.numpy as jnp
from jax import lax
from jax.experimental import pallas as pl
from jax.experimental.pallas import tpu as pltpu

F32 = jnp.float32
BF16 = jnp.bfloat16

D = 2048
D_FF = 5504
D_FF_PAD = 5632
HEAD = 64
N_HEADS = 32
N_GROUPS = 4
GROUP_W = D // N_GROUPS
N_STATE = 128
CONV_W = 4
LORA_W = 256
LRU_BLOCKS = 8
LRU_BLK = D // LRU_BLOCKS
LRU_C = 8.0
EPS = 1e-6
GN_EPS = 64e-5
SUBLANES = 8
LANES = 128
CHUNK = 128
WKV_CHUNK = 64
WKV_LANES = 256
WKV_GROUPS = 4

COL_RKV = 0
COL_Z = 3 * D
COL_X = 4 * D
COL_BC = 5 * D
COL_LORA = 5 * D + 2 * N_GROUPS * N_STATE
COL_DT = COL_LORA + LORA_W
PROJ_W = COL_DT + 256

TM_FFN = 512
TF_FFN = 512
TM_PROJ = 1024
TN_PROJ = 512
TM_OUT = 512
TN_OUT = 1024
TM_POST = 256
W_IN_ROWS = 64


def _cparams(sem, vmem_mib):
    return pltpu.CompilerParams(dimension_semantics=sem, vmem_limit_bytes=vmem_mib * 1024 * 1024)


def _softplus(x):
    return jnp.maximum(x, 0.0) + jnp.log(1.0 + jnp.exp(-jnp.abs(x)))


def _silu(x):
    return x * jax.nn.sigmoid(x)


def _rms(x, gain):
    ms = jnp.mean(x * x, axis=-1, keepdims=True)
    return x * lax.rsqrt(ms + EPS) * gain


def _ffn_body(x_ref, g_ref, wg_ref, wu_ref, wo_ref, fg_ref, o_ref, xn_ref, acc_ref, *, nf, final):
    f = pl.program_id(1)

    @pl.when(f == 0)
    def _():
        xn_ref[...] = _rms(x_ref[...], g_ref[...]).astype(BF16)
        acc_ref[...] = jnp.zeros_like(acc_ref)

    xn = xn_ref[...]
    gate = jnp.dot(xn, wg_ref[...], preferred_element_type=F32)
    up = jnp.dot(xn, wu_ref[...], preferred_element_type=F32)
    h = (_silu(gate) * up).astype(BF16)
    acc_ref[...] += jnp.dot(h, wo_ref[...], preferred_element_type=F32)

    @pl.when(f == nf - 1)
    def _():
        y = x_ref[...] + 0.5 * acc_ref[...]
        if final:
            y = _rms(y, fg_ref[...])
        o_ref[...] = y


def _ffn(x, gain, w_gu, w_dn, which, final_gain=None):
    m = x.shape[0]
    li, si = which
    nf = D_FF_PAD // TF_FFN
    final = final_gain is not None
    fg = final_gain if final else gain
    return pl.pallas_call(
        functools.partial(_ffn_body, nf=nf, final=final),
        out_shape=jax.ShapeDtypeStruct((m, D), F32),
        grid=(m // TM_FFN, nf),
        in_specs=[
            pl.BlockSpec((TM_FFN, D), lambda i, f: (i, 0)),
            pl.BlockSpec((1, D), lambda i, f: (0, 0)),
            pl.BlockSpec((None, None, D, TF_FFN), lambda i, f: (li, si, 0, f)),
            pl.BlockSpec((None, None, D, TF_FFN), lambda i, f: (li, si, 0, f + nf)),
            pl.BlockSpec((None, None, TF_FFN, D), lambda i, f: (li, si, f, 0)),
            pl.BlockSpec((1, D), lambda i, f: (0, 0)),
        ],
        out_specs=pl.BlockSpec((TM_FFN, D), lambda i, f: (i, 0)),
        scratch_shapes=[pltpu.VMEM((TM_FFN, D), BF16), pltpu.VMEM((TM_FFN, D), F32)],
        compiler_params=_cparams(("parallel", "arbitrary"), 48),
        name="ffn",
    )(x, gain, w_gu, w_gu, w_dn, fg)


def _proj_body(x_ref, g_ref, w_ref, o_ref, xn_ref):
    @pl.when(pl.program_id(1) == 0)
    def _():
        xn_ref[...] = _rms(x_ref[...], g_ref[...]).astype(BF16)

    o_ref[...] = jnp.dot(xn_ref[...], w_ref[...], preferred_element_type=F32)


def _proj(x, gain, w):
    m = x.shape[0]
    n = w.shape[1]
    return pl.pallas_call(
        _proj_body,
        out_shape=jax.ShapeDtypeStruct((m, n), F32),
        grid=(m // TM_PROJ, n // TN_PROJ),
        in_specs=[
            pl.BlockSpec((TM_PROJ, D), lambda i, j: (i, 0)),
            pl.BlockSpec((1, D), lambda i, j: (0, 0)),
            pl.BlockSpec((D, TN_PROJ), lambda i, j: (0, j)),
        ],
        out_specs=pl.BlockSpec((TM_PROJ, TN_PROJ), lambda i, j: (i, j)),
        scratch_shapes=[pltpu.VMEM((TM_PROJ, D), BF16)],
        compiler_params=_cparams(("parallel", "arbitrary"), 40),
        name="proj",
    )(x, gain, w)


def _out2_body(res_ref, ya_ref, yb_ref, wa_ref, wb_ref, o_ref):
    acc = jnp.dot(ya_ref[...].astype(BF16), wa_ref[...], preferred_element_type=F32)
    acc = acc + jnp.dot(yb_ref[...].astype(BF16), wb_ref[...], preferred_element_type=F32)
    o_ref[...] = res_ref[...] + acc


def _out1_body(res_ref, y_ref, w_ref, o_ref):
    o_ref[...] = res_ref[...] + jnp.dot(y_ref[...].astype(BF16), w_ref[...], preferred_element_type=F32)


def _out_proj(res, ys, ws):
    m = res.shape[0]
    body = _out2_body if len(ys) == 2 else _out1_body
    y_spec = pl.BlockSpec((TM_OUT, D), lambda j, i: (i, 0))
    w_spec = pl.BlockSpec((D, TN_OUT), lambda j, i: (0, j))
    r_spec = pl.BlockSpec((TM_OUT, TN_OUT), lambda j, i: (i, j))
    return pl.pallas_call(
        body,
        out_shape=jax.ShapeDtypeStruct((m, D), F32),
        grid=(D // TN_OUT, m // TM_OUT),
        in_specs=[r_spec] + [y_spec] * len(ys) + [w_spec] * len(ws),
        out_specs=r_spec,
        compiler_params=_cparams(("arbitrary", "arbitrary"), 48),
        name="out_proj",
    )(res, *ys, *ws)


def _cast_pad_body(w_ref, o_ref, *, axis, zero_blocks):
    c = pl.program_id(axis)
    is_pad = functools.reduce(jnp.logical_or, [c == z for z in zero_blocks])
    o_ref[...] = jnp.where(is_pad, 0.0, w_ref[...]).astype(BF16)


def _cast_pad_halves_body(w_ref, o_ref):
    rows = o_ref.shape[0]
    zeros = jnp.zeros((rows, D_FF_PAD - D_FF), BF16)
    for half in range(2):
        o_ref[:, half * D_FF_PAD:half * D_FF_PAD + D_FF] = w_ref[:, half * D_FF:(half + 1) * D_FF].astype(BF16)
        o_ref[:, half * D_FF_PAD + D_FF:(half + 1) * D_FF_PAD] = zeros


def _ffn_weights(w_in, w_out):
    n_layers, n_slots = w_in.shape[:2]
    nb = D_FF // LANES
    nbp = D_FF_PAD // LANES
    w_gu = pl.pallas_call(
        _cast_pad_halves_body,
        out_shape=jax.ShapeDtypeStruct((n_layers, n_slots, D, 2 * D_FF_PAD), BF16),
        grid=(n_layers * n_slots, D // W_IN_ROWS),
        in_specs=[pl.BlockSpec((None, None, W_IN_ROWS, 2 * D_FF), lambda q, r: (q // n_slots, q % n_slots, r, 0))],
        out_specs=pl.BlockSpec((None, None, W_IN_ROWS, 2 * D_FF_PAD), lambda q, r: (q // n_slots, q % n_slots, r, 0)),
        compiler_params=_cparams(("parallel", "parallel"), 32),
        name="ffn_w_in",
    )(w_in)
    w_dn = pl.pallas_call(
        functools.partial(_cast_pad_body, axis=1, zero_blocks=tuple(range(nb, nbp))),
        out_shape=jax.ShapeDtypeStruct((n_layers, n_slots, D_FF_PAD, D), BF16),
        grid=(n_layers * n_slots, nbp),
        in_specs=[pl.BlockSpec((None, None, LANES, D),
                               lambda q, r: (q // n_slots, q % n_slots, jnp.minimum(r, nb - 1), 0))],
        out_specs=pl.BlockSpec((None, None, LANES, D), lambda q, r: (q // n_slots, q % n_slots, r, 0)),
        compiler_params=_cparams(("parallel", "parallel"), 32),
        name="ffn_w_out",
    )(w_out)
    return w_gu, w_dn


def _carried_window(buf, cur_ref, init_ref, first, q):
    @pl.when(first)
    def _():
        buf[0:SUBLANES, :] = init_ref[0]

    buf[SUBLANES:SUBLANES + q, :] = cur_ref[...]


def _advance_window(buf, q):
    tail = buf[q:q + SUBLANES, :]
    buf[0:SUBLANES, :] = tail


def _causal_conv(buf, cw_ref, cb_ref, q):
    acc = cb_ref[...] + cw_ref[0:1, :] * buf[pl.ds(SUBLANES - CONV_W + 1, q), :]
    for k in range(1, CONV_W):
        acc = acc + cw_ref[k:k + 1, :] * buf[pl.ds(SUBLANES - CONV_W + 1 + k, q), :]
    return acc


def _cumsum_rows(x):
    n = x.shape[0]
    row = lax.broadcasted_iota(jnp.int32, x.shape, 0)
    s = 1
    while s < n:
        x = x + jnp.where(row >= s, pltpu.roll(x, s, axis=0), 0.0)
        s *= 2
    return x


def _expand_heads(a, rows):
    lane = lax.broadcasted_iota(jnp.int32, (rows, LANES), 1)
    low = lane < HEAD
    pieces = []
    for j in range(N_HEADS // 2):
        e0 = jnp.broadcast_to(a[:, 2 * j:2 * j + 1], (rows, LANES))
        e1 = jnp.broadcast_to(a[:, 2 * j + 1:2 * j + 2], (rows, LANES))
        pieces.append(jnp.where(low, e0, e1))
    return jnp.concatenate(pieces, axis=1)


def _head_allsum(x):
    width = x.shape[-1]
    blk = 4 * HEAD
    r = lax.broadcasted_iota(jnp.int32, (blk, blk), 0) // HEAD
    c = lax.broadcasted_iota(jnp.int32, (blk, blk), 1) // HEAD
    ones_bd = jnp.where(r == c, 1.0, 0.0).astype(BF16)
    hi = x.astype(BF16)
    rem = x - hi.astype(F32)
    mid = rem.astype(BF16)
    lo = (rem - mid.astype(F32)).astype(BF16)
    dot = lambda p: jnp.dot(p, ones_bd, preferred_element_type=F32)
    cols = []
    for j in range(width // blk):
        sl = slice(j * blk, (j + 1) * blk)
        cols.append(dot(hi[:, sl]) + (dot(mid[:, sl]) + dot(lo[:, sl])))
    return jnp.concatenate(cols, axis=1)


def _ssd_body(z_ref, x_ref, bc_ref, dt_ref, cix_ref, cibc_ref, h0_ref,
              cwx_ref, cwbc_ref, cbx_ref, cbbc_ref, dtb_ref, alog_ref, dsk_ref, gn_ref,
              y_ref, ho_ref, xbuf, bcbuf, zbuf, dtbuf, h_s, *, q, nc):
    c = pl.program_id(1)
    padded = q < CHUNK
    if padded:
        xbuf[...] = jnp.zeros_like(xbuf)
        bcbuf[...] = jnp.zeros_like(bcbuf)
        zbuf[...] = jnp.zeros_like(zbuf)
        dtbuf[...] = jnp.zeros_like(dtbuf)
    _carried_window(xbuf, x_ref, cix_ref, c == 0, q)
    _carried_window(bcbuf, bc_ref, cibc_ref, c == 0, q)

    @pl.when(c == 0)
    def _():
        h_s[...] = h0_ref[0]

    row = lax.broadcasted_iota(jnp.int32, (CHUNK, LANES), 0)
    xs = _silu(_causal_conv(xbuf, cwx_ref, cbx_ref, CHUNK))
    bcv = _silu(_causal_conv(bcbuf, cwbc_ref, cbbc_ref, CHUNK))
    if padded:
        zbuf[0:q, :] = z_ref[...]
        dtbuf[0:q, :] = dt_ref[...]
        z = zbuf[...]
        dt_raw = dtbuf[...]
    else:
        _advance_window(xbuf, q)
        _advance_window(bcbuf, q)
        z = z_ref[...]
        dt_raw = dt_ref[...]

    dt = _softplus(dt_raw + dtb_ref[...])
    if padded:
        dt = jnp.where(row < q, dt, 0.0)
    a_head = -jnp.exp(alog_ref[...])
    cum = _cumsum_rows(dt * a_head)
    cum_last = cum[CHUNK - 1:CHUNK, :]
    cum_t = cum.T
    dtx = _expand_heads(dt, CHUNK)
    ecx = _expand_heads(jnp.exp(cum), CHUNK)
    tlx = _expand_heads(jnp.exp(cum_last - cum), CHUNK)
    xdt = xs * dtx
    xdtw = (xdt * tlx).astype(BF16)

    ti = lax.broadcasted_iota(jnp.int32, (CHUNK, CHUNK), 0)
    tj = lax.broadcasted_iota(jnp.int32, (CHUNK, CHUNK), 1)
    causal = ti >= tj
    low = tj < HEAD
    nt = (((1,), (1,)), ((), ()))
    tn = (((0,), (0,)), ((), ()))

    for g in range(N_GROUPS):
        gsl = slice(g * GROUP_W, (g + 1) * GROUP_W)
        bg = bcv[:, g * N_STATE:(g + 1) * N_STATE].astype(BF16)
        cg = bcv[:, (N_GROUPS + g) * N_STATE:(N_GROUPS + g + 1) * N_STATE].astype(BF16)
        cb = lax.dot_general(cg, bg, nt, preferred_element_type=F32)
        hg = h_s[gsl, :]
        y_off = lax.dot_general(cg, hg.astype(BF16), nt, preferred_element_type=F32) * ecx[:, gsl]
        st = lax.dot_general(xdtw[:, gsl], bg, tn, preferred_element_type=F32)
        y_pairs = []
        for j in range(GROUP_W // LANES):
            h0 = g * (GROUP_W // HEAD) + 2 * j
            ms = []
            for h in (h0, h0 + 1):
                seg = cum[:, h:h + 1] - cum_t[h:h + 1, :]
                decay = jnp.exp(jnp.where(causal, seg, -jnp.inf))
                ms.append((cb * decay).astype(BF16))
            psl = slice(h0 * HEAD, (h0 + 2) * HEAD)
            slab = xdt[:, psl]
            rhs = jnp.concatenate([jnp.where(low, slab, 0.0), jnp.where(low, 0.0, slab)], axis=0).astype(BF16)
            y_pairs.append(jnp.dot(jnp.concatenate(ms, axis=1), rhs, preferred_element_type=F32))
            for h in (h0, h0 + 1):
                hsl = slice(h * HEAD, (h + 1) * HEAD)
                dec = jnp.exp(jnp.broadcast_to(cum_last[:, h:h + 1], (HEAD, N_STATE)))
                h_s[hsl, :] = h_s[hsl, :] * dec + st[(h % 8) * HEAD:(h % 8 + 1) * HEAD, :]
        y = jnp.concatenate(y_pairs, axis=1) + y_off + xs[:, gsl] * dsk_ref[:, gsl]
        y = y * _silu(z[:, gsl])
        y = y * lax.rsqrt(jnp.mean(y * y, axis=-1, keepdims=True) + EPS) * gn_ref[:, gsl]
        if padded:
            y_ref[:, gsl] = y[0:q, :]
        else:
            y_ref[:, gsl] = y

    @pl.when(c == nc - 1)
    def _():
        ho_ref[0] = h_s[...]


def _ssd(p, conv_x, conv_bc, h0, prm, nseq, seq_len):
    q = min(CHUNK, seq_len)
    nc = seq_len // q
    rows = lambda w, col: pl.BlockSpec((q, w), lambda s, c: (s * nc + c, col))
    per_seq = lambda shape: pl.BlockSpec((1,) + shape, lambda s, c: (s,) + (0,) * len(shape))
    const = lambda shape: pl.BlockSpec(shape, lambda s, c: (0,) * len(shape))
    y, h_last = pl.pallas_call(
        functools.partial(_ssd_body, q=q, nc=nc),
        out_shape=(jax.ShapeDtypeStruct((nseq * seq_len, D), F32),
                   jax.ShapeDtypeStruct((nseq, D, N_STATE), F32)),
        grid=(nseq, nc),
        in_specs=[
            rows(D, COL_Z // D), rows(D, COL_X // D), rows(2 * N_GROUPS * N_STATE, COL_BC // 1024),
            rows(LANES, COL_DT // LANES),
            per_seq((SUBLANES, D)), per_seq((SUBLANES, 1024)), per_seq((D, N_STATE)),
            const((CONV_W, D)), const((CONV_W, 1024)), const((1, D)), const((1, 1024)),
            const((1, LANES)), const((1, LANES)), const((1, D)), const((1, D)),
        ],
        out_specs=(pl.BlockSpec((q, D), lambda s, c: (s * nc + c, 0)), per_seq((D, N_STATE))),
        scratch_shapes=[
            pltpu.VMEM((CHUNK + SUBLANES, D), F32), pltpu.VMEM((CHUNK + SUBLANES, 1024), F32),
            pltpu.VMEM((CHUNK, D), F32), pltpu.VMEM((CHUNK, LANES), F32),
            pltpu.VMEM((D, N_STATE), F32),
        ],
        compiler_params=_cparams(("parallel", "arbitrary"), 40),
        name="ssd",
    )(p, p, p, p, conv_x, conv_bc, h0, *prm)
    return y, h_last


def _rwkv_pre_body(rkv_ref, lora_ref, si_rkv_ref, si_lora_ref, mu_rkv_ref, mu_lora_ref,
                   w0_ref, w2_ref, a0_ref, a2_ref, g2_ref, kk_ref, ka_ref, rk_ref,
                   r_out, w_out, k_out, v_out, kk_out, b_out, bonus_out, g_out,
                   rkvbuf, lorabuf, *, q, log_decay):
    c = pl.program_id(1)
    _carried_window(rkvbuf, rkv_ref, si_rkv_ref, c == 0, q)
    _carried_window(lorabuf, lora_ref, si_lora_ref, c == 0, q)

    def shifted(buf, cur_ref, mu_ref):
        cur = cur_ref[...]
        prev = buf[pl.ds(SUBLANES - 1, q), :]
        return cur + mu_ref[...] * (prev - cur)

    ps = shifted(rkvbuf, rkv_ref, mu_rkv_ref)
    lo_in = shifted(lorabuf, lora_ref, mu_lora_ref)
    _advance_window(rkvbuf, q)
    _advance_window(lorabuf, q)
    r = ps[:, 0:D]
    k = ps[:, D:2 * D]
    v = ps[:, 2 * D:3 * D]

    lw = jnp.dot(jnp.tanh(lo_in).astype(BF16), w2_ref[...], preferred_element_type=F32)
    la = jnp.dot(lo_in.astype(BF16), a2_ref[...], preferred_element_type=F32)
    g = jnp.dot(jax.nn.sigmoid(lo_in).astype(BF16), g2_ref[...], preferred_element_type=F32)
    wlog = -_softplus(-(w0_ref[...] + lw)) - 0.5
    log_w = -jnp.exp(wlog)
    a = jax.nn.sigmoid(a0_ref[...] + la)
    kkf = k * kk_ref[...]
    norm = jnp.maximum(jnp.sqrt(_head_allsum(kkf * kkf)), 1e-12)
    kk = kkf / norm
    k2 = k * (1.0 + (a - 1.0) * ka_ref[...])
    bonus = _head_allsum(r * k2 * rk_ref[...]) * v
    r_out[...] = r
    w_out[...] = log_w if log_decay else jnp.exp(log_w)
    k_out[...] = k2
    v_out[...] = v
    kk_out[...] = kk
    b_out[...] = kk * a
    bonus_out[...] = bonus
    g_out[...] = g


def _rwkv_pre(p, shift_rkv, shift_lora, prm, nseq, seq_len, log_decay):
    q = min(CHUNK, seq_len)
    nc = seq_len // q
    per_seq = lambda shape: pl.BlockSpec((1,) + shape, lambda s, c: (s,) + (0,) * len(shape))
    const = lambda shape: pl.BlockSpec(shape, lambda s, c: (0,) * len(shape))
    tile = pl.BlockSpec((q, D), lambda s, c: (s * nc + c, 0))
    sds = jax.ShapeDtypeStruct((nseq * seq_len, D), F32)
    return pl.pallas_call(
        functools.partial(_rwkv_pre_body, q=q, log_decay=log_decay),
        out_shape=(sds,) * 8,
        grid=(nseq, nc),
        in_specs=[
            pl.BlockSpec((q, 3 * D), lambda s, c: (s * nc + c, 0)),
            pl.BlockSpec((q, LORA_W), lambda s, c: (s * nc + c, COL_LORA // LORA_W)),
            per_seq((SUBLANES, 3 * D)), per_seq((SUBLANES, LORA_W)),
            const((1, 3 * D)), const((1, LORA_W)),
            const((1, D)), const((LORA_W, D)), const((1, D)), const((LORA_W, D)), const((LORA_W, D)),
            const((1, D)), const((1, D)), const((1, D)),
        ],
        out_specs=(tile,) * 8,
        scratch_shapes=[pltpu.VMEM((CHUNK + SUBLANES, 3 * D), F32), pltpu.VMEM((CHUNK + SUBLANES, LORA_W), F32)],
        compiler_params=_cparams(("parallel", "arbitrary"), 48),
        name="rwkv_pre",
    )(p, p, shift_rkv, shift_lora, *prm)


def _split_bf16(x):
    hi = x.astype(BF16)
    return hi, (x - hi.astype(F32)).astype(BF16)


_NN = (((1,), (0,)), ((), ()))
_NT = (((1,), (1,)), ((), ()))
_TN = (((0,), (0,)), ((), ()))


def _solve_unit_lower(n, rhs):
    rows, width = rhs.shape
    nblk, ncol = rows // SUBLANES, width // LANES
    nb = [[n[SUBLANES * i:SUBLANES * (i + 1), LANES * j:LANES * (j + 1)] for j in range(ncol)] for i in range(nblk)]
    xb = [[rhs[SUBLANES * i:SUBLANES * (i + 1), LANES * j:LANES * (j + 1)] for j in range(ncol)] for i in range(nblk)]
    low = lax.broadcasted_iota(jnp.int32, (SUBLANES, LANES), 1) < HEAD
    for s in range(rows - 1):
        i0, r0 = divmod(s, SUBLANES)
        idx = jnp.where(low, s, HEAD + s)
        for j in range(ncol):
            row = xb[i0][j][r0:r0 + 1, :]
            for i in range(i0 if r0 < SUBLANES - 1 else i0 + 1, nblk):
                xb[i][j] = xb[i][j] - jnp.take_along_axis(nb[i][j], idx, axis=1) * row
    return jnp.concatenate([jnp.concatenate(xr, axis=1) for xr in xb], axis=0)


def _wkv_tile(r, lw, k, v, kk, b, st):
    rows, width = r.shape
    nh = WKV_LANES // HEAD
    groups = [slice(g * WKV_LANES, (g + 1) * WKV_LANES) for g in range(width // WKV_LANES)]
    cl = _cumsum_rows(lw)
    cl_last = cl[rows - 1:rows, :]
    p_inv = jnp.exp(-cl)
    p_end = jnp.exp(cl_last - cl)
    x2h, x2l = _split_bf16(jnp.concatenate([kk * jnp.exp(cl - lw), r * jnp.exp(cl)], axis=0))
    k_hat = k * p_inv
    b_hat = b * p_inv
    k_end = k * p_end
    b_end = -(b * p_end)

    bd_r = lax.broadcasted_iota(jnp.int32, (nh * rows, WKV_LANES), 0) // rows
    bd_c = lax.broadcasted_iota(jnp.int32, (nh * rows, WKV_LANES), 1) // HEAD
    bd_mask = jnp.where(bd_r == bd_c, 1.0, 0.0).astype(BF16)

    def per_head_rows(y):
        return [jnp.concatenate([part] * nh, axis=0) * bd_mask for part in _split_bf16(y)]

    def dot3(ah, al, bh, bl, dims):
        dg = lambda x, y: lax.dot_general(x, y, dims, preferred_element_type=F32)
        return dg(ah, bh) + (dg(ah, bl) + dg(al, bh))

    t_i = lax.broadcasted_iota(jnp.int32, (rows, WKV_LANES), 0)
    lane = lax.broadcasted_iota(jnp.int32, (rows, WKV_LANES), 1)
    s_i = lane & (HEAD - 1)
    strict = t_i > s_i
    incl = t_i >= s_i
    head_of_lane = lane // HEAD

    a_kb, a_rb, base, o_part = [], [], [], []
    for gs in groups:
        ak = dot3(x2h[:, gs], x2l[:, gs], *per_head_rows(k_hat[:, gs]), _NT)
        ab = dot3(x2h[:, gs], x2l[:, gs], *per_head_rows(b_hat[:, gs]), _NT)
        a_k = jnp.concatenate([jnp.where(strict, ak[0:rows], 0.0), jnp.where(incl, ak[rows:], 0.0)], axis=0)
        xs = dot3(x2h[:, gs], x2l[:, gs], *per_head_rows(st[:, gs]), _NN)
        av = dot3(*_split_bf16(a_k), *per_head_rows(v[:, gs]), _NN)
        a_kb.append(jnp.where(strict, ab[0:rows], 0.0))
        a_rb.append(jnp.where(incl, ab[rows:], 0.0))
        base.append(xs[0:rows] + av[0:rows])
        o_part.append(xs[rows:] + av[rows:])
    sa = _solve_unit_lower(jnp.concatenate(a_kb, axis=1), jnp.concatenate(base, axis=1))

    o, st_new = [], []
    for g, gs in enumerate(groups):
        o.append(o_part[g] - dot3(*_split_bf16(a_rb[g]), *per_head_rows(sa[:, gs]), _NN))
        eye = jnp.where(t_i == s_i, jnp.exp(cl_last[:, gs]), 0.0)
        lhs = jnp.concatenate([k_end[:, gs], b_end[:, gs], eye], axis=0)
        rhs = jnp.concatenate([v[:, gs], sa[:, gs], st[:, gs]], axis=0)
        full = dot3(*_split_bf16(lhs), *_split_bf16(rhs), _TN)
        acc = jnp.where(head_of_lane == 0, full[0:HEAD, :], 0.0)
        for h in range(1, nh):
            acc = acc + jnp.where(head_of_lane == h, full[h * HEAD:(h + 1) * HEAD, :], 0.0)
        st_new.append(acc)
    return jnp.concatenate(o, axis=1), jnp.concatenate(st_new, axis=1)


def _wkv_long_body(r_ref, lw_ref, k_ref, v_ref, kk_ref, b_ref, o_ref, so_ref, st_s, *, nc):
    c = pl.program_id(2)

    @pl.when(c == 0)
    def _():
        st_s[...] = jnp.zeros_like(st_s)

    o, st_new = _wkv_tile(r_ref[...], lw_ref[...], k_ref[...], v_ref[...], kk_ref[...], b_ref[...], st_s[...])
    o_ref[...] = o
    st_s[...] = st_new

    @pl.when(c == nc - 1)
    def _():
        for j in range(WKV_GROUPS * WKV_LANES // LANES):
            js = slice(j * LANES, (j + 1) * LANES)
            sq = jnp.concatenate([st_s[:, js], jnp.zeros((LANES - HEAD, LANES), F32)], axis=0)
            so_ref[0, js, :] = sq.T[:, 0:HEAD]


def _wkv_long(r, lw, k, v, kk, b, nseq, seq_len):
    nc = seq_len // WKV_CHUNK
    width = WKV_GROUPS * WKV_LANES
    tile = pl.BlockSpec((WKV_CHUNK, width), lambda s, hg, c: (s * nc + c, hg))
    o, s_last = pl.pallas_call(
        functools.partial(_wkv_long_body, nc=nc),
        out_shape=(jax.ShapeDtypeStruct((nseq * seq_len, D), F32),
                   jax.ShapeDtypeStruct((nseq, D, HEAD), F32)),
        grid=(nseq, D // width, nc),
        in_specs=[tile] * 6,
        out_specs=(tile, pl.BlockSpec((1, width, HEAD), lambda s, hg, c: (s, hg, 0))),
        scratch_shapes=[pltpu.VMEM((HEAD, width), F32)],
        compiler_params=_cparams(("parallel", "parallel", "arbitrary"), 32),
        name="wkv_long",
    )(r, lw, k, v, kk, b)
    return o, s_last


def _wkv_short_body(r_ref, w_ref, k_ref, v_ref, kk_ref, b_ref, s0_ref, o_ref, so_ref, *, steps):
    def per_v(vi, carry):
        s = s0_ref[0, vi]
        for t in range(steps):
            vrow = v_ref[t, 0, pl.ds(vi, 1), :]
            sa = jnp.sum(s * kk_ref[t, 0], axis=0, keepdims=True)
            s = s * w_ref[t, 0] - b_ref[t, 0] * sa + k_ref[t, 0] * vrow
            o_ref[t, 0, pl.ds(vi, 1), :] = jnp.sum(s * r_ref[t, 0], axis=0, keepdims=True)
        so_ref[0, vi] = s
        return carry

    lax.fori_loop(0, HEAD, per_v, 0)


def _wkv_short(r, w, k, v, kk, b, s0, nseq, seq_len):
    vec = pl.BlockSpec((seq_len, 1, HEAD, nseq), lambda h: (0, h, 0, 0))
    st = pl.BlockSpec((1, HEAD, HEAD, nseq), lambda h: (h, 0, 0, 0))
    return pl.pallas_call(
        functools.partial(_wkv_short_body, steps=seq_len),
        out_shape=(jax.ShapeDtypeStruct((seq_len, N_HEADS, HEAD, nseq), F32),
                   jax.ShapeDtypeStruct((N_HEADS, HEAD, HEAD, nseq), F32)),
        grid=(N_HEADS,),
        in_specs=[vec] * 6 + [st],
        out_specs=(vec, st),
        compiler_params=_cparams(("parallel",), 32),
        name="wkv_short",
    )(r, w, k, v, kk, b, s0)


def _rwkv_post_body(o_ref, bonus_ref, g_ref, lnw_ref, lnb_ref, y_ref):
    o = o_ref[...]
    mean = _head_allsum(o) * (1.0 / HEAD)
    cen = o - mean
    var = _head_allsum(cen * cen) * (1.0 / HEAD)
    on = cen * lax.rsqrt(var + GN_EPS) * lnw_ref[...] + lnb_ref[...]
    y_ref[...] = (on + bonus_ref[...]) * g_ref[...]


def _rwkv_post(o, bonus, g, ln_w, ln_b):
    m = o.shape[0]
    tile = pl.BlockSpec((TM_POST, D), lambda i: (i, 0))
    const = pl.BlockSpec((1, D), lambda i: (0, 0))
    return pl.pallas_call(
        _rwkv_post_body,
        out_shape=jax.ShapeDtypeStruct((m, D), F32),
        grid=(m // TM_POST,),
        in_specs=[tile, tile, tile, const, const],
        out_specs=tile,
        compiler_params=_cparams(("parallel",), 32),
        name="rwkv_post",
    )(o, bonus, g, ln_w, ln_b)


def _lru_body(gate_ref, x_ref, ci_ref, h0_ref, cw_ref, cb_ref, wga_ref, bga_ref, wgx_ref, bgx_ref, lam_ref,
              y_ref, ho_ref, xbuf, a_s, b_s, h_s, hc_s, *, q, nc):
    c = pl.program_id(1)
    _carried_window(xbuf, x_ref, ci_ref, c == 0, q)

    @pl.when(c == 0)
    def _():
        hc_s[...] = h0_ref[0]

    xc = _causal_conv(xbuf, cw_ref, cb_ref, q)
    _advance_window(xbuf, q)
    ra, rx = [], []
    for blk in range(LRU_BLOCKS):
        xh = xc[:, blk * LRU_BLK:(blk + 1) * LRU_BLK].astype(BF16)
        ra.append(jnp.dot(xh, wga_ref[blk], preferred_element_type=F32))
        rx.append(jnp.dot(xh, wgx_ref[blk], preferred_element_type=F32))
    rg = jax.nn.sigmoid(jnp.concatenate(ra, axis=1) + bga_ref[...])
    ig = jax.nn.sigmoid(jnp.concatenate(rx, axis=1) + bgx_ref[...])
    log_a = -LRU_C * rg * _softplus(-lam_ref[...])
    a_s[...] = jnp.exp(log_a)
    b_s[...] = jnp.sqrt(1.0 - jnp.exp(2.0 * log_a)) * (ig * xc)

    def step(t, h):
        h = a_s[pl.ds(t, 1), :] * h + b_s[pl.ds(t, 1), :]
        h_s[pl.ds(t, 1), :] = h
        return h

    h = lax.fori_loop(0, q, step, hc_s[...])
    hc_s[...] = h
    y_ref[...] = h_s[...] * jax.nn.gelu(gate_ref[...])

    @pl.when(c == nc - 1)
    def _():
        ho_ref[0] = h


def _lru(p, conv_init, h0, prm, nseq, seq_len):
    q = min(CHUNK, seq_len)
    nc = seq_len // q
    per_seq = lambda shape: pl.BlockSpec((1,) + shape, lambda s, c: (s,) + (0,) * len(shape))
    const = lambda shape: pl.BlockSpec(shape, lambda s, c: (0,) * len(shape))
    return pl.pallas_call(
        functools.partial(_lru_body, q=q, nc=nc),
        out_shape=(jax.ShapeDtypeStruct((nseq * seq_len, D), F32),
                   jax.ShapeDtypeStruct((nseq, 1, D), F32)),
        grid=(nseq, nc),
        in_specs=[
            pl.BlockSpec((q, D), lambda s, c: (s * nc + c, 0)),
            pl.BlockSpec((q, D), lambda s, c: (s * nc + c, 1)),
            per_seq((SUBLANES, D)), per_seq((1, D)),
            const((CONV_W, D)), const((1, D)),
            const((LRU_BLOCKS, LRU_BLK, LRU_BLK)), const((1, D)),
            const((LRU_BLOCKS, LRU_BLK, LRU_BLK)), const((1, D)), const((1, D)),
        ],
        out_specs=(pl.BlockSpec((q, D), lambda s, c: (s * nc + c, 0)), per_seq((1, D))),
        scratch_shapes=[pltpu.VMEM((CHUNK + SUBLANES, D), F32), pltpu.VMEM((q, D), F32), pltpu.VMEM((q, D), F32),
                        pltpu.VMEM((q, D), F32), pltpu.VMEM((1, D), F32)],
        compiler_params=_cparams(("parallel", "arbitrary"), 32),
        name="lru",
    )(p, p, conv_init, h0, *prm)


def _pad_front_rows(buf):
    return jnp.pad(buf, ((0, 0), (SUBLANES - buf.shape[1], 0), (0, 0)))


def _row2(v):
    return v.reshape(1, -1)


def kernel(x_prompt, x_sample, state_ssm_a, state_conv_a, state_wkv_b, state_shift_b, state_lru_c, state_conv_c, norm_gain, w_ffn_in, w_ffn_out, w_in_ab, conv_w_a, conv_b_a, dt_bias_a, a_log_a, d_skip_a, gnorm_a, mu_b, w0_b, w2_b, a0_b, a2_b, g2_b, k_k_b, k_a_b, r_k_b, ln_w_b, ln_b_b, w_out_ab, w_in_c, conv_w_c, conv_b_c, w_gate_a_c, b_gate_a_c, w_gate_x_c, b_gate_x_c, lambda_c, w_out_c, final_norm_gain):
    w_gu, w_dn = _ffn_weights(w_ffn_in, w_ffn_out)

    in_a = D + (D + 2 * N_GROUPS * N_STATE) + N_HEADS
    wab = w_in_ab[0]
    bc_w = 2 * N_GROUPS * N_STATE
    w_proj0 = jnp.concatenate([
        wab[:, in_a:in_a + 3 * D],
        wab[:, 0:D],
        wab[:, D:2 * D],
        wab[:, 2 * D:2 * D + bc_w],
        wab[:, in_a + 3 * D:in_a + 3 * D + LORA_W],
        wab[:, in_a - N_HEADS:in_a],
        jnp.zeros((D, PROJ_W - COL_DT - N_HEADS), F32),
    ], axis=1).astype(BF16)
    w_proj1 = w_in_c[0].astype(BF16)
    w_out0 = w_out_ab[0].astype(BF16)
    w_out1 = w_out_c[0].astype(BF16)

    pad_lanes = lambda v: jnp.pad(v.reshape(1, -1), ((0, 0), (0, LANES - v.shape[-1])))
    rep_head = lambda v: jnp.repeat(v, HEAD).reshape(1, D)
    ssd_prm = (conv_w_a[0][:, :D], conv_w_a[0][:, D:], _row2(conv_b_a[0][:D]), _row2(conv_b_a[0][D:]),
               pad_lanes(dt_bias_a[0]), pad_lanes(a_log_a[0]), rep_head(d_skip_a[0]), _row2(gnorm_a[0]))
    lora_rows = lambda w, lo: jnp.pad(w, ((lo, LORA_W - lo - w.shape[0]), (0, 0))).astype(BF16)
    mu = mu_b[0]
    rwkv_prm = (_row2(mu[:3 * D]), _row2(mu[3 * D:]),
                _row2(w0_b[0]), lora_rows(w2_b[0], 0), _row2(a0_b[0]), lora_rows(a2_b[0], 64),
                lora_rows(g2_b[0], 128), _row2(k_k_b[0]), _row2(k_a_b[0]), _row2(r_k_b[0]))
    lru_prm = (conv_w_c[0], _row2(conv_b_c[0]), w_gate_a_c[0].astype(BF16), _row2(b_gate_a_c[0]),
               w_gate_x_c[0].astype(BF16), _row2(b_gate_x_c[0]), _row2(lambda_c[0]))

    def trunk(x3, ssm0, conva0, wkv0, shift0, lru0, convc0):
        nseq, seq_len, _ = x3.shape
        x = x3.reshape(nseq * seq_len, D)
        tail = lambda arr, n: arr.reshape(nseq, seq_len, arr.shape[-1])[:, seq_len - n:, :]

        x = _ffn(x, _row2(norm_gain[0, 0]), w_gu, w_dn, (0, 0))
        p = _proj(x, _row2(norm_gain[0, 1]), w_proj0)
        conv_x = _pad_front_rows(conva0[:, :, :D])
        conv_bc = _pad_front_rows(conva0[:, :, D:])
        ya, ssm_n = _ssd(p, conv_x, conv_bc, ssm0.reshape(nseq, D, N_STATE), ssd_prm, nseq, seq_len)
        p_tail = tail(p, CONV_W - 1)
        conva_n = jnp.concatenate([p_tail[:, :, COL_X:COL_X + D], p_tail[:, :, COL_BC:COL_BC + bc_w]], axis=-1)
        shift_n = jnp.concatenate([p_tail[:, -1:, :3 * D], p_tail[:, -1:, COL_LORA:COL_LORA + LORA_W]], axis=-1)

        r, w, k, v, kk, b, bonus, g = _rwkv_pre(p, _pad_front_rows(shift0[:, :, :3 * D]),
                                                _pad_front_rows(shift0[:, :, 3 * D:]), rwkv_prm, nseq, seq_len,
                                                log_decay=wkv0 is None)
        if wkv0 is None:
            o, wkv_n = _wkv_long(r, w, k, v, kk, b, nseq, seq_len)
            wkv_n = wkv_n.reshape(nseq, N_HEADS, HEAD, HEAD)
        else:
            to_lanes = lambda a: a.reshape(nseq, seq_len, N_HEADS, HEAD).transpose(1, 2, 3, 0)
            o, wkv_n = _wkv_short(*[to_lanes(a) for a in (r, w, k, v, kk, b)], wkv0.transpose(1, 2, 3, 0),
                                  nseq, seq_len)
            o = o.transpose(3, 0, 1, 2).reshape(nseq * seq_len, D)
            wkv_n = wkv_n.transpose(3, 0, 1, 2)
        yb = _rwkv_post(o, bonus, g, _row2(ln_w_b[0]), _row2(ln_b_b[0]))
        x = _out_proj(x, (ya, yb), (w_out0[:D], w_out0[D:]))
        x = _ffn(x, _row2(norm_gain[0, 2]), w_gu, w_dn, (0, 1))

        x = _ffn(x, _row2(norm_gain[1, 0]), w_gu, w_dn, (1, 0))
        pc = _proj(x, _row2(norm_gain[1, 1]), w_proj1)
        yc, lru_n = _lru(pc, _pad_front_rows(convc0), lru0.reshape(nseq, 1, D), lru_prm, nseq, seq_len)
        convc_n = tail(pc, CONV_W - 1)[:, :, D:]
        x = _out_proj(x, (yc,), (w_out1,))
        y = _ffn(x, _row2(norm_gain[1, 2]), w_gu, w_dn, (1, 1), final_gain=_row2(final_norm_gain))

        return (y.reshape(nseq, seq_len, D), ssm_n.reshape(1, nseq, N_HEADS, HEAD, N_STATE), conva_n[None],
                wkv_n[None], shift_n[None], lru_n.reshape(1, nseq, D), convc_n[None])

    bp = x_prompt.shape[0]
    zeros = lambda s: jnp.zeros((bp,) + s.shape[2:], F32)
    outs_p = trunk(x_prompt, zeros(state_ssm_a), zeros(state_conv_a), None, zeros(state_shift_b),
                   zeros(state_lru_c), zeros(state_conv_c))
    outs_s = trunk(x_sample, state_ssm_a[0], state_conv_a[0], state_wkv_b[0], state_shift_b[0],
                   state_lru_c[0], state_conv_c[0])
    return (outs_p[0], outs_s[0]) + outs_p[1:] + outs_s[1:]
```

```python
import functools

import jax
import jax.numpy as jnp
from jax import lax
from jax.experimental import pallas as pl
from jax.experimental.pallas import tpu as pltpu

F32 = jnp.float32
BF16 = jnp.bfloat16

D = 2048
D_FF = 5504
D_FF_PAD = 5632
HEAD = 64
N_HEADS = 32
N_GROUPS = 4
GROUP_W = D // N_GROUPS
N_STATE = 128
CONV_W = 4
LORA_W = 256
LRU_BLOCKS = 8
LRU_BLK = D // LRU_BLOCKS
LRU_C = 8.0
EPS = 1e-6
GN_EPS = 64e-5
SUBLANES = 8
LANES = 128
CHUNK = 128
WKV_CHUNK = 64
WKV_LANES = 256
WKV_GROUPS = 4
WKV_SHORT_ROWS = 4

COL_RKV = 0
COL_Z = 3 * D
COL_X = 4 * D
COL_BC = 5 * D
COL_LORA = 5 * D + 2 * N_GROUPS * N_STATE
COL_DT = COL_LORA + LORA_W
PROJ_W = COL_DT + 256

TM_FFN = 512
TF_FFN = 512
TM_PROJ = 1024
TN_PROJ = 512
TM_OUT = 512
TN_OUT = 1024
TM_POST = 256
W_IN_ROWS = 64


def _cparams(sem, vmem_mib):
    return pltpu.CompilerParams(dimension_semantics=sem, vmem_limit_bytes=vmem_mib * 1024 * 1024)


def _softplus(x):
    return jnp.maximum(x, 0.0) + jnp.log(1.0 + jnp.exp(-jnp.abs(x)))


def _silu(x):
    return x * jax.nn.sigmoid(x)


def _rms(x, gain):
    ms = jnp.mean(x * x, axis=-1, keepdims=True)
    return x * lax.rsqrt(ms + EPS) * gain


def _ffn_body(x_ref, g_ref, wg_ref, wu_ref, wo_ref, fg_ref, o_ref, xn_ref, acc_ref, *, nf, final):
    f = pl.program_id(1)

    @pl.when(f == 0)
    def _():
        xn_ref[...] = _rms(x_ref[...], g_ref[...]).astype(BF16)
        acc_ref[...] = jnp.zeros_like(acc_ref)

    xn = xn_ref[...]
    gate = jnp.dot(xn, wg_ref[...], preferred_element_type=F32)
    up = jnp.dot(xn, wu_ref[...], preferred_element_type=F32)
    h = (_silu(gate) * up).astype(BF16)
    acc_ref[...] += jnp.dot(h, wo_ref[...], preferred_element_type=F32)

    @pl.when(f == nf - 1)
    def _():
        y = x_ref[...] + 0.5 * acc_ref[...]
        if final:
            y = _rms(y, fg_ref[...])
        o_ref[...] = y


def _ffn(x, gain, w_gu, w_dn, which, final_gain=None):
    m = x.shape[0]
    li, si = which
    nf = D_FF_PAD // TF_FFN
    final = final_gain is not None
    fg = final_gain if final else gain
    return pl.pallas_call(
        functools.partial(_ffn_body, nf=nf, final=final),
        out_shape=jax.ShapeDtypeStruct((m, D), F32),
        grid=(m // TM_FFN, nf),
        in_specs=[
            pl.BlockSpec((TM_FFN, D), lambda i, f: (i, 0)),
            pl.BlockSpec((1, D), lambda i, f: (0, 0)),
            pl.BlockSpec((None, None, D, TF_FFN), lambda i, f: (li, si, 0, f)),
            pl.BlockSpec((None, None, D, TF_FFN), lambda i, f: (li, si, 0, f + nf)),
            pl.BlockSpec((None, None, TF_FFN, D), lambda i, f: (li, si, f, 0)),
            pl.BlockSpec((1, D), lambda i, f: (0, 0)),
        ],
        out_specs=pl.BlockSpec((TM_FFN, D), lambda i, f: (i, 0)),
        scratch_shapes=[pltpu.VMEM((TM_FFN, D), BF16), pltpu.VMEM((TM_FFN, D), F32)],
        compiler_params=_cparams(("parallel", "arbitrary"), 48),
        name="ffn",
    )(x, gain, w_gu, w_gu, w_dn, fg)


def _proj_body(x_ref, g_ref, w_ref, o_ref, xn_ref):
    @pl.when(pl.program_id(1) == 0)
    def _():
        xn_ref[...] = _rms(x_ref[...], g_ref[...]).astype(BF16)

    o_ref[...] = jnp.dot(xn_ref[...], w_ref[...], preferred_element_type=F32)


def _proj(x, gain, w):
    m = x.shape[0]
    n = w.shape[1]
    return pl.pallas_call(
        _proj_body,
        out_shape=jax.ShapeDtypeStruct((m, n), F32),
        grid=(m // TM_PROJ, n // TN_PROJ),
        in_specs=[
            pl.BlockSpec((TM_PROJ, D), lambda i, j: (i, 0)),
            pl.BlockSpec((1, D), lambda i, j: (0, 0)),
            pl.BlockSpec((D, TN_PROJ), lambda i, j: (0, j)),
        ],
        out_specs=pl.BlockSpec((TM_PROJ, TN_PROJ), lambda i, j: (i, j)),
        scratch_shapes=[pltpu.VMEM((TM_PROJ, D), BF16)],
        compiler_params=_cparams(("parallel", "arbitrary"), 40),
        name="proj",
    )(x, gain, w)


def _out2_body(res_ref, ya_ref, yb_ref, wa_ref, wb_ref, o_ref):
    acc = jnp.dot(ya_ref[...].astype(BF16), wa_ref[...], preferred_element_type=F32)
    acc = acc + jnp.dot(yb_ref[...].astype(BF16), wb_ref[...], preferred_element_type=F32)
    o_ref[...] = res_ref[...] + acc


def _out1_body(res_ref, y_ref, w_ref, o_ref):
    o_ref[...] = res_ref[...] + jnp.dot(y_ref[...].astype(BF16), w_ref[...], preferred_element_type=F32)


def _out_proj(res, ys, ws):
    m = res.shape[0]
    body = _out2_body if len(ys) == 2 else _out1_body
    y_spec = pl.BlockSpec((TM_OUT, D), lambda j, i: (i, 0))
    w_spec = pl.BlockSpec((D, TN_OUT), lambda j, i: (0, j))
    r_spec = pl.BlockSpec((TM_OUT, TN_OUT), lambda j, i: (i, j))
    return pl.pallas_call(
        body,
        out_shape=jax.ShapeDtypeStruct((m, D), F32),
        grid=(D // TN_OUT, m // TM_OUT),
        in_specs=[r_spec] + [y_spec] * len(ys) + [w_spec] * len(ws),
        out_specs=r_spec,
        compiler_params=_cparams(("arbitrary", "arbitrary"), 48),
        name="out_proj",
    )(res, *ys, *ws)


def _cast_pad_body(w_ref, o_ref, *, axis, zero_blocks):
    c = pl.program_id(axis)
    is_pad = functools.reduce(jnp.logical_or, [c == z for z in zero_blocks])
    o_ref[...] = jnp.where(is_pad, 0.0, w_ref[...]).astype(BF16)


def _cast_pad_halves_body(w_ref, o_ref):
    rows = o_ref.shape[0]
    zeros = jnp.zeros((rows, D_FF_PAD - D_FF), BF16)
    for half in range(2):
        o_ref[:, half * D_FF_PAD:half * D_FF_PAD + D_FF] = w_ref[:, half * D_FF:(half + 1) * D_FF].astype(BF16)
        o_ref[:, half * D_FF_PAD + D_FF:(half + 1) * D_FF_PAD] = zeros


def _ffn_weights(w_in, w_out):
    n_layers, n_slots = w_in.shape[:2]
    nb = D_FF // LANES
    nbp = D_FF_PAD // LANES
    w_gu = pl.pallas_call(
        _cast_pad_halves_body,
        out_shape=jax.ShapeDtypeStruct((n_layers, n_slots, D, 2 * D_FF_PAD), BF16),
        grid=(n_layers * n_slots, D // W_IN_ROWS),
        in_specs=[pl.BlockSpec((None, None, W_IN_ROWS, 2 * D_FF), lambda q, r: (q // n_slots, q % n_slots, r, 0))],
        out_specs=pl.BlockSpec((None, None, W_IN_ROWS, 2 * D_FF_PAD), lambda q, r: (q // n_slots, q % n_slots, r, 0)),
        compiler_params=_cparams(("parallel", "parallel"), 32),
        name="ffn_w_in",
    )(w_in)
    w_dn = pl.pallas_call(
        functools.partial(_cast_pad_body, axis=1, zero_blocks=tuple(range(nb, nbp))),
        out_shape=jax.ShapeDtypeStruct((n_layers, n_slots, D_FF_PAD, D), BF16),
        grid=(n_layers * n_slots, nbp),
        in_specs=[pl.BlockSpec((None, None, LANES, D),
                               lambda q, r: (q // n_slots, q % n_slots, jnp.minimum(r, nb - 1), 0))],
        out_specs=pl.BlockSpec((None, None, LANES, D), lambda q, r: (q // n_slots, q % n_slots, r, 0)),
        compiler_params=_cparams(("parallel", "parallel"), 32),
        name="ffn_w_out",
    )(w_out)
    return w_gu, w_dn


def _carried_window(buf, cur_ref, init_ref, first, q):
    @pl.when(first)
    def _():
        buf[0:SUBLANES, :] = init_ref[0]

    buf[SUBLANES:SUBLANES + q, :] = cur_ref[...]


def _advance_window(buf, q):
    tail = buf[q:q + SUBLANES, :]
    buf[0:SUBLANES, :] = tail


def _causal_conv(buf, cw_ref, cb_ref, q):
    acc = cb_ref[...] + cw_ref[0:1, :] * buf[pl.ds(SUBLANES - CONV_W + 1, q), :]
    for k in range(1, CONV_W):
        acc = acc + cw_ref[k:k + 1, :] * buf[pl.ds(SUBLANES - CONV_W + 1 + k, q), :]
    return acc


def _cumsum_rows(x):
    n = x.shape[0]
    row = lax.broadcasted_iota(jnp.int32, x.shape, 0)
    s = 1
    while s < n:
        x = x + jnp.where(row >= s, pltpu.roll(x, s, axis=0), 0.0)
        s *= 2
    return x


def _expand_heads(a, rows):
    lane = lax.broadcasted_iota(jnp.int32, (rows, LANES), 1)
    low = lane < HEAD
    pieces = []
    for j in range(N_HEADS // 2):
        e0 = jnp.broadcast_to(a[:, 2 * j:2 * j + 1], (rows, LANES))
        e1 = jnp.broadcast_to(a[:, 2 * j + 1:2 * j + 2], (rows, LANES))
        pieces.append(jnp.where(low, e0, e1))
    return jnp.concatenate(pieces, axis=1)


def _head_allsum(x):
    width = x.shape[-1]
    blk = 4 * HEAD
    r = lax.broadcasted_iota(jnp.int32, (blk, blk), 0) // HEAD
    c = lax.broadcasted_iota(jnp.int32, (blk, blk), 1) // HEAD
    ones_bd = jnp.where(r == c, 1.0, 0.0).astype(BF16)
    hi = x.astype(BF16)
    rem = x - hi.astype(F32)
    mid = rem.astype(BF16)
    lo = (rem - mid.astype(F32)).astype(BF16)
    dot = lambda p: jnp.dot(p, ones_bd, preferred_element_type=F32)
    cols = []
    for j in range(width // blk):
        sl = slice(j * blk, (j + 1) * blk)
        cols.append(dot(hi[:, sl]) + (dot(mid[:, sl]) + dot(lo[:, sl])))
    return jnp.concatenate(cols, axis=1)


def _ssd_body(z_ref, x_ref, bc_ref, dt_ref, cix_ref, cibc_ref, h0_ref,
              cwx_ref, cwbc_ref, cbx_ref, cbbc_ref, dtb_ref, alog_ref, dsk_ref, gn_ref,
              y_ref, ho_ref, xbuf, bcbuf, zbuf, dtbuf, h_s, *, q, nc):
    c = pl.program_id(1)
    padded = q < CHUNK
    if padded:
        xbuf[...] = jnp.zeros_like(xbuf)
        bcbuf[...] = jnp.zeros_like(bcbuf)
        zbuf[...] = jnp.zeros_like(zbuf)
        dtbuf[...] = jnp.zeros_like(dtbuf)
    _carried_window(xbuf, x_ref, cix_ref, c == 0, q)
    _carried_window(bcbuf, bc_ref, cibc_ref, c == 0, q)

    @pl.when(c == 0)
    def _():
        h_s[...] = h0_ref[0]

    row = lax.broadcasted_iota(jnp.int32, (CHUNK, LANES), 0)
    xs = _silu(_causal_conv(xbuf, cwx_ref, cbx_ref, CHUNK))
    bcv = _silu(_causal_conv(bcbuf, cwbc_ref, cbbc_ref, CHUNK))
    if padded:
        zbuf[0:q, :] = z_ref[...]
        dtbuf[0:q, :] = dt_ref[...]
        z = zbuf[...]
        dt_raw = dtbuf[...]
    else:
        _advance_window(xbuf, q)
        _advance_window(bcbuf, q)
        z = z_ref[...]
        dt_raw = dt_ref[...]

    dt = _softplus(dt_raw + dtb_ref[...])
    if padded:
        dt = jnp.where(row < q, dt, 0.0)
    a_head = -jnp.exp(alog_ref[...])
    cum = _cumsum_rows(dt * a_head)
    cum_last = cum[CHUNK - 1:CHUNK, :]
    cum_t = cum.T
    dtx = _expand_heads(dt, CHUNK)
    ecx = _expand_heads(jnp.exp(cum), CHUNK)
    tlx = _expand_heads(jnp.exp(cum_last - cum), CHUNK)
    xdt = xs * dtx
    xdtw = (xdt * tlx).astype(BF16)

    ti = lax.broadcasted_iota(jnp.int32, (CHUNK, CHUNK), 0)
    tj = lax.broadcasted_iota(jnp.int32, (CHUNK, CHUNK), 1)
    causal = ti >= tj
    low = tj < HEAD
    nt = (((1,), (1,)), ((), ()))
    tn = (((0,), (0,)), ((), ()))

    for g in range(N_GROUPS):
        gsl = slice(g * GROUP_W, (g + 1) * GROUP_W)
        bg = bcv[:, g * N_STATE:(g + 1) * N_STATE].astype(BF16)
        cg = bcv[:, (N_GROUPS + g) * N_STATE:(N_GROUPS + g + 1) * N_STATE].astype(BF16)
        cb = lax.dot_general(cg, bg, nt, preferred_element_type=F32)
        hg = h_s[gsl, :]
        y_off = lax.dot_general(cg, hg.astype(BF16), nt, preferred_element_type=F32) * ecx[:, gsl]
        st = lax.dot_general(xdtw[:, gsl], bg, tn, preferred_element_type=F32)
        y_pairs = []
        for j in range(GROUP_W // LANES):
            h0 = g * (GROUP_W // HEAD) + 2 * j
            ms = []
            for h in (h0, h0 + 1):
                seg = cum[:, h:h + 1] - cum_t[h:h + 1, :]
                decay = jnp.exp(jnp.where(causal, seg, -jnp.inf))
                ms.append((cb * decay).astype(BF16))
            psl = slice(h0 * HEAD, (h0 + 2) * HEAD)
            slab = xdt[:, psl]
            rhs = jnp.concatenate([jnp.where(low, slab, 0.0), jnp.where(low, 0.0, slab)], axis=0).astype(BF16)
            y_pairs.append(jnp.dot(jnp.concatenate(ms, axis=1), rhs, preferred_element_type=F32))
            for h in (h0, h0 + 1):
                hsl = slice(h * HEAD, (h + 1) * HEAD)
                dec = jnp.exp(jnp.broadcast_to(cum_last[:, h:h + 1], (HEAD, N_STATE)))
                h_s[hsl, :] = h_s[hsl, :] * dec + st[(h % 8) * HEAD:(h % 8 + 1) * HEAD, :]
        y = jnp.concatenate(y_pairs, axis=1) + y_off + xs[:, gsl] * dsk_ref[:, gsl]
        y = y * _silu(z[:, gsl])
        y = y * lax.rsqrt(jnp.mean(y * y, axis=-1, keepdims=True) + EPS) * gn_ref[:, gsl]
        if padded:
            y_ref[:, gsl] = y[0:q, :]
        else:
            y_ref[:, gsl] = y

    @pl.when(c == nc - 1)
    def _():
        ho_ref[0] = h_s[...]


def _ssd(p, conv_x, conv_bc, h0, prm, nseq, seq_len):
    q = min(CHUNK, seq_len)
    nc = seq_len // q
    rows = lambda w, col: pl.BlockSpec((q, w), lambda s, c: (s * nc + c, col))
    per_seq = lambda shape: pl.BlockSpec((1,) + shape, lambda s, c: (s,) + (0,) * len(shape))
    const = lambda shape: pl.BlockSpec(shape, lambda s, c: (0,) * len(shape))
    y, h_last = pl.pallas_call(
        functools.partial(_ssd_body, q=q, nc=nc),
        out_shape=(jax.ShapeDtypeStruct((nseq * seq_len, D), F32),
                   jax.ShapeDtypeStruct((nseq, D, N_STATE), F32)),
        grid=(nseq, nc),
        in_specs=[
            rows(D, COL_Z // D), rows(D, COL_X // D), rows(2 * N_GROUPS * N_STATE, COL_BC // 1024),
            rows(LANES, COL_DT // LANES),
            per_seq((SUBLANES, D)), per_seq((SUBLANES, 1024)), per_seq((D, N_STATE)),
            const((CONV_W, D)), const((CONV_W, 1024)), const((1, D)), const((1, 1024)),
            const((1, LANES)), const((1, LANES)), const((1, D)), const((1, D)),
        ],
        out_specs=(pl.BlockSpec((q, D), lambda s, c: (s * nc + c, 0)), per_seq((D, N_STATE))),
        scratch_shapes=[
            pltpu.VMEM((CHUNK + SUBLANES, D), F32), pltpu.VMEM((CHUNK + SUBLANES, 1024), F32),
            pltpu.VMEM((CHUNK, D), F32), pltpu.VMEM((CHUNK, LANES), F32),
            pltpu.VMEM((D, N_STATE), F32),
        ],
        compiler_params=_cparams(("parallel", "arbitrary"), 40),
        name="ssd",
    )(p, p, p, p, conv_x, conv_bc, h0, *prm)
    return y, h_last


def _rwkv_pre_body(rkv_ref, lora_ref, si_rkv_ref, si_lora_ref, mu_rkv_ref, mu_lora_ref,
                   w0_ref, w2_ref, a0_ref, a2_ref, g2_ref, kk_ref, ka_ref, rk_ref,
                   r_out, w_out, k_out, v_out, kk_out, b_out, bonus_out, g_out,
                   rkvbuf, lorabuf, *, q, log_decay, short_len):
    if short_len:
        def shifted(buf, cur_ref, si_ref, mu_ref):
            cur = cur_ref[...]
            first = lax.broadcasted_iota(jnp.int32, cur.shape, 0) % short_len == 0
            prev = jnp.where(first, si_ref[...], pltpu.roll(cur, 1, axis=0))
            return cur + mu_ref[...] * (prev - cur)
    else:
        c = pl.program_id(1)
        _carried_window(rkvbuf, rkv_ref, si_rkv_ref, c == 0, q)
        _carried_window(lorabuf, lora_ref, si_lora_ref, c == 0, q)

        def shifted(buf, cur_ref, si_ref, mu_ref):
            cur = cur_ref[...]
            prev = buf[pl.ds(SUBLANES - 1, q), :]
            return cur + mu_ref[...] * (prev - cur)

    ps = shifted(rkvbuf, rkv_ref, si_rkv_ref, mu_rkv_ref)
    lo_in = shifted(lorabuf, lora_ref, si_lora_ref, mu_lora_ref)
    if not short_len:
        _advance_window(rkvbuf, q)
        _advance_window(lorabuf, q)
    r = ps[:, 0:D]
    k = ps[:, D:2 * D]
    v = ps[:, 2 * D:3 * D]

    lw = jnp.dot(jnp.tanh(lo_in).astype(BF16), w2_ref[...], preferred_element_type=F32)
    la = jnp.dot(lo_in.astype(BF16), a2_ref[...], preferred_element_type=F32)
    g = jnp.dot(jax.nn.sigmoid(lo_in).astype(BF16), g2_ref[...], preferred_element_type=F32)
    wlog = -_softplus(-(w0_ref[...] + lw)) - 0.5
    log_w = -jnp.exp(wlog)
    a = jax.nn.sigmoid(a0_ref[...] + la)
    kkf = k * kk_ref[...]
    norm = jnp.maximum(jnp.sqrt(_head_allsum(kkf * kkf)), 1e-12)
    kk = kkf / norm
    k2 = k * (1.0 + (a - 1.0) * ka_ref[...])
    bonus = _head_allsum(r * k2 * rk_ref[...]) * v
    r_out[...] = r
    w_out[...] = log_w if log_decay else jnp.exp(log_w)
    k_out[...] = k2
    v_out[...] = v
    kk_out[...] = kk
    b_out[...] = kk * a
    bonus_out[...] = bonus
    g_out[...] = g


def _rwkv_pre(p, shift_state, prm, nseq, seq_len, log_decay):
    short_len = seq_len if seq_len < CHUNK else 0
    q = CHUNK if short_len else min(CHUNK, seq_len)
    nc = 1 if short_len else seq_len // q
    ntile = nseq * seq_len // q // nc
    const = lambda shape: pl.BlockSpec(shape, lambda s, c: (0,) * len(shape))
    tile = pl.BlockSpec((q, D), lambda s, c: (s * nc + c, 0))
    if short_len:
        dense = lambda a: jnp.pad(a, ((0, 0), (0, seq_len - 1), (0, 0))).reshape(nseq * seq_len, a.shape[-1])
        si = (dense(shift_state[:, :, :3 * D]), dense(shift_state[:, :, 3 * D:]))
        si_specs = [pl.BlockSpec((q, 3 * D), lambda s, c: (s, 0)), pl.BlockSpec((q, LORA_W), lambda s, c: (s, 0))]
    else:
        si = (_pad_front_rows(shift_state[:, :, :3 * D]), _pad_front_rows(shift_state[:, :, 3 * D:]))
        si_specs = [pl.BlockSpec((1, SUBLANES, 3 * D), lambda s, c: (s, 0, 0)),
                    pl.BlockSpec((1, SUBLANES, LORA_W), lambda s, c: (s, 0, 0))]
    sds = jax.ShapeDtypeStruct((nseq * seq_len, D), F32)
    return pl.pallas_call(
        functools.partial(_rwkv_pre_body, q=q, log_decay=log_decay, short_len=short_len),
        out_shape=(sds,) * 8,
        grid=(ntile, nc),
        in_specs=[
            pl.BlockSpec((q, 3 * D), lambda s, c: (s * nc + c, 0)),
            pl.BlockSpec((q, LORA_W), lambda s, c: (s * nc + c, COL_LORA // LORA_W)),
            *si_specs,
            const((1, 3 * D)), const((1, LORA_W)),
            const((1, D)), const((LORA_W, D)), const((1, D)), const((LORA_W, D)), const((LORA_W, D)),
            const((1, D)), const((1, D)), const((1, D)),
        ],
        out_specs=(tile,) * 8,
        scratch_shapes=[pltpu.VMEM((CHUNK + SUBLANES, 3 * D), F32), pltpu.VMEM((CHUNK + SUBLANES, LORA_W), F32)],
        compiler_params=_cparams(("parallel", "arbitrary"), 48),
        name="rwkv_pre",
    )(p, p, *si, *prm)


def _split_bf16(x):
    hi = x.astype(BF16)
    return hi, (x - hi.astype(F32)).astype(BF16)


_NN = (((1,), (0,)), ((), ()))
_NT = (((1,), (1,)), ((), ()))
_TN = (((0,), (0,)), ((), ()))


def _solve_unit_lower(n, rhs):
    rows, width = rhs.shape
    nblk, ncol = rows // SUBLANES, width // LANES
    nb = [[n[SUBLANES * i:SUBLANES * (i + 1), LANES * j:LANES * (j + 1)] for j in range(ncol)] for i in range(nblk)]
    xb = [[rhs[SUBLANES * i:SUBLANES * (i + 1), LANES * j:LANES * (j + 1)] for j in range(ncol)] for i in range(nblk)]
    low = lax.broadcasted_iota(jnp.int32, (SUBLANES, LANES), 1) < HEAD
    for s in range(rows - 1):
        i0, r0 = divmod(s, SUBLANES)
        idx = jnp.where(low, s, HEAD + s)
        for j in range(ncol):
            row = xb[i0][j][r0:r0 + 1, :]
            for i in range(i0 if r0 < SUBLANES - 1 else i0 + 1, nblk):
                xb[i][j] = xb[i][j] - jnp.take_along_axis(nb[i][j], idx, axis=1) * row
    return jnp.concatenate([jnp.concatenate(xr, axis=1) for xr in xb], axis=0)


def _wkv_tile(r, lw, k, v, kk, b, st):
    rows, width = r.shape
    nh = WKV_LANES // HEAD
    groups = [slice(g * WKV_LANES, (g + 1) * WKV_LANES) for g in range(width // WKV_LANES)]
    cl = _cumsum_rows(lw)
    cl_last = cl[rows - 1:rows, :]
    p_inv = jnp.exp(-cl)
    p_end = jnp.exp(cl_last - cl)
    x2h, x2l = _split_bf16(jnp.concatenate([kk * jnp.exp(cl - lw), r * jnp.exp(cl)], axis=0))
    k_hat = k * p_inv
    b_hat = b * p_inv
    k_end = k * p_end
    b_end = -(b * p_end)

    bd_r = lax.broadcasted_iota(jnp.int32, (nh * rows, WKV_LANES), 0) // rows
    bd_c = lax.broadcasted_iota(jnp.int32, (nh * rows, WKV_LANES), 1) // HEAD
    bd_mask = jnp.where(bd_r == bd_c, 1.0, 0.0).astype(BF16)

    def per_head_rows(y):
        return [jnp.concatenate([part] * nh, axis=0) * bd_mask for part in _split_bf16(y)]

    def dot3(ah, al, bh, bl, dims):
        dg = lambda x, y: lax.dot_general(x, y, dims, preferred_element_type=F32)
        return dg(ah, bh) + (dg(ah, bl) + dg(al, bh))

    t_i = lax.broadcasted_iota(jnp.int32, (rows, WKV_LANES), 0)
    lane = lax.broadcasted_iota(jnp.int32, (rows, WKV_LANES), 1)
    s_i = lane & (HEAD - 1)
    strict = t_i > s_i
    incl = t_i >= s_i
    head_of_lane = lane // HEAD

    a_kb, a_rb, base, o_part = [], [], [], []
    for gs in groups:
        ak = dot3(x2h[:, gs], x2l[:, gs], *per_head_rows(k_hat[:, gs]), _NT)
        ab = dot3(x2h[:, gs], x2l[:, gs], *per_head_rows(b_hat[:, gs]), _NT)
        a_k = jnp.concatenate([jnp.where(strict, ak[0:rows], 0.0), jnp.where(incl, ak[rows:], 0.0)], axis=0)
        xs = dot3(x2h[:, gs], x2l[:, gs], *per_head_rows(st[:, gs]), _NN)
        av = dot3(*_split_bf16(a_k), *per_head_rows(v[:, gs]), _NN)
        a_kb.append(jnp.where(strict, ab[0:rows], 0.0))
        a_rb.append(jnp.where(incl, ab[rows:], 0.0))
        base.append(xs[0:rows] + av[0:rows])
        o_part.append(xs[rows:] + av[rows:])
    sa = _solve_unit_lower(jnp.concatenate(a_kb, axis=1), jnp.concatenate(base, axis=1))

    o, st_new = [], []
    for g, gs in enumerate(groups):
        o.append(o_part[g] - dot3(*_split_bf16(a_rb[g]), *per_head_rows(sa[:, gs]), _NN))
        eye = jnp.where(t_i == s_i, jnp.exp(cl_last[:, gs]), 0.0)
        lhs = jnp.concatenate([k_end[:, gs], b_end[:, gs], eye], axis=0)
        rhs = jnp.concatenate([v[:, gs], sa[:, gs], st[:, gs]], axis=0)
        full = dot3(*_split_bf16(lhs), *_split_bf16(rhs), _TN)
        acc = jnp.where(head_of_lane == 0, full[0:HEAD, :], 0.0)
        for h in range(1, nh):
            acc = acc + jnp.where(head_of_lane == h, full[h * HEAD:(h + 1) * HEAD, :], 0.0)
        st_new.append(acc)
    return jnp.concatenate(o, axis=1), jnp.concatenate(st_new, axis=1)


def _wkv_long_body(r_ref, lw_ref, k_ref, v_ref, kk_ref, b_ref, o_ref, so_ref, st_s, *, nc):
    c = pl.program_id(2)

    @pl.when(c == 0)
    def _():
        st_s[...] = jnp.zeros_like(st_s)

    o, st_new = _wkv_tile(r_ref[...], lw_ref[...], k_ref[...], v_ref[...], kk_ref[...], b_ref[...], st_s[...])
    o_ref[...] = o
    st_s[...] = st_new

    @pl.when(c == nc - 1)
    def _():
        for j in range(WKV_GROUPS * WKV_LANES // LANES):
            js = slice(j * LANES, (j + 1) * LANES)
            sq = jnp.concatenate([st_s[:, js], jnp.zeros((LANES - HEAD, LANES), F32)], axis=0)
            so_ref[0, js, :] = sq.T[:, 0:HEAD]


def _wkv_long(r, lw, k, v, kk, b, nseq, seq_len):
    nc = seq_len // WKV_CHUNK
    width = WKV_GROUPS * WKV_LANES
    tile = pl.BlockSpec((WKV_CHUNK, width), lambda s, hg, c: (s * nc + c, hg))
    o, s_last = pl.pallas_call(
        functools.partial(_wkv_long_body, nc=nc),
        out_shape=(jax.ShapeDtypeStruct((nseq * seq_len, D), F32),
                   jax.ShapeDtypeStruct((nseq, D, HEAD), F32)),
        grid=(nseq, D // width, nc),
        in_specs=[tile] * 6,
        out_specs=(tile, pl.BlockSpec((1, width, HEAD), lambda s, hg, c: (s, hg, 0))),
        scratch_shapes=[pltpu.VMEM((HEAD, width), F32)],
        compiler_params=_cparams(("parallel", "parallel", "arbitrary"), 32),
        name="wkv_long",
    )(r, lw, k, v, kk, b)
    return o, s_last


def _wkv_short_body(r_ref, w_ref, k_ref, v_ref, kk_ref, b_ref, s0_ref, o_ref, so_ref, *, steps):
    def per_v_group(i, carry):
        vis = [i * WKV_SHORT_ROWS + u for u in range(WKV_SHORT_ROWS)]
        ss = [s0_ref[0, vi] for vi in vis]
        for t in range(steps):
            kk_t, w_t, b_t, k_t, r_t = kk_ref[t, 0], w_ref[t, 0], b_ref[t, 0], k_ref[t, 0], r_ref[t, 0]
            for u, vi in enumerate(vis):
                vrow = v_ref[t, 0, pl.ds(vi, 1), :]
                sa = jnp.sum(ss[u] * kk_t, axis=0, keepdims=True)
                ss[u] = ss[u] * w_t - b_t * sa + k_t * vrow
                o_ref[t, 0, pl.ds(vi, 1), :] = jnp.sum(ss[u] * r_t, axis=0, keepdims=True)
        for u, vi in enumerate(vis):
            so_ref[0, vi] = ss[u]
        return carry

    lax.fori_loop(0, HEAD // WKV_SHORT_ROWS, per_v_group, 0)


def _wkv_short(r, w, k, v, kk, b, s0, nseq, seq_len):
    vec = pl.BlockSpec((seq_len, 1, HEAD, nseq), lambda h: (0, h, 0, 0))
    st = pl.BlockSpec((1, HEAD, HEAD, nseq), lambda h: (h, 0, 0, 0))
    return pl.pallas_call(
        functools.partial(_wkv_short_body, steps=seq_len),
        out_shape=(jax.ShapeDtypeStruct((seq_len, N_HEADS, HEAD, nseq), F32),
                   jax.ShapeDtypeStruct((N_HEADS, HEAD, HEAD, nseq), F32)),
        grid=(N_HEADS,),
        in_specs=[vec] * 6 + [st],
        out_specs=(vec, st),
        compiler_params=_cparams(("parallel",), 32),
        name="wkv_short",
    )(r, w, k, v, kk, b, s0)


def _rwkv_post_body(o_ref, bonus_ref, g_ref, lnw_ref, lnb_ref, y_ref):
    o = o_ref[...]
    mean = _head_allsum(o) * (1.0 / HEAD)
    cen = o - mean
    var = _head_allsum(cen * cen) * (1.0 / HEAD)
    on = cen * lax.rsqrt(var + GN_EPS) * lnw_ref[...] + lnb_ref[...]
    y_ref[...] = (on + bonus_ref[...]) * g_ref[...]


def _rwkv_post(o, bonus, g, ln_w, ln_b):
    m = o.shape[0]
    tile = pl.BlockSpec((TM_POST, D), lambda i: (i, 0))
    const = pl.BlockSpec((1, D), lambda i: (0, 0))
    return pl.pallas_call(
        _rwkv_post_body,
        out_shape=jax.ShapeDtypeStruct((m, D), F32),
        grid=(m // TM_POST,),
        in_specs=[tile, tile, tile, const, const],
        out_specs=tile,
        compiler_params=_cparams(("parallel",), 32),
        name="rwkv_post",
    )(o, bonus, g, ln_w, ln_b)


def _lru_coeffs(xc, wga_ref, bga_ref, wgx_ref, bgx_ref, lam_ref):
    ra, rx = [], []
    for blk in range(LRU_BLOCKS):
        xh = xc[:, blk * LRU_BLK:(blk + 1) * LRU_BLK].astype(BF16)
        ra.append(jnp.dot(xh, wga_ref[blk], preferred_element_type=F32))
        rx.append(jnp.dot(xh, wgx_ref[blk], preferred_element_type=F32))
    rg = jax.nn.sigmoid(jnp.concatenate(ra, axis=1) + bga_ref[...])
    ig = jax.nn.sigmoid(jnp.concatenate(rx, axis=1) + bgx_ref[...])
    log_a = -LRU_C * rg * _softplus(-lam_ref[...])
    return jnp.exp(log_a), jnp.sqrt(1.0 - jnp.exp(2.0 * log_a)) * (ig * xc)


def _lru_body(gate_ref, x_ref, ci_ref, h0_ref, cw_ref, cb_ref, wga_ref, bga_ref, wgx_ref, bgx_ref, lam_ref,
              y_ref, ho_ref, xbuf, a_s, b_s, h_s, hc_s, *, q, nc):
    c = pl.program_id(1)
    _carried_window(xbuf, x_ref, ci_ref, c == 0, q)

    @pl.when(c == 0)
    def _():
        hc_s[...] = h0_ref[0]

    xc = _causal_conv(xbuf, cw_ref, cb_ref, q)
    _advance_window(xbuf, q)
    a_s[...], b_s[...] = _lru_coeffs(xc, wga_ref, bga_ref, wgx_ref, bgx_ref, lam_ref)

    def step(t, h):
        h = a_s[pl.ds(t, 1), :] * h + b_s[pl.ds(t, 1), :]
        h_s[pl.ds(t, 1), :] = h
        return h

    h = lax.fori_loop(0, q, step, hc_s[...])
    hc_s[...] = h
    y_ref[...] = h_s[...] * jax.nn.gelu(gate_ref[...])

    @pl.when(c == nc - 1)
    def _():
        ho_ref[0] = h


def _lru(p, conv_init, h0, prm, nseq, seq_len):
    q = min(CHUNK, seq_len)
    nc = seq_len // q
    per_seq = lambda shape: pl.BlockSpec((1,) + shape, lambda s, c: (s,) + (0,) * len(shape))
    const = lambda shape: pl.BlockSpec(shape, lambda s, c: (0,) * len(shape))
    return pl.pallas_call(
        functools.partial(_lru_body, q=q, nc=nc),
        out_shape=(jax.ShapeDtypeStruct((nseq * seq_len, D), F32),
                   jax.ShapeDtypeStruct((nseq, 1, D), F32)),
        grid=(nseq, nc),
        in_specs=[
            pl.BlockSpec((q, D), lambda s, c: (s * nc + c, 0)),
            pl.BlockSpec((q, D), lambda s, c: (s * nc + c, 1)),
            per_seq((SUBLANES, D)), per_seq((1, D)),
            const((CONV_W, D)), const((1, D)),
            const((LRU_BLOCKS, LRU_BLK, LRU_BLK)), const((1, D)),
            const((LRU_BLOCKS, LRU_BLK, LRU_BLK)), const((1, D)), const((1, D)),
        ],
        out_specs=(pl.BlockSpec((q, D), lambda s, c: (s * nc + c, 0)), per_seq((1, D))),
        scratch_shapes=[pltpu.VMEM((CHUNK + SUBLANES, D), F32), pltpu.VMEM((q, D), F32), pltpu.VMEM((q, D), F32),
                        pltpu.VMEM((q, D), F32), pltpu.VMEM((1, D), F32)],
        compiler_params=_cparams(("parallel", "arbitrary"), 32),
        name="lru",
    )(p, p, conv_init, h0, *prm)


def _lru_short_body(gate_ref, x_ref, p1_ref, p2_ref, p3_ref, h0_ref, cw_ref, cb_ref, wga_ref, bga_ref, wgx_ref,
                    bgx_ref, lam_ref, y_ref, h_ref, *, seq_len):
    cur = x_ref[...]
    t = lax.broadcasted_iota(jnp.int32, cur.shape, 0) % seq_len
    prev = lambda s, ref: jnp.where(t >= s, pltpu.roll(cur, s, axis=0), ref[...])
    xc = (cb_ref[...] + cw_ref[0:1, :] * prev(3, p3_ref) + cw_ref[1:2, :] * prev(2, p2_ref)
          + cw_ref[2:3, :] * prev(1, p1_ref) + cw_ref[3:4, :] * cur)
    a, b = _lru_coeffs(xc, wga_ref, bga_ref, wgx_ref, bgx_ref, lam_ref)
    s = 1
    while s < seq_len:
        keep = t >= s
        b = jnp.where(keep, a * pltpu.roll(b, s, axis=0) + b, b)
        a = jnp.where(keep, a * pltpu.roll(a, s, axis=0), a)
        s *= 2
    h = a * h0_ref[...] + b
    h_ref[...] = h
    y_ref[...] = h * jax.nn.gelu(gate_ref[...])


def _lru_short(p, conv_state, h0, prm, nseq, seq_len):
    rows = nseq * seq_len
    dense = lambda s: jnp.pad(conv_state[:, CONV_W - 1 - s:, :], ((0, 0), (0, seq_len - s), (0, 0))).reshape(rows, D)
    h0_rows = jnp.repeat(h0, seq_len, axis=0)
    tile = lambda col: pl.BlockSpec((CHUNK, D), lambda i: (i, col))
    const = lambda shape: pl.BlockSpec(shape, lambda i: (0,) * len(shape))
    sds = jax.ShapeDtypeStruct((rows, D), F32)
    return pl.pallas_call(
        functools.partial(_lru_short_body, seq_len=seq_len),
        out_shape=(sds, sds),
        grid=(rows // CHUNK,),
        in_specs=[tile(0), tile(1), tile(0), tile(0), tile(0), tile(0),
                  const((CONV_W, D)), const((1, D)),
                  const((LRU_BLOCKS, LRU_BLK, LRU_BLK)), const((1, D)),
                  const((LRU_BLOCKS, LRU_BLK, LRU_BLK)), const((1, D)), const((1, D))],
        out_specs=(tile(0), tile(0)),
        compiler_params=_cparams(("parallel",), 32),
        name="lru_short",
    )(p, p, dense(1), dense(2), dense(3), h0_rows, *prm)


def _pad_front_rows(buf):
    return jnp.pad(buf, ((0, 0), (SUBLANES - buf.shape[1], 0), (0, 0)))


def _row2(v):
    return v.reshape(1, -1)


def kernel(x_prompt, x_sample, state_ssm_a, state_conv_a, state_wkv_b, state_shift_b, state_lru_c, state_conv_c, norm_gain, w_ffn_in, w_ffn_out, w_in_ab, conv_w_a, conv_b_a, dt_bias_a, a_log_a, d_skip_a, gnorm_a, mu_b, w0_b, w2_b, a0_b, a2_b, g2_b, k_k_b, k_a_b, r_k_b, ln_w_b, ln_b_b, w_out_ab, w_in_c, conv_w_c, conv_b_c, w_gate_a_c, b_gate_a_c, w_gate_x_c, b_gate_x_c, lambda_c, w_out_c, final_norm_gain):
    w_gu, w_dn = _ffn_weights(w_ffn_in, w_ffn_out)

    in_a = D + (D + 2 * N_GROUPS * N_STATE) + N_HEADS
    wab = w_in_ab[0]
    bc_w = 2 * N_GROUPS * N_STATE
    w_proj0 = jnp.concatenate([
        wab[:, in_a:in_a + 3 * D],
        wab[:, 0:D],
        wab[:, D:2 * D],
        wab[:, 2 * D:2 * D + bc_w],
        wab[:, in_a + 3 * D:in_a + 3 * D + LORA_W],
        wab[:, in_a - N_HEADS:in_a],
        jnp.zeros((D, PROJ_W - COL_DT - N_HEADS), F32),
    ], axis=1).astype(BF16)
    w_proj1 = w_in_c[0].astype(BF16)
    w_out0 = w_out_ab[0].astype(BF16)
    w_out1 = w_out_c[0].astype(BF16)

    pad_lanes = lambda v: jnp.pad(v.reshape(1, -1), ((0, 0), (0, LANES - v.shape[-1])))
    rep_head = lambda v: jnp.repeat(v, HEAD).reshape(1, D)
    ssd_prm = (conv_w_a[0][:, :D], conv_w_a[0][:, D:], _row2(conv_b_a[0][:D]), _row2(conv_b_a[0][D:]),
               pad_lanes(dt_bias_a[0]), pad_lanes(a_log_a[0]), rep_head(d_skip_a[0]), _row2(gnorm_a[0]))
    lora_rows = lambda w, lo: jnp.pad(w, ((lo, LORA_W - lo - w.shape[0]), (0, 0))).astype(BF16)
    mu = mu_b[0]
    rwkv_prm = (_row2(mu[:3 * D]), _row2(mu[3 * D:]),
                _row2(w0_b[0]), lora_rows(w2_b[0], 0), _row2(a0_b[0]), lora_rows(a2_b[0], 64),
                lora_rows(g2_b[0], 128), _row2(k_k_b[0]), _row2(k_a_b[0]), _row2(r_k_b[0]))
    lru_prm = (conv_w_c[0], _row2(conv_b_c[0]), w_gate_a_c[0].astype(BF16), _row2(b_gate_a_c[0]),
               w_gate_x_c[0].astype(BF16), _row2(b_gate_x_c[0]), _row2(lambda_c[0]))

    def trunk(x3, ssm0, conva0, wkv0, shift0, lru0, convc0):
        nseq, seq_len, _ = x3.shape
        x = x3.reshape(nseq * seq_len, D)
        tail = lambda arr, n: arr.reshape(nseq, seq_len, arr.shape[-1])[:, seq_len - n:, :]

        x = _ffn(x, _row2(norm_gain[0, 0]), w_gu, w_dn, (0, 0))
        p = _proj(x, _row2(norm_gain[0, 1]), w_proj0)
        conv_x = _pad_front_rows(conva0[:, :, :D])
        conv_bc = _pad_front_rows(conva0[:, :, D:])
        ya, ssm_n = _ssd(p, conv_x, conv_bc, ssm0.reshape(nseq, D, N_STATE), ssd_prm, nseq, seq_len)
        p_tail = tail(p, CONV_W - 1)
        conva_n = jnp.concatenate([p_tail[:, :, COL_X:COL_X + D], p_tail[:, :, COL_BC:COL_BC + bc_w]], axis=-1)
        shift_n = jnp.concatenate([p_tail[:, -1:, :3 * D], p_tail[:, -1:, COL_LORA:COL_LORA + LORA_W]], axis=-1)

        r, w, k, v, kk, b, bonus, g = _rwkv_pre(p, shift0, rwkv_prm, nseq, seq_len, log_decay=wkv0 is None)
        if wkv0 is None:
            o, wkv_n = _wkv_long(r, w, k, v, kk, b, nseq, seq_len)
            wkv_n = wkv_n.reshape(nseq, N_HEADS, HEAD, HEAD)
        else:
            to_lanes = lambda a: a.reshape(nseq, seq_len, N_HEADS, HEAD).transpose(1, 2, 3, 0)
            o, wkv_n = _wkv_short(*[to_lanes(a) for a in (r, w, k, v, kk, b)], wkv0.transpose(1, 2, 3, 0),
                                  nseq, seq_len)
            o = o.transpose(3, 0, 1, 2).reshape(nseq * seq_len, D)
            wkv_n = wkv_n.transpose(3, 0, 1, 2)
        yb = _rwkv_post(o, bonus, g, _row2(ln_w_b[0]), _row2(ln_b_b[0]))
        x = _out_proj(x, (ya, yb), (w_out0[:D], w_out0[D:]))
        x = _ffn(x, _row2(norm_gain[0, 2]), w_gu, w_dn, (0, 1))

        x = _ffn(x, _row2(norm_gain[1, 0]), w_gu, w_dn, (1, 0))
        pc = _proj(x, _row2(norm_gain[1, 1]), w_proj1)
        if seq_len >= CHUNK:
            yc, lru_n = _lru(pc, _pad_front_rows(convc0), lru0.reshape(nseq, 1, D), lru_prm, nseq, seq_len)
        else:
            yc, h_rows = _lru_short(pc, convc0, lru0, lru_prm, nseq, seq_len)
            lru_n = tail(h_rows, 1)
        convc_n = tail(pc, CONV_W - 1)[:, :, D:]
        x = _out_proj(x, (yc,), (w_out1,))
        y = _ffn(x, _row2(norm_gain[1, 2]), w_gu, w_dn, (1, 1), final_gain=_row2(final_norm_gain))

        return (y.reshape(nseq, seq_len, D), ssm_n.reshape(1, nseq, N_HEADS, HEAD, N_STATE), conva_n[None],
                wkv_n[None], shift_n[None], lru_n.reshape(1, nseq, D), convc_n[None])

    bp = x_prompt.shape[0]
    zeros = lambda s: jnp.zeros((bp,) + s.shape[2:], F32)
    outs_p = trunk(x_prompt, zeros(state_ssm_a), zeros(state_conv_a), None, zeros(state_shift_b),
                   zeros(state_lru_c), zeros(state_conv_c))
    outs_s = trunk(x_sample, state_ssm_a[0], state_conv_a[0], state_wkv_b[0], state_shift_b[0],
                   state_lru_c[0], state_conv_c[0])
    return (outs_p[0], outs_s[0]) + outs_p[1:] + outs_s[1:]
```

```python
import functools

import jax
import jax.numpy as jnp
from jax import lax
from jax.experimental import pallas as pl
from jax.experimental.pallas import tpu as pltpu

F32 = jnp.float32
BF16 = jnp.bfloat16

D = 2048
D_FF = 5504
D_FF_PAD = 5632
HEAD = 64
N_HEADS = 32
N_GROUPS = 4
GROUP_W = D // N_GROUPS
N_STATE = 128
CONV_W = 4
LORA_W = 256
LRU_BLOCKS = 8
LRU_BLK = D // LRU_BLOCKS
LRU_C = 8.0
EPS = 1e-6
GN_EPS = 64e-5
SUBLANES = 8
LANES = 128
CHUNK = 128
WKV_CHUNK = 64
WKV_LANES = 256
WKV_GROUPS = 4
WKV_SHORT_ROWS = 4
SSD_SHORT_SEQS = 8

COL_RKV = 0
COL_Z = 3 * D
COL_X = 4 * D
COL_BC = 5 * D
COL_LORA = 5 * D + 2 * N_GROUPS * N_STATE
COL_DT = COL_LORA + LORA_W
PROJ_W = COL_DT + 256

TM_FFN = 512
TF_FFN = 512
TM_PROJ = 1024
TN_PROJ = 512
TM_OUT = 512
TN_OUT = 1024
TM_POST = 256
W_IN_ROWS = 64


def _cparams(sem, vmem_mib):
    return pltpu.CompilerParams(dimension_semantics=sem, vmem_limit_bytes=vmem_mib * 1024 * 1024)


def _softplus(x):
    return jnp.maximum(x, 0.0) + jnp.log(1.0 + jnp.exp(-jnp.abs(x)))


def _silu(x):
    return x * jax.nn.sigmoid(x)


def _rms(x, gain):
    ms = jnp.mean(x * x, axis=-1, keepdims=True)
    return x * lax.rsqrt(ms + EPS) * gain


def _ffn_body(x_ref, g_ref, wg_ref, wu_ref, wo_ref, fg_ref, o_ref, xn_ref, acc_ref, *, nf, final):
    f = pl.program_id(1)

    @pl.when(f == 0)
    def _():
        xn_ref[...] = _rms(x_ref[...], g_ref[...]).astype(BF16)
        acc_ref[...] = jnp.zeros_like(acc_ref)

    xn = xn_ref[...]
    gate = jnp.dot(xn, wg_ref[...], preferred_element_type=F32)
    up = jnp.dot(xn, wu_ref[...], preferred_element_type=F32)
    h = (_silu(gate) * up).astype(BF16)
    acc_ref[...] += jnp.dot(h, wo_ref[...], preferred_element_type=F32)

    @pl.when(f == nf - 1)
    def _():
        y = x_ref[...] + 0.5 * acc_ref[...]
        if final:
            y = _rms(y, fg_ref[...])
        o_ref[...] = y


def _ffn(x, gain, w_gu, w_dn, which, final_gain=None):
    m = x.shape[0]
    li, si = which
    nf = D_FF_PAD // TF_FFN
    final = final_gain is not None
    fg = final_gain if final else gain
    return pl.pallas_call(
        functools.partial(_ffn_body, nf=nf, final=final),
        out_shape=jax.ShapeDtypeStruct((m, D), F32),
        grid=(m // TM_FFN, nf),
        in_specs=[
            pl.BlockSpec((TM_FFN, D), lambda i, f: (i, 0)),
            pl.BlockSpec((1, D), lambda i, f: (0, 0)),
            pl.BlockSpec((None, None, D, TF_FFN), lambda i, f: (li, si, 0, f)),
            pl.BlockSpec((None, None, D, TF_FFN), lambda i, f: (li, si, 0, f + nf)),
            pl.BlockSpec((None, None, TF_FFN, D), lambda i, f: (li, si, f, 0)),
            pl.BlockSpec((1, D), lambda i, f: (0, 0)),
        ],
        out_specs=pl.BlockSpec((TM_FFN, D), lambda i, f: (i, 0)),
        scratch_shapes=[pltpu.VMEM((TM_FFN, D), BF16), pltpu.VMEM((TM_FFN, D), F32)],
        compiler_params=_cparams(("parallel", "arbitrary"), 48),
        name="ffn",
    )(x, gain, w_gu, w_gu, w_dn, fg)


def _proj_body(x_ref, g_ref, w_ref, o_ref, xn_ref):
    @pl.when(pl.program_id(1) == 0)
    def _():
        xn_ref[...] = _rms(x_ref[...], g_ref[...]).astype(BF16)

    o_ref[...] = jnp.dot(xn_ref[...], w_ref[...], preferred_element_type=F32)


def _proj(x, gain, w):
    m = x.shape[0]
    n = w.shape[1]
    return pl.pallas_call(
        _proj_body,
        out_shape=jax.ShapeDtypeStruct((m, n), F32),
        grid=(m // TM_PROJ, n // TN_PROJ),
        in_specs=[
            pl.BlockSpec((TM_PROJ, D), lambda i, j: (i, 0)),
            pl.BlockSpec((1, D), lambda i, j: (0, 0)),
            pl.BlockSpec((D, TN_PROJ), lambda i, j: (0, j)),
        ],
        out_specs=pl.BlockSpec((TM_PROJ, TN_PROJ), lambda i, j: (i, j)),
        scratch_shapes=[pltpu.VMEM((TM_PROJ, D), BF16)],
        compiler_params=_cparams(("parallel", "arbitrary"), 40),
        name="proj",
    )(x, gain, w)


def _out2_body(res_ref, ya_ref, yb_ref, wa_ref, wb_ref, o_ref):
    acc = jnp.dot(ya_ref[...].astype(BF16), wa_ref[...], preferred_element_type=F32)
    acc = acc + jnp.dot(yb_ref[...].astype(BF16), wb_ref[...], preferred_element_type=F32)
    o_ref[...] = res_ref[...] + acc


def _out1_body(res_ref, y_ref, w_ref, o_ref):
    o_ref[...] = res_ref[...] + jnp.dot(y_ref[...].astype(BF16), w_ref[...], preferred_element_type=F32)


def _out_proj(res, ys, ws):
    m = res.shape[0]
    body = _out2_body if len(ys) == 2 else _out1_body
    y_spec = pl.BlockSpec((TM_OUT, D), lambda j, i: (i, 0))
    w_spec = pl.BlockSpec((D, TN_OUT), lambda j, i: (0, j))
    r_spec = pl.BlockSpec((TM_OUT, TN_OUT), lambda j, i: (i, j))
    return pl.pallas_call(
        body,
        out_shape=jax.ShapeDtypeStruct((m, D), F32),
        grid=(D // TN_OUT, m // TM_OUT),
        in_specs=[r_spec] + [y_spec] * len(ys) + [w_spec] * len(ws),
        out_specs=r_spec,
        compiler_params=_cparams(("arbitrary", "arbitrary"), 48),
        name="out_proj",
    )(res, *ys, *ws)


def _cast_pad_body(w_ref, o_ref, *, axis, zero_blocks):
    c = pl.program_id(axis)
    is_pad = functools.reduce(jnp.logical_or, [c == z for z in zero_blocks])
    o_ref[...] = jnp.where(is_pad, 0.0, w_ref[...]).astype(BF16)


def _cast_pad_halves_body(w_ref, o_ref):
    rows = o_ref.shape[0]
    zeros = jnp.zeros((rows, D_FF_PAD - D_FF), BF16)
    for half in range(2):
        o_ref[:, half * D_FF_PAD:half * D_FF_PAD + D_FF] = w_ref[:, half * D_FF:(half + 1) * D_FF].astype(BF16)
        o_ref[:, half * D_FF_PAD + D_FF:(half + 1) * D_FF_PAD] = zeros


def _ffn_weights(w_in, w_out):
    n_layers, n_slots = w_in.shape[:2]
    nb = D_FF // LANES
    nbp = D_FF_PAD // LANES
    w_gu = pl.pallas_call(
        _cast_pad_halves_body,
        out_shape=jax.ShapeDtypeStruct((n_layers, n_slots, D, 2 * D_FF_PAD), BF16),
        grid=(n_layers * n_slots, D // W_IN_ROWS),
        in_specs=[pl.BlockSpec((None, None, W_IN_ROWS, 2 * D_FF), lambda q, r: (q // n_slots, q % n_slots, r, 0))],
        out_specs=pl.BlockSpec((None, None, W_IN_ROWS, 2 * D_FF_PAD), lambda q, r: (q // n_slots, q % n_slots, r, 0)),
        compiler_params=_cparams(("parallel", "parallel"), 32),
        name="ffn_w_in",
    )(w_in)
    w_dn = pl.pallas_call(
        functools.partial(_cast_pad_body, axis=1, zero_blocks=tuple(range(nb, nbp))),
        out_shape=jax.ShapeDtypeStruct((n_layers, n_slots, D_FF_PAD, D), BF16),
        grid=(n_layers * n_slots, nbp),
        in_specs=[pl.BlockSpec((None, None, LANES, D),
                               lambda q, r: (q // n_slots, q % n_slots, jnp.minimum(r, nb - 1), 0))],
        out_specs=pl.BlockSpec((None, None, LANES, D), lambda q, r: (q // n_slots, q % n_slots, r, 0)),
        compiler_params=_cparams(("parallel", "parallel"), 32),
        name="ffn_w_out",
    )(w_out)
    return w_gu, w_dn


def _carried_window(buf, cur_ref, init_ref, first, q):
    @pl.when(first)
    def _():
        buf[0:SUBLANES, :] = init_ref[0]

    buf[SUBLANES:SUBLANES + q, :] = cur_ref[...]


def _advance_window(buf, q):
    tail = buf[q:q + SUBLANES, :]
    buf[0:SUBLANES, :] = tail


def _causal_conv(buf, cw_ref, cb_ref, q):
    acc = cb_ref[...] + cw_ref[0:1, :] * buf[pl.ds(SUBLANES - CONV_W + 1, q), :]
    for k in range(1, CONV_W):
        acc = acc + cw_ref[k:k + 1, :] * buf[pl.ds(SUBLANES - CONV_W + 1 + k, q), :]
    return acc


def _cumsum_rows(x):
    n = x.shape[0]
    row = lax.broadcasted_iota(jnp.int32, x.shape, 0)
    s = 1
    while s < n:
        x = x + jnp.where(row >= s, pltpu.roll(x, s, axis=0), 0.0)
        s *= 2
    return x


def _expand_heads(a, rows):
    lane = lax.broadcasted_iota(jnp.int32, (rows, LANES), 1)
    low = lane < HEAD
    pieces = []
    for j in range(N_HEADS // 2):
        e0 = jnp.broadcast_to(a[:, 2 * j:2 * j + 1], (rows, LANES))
        e1 = jnp.broadcast_to(a[:, 2 * j + 1:2 * j + 2], (rows, LANES))
        pieces.append(jnp.where(low, e0, e1))
    return jnp.concatenate(pieces, axis=1)


def _head_allsum(x):
    width = x.shape[-1]
    blk = 4 * HEAD
    r = lax.broadcasted_iota(jnp.int32, (blk, blk), 0) // HEAD
    c = lax.broadcasted_iota(jnp.int32, (blk, blk), 1) // HEAD
    ones_bd = jnp.where(r == c, 1.0, 0.0).astype(BF16)
    hi = x.astype(BF16)
    rem = x - hi.astype(F32)
    mid = rem.astype(BF16)
    lo = (rem - mid.astype(F32)).astype(BF16)
    dot = lambda p: jnp.dot(p, ones_bd, preferred_element_type=F32)
    cols = []
    for j in range(width // blk):
        sl = slice(j * blk, (j + 1) * blk)
        cols.append(dot(hi[:, sl]) + (dot(mid[:, sl]) + dot(lo[:, sl])))
    return jnp.concatenate(cols, axis=1)


def _ssd_body(z_ref, x_ref, bc_ref, dt_ref, cix_ref, cibc_ref, h0_ref,
              cwx_ref, cwbc_ref, cbx_ref, cbbc_ref, dtb_ref, alog_ref, dsk_ref, gn_ref,
              y_ref, ho_ref, xbuf, bcbuf, h_s, *, nc):
    c = pl.program_id(1)
    q = CHUNK
    _carried_window(xbuf, x_ref, cix_ref, c == 0, q)
    _carried_window(bcbuf, bc_ref, cibc_ref, c == 0, q)

    @pl.when(c == 0)
    def _():
        h_s[...] = h0_ref[0]

    xs = _silu(_causal_conv(xbuf, cwx_ref, cbx_ref, CHUNK))
    bcv = _silu(_causal_conv(bcbuf, cwbc_ref, cbbc_ref, CHUNK))
    _advance_window(xbuf, q)
    _advance_window(bcbuf, q)
    z = z_ref[...]
    dt = _softplus(dt_ref[...] + dtb_ref[...])
    a_head = -jnp.exp(alog_ref[...])
    cum = _cumsum_rows(dt * a_head)
    cum_last = cum[CHUNK - 1:CHUNK, :]
    cum_t = cum.T
    dtx = _expand_heads(dt, CHUNK)
    ecx = _expand_heads(jnp.exp(cum), CHUNK)
    tlx = _expand_heads(jnp.exp(cum_last - cum), CHUNK)
    xdt = xs * dtx
    xdtw = (xdt * tlx).astype(BF16)

    ti = lax.broadcasted_iota(jnp.int32, (CHUNK, CHUNK), 0)
    tj = lax.broadcasted_iota(jnp.int32, (CHUNK, CHUNK), 1)
    causal = ti >= tj
    low = tj < HEAD
    nt = (((1,), (1,)), ((), ()))
    tn = (((0,), (0,)), ((), ()))

    for g in range(N_GROUPS):
        gsl = slice(g * GROUP_W, (g + 1) * GROUP_W)
        bg = bcv[:, g * N_STATE:(g + 1) * N_STATE].astype(BF16)
        cg = bcv[:, (N_GROUPS + g) * N_STATE:(N_GROUPS + g + 1) * N_STATE].astype(BF16)
        cb = lax.dot_general(cg, bg, nt, preferred_element_type=F32)
        hg = h_s[gsl, :]
        y_off = lax.dot_general(cg, hg.astype(BF16), nt, preferred_element_type=F32) * ecx[:, gsl]
        st = lax.dot_general(xdtw[:, gsl], bg, tn, preferred_element_type=F32)
        y_pairs = []
        for j in range(GROUP_W // LANES):
            h0 = g * (GROUP_W // HEAD) + 2 * j
            ms = []
            for h in (h0, h0 + 1):
                seg = cum[:, h:h + 1] - cum_t[h:h + 1, :]
                decay = jnp.exp(jnp.where(causal, seg, -jnp.inf))
                ms.append((cb * decay).astype(BF16))
            psl = slice(h0 * HEAD, (h0 + 2) * HEAD)
            slab = xdt[:, psl]
            rhs = jnp.concatenate([jnp.where(low, slab, 0.0), jnp.where(low, 0.0, slab)], axis=0).astype(BF16)
            y_pairs.append(jnp.dot(jnp.concatenate(ms, axis=1), rhs, preferred_element_type=F32))
            for h in (h0, h0 + 1):
                hsl = slice(h * HEAD, (h + 1) * HEAD)
                dec = jnp.exp(jnp.broadcast_to(cum_last[:, h:h + 1], (HEAD, N_STATE)))
                h_s[hsl, :] = h_s[hsl, :] * dec + st[(h % 8) * HEAD:(h % 8 + 1) * HEAD, :]
        y = jnp.concatenate(y_pairs, axis=1) + y_off + xs[:, gsl] * dsk_ref[:, gsl]
        y = y * _silu(z[:, gsl])
        y_ref[:, gsl] = y * lax.rsqrt(jnp.mean(y * y, axis=-1, keepdims=True) + EPS) * gn_ref[:, gsl]

    @pl.when(c == nc - 1)
    def _():
        ho_ref[0] = h_s[...]


def _ssd(p, conv_x, conv_bc, h0, prm, nseq, seq_len):
    q = CHUNK
    nc = seq_len // q
    rows = lambda w, col: pl.BlockSpec((q, w), lambda s, c: (s * nc + c, col))
    per_seq = lambda shape: pl.BlockSpec((1,) + shape, lambda s, c: (s,) + (0,) * len(shape))
    const = lambda shape: pl.BlockSpec(shape, lambda s, c: (0,) * len(shape))
    y, h_last = pl.pallas_call(
        functools.partial(_ssd_body, nc=nc),
        out_shape=(jax.ShapeDtypeStruct((nseq * seq_len, D), F32),
                   jax.ShapeDtypeStruct((nseq, D, N_STATE), F32)),
        grid=(nseq, nc),
        in_specs=[
            rows(D, COL_Z // D), rows(D, COL_X // D), rows(2 * N_GROUPS * N_STATE, COL_BC // 1024),
            rows(LANES, COL_DT // LANES),
            per_seq((SUBLANES, D)), per_seq((SUBLANES, 1024)), per_seq((D, N_STATE)),
            const((CONV_W, D)), const((CONV_W, 1024)), const((1, D)), const((1, 1024)),
            const((1, LANES)), const((1, LANES)), const((1, D)), const((1, D)),
        ],
        out_specs=(pl.BlockSpec((q, D), lambda s, c: (s * nc + c, 0)), per_seq((D, N_STATE))),
        scratch_shapes=[
            pltpu.VMEM((CHUNK + SUBLANES, D), F32), pltpu.VMEM((CHUNK + SUBLANES, 1024), F32),
            pltpu.VMEM((D, N_STATE), F32),
        ],
        compiler_params=_cparams(("parallel", "arbitrary"), 40),
        name="ssd",
    )(p, p, p, p, conv_x, conv_bc, h0, *prm)
    return y, h_last


def _ssd_short_body(z_ref, x_ref, bc_ref, dt_ref, px1_ref, px2_ref, px3_ref, pb1_ref, pb2_ref, pb3_ref, h0_ref,
                    cwx_ref, cwbc_ref, cbx_ref, cbbc_ref, dtb_ref, alog_ref, dsk_ref, gn_ref,
                    y_ref, ho_ref, *, seq_len):
    rows = x_ref.shape[0]
    nseq_t = rows // seq_len
    t = lax.broadcasted_iota(jnp.int32, (rows, LANES), 0) % seq_len

    def conv(cur_ref, p1_ref, p2_ref, p3_ref, cw_ref, cb_ref):
        cur = cur_ref[...]
        tt = lax.broadcasted_iota(jnp.int32, cur.shape, 0) % seq_len
        prev = lambda s, ref: jnp.where(tt >= s, pltpu.roll(cur, s, axis=0), ref[...])
        return (cb_ref[...] + cw_ref[0:1, :] * prev(3, p3_ref) + cw_ref[1:2, :] * prev(2, p2_ref)
                + cw_ref[2:3, :] * prev(1, p1_ref) + cw_ref[3:4, :] * cur)

    xs = _silu(conv(x_ref, px1_ref, px2_ref, px3_ref, cwx_ref, cbx_ref))
    bcv = _silu(conv(bc_ref, pb1_ref, pb2_ref, pb3_ref, cwbc_ref, cbbc_ref))
    z = z_ref[...]
    dt = _softplus(dt_ref[...] + dtb_ref[...])
    cum = dt * -jnp.exp(alog_ref[...])
    s = 1
    while s < seq_len:
        cum = cum + jnp.where(t >= s, pltpu.roll(cum, s, axis=0), 0.0)
        s *= 2
    tot = cum
    s = 1
    while s < seq_len:
        tot = jnp.where(t + s < seq_len, pltpu.roll(tot, rows - s, axis=0), tot)
        s *= 2
    cum_sq = cum if rows == LANES else jnp.concatenate([cum, jnp.zeros((LANES - rows, LANES), F32)], axis=0)
    cum_t = cum_sq.T[:, 0:rows]
    dtx = _expand_heads(dt, rows)
    ecx = _expand_heads(jnp.exp(cum), rows)
    tlx = _expand_heads(jnp.exp(tot - cum), rows)
    xdt = xs * dtx
    xdtw = (xdt * tlx).astype(BF16)

    ti = lax.broadcasted_iota(jnp.int32, (rows, rows), 0)
    tj = lax.broadcasted_iota(jnp.int32, (rows, rows), 1)
    same_causal = (ti >= tj) & (ti // seq_len == tj // seq_len)
    low = lax.broadcasted_iota(jnp.int32, (rows, LANES), 1) < HEAD
    row_seq = lax.broadcasted_iota(jnp.int32, (rows, N_STATE), 0) // seq_len

    def per_seq_cols(m):
        return jnp.concatenate([jnp.where(row_seq == q, m, 0.0) for q in range(nseq_t)], axis=1).astype(BF16)

    for g in range(N_GROUPS):
        gsl = slice(g * GROUP_W, (g + 1) * GROUP_W)
        bg_f = bcv[:, g * N_STATE:(g + 1) * N_STATE]
        cg_f = bcv[:, (N_GROUPS + g) * N_STATE:(N_GROUPS + g + 1) * N_STATE]
        cb = lax.dot_general(cg_f.astype(BF16), bg_f.astype(BF16), _NT, preferred_element_type=F32)
        h_cat = jnp.concatenate([h0_ref[q, gsl, :] for q in range(nseq_t)], axis=1).astype(BF16)
        y_off = lax.dot_general(per_seq_cols(cg_f), h_cat, _NT, preferred_element_type=F32) * ecx[:, gsl]
        st = lax.dot_general(xdtw[:, gsl], per_seq_cols(bg_f), _TN, preferred_element_type=F32)
        y_pairs = []
        for j in range(GROUP_W // LANES):
            h_lo = g * (GROUP_W // HEAD) + 2 * j
            ms = []
            for h in (h_lo, h_lo + 1):
                seg = cum[:, h:h + 1] - cum_t[h:h + 1, :]
                decay = jnp.exp(jnp.where(same_causal, seg, -jnp.inf))
                ms.append((cb * decay).astype(BF16))
            slab = xdt[:, h_lo * HEAD:(h_lo + 2) * HEAD]
            rhs = jnp.concatenate([jnp.where(low, slab, 0.0), jnp.where(low, 0.0, slab)], axis=0).astype(BF16)
            y_pairs.append(jnp.dot(jnp.concatenate(ms, axis=1), rhs, preferred_element_type=F32))
            for h in (h_lo, h_lo + 1):
                hsl = slice(h * HEAD, (h + 1) * HEAD)
                hg = slice((h % 8) * HEAD, (h % 8 + 1) * HEAD)
                for q in range(nseq_t):
                    dec = jnp.exp(jnp.broadcast_to(tot[q * seq_len:q * seq_len + 1, h:h + 1], (HEAD, N_STATE)))
                    ho_ref[q, hsl, :] = h0_ref[q, hsl, :] * dec + st[hg, q * N_STATE:(q + 1) * N_STATE]
        y = jnp.concatenate(y_pairs, axis=1) + y_off + xs[:, gsl] * dsk_ref[:, gsl]
        y = y * _silu(z[:, gsl])
        y_ref[:, gsl] = y * lax.rsqrt(jnp.mean(y * y, axis=-1, keepdims=True) + EPS) * gn_ref[:, gsl]


def _ssd_short(p, conv_state, h0, prm, nseq, seq_len):
    rows = nseq * seq_len
    tr = SSD_SHORT_SEQS * seq_len
    bc_w = 2 * N_GROUPS * N_STATE

    def dense(s, lo, hi):
        part = conv_state[:, CONV_W - 1 - s:, lo:hi]
        return jnp.pad(part, ((0, 0), (0, seq_len - s), (0, 0))).reshape(rows, hi - lo)

    tile = lambda w, col: pl.BlockSpec((tr, w), lambda i: (i, col))
    const = lambda shape: pl.BlockSpec(shape, lambda i: (0,) * len(shape))
    state = pl.BlockSpec((SSD_SHORT_SEQS, D, N_STATE), lambda i: (i, 0, 0))
    return pl.pallas_call(
        functools.partial(_ssd_short_body, seq_len=seq_len),
        out_shape=(jax.ShapeDtypeStruct((rows, D), F32), jax.ShapeDtypeStruct((nseq, D, N_STATE), F32)),
        grid=(nseq // SSD_SHORT_SEQS,),
        in_specs=[
            tile(D, COL_Z // D), tile(D, COL_X // D), tile(bc_w, COL_BC // bc_w), tile(LANES, COL_DT // LANES),
            tile(D, 0), tile(D, 0), tile(D, 0), tile(bc_w, 0), tile(bc_w, 0), tile(bc_w, 0), state,
            const((CONV_W, D)), const((CONV_W, bc_w)), const((1, D)), const((1, bc_w)),
            const((1, LANES)), const((1, LANES)), const((1, D)), const((1, D)),
        ],
        out_specs=(tile(D, 0), state),
        compiler_params=_cparams(("parallel",), 48),
        name="ssd_short",
    )(p, p, p, p, dense(1, 0, D), dense(2, 0, D), dense(3, 0, D), dense(1, D, D + bc_w), dense(2, D, D + bc_w),
      dense(3, D, D + bc_w), h0, *prm)


def _rwkv_pre_body(rkv_ref, lora_ref, si_rkv_ref, si_lora_ref, mu_rkv_ref, mu_lora_ref,
                   w0_ref, w2_ref, a0_ref, a2_ref, g2_ref, kk_ref, ka_ref, rk_ref,
                   r_out, w_out, k_out, v_out, kk_out, b_out, bonus_out, g_out,
                   rkvbuf, lorabuf, *, q, log_decay, short_len):
    if short_len:
        def shifted(buf, cur_ref, si_ref, mu_ref):
            cur = cur_ref[...]
            first = lax.broadcasted_iota(jnp.int32, cur.shape, 0) % short_len == 0
            prev = jnp.where(first, si_ref[...], pltpu.roll(cur, 1, axis=0))
            return cur + mu_ref[...] * (prev - cur)
    else:
        c = pl.program_id(1)
        _carried_window(rkvbuf, rkv_ref, si_rkv_ref, c == 0, q)
        _carried_window(lorabuf, lora_ref, si_lora_ref, c == 0, q)

        def shifted(buf, cur_ref, si_ref, mu_ref):
            cur = cur_ref[...]
            prev = buf[pl.ds(SUBLANES - 1, q), :]
            return cur + mu_ref[...] * (prev - cur)

    ps = shifted(rkvbuf, rkv_ref, si_rkv_ref, mu_rkv_ref)
    lo_in = shifted(lorabuf, lora_ref, si_lora_ref, mu_lora_ref)
    if not short_len:
        _advance_window(rkvbuf, q)
        _advance_window(lorabuf, q)
    r = ps[:, 0:D]
    k = ps[:, D:2 * D]
    v = ps[:, 2 * D:3 * D]

    lw = jnp.dot(jnp.tanh(lo_in).astype(BF16), w2_ref[...], preferred_element_type=F32)
    la = jnp.dot(lo_in.astype(BF16), a2_ref[...], preferred_element_type=F32)
    g = jnp.dot(jax.nn.sigmoid(lo_in).astype(BF16), g2_ref[...], preferred_element_type=F32)
    wlog = -_softplus(-(w0_ref[...] + lw)) - 0.5
    log_w = -jnp.exp(wlog)
    a = jax.nn.sigmoid(a0_ref[...] + la)
    kkf = k * kk_ref[...]
    norm = jnp.maximum(jnp.sqrt(_head_allsum(kkf * kkf)), 1e-12)
    kk = kkf / norm
    k2 = k * (1.0 + (a - 1.0) * ka_ref[...])
    bonus = _head_allsum(r * k2 * rk_ref[...]) * v
    r_out[...] = r
    w_out[...] = log_w if log_decay else jnp.exp(log_w)
    k_out[...] = k2
    v_out[...] = v
    kk_out[...] = kk
    b_out[...] = kk * a
    bonus_out[...] = bonus
    g_out[...] = g


def _rwkv_pre(p, shift_state, prm, nseq, seq_len, log_decay):
    short_len = seq_len if seq_len < CHUNK else 0
    q = CHUNK if short_len else min(CHUNK, seq_len)
    nc = 1 if short_len else seq_len // q
    ntile = nseq * seq_len // q // nc
    const = lambda shape: pl.BlockSpec(shape, lambda s, c: (0,) * len(shape))
    tile = pl.BlockSpec((q, D), lambda s, c: (s * nc + c, 0))
    if short_len:
        dense = lambda a: jnp.pad(a, ((0, 0), (0, seq_len - 1), (0, 0))).reshape(nseq * seq_len, a.shape[-1])
        si = (dense(shift_state[:, :, :3 * D]), dense(shift_state[:, :, 3 * D:]))
        si_specs = [pl.BlockSpec((q, 3 * D), lambda s, c: (s, 0)), pl.BlockSpec((q, LORA_W), lambda s, c: (s, 0))]
    else:
        si = (_pad_front_rows(shift_state[:, :, :3 * D]), _pad_front_rows(shift_state[:, :, 3 * D:]))
        si_specs = [pl.BlockSpec((1, SUBLANES, 3 * D), lambda s, c: (s, 0, 0)),
                    pl.BlockSpec((1, SUBLANES, LORA_W), lambda s, c: (s, 0, 0))]
    sds = jax.ShapeDtypeStruct((nseq * seq_len, D), F32)
    return pl.pallas_call(
        functools.partial(_rwkv_pre_body, q=q, log_decay=log_decay, short_len=short_len),
        out_shape=(sds,) * 8,
        grid=(ntile, nc),
        in_specs=[
            pl.BlockSpec((q, 3 * D), lambda s, c: (s * nc + c, 0)),
            pl.BlockSpec((q, LORA_W), lambda s, c: (s * nc + c, COL_LORA // LORA_W)),
            *si_specs,
            const((1, 3 * D)), const((1, LORA_W)),
            const((1, D)), const((LORA_W, D)), const((1, D)), const((LORA_W, D)), const((LORA_W, D)),
            const((1, D)), const((1, D)), const((1, D)),
        ],
        out_specs=(tile,) * 8,
        scratch_shapes=[pltpu.VMEM((CHUNK + SUBLANES, 3 * D), F32), pltpu.VMEM((CHUNK + SUBLANES, LORA_W), F32)],
        compiler_params=_cparams(("parallel", "arbitrary"), 48),
        name="rwkv_pre",
    )(p, p, *si, *prm)


def _split_bf16(x):
    hi = x.astype(BF16)
    return hi, (x - hi.astype(F32)).astype(BF16)


_NN = (((1,), (0,)), ((), ()))
_NT = (((1,), (1,)), ((), ()))
_TN = (((0,), (0,)), ((), ()))


def _solve_unit_lower(n, rhs):
    rows, width = rhs.shape
    nblk, ncol = rows // SUBLANES, width // LANES
    nb = [[n[SUBLANES * i:SUBLANES * (i + 1), LANES * j:LANES * (j + 1)] for j in range(ncol)] for i in range(nblk)]
    xb = [[rhs[SUBLANES * i:SUBLANES * (i + 1), LANES * j:LANES * (j + 1)] for j in range(ncol)] for i in range(nblk)]
    low = lax.broadcasted_iota(jnp.int32, (SUBLANES, LANES), 1) < HEAD
    for s in range(rows - 1):
        i0, r0 = divmod(s, SUBLANES)
        idx = jnp.where(low, s, HEAD + s)
        for j in range(ncol):
            row = xb[i0][j][r0:r0 + 1, :]
            for i in range(i0 if r0 < SUBLANES - 1 else i0 + 1, nblk):
                xb[i][j] = xb[i][j] - jnp.take_along_axis(nb[i][j], idx, axis=1) * row
    return jnp.concatenate([jnp.concatenate(xr, axis=1) for xr in xb], axis=0)


def _wkv_tile(r, lw, k, v, kk, b, st):
    rows, width = r.shape
    nh = WKV_LANES // HEAD
    groups = [slice(g * WKV_LANES, (g + 1) * WKV_LANES) for g in range(width // WKV_LANES)]
    cl = _cumsum_rows(lw)
    cl_last = cl[rows - 1:rows, :]
    p_inv = jnp.exp(-cl)
    p_end = jnp.exp(cl_last - cl)
    x2h, x2l = _split_bf16(jnp.concatenate([kk * jnp.exp(cl - lw), r * jnp.exp(cl)], axis=0))
    k_hat = k * p_inv
    b_hat = b * p_inv
    k_end = k * p_end
    b_end = -(b * p_end)

    bd_r = lax.broadcasted_iota(jnp.int32, (nh * rows, WKV_LANES), 0) // rows
    bd_c = lax.broadcasted_iota(jnp.int32, (nh * rows, WKV_LANES), 1) // HEAD
    bd_mask = jnp.where(bd_r == bd_c, 1.0, 0.0).astype(BF16)

    def per_head_rows(y):
        return [jnp.concatenate([part] * nh, axis=0) * bd_mask for part in _split_bf16(y)]

    def dot3(ah, al, bh, bl, dims):
        dg = lambda x, y: lax.dot_general(x, y, dims, preferred_element_type=F32)
        return dg(ah, bh) + (dg(ah, bl) + dg(al, bh))

    t_i = lax.broadcasted_iota(jnp.int32, (rows, WKV_LANES), 0)
    lane = lax.broadcasted_iota(jnp.int32, (rows, WKV_LANES), 1)
    s_i = lane & (HEAD - 1)
    strict = t_i > s_i
    incl = t_i >= s_i
    head_of_lane = lane // HEAD

    a_kb, a_rb, base, o_part = [], [], [], []
    for gs in groups:
        ak = dot3(x2h[:, gs], x2l[:, gs], *per_head_rows(k_hat[:, gs]), _NT)
        ab = dot3(x2h[:, gs], x2l[:, gs], *per_head_rows(b_hat[:, gs]), _NT)
        a_k = jnp.concatenate([jnp.where(strict, ak[0:rows], 0.0), jnp.where(incl, ak[rows:], 0.0)], axis=0)
        xs = dot3(x2h[:, gs], x2l[:, gs], *per_head_rows(st[:, gs]), _NN)
        av = dot3(*_split_bf16(a_k), *per_head_rows(v[:, gs]), _NN)
        a_kb.append(jnp.where(strict, ab[0:rows], 0.0))
        a_rb.append(jnp.where(incl, ab[rows:], 0.0))
        base.append(xs[0:rows] + av[0:rows])
        o_part.append(xs[rows:] + av[rows:])
    sa = _solve_unit_lower(jnp.concatenate(a_kb, axis=1), jnp.concatenate(base, axis=1))

    o, st_new = [], []
    for g, gs in enumerate(groups):
        o.append(o_part[g] - dot3(*_split_bf16(a_rb[g]), *per_head_rows(sa[:, gs]), _NN))
        eye = jnp.where(t_i == s_i, jnp.exp(cl_last[:, gs]), 0.0)
        lhs = jnp.concatenate([k_end[:, gs], b_end[:, gs], eye], axis=0)
        rhs = jnp.concatenate([v[:, gs], sa[:, gs], st[:, gs]], axis=0)
        full = dot3(*_split_bf16(lhs), *_split_bf16(rhs), _TN)
        acc = jnp.where(head_of_lane == 0, full[0:HEAD, :], 0.0)
        for h in range(1, nh):
            acc = acc + jnp.where(head_of_lane == h, full[h * HEAD:(h + 1) * HEAD, :], 0.0)
        st_new.append(acc)
    return jnp.concatenate(o, axis=1), jnp.concatenate(st_new, axis=1)


def _wkv_long_body(r_ref, lw_ref, k_ref, v_ref, kk_ref, b_ref, o_ref, so_ref, st_s, *, nc):
    c = pl.program_id(2)

    @pl.when(c == 0)
    def _():
        st_s[...] = jnp.zeros_like(st_s)

    o, st_new = _wkv_tile(r_ref[...], lw_ref[...], k_ref[...], v_ref[...], kk_ref[...], b_ref[...], st_s[...])
    o_ref[...] = o
    st_s[...] = st_new

    @pl.when(c == nc - 1)
    def _():
        for j in range(WKV_GROUPS * WKV_LANES // LANES):
            js = slice(j * LANES, (j + 1) * LANES)
            sq = jnp.concatenate([st_s[:, js], jnp.zeros((LANES - HEAD, LANES), F32)], axis=0)
            so_ref[0, js, :] = sq.T[:, 0:HEAD]


def _wkv_long(r, lw, k, v, kk, b, nseq, seq_len):
    nc = seq_len // WKV_CHUNK
    width = WKV_GROUPS * WKV_LANES
    tile = pl.BlockSpec((WKV_CHUNK, width), lambda s, hg, c: (s * nc + c, hg))
    o, s_last = pl.pallas_call(
        functools.partial(_wkv_long_body, nc=nc),
        out_shape=(jax.ShapeDtypeStruct((nseq * seq_len, D), F32),
                   jax.ShapeDtypeStruct((nseq, D, HEAD), F32)),
        grid=(nseq, D // width, nc),
        in_specs=[tile] * 6,
        out_specs=(tile, pl.BlockSpec((1, width, HEAD), lambda s, hg, c: (s, hg, 0))),
        scratch_shapes=[pltpu.VMEM((HEAD, width), F32)],
        compiler_params=_cparams(("parallel", "parallel", "arbitrary"), 32),
        name="wkv_long",
    )(r, lw, k, v, kk, b)
    return o, s_last


def _wkv_short_body(r_ref, w_ref, k_ref, v_ref, kk_ref, b_ref, s0_ref, o_ref, so_ref, *, steps):
    def per_v_group(i, carry):
        vis = [i * WKV_SHORT_ROWS + u for u in range(WKV_SHORT_ROWS)]
        ss = [s0_ref[0, vi] for vi in vis]
        for t in range(steps):
            kk_t, w_t, b_t, k_t, r_t = kk_ref[t, 0], w_ref[t, 0], b_ref[t, 0], k_ref[t, 0], r_ref[t, 0]
            for u, vi in enumerate(vis):
                vrow = v_ref[t, 0, pl.ds(vi, 1), :]
                sa = jnp.sum(ss[u] * kk_t, axis=0, keepdims=True)
                ss[u] = ss[u] * w_t - b_t * sa + k_t * vrow
                o_ref[t, 0, pl.ds(vi, 1), :] = jnp.sum(ss[u] * r_t, axis=0, keepdims=True)
        for u, vi in enumerate(vis):
            so_ref[0, vi] = ss[u]
        return carry

    lax.fori_loop(0, HEAD // WKV_SHORT_ROWS, per_v_group, 0)


def _wkv_short(r, w, k, v, kk, b, s0, nseq, seq_len):
    vec = pl.BlockSpec((seq_len, 1, HEAD, nseq), lambda h: (0, h, 0, 0))
    st = pl.BlockSpec((1, HEAD, HEAD, nseq), lambda h: (h, 0, 0, 0))
    return pl.pallas_call(
        functools.partial(_wkv_short_body, steps=seq_len),
        out_shape=(jax.ShapeDtypeStruct((seq_len, N_HEADS, HEAD, nseq), F32),
                   jax.ShapeDtypeStruct((N_HEADS, HEAD, HEAD, nseq), F32)),
        grid=(N_HEADS,),
        in_specs=[vec] * 6 + [st],
        out_specs=(vec, st),
        compiler_params=_cparams(("parallel",), 32),
        name="wkv_short",
    )(r, w, k, v, kk, b, s0)


def _rwkv_post_body(o_ref, bonus_ref, g_ref, lnw_ref, lnb_ref, y_ref):
    o = o_ref[...]
    mean = _head_allsum(o) * (1.0 / HEAD)
    cen = o - mean
    var = _head_allsum(cen * cen) * (1.0 / HEAD)
    on = cen * lax.rsqrt(var + GN_EPS) * lnw_ref[...] + lnb_ref[...]
    y_ref[...] = (on + bonus_ref[...]) * g_ref[...]


def _rwkv_post(o, bonus, g, ln_w, ln_b):
    m = o.shape[0]
    tile = pl.BlockSpec((TM_POST, D), lambda i: (i, 0))
    const = pl.BlockSpec((1, D), lambda i: (0, 0))
    return pl.pallas_call(
        _rwkv_post_body,
        out_shape=jax.ShapeDtypeStruct((m, D), F32),
        grid=(m // TM_POST,),
        in_specs=[tile, tile, tile, const, const],
        out_specs=tile,
        compiler_params=_cparams(("parallel",), 32),
        name="rwkv_post",
    )(o, bonus, g, ln_w, ln_b)


def _lru_coeffs(xc, wga_ref, bga_ref, wgx_ref, bgx_ref, lam_ref):
    ra, rx = [], []
    for blk in range(LRU_BLOCKS):
        xh = xc[:, blk * LRU_BLK:(blk + 1) * LRU_BLK].astype(BF16)
        ra.append(jnp.dot(xh, wga_ref[blk], preferred_element_type=F32))
        rx.append(jnp.dot(xh, wgx_ref[blk], preferred_element_type=F32))
    rg = jax.nn.sigmoid(jnp.concatenate(ra, axis=1) + bga_ref[...])
    ig = jax.nn.sigmoid(jnp.concatenate(rx, axis=1) + bgx_ref[...])
    log_a = -LRU_C * rg * _softplus(-lam_ref[...])
    return jnp.exp(log_a), jnp.sqrt(1.0 - jnp.exp(2.0 * log_a)) * (ig * xc)


def _lru_body(gate_ref, x_ref, ci_ref, h0_ref, cw_ref, cb_ref, wga_ref, bga_ref, wgx_ref, bgx_ref, lam_ref,
              y_ref, ho_ref, xbuf, a_s, b_s, h_s, hc_s, *, q, nc):
    c = pl.program_id(1)
    _carried_window(xbuf, x_ref, ci_ref, c == 0, q)

    @pl.when(c == 0)
    def _():
        hc_s[...] = h0_ref[0]

    xc = _causal_conv(xbuf, cw_ref, cb_ref, q)
    _advance_window(xbuf, q)
    a_s[...], b_s[...] = _lru_coeffs(xc, wga_ref, bga_ref, wgx_ref, bgx_ref, lam_ref)

    def step(t, h):
        h = a_s[pl.ds(t, 1), :] * h + b_s[pl.ds(t, 1), :]
        h_s[pl.ds(t, 1), :] = h
        return h

    h = lax.fori_loop(0, q, step, hc_s[...])
    hc_s[...] = h
    y_ref[...] = h_s[...] * jax.nn.gelu(gate_ref[...])

    @pl.when(c == nc - 1)
    def _():
        ho_ref[0] = h


def _lru(p, conv_init, h0, prm, nseq, seq_len):
    q = min(CHUNK, seq_len)
    nc = seq_len // q
    per_seq = lambda shape: pl.BlockSpec((1,) + shape, lambda s, c: (s,) + (0,) * len(shape))
    const = lambda shape: pl.BlockSpec(shape, lambda s, c: (0,) * len(shape))
    return pl.pallas_call(
        functools.partial(_lru_body, q=q, nc=nc),
        out_shape=(jax.ShapeDtypeStruct((nseq * seq_len, D), F32),
                   jax.ShapeDtypeStruct((nseq, 1, D), F32)),
        grid=(nseq, nc),
        in_specs=[
            pl.BlockSpec((q, D), lambda s, c: (s * nc + c, 0)),
            pl.BlockSpec((q, D), lambda s, c: (s * nc + c, 1)),
            per_seq((SUBLANES, D)), per_seq((1, D)),
            const((CONV_W, D)), const((1, D)),
            const((LRU_BLOCKS, LRU_BLK, LRU_BLK)), const((1, D)),
            const((LRU_BLOCKS, LRU_BLK, LRU_BLK)), const((1, D)), const((1, D)),
        ],
        out_specs=(pl.BlockSpec((q, D), lambda s, c: (s * nc + c, 0)), per_seq((1, D))),
        scratch_shapes=[pltpu.VMEM((CHUNK + SUBLANES, D), F32), pltpu.VMEM((q, D), F32), pltpu.VMEM((q, D), F32),
                        pltpu.VMEM((q, D), F32), pltpu.VMEM((1, D), F32)],
        compiler_params=_cparams(("parallel", "arbitrary"), 32),
        name="lru",
    )(p, p, conv_init, h0, *prm)


def _lru_short_body(gate_ref, x_ref, p1_ref, p2_ref, p3_ref, h0_ref, cw_ref, cb_ref, wga_ref, bga_ref, wgx_ref,
                    bgx_ref, lam_ref, y_ref, h_ref, *, seq_len):
    cur = x_ref[...]
    t = lax.broadcasted_iota(jnp.int32, cur.shape, 0) % seq_len
    prev = lambda s, ref: jnp.where(t >= s, pltpu.roll(cur, s, axis=0), ref[...])
    xc = (cb_ref[...] + cw_ref[0:1, :] * prev(3, p3_ref) + cw_ref[1:2, :] * prev(2, p2_ref)
          + cw_ref[2:3, :] * prev(1, p1_ref) + cw_ref[3:4, :] * cur)
    a, b = _lru_coeffs(xc, wga_ref, bga_ref, wgx_ref, bgx_ref, lam_ref)
    s = 1
    while s < seq_len:
        keep = t >= s
        b = jnp.where(keep, a * pltpu.roll(b, s, axis=0) + b, b)
        a = jnp.where(keep, a * pltpu.roll(a, s, axis=0), a)
        s *= 2
    h = a * h0_ref[...] + b
    h_ref[...] = h
    y_ref[...] = h * jax.nn.gelu(gate_ref[...])


def _lru_short(p, conv_state, h0, prm, nseq, seq_len):
    rows = nseq * seq_len
    dense = lambda s: jnp.pad(conv_state[:, CONV_W - 1 - s:, :], ((0, 0), (0, seq_len - s), (0, 0))).reshape(rows, D)
    h0_rows = jnp.repeat(h0, seq_len, axis=0)
    tile = lambda col: pl.BlockSpec((CHUNK, D), lambda i: (i, col))
    const = lambda shape: pl.BlockSpec(shape, lambda i: (0,) * len(shape))
    sds = jax.ShapeDtypeStruct((rows, D), F32)
    return pl.pallas_call(
        functools.partial(_lru_short_body, seq_len=seq_len),
        out_shape=(sds, sds),
        grid=(rows // CHUNK,),
        in_specs=[tile(0), tile(1), tile(0), tile(0), tile(0), tile(0),
                  const((CONV_W, D)), const((1, D)),
                  const((LRU_BLOCKS, LRU_BLK, LRU_BLK)), const((1, D)),
                  const((LRU_BLOCKS, LRU_BLK, LRU_BLK)), const((1, D)), const((1, D))],
        out_specs=(tile(0), tile(0)),
        compiler_params=_cparams(("parallel",), 32),
        name="lru_short",
    )(p, p, dense(1), dense(2), dense(3), h0_rows, *prm)


def _pad_front_rows(buf):
    return jnp.pad(buf, ((0, 0), (SUBLANES - buf.shape[1], 0), (0, 0)))


def _row2(v):
    return v.reshape(1, -1)


def kernel(x_prompt, x_sample, state_ssm_a, state_conv_a, state_wkv_b, state_shift_b, state_lru_c, state_conv_c, norm_gain, w_ffn_in, w_ffn_out, w_in_ab, conv_w_a, conv_b_a, dt_bias_a, a_log_a, d_skip_a, gnorm_a, mu_b, w0_b, w2_b, a0_b, a2_b, g2_b, k_k_b, k_a_b, r_k_b, ln_w_b, ln_b_b, w_out_ab, w_in_c, conv_w_c, conv_b_c, w_gate_a_c, b_gate_a_c, w_gate_x_c, b_gate_x_c, lambda_c, w_out_c, final_norm_gain):
    w_gu, w_dn = _ffn_weights(w_ffn_in, w_ffn_out)

    in_a = D + (D + 2 * N_GROUPS * N_STATE) + N_HEADS
    wab = w_in_ab[0]
    bc_w = 2 * N_GROUPS * N_STATE
    w_proj0 = jnp.concatenate([
        wab[:, in_a:in_a + 3 * D],
        wab[:, 0:D],
        wab[:, D:2 * D],
        wab[:, 2 * D:2 * D + bc_w],
        wab[:, in_a + 3 * D:in_a + 3 * D + LORA_W],
        wab[:, in_a - N_HEADS:in_a],
        jnp.zeros((D, PROJ_W - COL_DT - N_HEADS), F32),
    ], axis=1).astype(BF16)
    w_proj1 = w_in_c[0].astype(BF16)
    w_out0 = w_out_ab[0].astype(BF16)
    w_out1 = w_out_c[0].astype(BF16)

    pad_lanes = lambda v: jnp.pad(v.reshape(1, -1), ((0, 0), (0, LANES - v.shape[-1])))
    rep_head = lambda v: jnp.repeat(v, HEAD).reshape(1, D)
    ssd_prm = (conv_w_a[0][:, :D], conv_w_a[0][:, D:], _row2(conv_b_a[0][:D]), _row2(conv_b_a[0][D:]),
               pad_lanes(dt_bias_a[0]), pad_lanes(a_log_a[0]), rep_head(d_skip_a[0]), _row2(gnorm_a[0]))
    lora_rows = lambda w, lo: jnp.pad(w, ((lo, LORA_W - lo - w.shape[0]), (0, 0))).astype(BF16)
    mu = mu_b[0]
    rwkv_prm = (_row2(mu[:3 * D]), _row2(mu[3 * D:]),
                _row2(w0_b[0]), lora_rows(w2_b[0], 0), _row2(a0_b[0]), lora_rows(a2_b[0], 64),
                lora_rows(g2_b[0], 128), _row2(k_k_b[0]), _row2(k_a_b[0]), _row2(r_k_b[0]))
    lru_prm = (conv_w_c[0], _row2(conv_b_c[0]), w_gate_a_c[0].astype(BF16), _row2(b_gate_a_c[0]),
               w_gate_x_c[0].astype(BF16), _row2(b_gate_x_c[0]), _row2(lambda_c[0]))

    def trunk(x3, ssm0, conva0, wkv0, shift0, lru0, convc0):
        nseq, seq_len, _ = x3.shape
        x = x3.reshape(nseq * seq_len, D)
        tail = lambda arr, n: arr.reshape(nseq, seq_len, arr.shape[-1])[:, seq_len - n:, :]

        x = _ffn(x, _row2(norm_gain[0, 0]), w_gu, w_dn, (0, 0))
        p = _proj(x, _row2(norm_gain[0, 1]), w_proj0)
        if seq_len >= CHUNK:
            ya, ssm_n = _ssd(p, _pad_front_rows(conva0[:, :, :D]), _pad_front_rows(conva0[:, :, D:]),
                             ssm0.reshape(nseq, D, N_STATE), ssd_prm, nseq, seq_len)
        else:
            ya, ssm_n = _ssd_short(p, conva0, ssm0.reshape(nseq, D, N_STATE), ssd_prm, nseq, seq_len)
        p_tail = tail(p, CONV_W - 1)
        conva_n = jnp.concatenate([p_tail[:, :, COL_X:COL_X + D], p_tail[:, :, COL_BC:COL_BC + bc_w]], axis=-1)
        shift_n = jnp.concatenate([p_tail[:, -1:, :3 * D], p_tail[:, -1:, COL_LORA:COL_LORA + LORA_W]], axis=-1)

        r, w, k, v, kk, b, bonus, g = _rwkv_pre(p, shift0, rwkv_prm, nseq, seq_len, log_decay=wkv0 is None)
        if wkv0 is None:
            o, wkv_n = _wkv_long(r, w, k, v, kk, b, nseq, seq_len)
            wkv_n = wkv_n.reshape(nseq, N_HEADS, HEAD, HEAD)
        else:
            to_lanes = lambda a: a.reshape(nseq, seq_len, N_HEADS, HEAD).transpose(1, 2, 3, 0)
            o, wkv_n = _wkv_short(*[to_lanes(a) for a in (r, w, k, v, kk, b)], wkv0.transpose(1, 2, 3, 0),
                                  nseq, seq_len)
            o = o.transpose(3, 0, 1, 2).reshape(nseq * seq_len, D)
            wkv_n = wkv_n.transpose(3, 0, 1, 2)
        yb = _rwkv_post(o, bonus, g, _row2(ln_w_b[0]), _row2(ln_b_b[0]))
        x = _out_proj(x, (ya, yb), (w_out0[:D], w_out0[D:]))
        x = _ffn(x, _row2(norm_gain[0, 2]), w_gu, w_dn, (0, 1))

        x = _ffn(x, _row2(norm_gain[1, 0]), w_gu, w_dn, (1, 0))
        pc = _proj(x, _row2(norm_gain[1, 1]), w_proj1)
        if seq_len >= CHUNK:
            yc, lru_n = _lru(pc, _pad_front_rows(convc0), lru0.reshape(nseq, 1, D), lru_prm, nseq, seq_len)
        else:
            yc, h_rows = _lru_short(pc, convc0, lru0, lru_prm, nseq, seq_len)
            lru_n = tail(h_rows, 1)
        convc_n = tail(pc, CONV_W - 1)[:, :, D:]
        x = _out_proj(x, (yc,), (w_out1,))
        y = _ffn(x, _row2(norm_gain[1, 2]), w_gu, w_dn, (1, 1), final_gain=_row2(final_norm_gain))

        return (y.reshape(nseq, seq_len, D), ssm_n.reshape(1, nseq, N_HEADS, HEAD, N_STATE), conva_n[None],
                wkv_n[None], shift_n[None], lru_n.reshape(1, nseq, D), convc_n[None])

    bp = x_prompt.shape[0]
    zeros = lambda s: jnp.zeros((bp,) + s.shape[2:], F32)
    outs_p = trunk(x_prompt, zeros(state_ssm_a), zeros(state_conv_a), None, zeros(state_shift_b),
                   zeros(state_lru_c), zeros(state_conv_c))
    outs_s = trunk(x_sample, state_ssm_a[0], state_conv_a[0], state_wkv_b[0], state_shift_b[0],
                   state_lru_c[0], state_conv_c[0])
    return (outs_p[0], outs_s[0]) + outs_p[1:] + outs_s[1:]
```

```python
import functools

import jax
import jax.numpy as jnp
from jax import lax
from jax.experimental import pallas as pl
from jax.experimental.pallas import tpu as pltpu

F32 = jnp.float32
BF16 = jnp.bfloat16

D = 2048
D_FF = 5504
D_FF_PAD = 5632
HEAD = 64
N_HEADS = 32
N_GROUPS = 4
GROUP_W = D // N_GROUPS
N_STATE = 128
CONV_W = 4
LORA_W = 256
LRU_BLOCKS = 8
LRU_BLK = D // LRU_BLOCKS
LRU_C = 8.0
EPS = 1e-6
GN_EPS = 64e-5
SUBLANES = 8
LANES = 128
CHUNK = 128
WKV_CHUNK = 64
WKV_LANES = 256
WKV_GROUPS = 4
WKV_SHORT_ROWS = 4
SSD_SHORT_SEQS = 8

COL_RKV = 0
COL_Z = 3 * D
COL_X = 4 * D
COL_BC = 5 * D
COL_LORA = 5 * D + 2 * N_GROUPS * N_STATE
COL_DT = COL_LORA + LORA_W
PROJ_W = COL_DT + 256

TM_FFN = 512
TF_FFN = 512
TM_PROJ = 1024
TN_PROJ = 512
TM_OUT = 512
TN_OUT = 1024
TM_POST = 256
W_IN_ROWS = 64


def _cparams(sem, vmem_mib):
    return pltpu.CompilerParams(dimension_semantics=sem, vmem_limit_bytes=vmem_mib * 1024 * 1024)


def _softplus(x):
    return jnp.maximum(x, 0.0) + jnp.log(1.0 + jnp.exp(-jnp.abs(x)))


def _silu(x):
    return x * jax.nn.sigmoid(x)


def _rms(x, gain):
    ms = jnp.mean(x * x, axis=-1, keepdims=True)
    return x * lax.rsqrt(ms + EPS) * gain


def _ffn_body(x_ref, g_ref, wg_ref, wu_ref, wo_ref, fg_ref, o_ref, xn_ref, acc_ref, *, nf, final):
    f = pl.program_id(1)

    @pl.when(f == 0)
    def _():
        xn_ref[...] = _rms(x_ref[...], g_ref[...]).astype(BF16)
        acc_ref[...] = jnp.zeros_like(acc_ref)

    xn = xn_ref[...]
    gate = jnp.dot(xn, wg_ref[...], preferred_element_type=F32)
    up = jnp.dot(xn, wu_ref[...], preferred_element_type=F32)
    h = (_silu(gate) * up).astype(BF16)
    acc_ref[...] += jnp.dot(h, wo_ref[...], preferred_element_type=F32)

    @pl.when(f == nf - 1)
    def _():
        y = x_ref[...] + 0.5 * acc_ref[...]
        if final:
            y = _rms(y, fg_ref[...])
        o_ref[...] = y


def _ffn(x, gain, w_gu, w_dn, which, final_gain=None):
    m = x.shape[0]
    li, si = which
    nf = D_FF_PAD // TF_FFN
    final = final_gain is not None
    fg = final_gain if final else gain
    return pl.pallas_call(
        functools.partial(_ffn_body, nf=nf, final=final),
        out_shape=jax.ShapeDtypeStruct((m, D), F32),
        grid=(m // TM_FFN, nf),
        in_specs=[
            pl.BlockSpec((TM_FFN, D), lambda i, f: (i, 0)),
            pl.BlockSpec((1, D), lambda i, f: (0, 0)),
            pl.BlockSpec((None, None, D, TF_FFN), lambda i, f: (li, si, 0, f)),
            pl.BlockSpec((None, None, D, TF_FFN), lambda i, f: (li, si, 0, f + nf)),
            pl.BlockSpec((None, None, TF_FFN, D), lambda i, f: (li, si, f, 0)),
            pl.BlockSpec((1, D), lambda i, f: (0, 0)),
        ],
        out_specs=pl.BlockSpec((TM_FFN, D), lambda i, f: (i, 0)),
        scratch_shapes=[pltpu.VMEM((TM_FFN, D), BF16), pltpu.VMEM((TM_FFN, D), F32)],
        compiler_params=_cparams(("parallel", "arbitrary"), 48),
        name="ffn",
    )(x, gain, w_gu, w_gu, w_dn, fg)


def _proj_body(x_ref, g_ref, w_ref, o_ref, xn_ref):
    @pl.when(pl.program_id(1) == 0)
    def _():
        xn_ref[...] = _rms(x_ref[...], g_ref[...]).astype(BF16)

    o_ref[...] = jnp.dot(xn_ref[...], w_ref[...], preferred_element_type=F32)


def _proj(x, gain, w):
    m = x.shape[0]
    n = w.shape[1]
    return pl.pallas_call(
        _proj_body,
        out_shape=jax.ShapeDtypeStruct((m, n), F32),
        grid=(m // TM_PROJ, n // TN_PROJ),
        in_specs=[
            pl.BlockSpec((TM_PROJ, D), lambda i, j: (i, 0)),
            pl.BlockSpec((1, D), lambda i, j: (0, 0)),
            pl.BlockSpec((D, TN_PROJ), lambda i, j: (0, j)),
        ],
        out_specs=pl.BlockSpec((TM_PROJ, TN_PROJ), lambda i, j: (i, j)),
        scratch_shapes=[pltpu.VMEM((TM_PROJ, D), BF16)],
        compiler_params=_cparams(("parallel", "arbitrary"), 40),
        name="proj",
    )(x, gain, w)


def _out2_body(res_ref, ya_ref, yb_ref, wa_ref, wb_ref, o_ref):
    acc = jnp.dot(ya_ref[...].astype(BF16), wa_ref[...], preferred_element_type=F32)
    acc = acc + jnp.dot(yb_ref[...].astype(BF16), wb_ref[...], preferred_element_type=F32)
    o_ref[...] = res_ref[...] + acc


def _out1_body(res_ref, y_ref, w_ref, o_ref):
    o_ref[...] = res_ref[...] + jnp.dot(y_ref[...].astype(BF16), w_ref[...], preferred_element_type=F32)


def _out_proj(res, ys, ws):
    m = res.shape[0]
    body = _out2_body if len(ys) == 2 else _out1_body
    y_spec = pl.BlockSpec((TM_OUT, D), lambda j, i: (i, 0))
    w_spec = pl.BlockSpec((D, TN_OUT), lambda j, i: (0, j))
    r_spec = pl.BlockSpec((TM_OUT, TN_OUT), lambda j, i: (i, j))
    return pl.pallas_call(
        body,
        out_shape=jax.ShapeDtypeStruct((m, D), F32),
        grid=(D // TN_OUT, m // TM_OUT),
        in_specs=[r_spec] + [y_spec] * len(ys) + [w_spec] * len(ws),
        out_specs=r_spec,
        compiler_params=_cparams(("arbitrary", "arbitrary"), 48),
        name="out_proj",
    )(res, *ys, *ws)


def _cast_pad_body(w_ref, o_ref, *, axis, zero_blocks):
    c = pl.program_id(axis)
    is_pad = functools.reduce(jnp.logical_or, [c == z for z in zero_blocks])
    o_ref[...] = jnp.where(is_pad, 0.0, w_ref[...]).astype(BF16)


def _cast_pad_halves_body(w_ref, o_ref):
    rows = o_ref.shape[0]
    zeros = jnp.zeros((rows, D_FF_PAD - D_FF), BF16)
    for half in range(2):
        o_ref[:, half * D_FF_PAD:half * D_FF_PAD + D_FF] = w_ref[:, half * D_FF:(half + 1) * D_FF].astype(BF16)
        o_ref[:, half * D_FF_PAD + D_FF:(half + 1) * D_FF_PAD] = zeros


def _ffn_weights(w_in, w_out):
    n_layers, n_slots = w_in.shape[:2]
    nb = D_FF // LANES
    nbp = D_FF_PAD // LANES
    w_gu = pl.pallas_call(
        _cast_pad_halves_body,
        out_shape=jax.ShapeDtypeStruct((n_layers, n_slots, D, 2 * D_FF_PAD), BF16),
        grid=(n_layers * n_slots, D // W_IN_ROWS),
        in_specs=[pl.BlockSpec((None, None, W_IN_ROWS, 2 * D_FF), lambda q, r: (q // n_slots, q % n_slots, r, 0))],
        out_specs=pl.BlockSpec((None, None, W_IN_ROWS, 2 * D_FF_PAD), lambda q, r: (q // n_slots, q % n_slots, r, 0)),
        compiler_params=_cparams(("parallel", "parallel"), 32),
        name="ffn_w_in",
    )(w_in)
    w_dn = pl.pallas_call(
        functools.partial(_cast_pad_body, axis=1, zero_blocks=tuple(range(nb, nbp))),
        out_shape=jax.ShapeDtypeStruct((n_layers, n_slots, D_FF_PAD, D), BF16),
        grid=(n_layers * n_slots, nbp),
        in_specs=[pl.BlockSpec((None, None, LANES, D),
                               lambda q, r: (q // n_slots, q % n_slots, jnp.minimum(r, nb - 1), 0))],
        out_specs=pl.BlockSpec((None, None, LANES, D), lambda q, r: (q // n_slots, q % n_slots, r, 0)),
        compiler_params=_cparams(("parallel", "parallel"), 32),
        name="ffn_w_out",
    )(w_out)
    return w_gu, w_dn


def _carried_window(buf, cur_ref, init_ref, first, q):
    @pl.when(first)
    def _():
        buf[0:SUBLANES, :] = init_ref[0]

    buf[SUBLANES:SUBLANES + q, :] = cur_ref[...]


def _advance_window(buf, q):
    tail = buf[q:q + SUBLANES, :]
    buf[0:SUBLANES, :] = tail


def _causal_conv(buf, cw_ref, cb_ref, q):
    acc = cb_ref[...] + cw_ref[0:1, :] * buf[pl.ds(SUBLANES - CONV_W + 1, q), :]
    for k in range(1, CONV_W):
        acc = acc + cw_ref[k:k + 1, :] * buf[pl.ds(SUBLANES - CONV_W + 1 + k, q), :]
    return acc


def _cumsum_rows(x):
    n = x.shape[0]
    row = lax.broadcasted_iota(jnp.int32, x.shape, 0)
    s = 1
    while s < n:
        x = x + jnp.where(row >= s, pltpu.roll(x, s, axis=0), 0.0)
        s *= 2
    return x


def _expand_heads(a, rows):
    lane = lax.broadcasted_iota(jnp.int32, (rows, LANES), 1)
    low = lane < HEAD
    pieces = []
    for j in range(N_HEADS // 2):
        e0 = jnp.broadcast_to(a[:, 2 * j:2 * j + 1], (rows, LANES))
        e1 = jnp.broadcast_to(a[:, 2 * j + 1:2 * j + 2], (rows, LANES))
        pieces.append(jnp.where(low, e0, e1))
    return jnp.concatenate(pieces, axis=1)


def _head_allsum(x):
    width = x.shape[-1]
    blk = 4 * HEAD
    r = lax.broadcasted_iota(jnp.int32, (blk, blk), 0) // HEAD
    c = lax.broadcasted_iota(jnp.int32, (blk, blk), 1) // HEAD
    ones_bd = jnp.where(r == c, 1.0, 0.0).astype(BF16)
    hi = x.astype(BF16)
    rem = x - hi.astype(F32)
    mid = rem.astype(BF16)
    lo = (rem - mid.astype(F32)).astype(BF16)
    dot = lambda p: jnp.dot(p, ones_bd, preferred_element_type=F32)
    cols = []
    for j in range(width // blk):
        sl = slice(j * blk, (j + 1) * blk)
        cols.append(dot(hi[:, sl]) + (dot(mid[:, sl]) + dot(lo[:, sl])))
    return jnp.concatenate(cols, axis=1)


def _ssd_body(z_ref, x_ref, bc_ref, dt_ref, cix_ref, cibc_ref, h0_ref,
              cwx_ref, cwbc_ref, cbx_ref, cbbc_ref, dtb_ref, alog_ref, dsk_ref, gn_ref,
              y_ref, ho_ref, xbuf, bcbuf, h_s, *, nc):
    c = pl.program_id(1)
    q = CHUNK
    _carried_window(xbuf, x_ref, cix_ref, c == 0, q)
    _carried_window(bcbuf, bc_ref, cibc_ref, c == 0, q)

    @pl.when(c == 0)
    def _():
        h_s[...] = h0_ref[0]

    xs = _silu(_causal_conv(xbuf, cwx_ref, cbx_ref, CHUNK))
    bcv = _silu(_causal_conv(bcbuf, cwbc_ref, cbbc_ref, CHUNK))
    _advance_window(xbuf, q)
    _advance_window(bcbuf, q)
    z = z_ref[...]
    dt = _softplus(dt_ref[...] + dtb_ref[...])
    a_head = -jnp.exp(alog_ref[...])
    cum = _cumsum_rows(dt * a_head)
    cum_last = cum[CHUNK - 1:CHUNK, :]
    cum_t = cum.T
    dtx = _expand_heads(dt, CHUNK)
    ecx = _expand_heads(jnp.exp(cum), CHUNK)
    tlx = _expand_heads(jnp.exp(cum_last - cum), CHUNK)
    xdt = xs * dtx
    xdtw = (xdt * tlx).astype(BF16)

    ti = lax.broadcasted_iota(jnp.int32, (CHUNK, CHUNK), 0)
    tj = lax.broadcasted_iota(jnp.int32, (CHUNK, CHUNK), 1)
    causal = ti >= tj
    low = tj < HEAD
    nt = (((1,), (1,)), ((), ()))
    tn = (((0,), (0,)), ((), ()))

    for g in range(N_GROUPS):
        gsl = slice(g * GROUP_W, (g + 1) * GROUP_W)
        bg = bcv[:, g * N_STATE:(g + 1) * N_STATE].astype(BF16)
        cg = bcv[:, (N_GROUPS + g) * N_STATE:(N_GROUPS + g + 1) * N_STATE].astype(BF16)
        cb = lax.dot_general(cg, bg, nt, preferred_element_type=F32)
        hg = h_s[gsl, :]
        y_off = lax.dot_general(cg, hg.astype(BF16), nt, preferred_element_type=F32) * ecx[:, gsl]
        st = lax.dot_general(xdtw[:, gsl], bg, tn, preferred_element_type=F32)
        y_pairs = []
        for j in range(GROUP_W // LANES):
            h0 = g * (GROUP_W // HEAD) + 2 * j
            ms = []
            for h in (h0, h0 + 1):
                seg = cum[:, h:h + 1] - cum_t[h:h + 1, :]
                decay = jnp.exp(jnp.where(causal, seg, -jnp.inf))
                ms.append((cb * decay).astype(BF16))
            psl = slice(h0 * HEAD, (h0 + 2) * HEAD)
            slab = xdt[:, psl]
            rhs = jnp.concatenate([jnp.where(low, slab, 0.0), jnp.where(low, 0.0, slab)], axis=0).astype(BF16)
            y_pairs.append(jnp.dot(jnp.concatenate(ms, axis=1), rhs, preferred_element_type=F32))
            for h in (h0, h0 + 1):
                hsl = slice(h * HEAD, (h + 1) * HEAD)
                dec = jnp.exp(jnp.broadcast_to(cum_last[:, h:h + 1], (HEAD, N_STATE)))
                h_s[hsl, :] = h_s[hsl, :] * dec + st[(h % 8) * HEAD:(h % 8 + 1) * HEAD, :]
        y = jnp.concatenate(y_pairs, axis=1) + y_off + xs[:, gsl] * dsk_ref[:, gsl]
        y = y * _silu(z[:, gsl])
        y_ref[:, gsl] = y * lax.rsqrt(jnp.mean(y * y, axis=-1, keepdims=True) + EPS) * gn_ref[:, gsl]

    @pl.when(c == nc - 1)
    def _():
        ho_ref[0] = h_s[...]


def _ssd(p, conv_x, conv_bc, h0, prm, nseq, seq_len):
    q = CHUNK
    nc = seq_len // q
    rows = lambda w, col: pl.BlockSpec((q, w), lambda s, c: (s * nc + c, col))
    per_seq = lambda shape: pl.BlockSpec((1,) + shape, lambda s, c: (s,) + (0,) * len(shape))
    const = lambda shape: pl.BlockSpec(shape, lambda s, c: (0,) * len(shape))
    y, h_last = pl.pallas_call(
        functools.partial(_ssd_body, nc=nc),
        out_shape=(jax.ShapeDtypeStruct((nseq * seq_len, D), F32),
                   jax.ShapeDtypeStruct((nseq, D, N_STATE), F32)),
        grid=(nseq, nc),
        in_specs=[
            rows(D, COL_Z // D), rows(D, COL_X // D), rows(2 * N_GROUPS * N_STATE, COL_BC // 1024),
            rows(LANES, COL_DT // LANES),
            per_seq((SUBLANES, D)), per_seq((SUBLANES, 1024)), per_seq((D, N_STATE)),
            const((CONV_W, D)), const((CONV_W, 1024)), const((1, D)), const((1, 1024)),
            const((1, LANES)), const((1, LANES)), const((1, D)), const((1, D)),
        ],
        out_specs=(pl.BlockSpec((q, D), lambda s, c: (s * nc + c, 0)), per_seq((D, N_STATE))),
        scratch_shapes=[
            pltpu.VMEM((CHUNK + SUBLANES, D), F32), pltpu.VMEM((CHUNK + SUBLANES, 1024), F32),
            pltpu.VMEM((D, N_STATE), F32),
        ],
        compiler_params=_cparams(("parallel", "arbitrary"), 40),
        name="ssd",
    )(p, p, p, p, conv_x, conv_bc, h0, *prm)
    return y, h_last


def _ssd_short_body(z_ref, x_ref, bc_ref, dt_ref, px1_ref, px2_ref, px3_ref, pb1_ref, pb2_ref, pb3_ref, h0_ref,
                    cwx_ref, cwbc_ref, cbx_ref, cbbc_ref, dtb_ref, alog_ref, dsk_ref, gn_ref,
                    y_ref, ho_ref, *, seq_len):
    rows = x_ref.shape[0]
    nseq_t = rows // seq_len
    t = lax.broadcasted_iota(jnp.int32, (rows, LANES), 0) % seq_len

    def conv(cur_ref, p1_ref, p2_ref, p3_ref, cw_ref, cb_ref):
        cur = cur_ref[...]
        tt = lax.broadcasted_iota(jnp.int32, cur.shape, 0) % seq_len
        prev = lambda s, ref: jnp.where(tt >= s, pltpu.roll(cur, s, axis=0), ref[...])
        return (cb_ref[...] + cw_ref[0:1, :] * prev(3, p3_ref) + cw_ref[1:2, :] * prev(2, p2_ref)
                + cw_ref[2:3, :] * prev(1, p1_ref) + cw_ref[3:4, :] * cur)

    xs = _silu(conv(x_ref, px1_ref, px2_ref, px3_ref, cwx_ref, cbx_ref))
    bcv = _silu(conv(bc_ref, pb1_ref, pb2_ref, pb3_ref, cwbc_ref, cbbc_ref))
    z = z_ref[...]
    dt = _softplus(dt_ref[...] + dtb_ref[...])
    cum = dt * -jnp.exp(alog_ref[...])
    s = 1
    while s < seq_len:
        cum = cum + jnp.where(t >= s, pltpu.roll(cum, s, axis=0), 0.0)
        s *= 2
    tot = cum
    s = 1
    while s < seq_len:
        tot = jnp.where(t + s < seq_len, pltpu.roll(tot, rows - s, axis=0), tot)
        s *= 2
    cum_sq = cum if rows == LANES else jnp.concatenate([cum, jnp.zeros((LANES - rows, LANES), F32)], axis=0)
    cum_t = cum_sq.T[:, 0:rows]
    dtx = _expand_heads(dt, rows)
    ecx = _expand_heads(jnp.exp(cum), rows)
    tlx = _expand_heads(jnp.exp(tot - cum), rows)
    xdt = xs * dtx
    xdtw = (xdt * tlx).astype(BF16)

    ti = lax.broadcasted_iota(jnp.int32, (rows, rows), 0)
    tj = lax.broadcasted_iota(jnp.int32, (rows, rows), 1)
    same_causal = (ti >= tj) & (ti // seq_len == tj // seq_len)
    low = lax.broadcasted_iota(jnp.int32, (rows, LANES), 1) < HEAD
    row_seq = lax.broadcasted_iota(jnp.int32, (rows, N_STATE), 0) // seq_len

    def per_seq_cols(m):
        return jnp.concatenate([jnp.where(row_seq == q, m, 0.0) for q in range(nseq_t)], axis=1).astype(BF16)

    for g in range(N_GROUPS):
        gsl = slice(g * GROUP_W, (g + 1) * GROUP_W)
        bg_f = bcv[:, g * N_STATE:(g + 1) * N_STATE]
        cg_f = bcv[:, (N_GROUPS + g) * N_STATE:(N_GROUPS + g + 1) * N_STATE]
        cb = lax.dot_general(cg_f.astype(BF16), bg_f.astype(BF16), _NT, preferred_element_type=F32)
        h_cat = jnp.concatenate([h0_ref[q, gsl, :] for q in range(nseq_t)], axis=1).astype(BF16)
        y_off = lax.dot_general(per_seq_cols(cg_f), h_cat, _NT, preferred_element_type=F32) * ecx[:, gsl]
        st = lax.dot_general(xdtw[:, gsl], per_seq_cols(bg_f), _TN, preferred_element_type=F32)
        y_pairs = []
        for j in range(GROUP_W // LANES):
            h_lo = g * (GROUP_W // HEAD) + 2 * j
            ms = []
            for h in (h_lo, h_lo + 1):
                seg = cum[:, h:h + 1] - cum_t[h:h + 1, :]
                decay = jnp.exp(jnp.where(same_causal, seg, -jnp.inf))
                ms.append((cb * decay).astype(BF16))
            slab = xdt[:, h_lo * HEAD:(h_lo + 2) * HEAD]
            rhs = jnp.concatenate([jnp.where(low, slab, 0.0), jnp.where(low, 0.0, slab)], axis=0).astype(BF16)
            y_pairs.append(jnp.dot(jnp.concatenate(ms, axis=1), rhs, preferred_element_type=F32))
            for h in (h_lo, h_lo + 1):
                hsl = slice(h * HEAD, (h + 1) * HEAD)
                hg = slice((h % 8) * HEAD, (h % 8 + 1) * HEAD)
                for q in range(nseq_t):
                    dec = jnp.exp(jnp.broadcast_to(tot[q * seq_len:q * seq_len + 1, h:h + 1], (HEAD, N_STATE)))
                    ho_ref[q, hsl, :] = h0_ref[q, hsl, :] * dec + st[hg, q * N_STATE:(q + 1) * N_STATE]
        y = jnp.concatenate(y_pairs, axis=1) + y_off + xs[:, gsl] * dsk_ref[:, gsl]
        y = y * _silu(z[:, gsl])
        y_ref[:, gsl] = y * lax.rsqrt(jnp.mean(y * y, axis=-1, keepdims=True) + EPS) * gn_ref[:, gsl]


def _ssd_short(p, conv_state, h0, prm, nseq, seq_len):
    rows = nseq * seq_len
    tr = SSD_SHORT_SEQS * seq_len
    bc_w = 2 * N_GROUPS * N_STATE

    def dense(s, lo, hi):
        part = conv_state[:, CONV_W - 1 - s:, lo:hi]
        return jnp.pad(part, ((0, 0), (0, seq_len - s), (0, 0))).reshape(rows, hi - lo)

    tile = lambda w, col: pl.BlockSpec((tr, w), lambda i: (i, col))
    const = lambda shape: pl.BlockSpec(shape, lambda i: (0,) * len(shape))
    state = pl.BlockSpec((SSD_SHORT_SEQS, D, N_STATE), lambda i: (i, 0, 0))
    return pl.pallas_call(
        functools.partial(_ssd_short_body, seq_len=seq_len),
        out_shape=(jax.ShapeDtypeStruct((rows, D), F32), jax.ShapeDtypeStruct((nseq, D, N_STATE), F32)),
        grid=(nseq // SSD_SHORT_SEQS,),
        in_specs=[
            tile(D, COL_Z // D), tile(D, COL_X // D), tile(bc_w, COL_BC // bc_w), tile(LANES, COL_DT // LANES),
            tile(D, 0), tile(D, 0), tile(D, 0), tile(bc_w, 0), tile(bc_w, 0), tile(bc_w, 0), state,
            const((CONV_W, D)), const((CONV_W, bc_w)), const((1, D)), const((1, bc_w)),
            const((1, LANES)), const((1, LANES)), const((1, D)), const((1, D)),
        ],
        out_specs=(tile(D, 0), state),
        compiler_params=_cparams(("parallel",), 48),
        name="ssd_short",
    )(p, p, p, p, dense(1, 0, D), dense(2, 0, D), dense(3, 0, D), dense(1, D, D + bc_w), dense(2, D, D + bc_w),
      dense(3, D, D + bc_w), h0, *prm)


def _rwkv_pre_body(rkv_ref, lora_ref, si_rkv_ref, si_lora_ref, mu_rkv_ref, mu_lora_ref,
                   w0_ref, w2_ref, a0_ref, a2_ref, g2_ref, kk_ref, ka_ref, rk_ref,
                   r_out, w_out, k_out, v_out, kk_out, b_out, bonus_out, g_out,
                   rkvbuf, lorabuf, *, q, log_decay, short_len):
    if short_len:
        def shifted(buf, cur_ref, si_ref, mu_ref):
            cur = cur_ref[...]
            first = lax.broadcasted_iota(jnp.int32, cur.shape, 0) % short_len == 0
            prev = jnp.where(first, si_ref[...], pltpu.roll(cur, 1, axis=0))
            return cur + mu_ref[...] * (prev - cur)
    else:
        c = pl.program_id(1)
        _carried_window(rkvbuf, rkv_ref, si_rkv_ref, c == 0, q)
        _carried_window(lorabuf, lora_ref, si_lora_ref, c == 0, q)

        def shifted(buf, cur_ref, si_ref, mu_ref):
            cur = cur_ref[...]
            prev = buf[pl.ds(SUBLANES - 1, q), :]
            return cur + mu_ref[...] * (prev - cur)

    ps = shifted(rkvbuf, rkv_ref, si_rkv_ref, mu_rkv_ref)
    lo_in = shifted(lorabuf, lora_ref, si_lora_ref, mu_lora_ref)
    if not short_len:
        _advance_window(rkvbuf, q)
        _advance_window(lorabuf, q)
    r = ps[:, 0:D]
    k = ps[:, D:2 * D]
    v = ps[:, 2 * D:3 * D]

    lw = jnp.dot(jnp.tanh(lo_in).astype(BF16), w2_ref[...], preferred_element_type=F32)
    la = jnp.dot(lo_in.astype(BF16), a2_ref[...], preferred_element_type=F32)
    g = jnp.dot(jax.nn.sigmoid(lo_in).astype(BF16), g2_ref[...], preferred_element_type=F32)
    wlog = -_softplus(-(w0_ref[...] + lw)) - 0.5
    log_w = -jnp.exp(wlog)
    a = jax.nn.sigmoid(a0_ref[...] + la)
    kkf = k * kk_ref[...]
    norm = jnp.maximum(jnp.sqrt(_head_allsum(kkf * kkf)), 1e-12)
    kk = kkf / norm
    k2 = k * (1.0 + (a - 1.0) * ka_ref[...])
    bonus = _head_allsum(r * k2 * rk_ref[...]) * v
    r_out[...] = r
    w_out[...] = log_w if log_decay else jnp.exp(log_w)
    k_out[...] = k2
    v_out[...] = v
    kk_out[...] = kk
    b_out[...] = kk * a
    bonus_out[...] = bonus
    g_out[...] = g


def _rwkv_pre(p, shift_state, prm, nseq, seq_len, log_decay):
    short_len = seq_len if seq_len < CHUNK else 0
    q = CHUNK if short_len else min(CHUNK, seq_len)
    nc = 1 if short_len else seq_len // q
    ntile = nseq * seq_len // q // nc
    const = lambda shape: pl.BlockSpec(shape, lambda s, c: (0,) * len(shape))
    tile = pl.BlockSpec((q, D), lambda s, c: (s * nc + c, 0))
    if short_len:
        dense = lambda a: jnp.pad(a, ((0, 0), (0, seq_len - 1), (0, 0))).reshape(nseq * seq_len, a.shape[-1])
        si = (dense(shift_state[:, :, :3 * D]), dense(shift_state[:, :, 3 * D:]))
        si_specs = [pl.BlockSpec((q, 3 * D), lambda s, c: (s, 0)), pl.BlockSpec((q, LORA_W), lambda s, c: (s, 0))]
    else:
        si = (_pad_front_rows(shift_state[:, :, :3 * D]), _pad_front_rows(shift_state[:, :, 3 * D:]))
        si_specs = [pl.BlockSpec((1, SUBLANES, 3 * D), lambda s, c: (s, 0, 0)),
                    pl.BlockSpec((1, SUBLANES, LORA_W), lambda s, c: (s, 0, 0))]
    sds = jax.ShapeDtypeStruct((nseq * seq_len, D), F32)
    return pl.pallas_call(
        functools.partial(_rwkv_pre_body, q=q, log_decay=log_decay, short_len=short_len),
        out_shape=(sds,) * 8,
        grid=(ntile, nc),
        in_specs=[
            pl.BlockSpec((q, 3 * D), lambda s, c: (s * nc + c, 0)),
            pl.BlockSpec((q, LORA_W), lambda s, c: (s * nc + c, COL_LORA // LORA_W)),
            *si_specs,
            const((1, 3 * D)), const((1, LORA_W)),
            const((1, D)), const((LORA_W, D)), const((1, D)), const((LORA_W, D)), const((LORA_W, D)),
            const((1, D)), const((1, D)), const((1, D)),
        ],
        out_specs=(tile,) * 8,
        scratch_shapes=[pltpu.VMEM((CHUNK + SUBLANES, 3 * D), F32), pltpu.VMEM((CHUNK + SUBLANES, LORA_W), F32)],
        compiler_params=_cparams(("parallel", "arbitrary"), 48),
        name="rwkv_pre",
    )(p, p, *si, *prm)


def _rwkv_out(o, bonus, g, ln_w, ln_b):
    mean = _head_allsum(o) * (1.0 / HEAD)
    cen = o - mean
    var = _head_allsum(cen * cen) * (1.0 / HEAD)
    return (cen * lax.rsqrt(var + GN_EPS) * ln_w + ln_b + bonus) * g


def _split_bf16(x):
    hi = x.astype(BF16)
    return hi, (x - hi.astype(F32)).astype(BF16)


_NN = (((1,), (0,)), ((), ()))
_NT = (((1,), (1,)), ((), ()))
_TN = (((0,), (0,)), ((), ()))


def _solve_unit_lower(n, rhs, lower_left):
    rows, width = rhs.shape
    half = rows // 2
    nblk, ncol = rows // SUBLANES, width // LANES
    tiles = lambda a, r0, r1: [[a[SUBLANES * i:SUBLANES * (i + 1), LANES * j:LANES * (j + 1)] for j in range(ncol)]
                               for i in range(r0 // SUBLANES, r1 // SUBLANES)]
    nb = tiles(n, 0, rows)
    xb = tiles(rhs, 0, rows)
    low = lax.broadcasted_iota(jnp.int32, (SUBLANES, LANES), 1) < HEAD

    def substitute(lo, hi):
        for s in range(lo, hi - 1):
            i0, r0 = divmod(s, SUBLANES)
            idx = jnp.where(low, s, HEAD + s)
            for j in range(ncol):
                row = xb[i0][j][r0:r0 + 1, :]
                for i in range(i0 if r0 < SUBLANES - 1 else i0 + 1, hi // SUBLANES):
                    xb[i][j] = xb[i][j] - jnp.take_along_axis(nb[i][j], idx, axis=1) * row

    join = lambda blocks: jnp.concatenate([jnp.concatenate(xr, axis=1) for xr in blocks], axis=0)
    substitute(0, half)
    corr = tiles(lower_left(join(xb[:half // SUBLANES])), 0, half)
    for i in range(half // SUBLANES):
        for j in range(ncol):
            xb[half // SUBLANES + i][j] = xb[half // SUBLANES + i][j] - corr[i][j]
    substitute(half, rows)
    return join(xb)


def _wkv_tile(r, lw, k, v, kk, b, st):
    rows, width = r.shape
    nh = WKV_LANES // HEAD
    groups = [slice(g * WKV_LANES, (g + 1) * WKV_LANES) for g in range(width // WKV_LANES)]
    cl = _cumsum_rows(lw)
    cl_last = cl[rows - 1:rows, :]
    p_inv = jnp.exp(-cl)
    p_end = jnp.exp(cl_last - cl)
    x2h, x2l = _split_bf16(jnp.concatenate([kk * jnp.exp(cl - lw), r * jnp.exp(cl)], axis=0))
    k_hat = k * p_inv
    b_hat = b * p_inv
    k_end = k * p_end
    b_end = -(b * p_end)

    bd_r = lax.broadcasted_iota(jnp.int32, (nh * rows, WKV_LANES), 0) // rows
    bd_c = lax.broadcasted_iota(jnp.int32, (nh * rows, WKV_LANES), 1) // HEAD
    bd_mask = jnp.where(bd_r == bd_c, 1.0, 0.0).astype(BF16)

    def per_head_rows(y):
        return [jnp.concatenate([part] * nh, axis=0) * bd_mask for part in _split_bf16(y)]

    def dot3(ah, al, bh, bl, dims):
        dg = lambda x, y: lax.dot_general(x, y, dims, preferred_element_type=F32)
        return dg(ah, bh) + (dg(ah, bl) + dg(al, bh))

    t_i = lax.broadcasted_iota(jnp.int32, (rows, WKV_LANES), 0)
    lane = lax.broadcasted_iota(jnp.int32, (rows, WKV_LANES), 1)
    s_i = lane & (HEAD - 1)
    strict = t_i > s_i
    incl = t_i >= s_i
    head_of_lane = lane // HEAD

    a_kb, a_rb, base, o_part = [], [], [], []
    for gs in groups:
        ak = dot3(x2h[:, gs], x2l[:, gs], *per_head_rows(k_hat[:, gs]), _NT)
        ab = dot3(x2h[:, gs], x2l[:, gs], *per_head_rows(b_hat[:, gs]), _NT)
        a_k = jnp.concatenate([jnp.where(strict, ak[0:rows], 0.0), jnp.where(incl, ak[rows:], 0.0)], axis=0)
        xs = dot3(x2h[:, gs], x2l[:, gs], *per_head_rows(st[:, gs]), _NN)
        av = dot3(*_split_bf16(a_k), *per_head_rows(v[:, gs]), _NN)
        a_kb.append(jnp.where(strict, ab[0:rows], 0.0))
        a_rb.append(jnp.where(incl, ab[rows:], 0.0))
        base.append(xs[0:rows] + av[0:rows])
        o_part.append(xs[rows:] + av[rows:])
    half = rows // 2
    top_cols = (lax.broadcasted_iota(jnp.int32, (half, WKV_LANES), 1) & (HEAD - 1)) < half

    def lower_left(x_top):
        x_pad = jnp.concatenate([x_top, jnp.zeros_like(x_top)], axis=0)
        parts = []
        for g, gs in enumerate(groups):
            lhs = jnp.where(top_cols, a_kb[g][half:], 0.0)
            parts.append(dot3(*_split_bf16(lhs), *per_head_rows(x_pad[:, gs]), _NN))
        return jnp.concatenate(parts, axis=1)

    sa = _solve_unit_lower(jnp.concatenate(a_kb, axis=1), jnp.concatenate(base, axis=1), lower_left)

    o, st_new = [], []
    for g, gs in enumerate(groups):
        o.append(o_part[g] - dot3(*_split_bf16(a_rb[g]), *per_head_rows(sa[:, gs]), _NN))
        eye = jnp.where(t_i == s_i, jnp.exp(cl_last[:, gs]), 0.0)
        lhs = jnp.concatenate([k_end[:, gs], b_end[:, gs], eye], axis=0)
        rhs = jnp.concatenate([v[:, gs], sa[:, gs], st[:, gs]], axis=0)
        full = dot3(*_split_bf16(lhs), *_split_bf16(rhs), _TN)
        acc = jnp.where(head_of_lane == 0, full[0:HEAD, :], 0.0)
        for h in range(1, nh):
            acc = acc + jnp.where(head_of_lane == h, full[h * HEAD:(h + 1) * HEAD, :], 0.0)
        st_new.append(acc)
    return jnp.concatenate(o, axis=1), jnp.concatenate(st_new, axis=1)


def _wkv_long_body(r_ref, lw_ref, k_ref, v_ref, kk_ref, b_ref, bonus_ref, g_ref, lnw_ref, lnb_ref,
                   y_ref, so_ref, st_s, *, nc):
    c = pl.program_id(2)

    @pl.when(c == 0)
    def _():
        st_s[...] = jnp.zeros_like(st_s)

    o, st_new = _wkv_tile(r_ref[...], lw_ref[...], k_ref[...], v_ref[...], kk_ref[...], b_ref[...], st_s[...])
    y_ref[...] = _rwkv_out(o, bonus_ref[...], g_ref[...], lnw_ref[...], lnb_ref[...])
    st_s[...] = st_new

    @pl.when(c == nc - 1)
    def _():
        for j in range(WKV_GROUPS * WKV_LANES // LANES):
            js = slice(j * LANES, (j + 1) * LANES)
            sq = jnp.concatenate([st_s[:, js], jnp.zeros((LANES - HEAD, LANES), F32)], axis=0)
            so_ref[0, js, :] = sq.T[:, 0:HEAD]


def _wkv_long(r, lw, k, v, kk, b, bonus, g, ln_w, ln_b, nseq, seq_len):
    nc = seq_len // WKV_CHUNK
    width = WKV_GROUPS * WKV_LANES
    tile = pl.BlockSpec((WKV_CHUNK, width), lambda s, hg, c: (s * nc + c, hg))
    vec = pl.BlockSpec((1, width), lambda s, hg, c: (0, hg))
    y, s_last = pl.pallas_call(
        functools.partial(_wkv_long_body, nc=nc),
        out_shape=(jax.ShapeDtypeStruct((nseq * seq_len, D), F32),
                   jax.ShapeDtypeStruct((nseq, D, HEAD), F32)),
        grid=(nseq, D // width, nc),
        in_specs=[tile] * 8 + [vec, vec],
        out_specs=(tile, pl.BlockSpec((1, width, HEAD), lambda s, hg, c: (s, hg, 0))),
        scratch_shapes=[pltpu.VMEM((HEAD, width), F32)],
        compiler_params=_cparams(("parallel", "parallel", "arbitrary"), 32),
        name="wkv_long",
    )(r, lw, k, v, kk, b, bonus, g, ln_w, ln_b)
    return y, s_last


def _wkv_short_body(r_ref, w_ref, k_ref, v_ref, kk_ref, b_ref, s0_ref, o_ref, so_ref, *, steps):
    def per_v_group(i, carry):
        vis = [i * WKV_SHORT_ROWS + u for u in range(WKV_SHORT_ROWS)]
        ss = [s0_ref[0, vi] for vi in vis]
        for t in range(steps):
            kk_t, w_t, b_t, k_t, r_t = kk_ref[t, 0], w_ref[t, 0], b_ref[t, 0], k_ref[t, 0], r_ref[t, 0]
            for u, vi in enumerate(vis):
                vrow = v_ref[t, 0, pl.ds(vi, 1), :]
                sa = jnp.sum(ss[u] * kk_t, axis=0, keepdims=True)
                ss[u] = ss[u] * w_t - b_t * sa + k_t * vrow
                o_ref[t, 0, pl.ds(vi, 1), :] = jnp.sum(ss[u] * r_t, axis=0, keepdims=True)
        for u, vi in enumerate(vis):
            so_ref[0, vi] = ss[u]
        return carry

    lax.fori_loop(0, HEAD // WKV_SHORT_ROWS, per_v_group, 0)


def _wkv_short(r, w, k, v, kk, b, s0, nseq, seq_len):
    vec = pl.BlockSpec((seq_len, 1, HEAD, nseq), lambda h: (0, h, 0, 0))
    st = pl.BlockSpec((1, HEAD, HEAD, nseq), lambda h: (h, 0, 0, 0))
    return pl.pallas_call(
        functools.partial(_wkv_short_body, steps=seq_len),
        out_shape=(jax.ShapeDtypeStruct((seq_len, N_HEADS, HEAD, nseq), F32),
                   jax.ShapeDtypeStruct((N_HEADS, HEAD, HEAD, nseq), F32)),
        grid=(N_HEADS,),
        in_specs=[vec] * 6 + [st],
        out_specs=(vec, st),
        compiler_params=_cparams(("parallel",), 32),
        name="wkv_short",
    )(r, w, k, v, kk, b, s0)


def _rwkv_post_body(o_ref, bonus_ref, g_ref, lnw_ref, lnb_ref, y_ref):
    y_ref[...] = _rwkv_out(o_ref[...], bonus_ref[...], g_ref[...], lnw_ref[...], lnb_ref[...])


def _rwkv_post(o, bonus, g, ln_w, ln_b):
    m = o.shape[0]
    tile = pl.BlockSpec((TM_POST, D), lambda i: (i, 0))
    const = pl.BlockSpec((1, D), lambda i: (0, 0))
    return pl.pallas_call(
        _rwkv_post_body,
        out_shape=jax.ShapeDtypeStruct((m, D), F32),
        grid=(m // TM_POST,),
        in_specs=[tile, tile, tile, const, const],
        out_specs=tile,
        compiler_params=_cparams(("parallel",), 32),
        name="rwkv_post",
    )(o, bonus, g, ln_w, ln_b)


def _lru_coeffs(xc, wga_ref, bga_ref, wgx_ref, bgx_ref, lam_ref):
    ra, rx = [], []
    for blk in range(LRU_BLOCKS):
        xh = xc[:, blk * LRU_BLK:(blk + 1) * LRU_BLK].astype(BF16)
        ra.append(jnp.dot(xh, wga_ref[blk], preferred_element_type=F32))
        rx.append(jnp.dot(xh, wgx_ref[blk], preferred_element_type=F32))
    rg = jax.nn.sigmoid(jnp.concatenate(ra, axis=1) + bga_ref[...])
    ig = jax.nn.sigmoid(jnp.concatenate(rx, axis=1) + bgx_ref[...])
    log_a = -LRU_C * rg * _softplus(-lam_ref[...])
    return jnp.exp(log_a), jnp.sqrt(1.0 - jnp.exp(2.0 * log_a)) * (ig * xc)


def _lru_body(gate_ref, x_ref, ci_ref, h0_ref, cw_ref, cb_ref, wga_ref, bga_ref, wgx_ref, bgx_ref, lam_ref,
              y_ref, ho_ref, xbuf, a_s, b_s, h_s, hc_s, *, q, nc):
    c = pl.program_id(1)
    _carried_window(xbuf, x_ref, ci_ref, c == 0, q)

    @pl.when(c == 0)
    def _():
        hc_s[...] = h0_ref[0]

    xc = _causal_conv(xbuf, cw_ref, cb_ref, q)
    _advance_window(xbuf, q)
    a_s[...], b_s[...] = _lru_coeffs(xc, wga_ref, bga_ref, wgx_ref, bgx_ref, lam_ref)

    def step(t, h):
        h = a_s[pl.ds(t, 1), :] * h + b_s[pl.ds(t, 1), :]
        h_s[pl.ds(t, 1), :] = h
        return h

    h = lax.fori_loop(0, q, step, hc_s[...])
    hc_s[...] = h
    y_ref[...] = h_s[...] * jax.nn.gelu(gate_ref[...])

    @pl.when(c == nc - 1)
    def _():
        ho_ref[0] = h


def _lru(p, conv_init, h0, prm, nseq, seq_len):
    q = min(CHUNK, seq_len)
    nc = seq_len // q
    per_seq = lambda shape: pl.BlockSpec((1,) + shape, lambda s, c: (s,) + (0,) * len(shape))
    const = lambda shape: pl.BlockSpec(shape, lambda s, c: (0,) * len(shape))
    return pl.pallas_call(
        functools.partial(_lru_body, q=q, nc=nc),
        out_shape=(jax.ShapeDtypeStruct((nseq * seq_len, D), F32),
                   jax.ShapeDtypeStruct((nseq, 1, D), F32)),
        grid=(nseq, nc),
        in_specs=[
            pl.BlockSpec((q, D), lambda s, c: (s * nc + c, 0)),
            pl.BlockSpec((q, D), lambda s, c: (s * nc + c, 1)),
            per_seq((SUBLANES, D)), per_seq((1, D)),
            const((CONV_W, D)), const((1, D)),
            const((LRU_BLOCKS, LRU_BLK, LRU_BLK)), const((1, D)),
            const((LRU_BLOCKS, LRU_BLK, LRU_BLK)), const((1, D)), const((1, D)),
        ],
        out_specs=(pl.BlockSpec((q, D), lambda s, c: (s * nc + c, 0)), per_seq((1, D))),
        scratch_shapes=[pltpu.VMEM((CHUNK + SUBLANES, D), F32), pltpu.VMEM((q, D), F32), pltpu.VMEM((q, D), F32),
                        pltpu.VMEM((q, D), F32), pltpu.VMEM((1, D), F32)],
        compiler_params=_cparams(("parallel", "arbitrary"), 32),
        name="lru",
    )(p, p, conv_init, h0, *prm)


def _lru_short_body(gate_ref, x_ref, p1_ref, p2_ref, p3_ref, h0_ref, cw_ref, cb_ref, wga_ref, bga_ref, wgx_ref,
                    bgx_ref, lam_ref, y_ref, h_ref, *, seq_len):
    cur = x_ref[...]
    t = lax.broadcasted_iota(jnp.int32, cur.shape, 0) % seq_len
    prev = lambda s, ref: jnp.where(t >= s, pltpu.roll(cur, s, axis=0), ref[...])
    xc = (cb_ref[...] + cw_ref[0:1, :] * prev(3, p3_ref) + cw_ref[1:2, :] * prev(2, p2_ref)
          + cw_ref[2:3, :] * prev(1, p1_ref) + cw_ref[3:4, :] * cur)
    a, b = _lru_coeffs(xc, wga_ref, bga_ref, wgx_ref, bgx_ref, lam_ref)
    s = 1
    while s < seq_len:
        keep = t >= s
        b = jnp.where(keep, a * pltpu.roll(b, s, axis=0) + b, b)
        a = jnp.where(keep, a * pltpu.roll(a, s, axis=0), a)
        s *= 2
    h = a * h0_ref[...] + b
    h_ref[...] = h
    y_ref[...] = h * jax.nn.gelu(gate_ref[...])


def _lru_short(p, conv_state, h0, prm, nseq, seq_len):
    rows = nseq * seq_len
    dense = lambda s: jnp.pad(conv_state[:, CONV_W - 1 - s:, :], ((0, 0), (0, seq_len - s), (0, 0))).reshape(rows, D)
    h0_rows = jnp.repeat(h0, seq_len, axis=0)
    tile = lambda col: pl.BlockSpec((CHUNK, D), lambda i: (i, col))
    const = lambda shape: pl.BlockSpec(shape, lambda i: (0,) * len(shape))
    sds = jax.ShapeDtypeStruct((rows, D), F32)
    return pl.pallas_call(
        functools.partial(_lru_short_body, seq_len=seq_len),
        out_shape=(sds, sds),
        grid=(rows // CHUNK,),
        in_specs=[tile(0), tile(1), tile(0), tile(0), tile(0), tile(0),
                  const((CONV_W, D)), const((1, D)),
                  const((LRU_BLOCKS, LRU_BLK, LRU_BLK)), const((1, D)),
                  const((LRU_BLOCKS, LRU_BLK, LRU_BLK)), const((1, D)), const((1, D))],
        out_specs=(tile(0), tile(0)),
        compiler_params=_cparams(("parallel",), 32),
        name="lru_short",
    )(p, p, dense(1), dense(2), dense(3), h0_rows, *prm)


def _pad_front_rows(buf):
    return jnp.pad(buf, ((0, 0), (SUBLANES - buf.shape[1], 0), (0, 0)))


def _row2(v):
    return v.reshape(1, -1)


def kernel(x_prompt, x_sample, state_ssm_a, state_conv_a, state_wkv_b, state_shift_b, state_lru_c, state_conv_c, norm_gain, w_ffn_in, w_ffn_out, w_in_ab, conv_w_a, conv_b_a, dt_bias_a, a_log_a, d_skip_a, gnorm_a, mu_b, w0_b, w2_b, a0_b, a2_b, g2_b, k_k_b, k_a_b, r_k_b, ln_w_b, ln_b_b, w_out_ab, w_in_c, conv_w_c, conv_b_c, w_gate_a_c, b_gate_a_c, w_gate_x_c, b_gate_x_c, lambda_c, w_out_c, final_norm_gain):
    w_gu, w_dn = _ffn_weights(w_ffn_in, w_ffn_out)

    in_a = D + (D + 2 * N_GROUPS * N_STATE) + N_HEADS
    wab = w_in_ab[0]
    bc_w = 2 * N_GROUPS * N_STATE
    w_proj0 = jnp.concatenate([
        wab[:, in_a:in_a + 3 * D],
        wab[:, 0:D],
        wab[:, D:2 * D],
        wab[:, 2 * D:2 * D + bc_w],
        wab[:, in_a + 3 * D:in_a + 3 * D + LORA_W],
        wab[:, in_a - N_HEADS:in_a],
        jnp.zeros((D, PROJ_W - COL_DT - N_HEADS), F32),
    ], axis=1).astype(BF16)
    w_proj1 = w_in_c[0].astype(BF16)
    w_out0 = w_out_ab[0].astype(BF16)
    w_out1 = w_out_c[0].astype(BF16)

    pad_lanes = lambda v: jnp.pad(v.reshape(1, -1), ((0, 0), (0, LANES - v.shape[-1])))
    rep_head = lambda v: jnp.repeat(v, HEAD).reshape(1, D)
    ssd_prm = (conv_w_a[0][:, :D], conv_w_a[0][:, D:], _row2(conv_b_a[0][:D]), _row2(conv_b_a[0][D:]),
               pad_lanes(dt_bias_a[0]), pad_lanes(a_log_a[0]), rep_head(d_skip_a[0]), _row2(gnorm_a[0]))
    lora_rows = lambda w, lo: jnp.pad(w, ((lo, LORA_W - lo - w.shape[0]), (0, 0))).astype(BF16)
    mu = mu_b[0]
    rwkv_prm = (_row2(mu[:3 * D]), _row2(mu[3 * D:]),
                _row2(w0_b[0]), lora_rows(w2_b[0], 0), _row2(a0_b[0]), lora_rows(a2_b[0], 64),
                lora_rows(g2_b[0], 128), _row2(k_k_b[0]), _row2(k_a_b[0]), _row2(r_k_b[0]))
    lru_prm = (conv_w_c[0], _row2(conv_b_c[0]), w_gate_a_c[0].astype(BF16), _row2(b_gate_a_c[0]),
               w_gate_x_c[0].astype(BF16), _row2(b_gate_x_c[0]), _row2(lambda_c[0]))

    def trunk(x3, ssm0, conva0, wkv0, shift0, lru0, convc0):
        nseq, seq_len, _ = x3.shape
        x = x3.reshape(nseq * seq_len, D)
        tail = lambda arr, n: arr.reshape(nseq, seq_len, arr.shape[-1])[:, seq_len - n:, :]

        x = _ffn(x, _row2(norm_gain[0, 0]), w_gu, w_dn, (0, 0))
        p = _proj(x, _row2(norm_gain[0, 1]), w_proj0)
        if seq_len >= CHUNK:
            ya, ssm_n = _ssd(p, _pad_front_rows(conva0[:, :, :D]), _pad_front_rows(conva0[:, :, D:]),
                             ssm0.reshape(nseq, D, N_STATE), ssd_prm, nseq, seq_len)
        else:
            ya, ssm_n = _ssd_short(p, conva0, ssm0.reshape(nseq, D, N_STATE), ssd_prm, nseq, seq_len)
        p_tail = tail(p, CONV_W - 1)
        conva_n = jnp.concatenate([p_tail[:, :, COL_X:COL_X + D], p_tail[:, :, COL_BC:COL_BC + bc_w]], axis=-1)
        shift_n = jnp.concatenate([p_tail[:, -1:, :3 * D], p_tail[:, -1:, COL_LORA:COL_LORA + LORA_W]], axis=-1)

        r, w, k, v, kk, b, bonus, g = _rwkv_pre(p, shift0, rwkv_prm, nseq, seq_len, log_decay=wkv0 is None)
        if wkv0 is None:
            yb, wkv_n = _wkv_long(r, w, k, v, kk, b, bonus, g, _row2(ln_w_b[0]), _row2(ln_b_b[0]), nseq, seq_len)
            wkv_n = wkv_n.reshape(nseq, N_HEADS, HEAD, HEAD)
        else:
            to_lanes = lambda a: a.reshape(nseq, seq_len, N_HEADS, HEAD).transpose(1, 2, 3, 0)
            o, wkv_n = _wkv_short(*[to_lanes(a) for a in (r, w, k, v, kk, b)], wkv0.transpose(1, 2, 3, 0),
                                  nseq, seq_len)
            o = o.transpose(3, 0, 1, 2).reshape(nseq * seq_len, D)
            wkv_n = wkv_n.transpose(3, 0, 1, 2)
            yb = _rwkv_post(o, bonus, g, _row2(ln_w_b[0]), _row2(ln_b_b[0]))
        x = _out_proj(x, (ya, yb), (w_out0[:D], w_out0[D:]))
        x = _ffn(x, _row2(norm_gain[0, 2]), w_gu, w_dn, (0, 1))

        x = _ffn(x, _row2(norm_gain[1, 0]), w_gu, w_dn, (1, 0))
        pc = _proj(x, _row2(norm_gain[1, 1]), w_proj1)
        if seq_len >= CHUNK:
            yc, lru_n = _lru(pc, _pad_front_rows(convc0), lru0.reshape(nseq, 1, D), lru_prm, nseq, seq_len)
        else:
            yc, h_rows = _lru_short(pc, convc0, lru0, lru_prm, nseq, seq_len)
            lru_n = tail(h_rows, 1)
        convc_n = tail(pc, CONV_W - 1)[:, :, D:]
        x = _out_proj(x, (yc,), (w_out1,))
        y = _ffn(x, _row2(norm_gain[1, 2]), w_gu, w_dn, (1, 1), final_gain=_row2(final_norm_gain))

        return (y.reshape(nseq, seq_len, D), ssm_n.reshape(1, nseq, N_HEADS, HEAD, N_STATE), conva_n[None],
                wkv_n[None], shift_n[None], lru_n.reshape(1, nseq, D), convc_n[None])

    bp = x_prompt.shape[0]
    zeros = lambda s: jnp.zeros((bp,) + s.shape[2:], F32)
    outs_p = trunk(x_prompt, zeros(state_ssm_a), zeros(state_conv_a), None, zeros(state_shift_b),
                   zeros(state_lru_c), zeros(state_conv_c))
    outs_s = trunk(x_sample, state_ssm_a[0], state_conv_a[0], state_wkv_b[0], state_shift_b[0],
                   state_lru_c[0], state_conv_c[0])
    return (outs_p[0], outs_s[0]) + outs_p[1:] + outs_s[1:]
```

```python
import functools

import jax
import jax.numpy as jnp
from jax import lax
from jax.experimental import pallas as pl
from jax.experimental.pallas import tpu as pltpu

F32 = jnp.float32
BF16 = jnp.bfloat16

D = 2048
D_FF = 5504
D_FF_PAD = 5632
HEAD = 64
N_HEADS = 32
N_GROUPS = 4
GROUP_W = D // N_GROUPS
N_STATE = 128
HEADS_PER_GROUP = N_HEADS // N_GROUPS
BC_W = 2 * N_GROUPS * N_STATE
CONV_W = 4
LORA_W = 256
LRU_BLOCKS = 8
LRU_BLK = D // LRU_BLOCKS
LRU_C = 8.0
EPS = 1e-6
GN_EPS = 64e-5
SUBLANES = 8
LANES = 128
CHUNK = 128
WKV_CHUNK = 64
WKV_LANES = 256
WKV_GROUPS = 4
WKV_SHORT_ROWS = 4
SSD_SHORT_SEQS = 8

COL_RKV = 0
COL_Z = 3 * D
COL_X = 4 * D
COL_BC = 5 * D
COL_LORA = 5 * D + 2 * N_GROUPS * N_STATE
COL_DT = COL_LORA + LORA_W
PROJ_W = COL_DT + 256

TM_FFN = 512
TF_FFN = 512
TM_PROJ = 1024
TN_PROJ = 512
TM_OUT = 512
TN_OUT = 1024
TM_POST = 256
W_IN_ROWS = 64


def _cparams(sem, vmem_mib):
    return pltpu.CompilerParams(dimension_semantics=sem, vmem_limit_bytes=vmem_mib * 1024 * 1024)


def _softplus(x):
    return jnp.maximum(x, 0.0) + jnp.log(1.0 + jnp.exp(-jnp.abs(x)))


def _silu(x):
    return x * jax.nn.sigmoid(x)


def _rms(x, gain):
    ms = jnp.mean(x * x, axis=-1, keepdims=True)
    return x * lax.rsqrt(ms + EPS) * gain


def _ffn_body(x_ref, g_ref, wg_ref, wu_ref, wo_ref, fg_ref, o_ref, xn_ref, acc_ref, *, nf, final):
    f = pl.program_id(1)

    @pl.when(f == 0)
    def _():
        xn_ref[...] = _rms(x_ref[...], g_ref[...]).astype(BF16)
        acc_ref[...] = jnp.zeros_like(acc_ref)

    xn = xn_ref[...]
    gate = jnp.dot(xn, wg_ref[...], preferred_element_type=F32)
    up = jnp.dot(xn, wu_ref[...], preferred_element_type=F32)
    h = (_silu(gate) * up).astype(BF16)
    acc_ref[...] += jnp.dot(h, wo_ref[...], preferred_element_type=F32)

    @pl.when(f == nf - 1)
    def _():
        y = x_ref[...] + 0.5 * acc_ref[...]
        if final:
            y = _rms(y, fg_ref[...])
        o_ref[...] = y


def _ffn(x, gain, w_gu, w_dn, which, final_gain=None):
    m = x.shape[0]
    li, si = which
    nf = D_FF_PAD // TF_FFN
    final = final_gain is not None
    fg = final_gain if final else gain
    return pl.pallas_call(
        functools.partial(_ffn_body, nf=nf, final=final),
        out_shape=jax.ShapeDtypeStruct((m, D), F32),
        grid=(m // TM_FFN, nf),
        in_specs=[
            pl.BlockSpec((TM_FFN, D), lambda i, f: (i, 0)),
            pl.BlockSpec((1, D), lambda i, f: (0, 0)),
            pl.BlockSpec((None, None, D, TF_FFN), lambda i, f: (li, si, 0, f)),
            pl.BlockSpec((None, None, D, TF_FFN), lambda i, f: (li, si, 0, f + nf)),
            pl.BlockSpec((None, None, TF_FFN, D), lambda i, f: (li, si, f, 0)),
            pl.BlockSpec((1, D), lambda i, f: (0, 0)),
        ],
        out_specs=pl.BlockSpec((TM_FFN, D), lambda i, f: (i, 0)),
        scratch_shapes=[pltpu.VMEM((TM_FFN, D), BF16), pltpu.VMEM((TM_FFN, D), F32)],
        compiler_params=_cparams(("parallel", "arbitrary"), 48),
        name="ffn",
    )(x, gain, w_gu, w_gu, w_dn, fg)


def _proj_body(x_ref, g_ref, w_ref, o_ref, xn_ref):
    @pl.when(pl.program_id(1) == 0)
    def _():
        xn_ref[...] = _rms(x_ref[...], g_ref[...]).astype(BF16)

    o_ref[...] = jnp.dot(xn_ref[...], w_ref[...], preferred_element_type=F32)


def _proj(x, gain, w):
    m = x.shape[0]
    n = w.shape[1]
    return pl.pallas_call(
        _proj_body,
        out_shape=jax.ShapeDtypeStruct((m, n), F32),
        grid=(m // TM_PROJ, n // TN_PROJ),
        in_specs=[
            pl.BlockSpec((TM_PROJ, D), lambda i, j: (i, 0)),
            pl.BlockSpec((1, D), lambda i, j: (0, 0)),
            pl.BlockSpec((D, TN_PROJ), lambda i, j: (0, j)),
        ],
        out_specs=pl.BlockSpec((TM_PROJ, TN_PROJ), lambda i, j: (i, j)),
        scratch_shapes=[pltpu.VMEM((TM_PROJ, D), BF16)],
        compiler_params=_cparams(("parallel", "arbitrary"), 40),
        name="proj",
    )(x, gain, w)


def _out2_body(res_ref, ya_ref, yb_ref, wa_ref, wb_ref, o_ref):
    acc = jnp.dot(ya_ref[...].astype(BF16), wa_ref[...], preferred_element_type=F32)
    acc = acc + jnp.dot(yb_ref[...].astype(BF16), wb_ref[...], preferred_element_type=F32)
    o_ref[...] = res_ref[...] + acc


def _out1_body(res_ref, y_ref, w_ref, o_ref):
    o_ref[...] = res_ref[...] + jnp.dot(y_ref[...].astype(BF16), w_ref[...], preferred_element_type=F32)


def _out_proj(res, ys, ws):
    m = res.shape[0]
    body = _out2_body if len(ys) == 2 else _out1_body
    y_spec = pl.BlockSpec((TM_OUT, D), lambda j, i: (i, 0))
    w_spec = pl.BlockSpec((D, TN_OUT), lambda j, i: (0, j))
    r_spec = pl.BlockSpec((TM_OUT, TN_OUT), lambda j, i: (i, j))
    return pl.pallas_call(
        body,
        out_shape=jax.ShapeDtypeStruct((m, D), F32),
        grid=(D // TN_OUT, m // TM_OUT),
        in_specs=[r_spec] + [y_spec] * len(ys) + [w_spec] * len(ws),
        out_specs=r_spec,
        compiler_params=_cparams(("arbitrary", "arbitrary"), 48),
        name="out_proj",
    )(res, *ys, *ws)


def _cast_pad_body(w_ref, o_ref, *, axis, zero_blocks):
    c = pl.program_id(axis)
    is_pad = functools.reduce(jnp.logical_or, [c == z for z in zero_blocks])
    o_ref[...] = jnp.where(is_pad, 0.0, w_ref[...]).astype(BF16)


def _cast_pad_halves_body(w_ref, o_ref):
    rows = o_ref.shape[0]
    zeros = jnp.zeros((rows, D_FF_PAD - D_FF), BF16)
    for half in range(2):
        o_ref[:, half * D_FF_PAD:half * D_FF_PAD + D_FF] = w_ref[:, half * D_FF:(half + 1) * D_FF].astype(BF16)
        o_ref[:, half * D_FF_PAD + D_FF:(half + 1) * D_FF_PAD] = zeros


def _reorder_cast_body(w_ref, o_ref, *, segments):
    col = 0
    for src, width in segments:
        o_ref[:, col:col + width] = w_ref[:, src:src + width].astype(BF16)
        col += width
    o_ref[:, col:] = jnp.zeros((o_ref.shape[0], o_ref.shape[1] - col), BF16)


def _reorder_cast_cols(w, segments, out_cols):
    rows, cols = w.shape
    return pl.pallas_call(
        functools.partial(_reorder_cast_body, segments=segments),
        out_shape=jax.ShapeDtypeStruct((rows, out_cols), BF16),
        grid=(rows // W_IN_ROWS,),
        in_specs=[pl.BlockSpec((W_IN_ROWS, cols), lambda r: (r, 0))],
        out_specs=pl.BlockSpec((W_IN_ROWS, out_cols), lambda r: (r, 0)),
        compiler_params=_cparams(("parallel",), 32),
        name="reorder_cast",
    )(w)


def _ffn_weights(w_in, w_out):
    n_layers, n_slots = w_in.shape[:2]
    nb = D_FF // LANES
    nbp = D_FF_PAD // LANES
    w_gu = pl.pallas_call(
        _cast_pad_halves_body,
        out_shape=jax.ShapeDtypeStruct((n_layers, n_slots, D, 2 * D_FF_PAD), BF16),
        grid=(n_layers * n_slots, D // W_IN_ROWS),
        in_specs=[pl.BlockSpec((None, None, W_IN_ROWS, 2 * D_FF), lambda q, r: (q // n_slots, q % n_slots, r, 0))],
        out_specs=pl.BlockSpec((None, None, W_IN_ROWS, 2 * D_FF_PAD), lambda q, r: (q // n_slots, q % n_slots, r, 0)),
        compiler_params=_cparams(("parallel", "parallel"), 32),
        name="ffn_w_in",
    )(w_in)
    w_dn = pl.pallas_call(
        functools.partial(_cast_pad_body, axis=1, zero_blocks=tuple(range(nb, nbp))),
        out_shape=jax.ShapeDtypeStruct((n_layers, n_slots, D_FF_PAD, D), BF16),
        grid=(n_layers * n_slots, nbp),
        in_specs=[pl.BlockSpec((None, None, LANES, D),
                               lambda q, r: (q // n_slots, q % n_slots, jnp.minimum(r, nb - 1), 0))],
        out_specs=pl.BlockSpec((None, None, LANES, D), lambda q, r: (q // n_slots, q % n_slots, r, 0)),
        compiler_params=_cparams(("parallel", "parallel"), 32),
        name="ffn_w_out",
    )(w_out)
    return w_gu, w_dn


def _carried_window(buf, cur_ref, init_ref, first, q):
    @pl.when(first)
    def _():
        buf[0:SUBLANES, :] = init_ref[0]

    buf[SUBLANES:SUBLANES + q, :] = cur_ref[...]


def _advance_window(buf, q):
    tail = buf[q:q + SUBLANES, :]
    buf[0:SUBLANES, :] = tail


def _causal_conv(buf, cw_ref, cb_ref, q):
    acc = cb_ref[...] + cw_ref[0:1, :] * buf[pl.ds(SUBLANES - CONV_W + 1, q), :]
    for k in range(1, CONV_W):
        acc = acc + cw_ref[k:k + 1, :] * buf[pl.ds(SUBLANES - CONV_W + 1 + k, q), :]
    return acc


def _cumsum_rows(x):
    n = x.shape[0]
    row = lax.broadcasted_iota(jnp.int32, x.shape, 0)
    s = 1
    while s < n:
        x = x + jnp.where(row >= s, pltpu.roll(x, s, axis=0), 0.0)
        s *= 2
    return x


def _expand_heads(a, rows):
    lane = lax.broadcasted_iota(jnp.int32, (rows, LANES), 1)
    low = lane < HEAD
    pieces = []
    for j in range(N_HEADS // 2):
        e0 = jnp.broadcast_to(a[:, 2 * j:2 * j + 1], (rows, LANES))
        e1 = jnp.broadcast_to(a[:, 2 * j + 1:2 * j + 2], (rows, LANES))
        pieces.append(jnp.where(low, e0, e1))
    return jnp.concatenate(pieces, axis=1)


def _head_allsum(x):
    width = x.shape[-1]
    blk = 4 * HEAD
    r = lax.broadcasted_iota(jnp.int32, (blk, blk), 0) // HEAD
    c = lax.broadcasted_iota(jnp.int32, (blk, blk), 1) // HEAD
    ones_bd = jnp.where(r == c, 1.0, 0.0).astype(BF16)
    hi = x.astype(BF16)
    rem = x - hi.astype(F32)
    mid = rem.astype(BF16)
    lo = (rem - mid.astype(F32)).astype(BF16)
    dot = lambda p: jnp.dot(p, ones_bd, preferred_element_type=F32)
    cols = []
    for j in range(width // blk):
        sl = slice(j * blk, (j + 1) * blk)
        cols.append(dot(hi[:, sl]) + (dot(mid[:, sl]) + dot(lo[:, sl])))
    return jnp.concatenate(cols, axis=1)


def _ssd_body(z_ref, x_ref, bc_ref, dt_ref, cix_ref, cibc_ref, h0_ref,
              cwx_ref, cwbc_ref, cbx_ref, cbbc_ref, dtb_ref, alog_ref, dsk_ref, gn_ref,
              y_ref, ho_ref, xbuf, bcbuf, h_s, *, nc):
    c = pl.program_id(1)
    q = CHUNK
    _carried_window(xbuf, x_ref, cix_ref, c == 0, q)
    _carried_window(bcbuf, bc_ref, cibc_ref, c == 0, q)

    @pl.when(c == 0)
    def _():
        h_s[...] = h0_ref[0]

    xs = _silu(_causal_conv(xbuf, cwx_ref, cbx_ref, CHUNK))
    bcv = _silu(_causal_conv(bcbuf, cwbc_ref, cbbc_ref, CHUNK))
    _advance_window(xbuf, q)
    _advance_window(bcbuf, q)
    z = z_ref[...]
    dt = _softplus(dt_ref[...] + dtb_ref[...])
    a_head = -jnp.exp(alog_ref[...])
    cum = _cumsum_rows(dt * a_head)
    cum_last = cum[CHUNK - 1:CHUNK, :]
    cum_t = cum.T
    dtx = _expand_heads(dt, CHUNK)
    ecx = _expand_heads(jnp.exp(cum), CHUNK)
    tlx = _expand_heads(jnp.exp(cum_last - cum), CHUNK)
    xdt = xs * dtx
    xdtw = (xdt * tlx).astype(BF16)

    ti = lax.broadcasted_iota(jnp.int32, (CHUNK, CHUNK), 0)
    tj = lax.broadcasted_iota(jnp.int32, (CHUNK, CHUNK), 1)
    causal = ti >= tj
    low = tj < HEAD
    nt = (((1,), (1,)), ((), ()))
    tn = (((0,), (0,)), ((), ()))

    for g in range(N_GROUPS):
        gsl = slice(g * GROUP_W, (g + 1) * GROUP_W)
        bg = bcv[:, g * N_STATE:(g + 1) * N_STATE].astype(BF16)
        cg = bcv[:, (N_GROUPS + g) * N_STATE:(N_GROUPS + g + 1) * N_STATE].astype(BF16)
        cb = lax.dot_general(cg, bg, nt, preferred_element_type=F32)
        hg = h_s[gsl, :]
        y_off = lax.dot_general(cg, hg.astype(BF16), nt, preferred_element_type=F32) * ecx[:, gsl]
        st = lax.dot_general(xdtw[:, gsl], bg, tn, preferred_element_type=F32)
        y_pairs = []
        for j in range(GROUP_W // LANES):
            h0 = g * (GROUP_W // HEAD) + 2 * j
            ms = []
            for h in (h0, h0 + 1):
                seg = cum[:, h:h + 1] - cum_t[h:h + 1, :]
                decay = jnp.exp(jnp.where(causal, seg, -jnp.inf))
                ms.append((cb * decay).astype(BF16))
            psl = slice(h0 * HEAD, (h0 + 2) * HEAD)
            slab = xdt[:, psl]
            rhs = jnp.concatenate([jnp.where(low, slab, 0.0), jnp.where(low, 0.0, slab)], axis=0).astype(BF16)
            y_pairs.append(jnp.dot(jnp.concatenate(ms, axis=1), rhs, preferred_element_type=F32))
            for h in (h0, h0 + 1):
                hsl = slice(h * HEAD, (h + 1) * HEAD)
                dec = jnp.exp(jnp.broadcast_to(cum_last[:, h:h + 1], (HEAD, N_STATE)))
                hg = slice((h % HEADS_PER_GROUP) * HEAD, (h % HEADS_PER_GROUP + 1) * HEAD)
                h_s[hsl, :] = h_s[hsl, :] * dec + st[hg, :]
        y = jnp.concatenate(y_pairs, axis=1) + y_off + xs[:, gsl] * dsk_ref[:, gsl]
        y = y * _silu(z[:, gsl])
        y_ref[:, gsl] = y * lax.rsqrt(jnp.mean(y * y, axis=-1, keepdims=True) + EPS) * gn_ref[:, gsl]

    @pl.when(c == nc - 1)
    def _():
        ho_ref[0] = h_s[...]


def _ssd(p, conv_x, conv_bc, h0, prm, nseq, seq_len):
    q = CHUNK
    nc = seq_len // q
    rows = lambda w, col: pl.BlockSpec((q, w), lambda s, c: (s * nc + c, col))
    per_seq = lambda shape: pl.BlockSpec((1,) + shape, lambda s, c: (s,) + (0,) * len(shape))
    const = lambda shape: pl.BlockSpec(shape, lambda s, c: (0,) * len(shape))
    y, h_last = pl.pallas_call(
        functools.partial(_ssd_body, nc=nc),
        out_shape=(jax.ShapeDtypeStruct((nseq * seq_len, D), F32),
                   jax.ShapeDtypeStruct((nseq, D, N_STATE), F32)),
        grid=(nseq, nc),
        in_specs=[
            rows(D, COL_Z // D), rows(D, COL_X // D), rows(BC_W, COL_BC // BC_W),
            rows(LANES, COL_DT // LANES),
            per_seq((SUBLANES, D)), per_seq((SUBLANES, BC_W)), per_seq((D, N_STATE)),
            const((CONV_W, D)), const((CONV_W, BC_W)), const((1, D)), const((1, BC_W)),
            const((1, LANES)), const((1, LANES)), const((1, D)), const((1, D)),
        ],
        out_specs=(pl.BlockSpec((q, D), lambda s, c: (s * nc + c, 0)), per_seq((D, N_STATE))),
        scratch_shapes=[
            pltpu.VMEM((CHUNK + SUBLANES, D), F32), pltpu.VMEM((CHUNK + SUBLANES, BC_W), F32),
            pltpu.VMEM((D, N_STATE), F32),
        ],
        compiler_params=_cparams(("parallel", "arbitrary"), 40),
        name="ssd",
    )(p, p, p, p, conv_x, conv_bc, h0, *prm)
    return y, h_last


def _ssd_short_body(z_ref, x_ref, bc_ref, dt_ref, px1_ref, px2_ref, px3_ref, pb1_ref, pb2_ref, pb3_ref, h0_ref,
                    cwx_ref, cwbc_ref, cbx_ref, cbbc_ref, dtb_ref, alog_ref, dsk_ref, gn_ref,
                    y_ref, ho_ref, *, seq_len):
    rows = x_ref.shape[0]
    nseq_t = rows // seq_len
    t = lax.broadcasted_iota(jnp.int32, (rows, LANES), 0) % seq_len

    def conv(cur_ref, p1_ref, p2_ref, p3_ref, cw_ref, cb_ref):
        cur = cur_ref[...]
        tt = lax.broadcasted_iota(jnp.int32, cur.shape, 0) % seq_len
        prev = lambda s, ref: jnp.where(tt >= s, pltpu.roll(cur, s, axis=0), ref[...])
        return (cb_ref[...] + cw_ref[0:1, :] * prev(3, p3_ref) + cw_ref[1:2, :] * prev(2, p2_ref)
                + cw_ref[2:3, :] * prev(1, p1_ref) + cw_ref[3:4, :] * cur)

    xs = _silu(conv(x_ref, px1_ref, px2_ref, px3_ref, cwx_ref, cbx_ref))
    bcv = _silu(conv(bc_ref, pb1_ref, pb2_ref, pb3_ref, cwbc_ref, cbbc_ref))
    z = z_ref[...]
    dt = _softplus(dt_ref[...] + dtb_ref[...])
    cum = dt * -jnp.exp(alog_ref[...])
    s = 1
    while s < seq_len:
        cum = cum + jnp.where(t >= s, pltpu.roll(cum, s, axis=0), 0.0)
        s *= 2
    tot = cum
    s = 1
    while s < seq_len:
        tot = jnp.where(t + s < seq_len, pltpu.roll(tot, rows - s, axis=0), tot)
        s *= 2
    cum_sq = cum if rows == LANES else jnp.concatenate([cum, jnp.zeros((LANES - rows, LANES), F32)], axis=0)
    cum_t = cum_sq.T[:, 0:rows]
    dtx = _expand_heads(dt, rows)
    ecx = _expand_heads(jnp.exp(cum), rows)
    tlx = _expand_heads(jnp.exp(tot - cum), rows)
    xdt = xs * dtx
    xdtw = (xdt * tlx).astype(BF16)

    ti = lax.broadcasted_iota(jnp.int32, (rows, rows), 0)
    tj = lax.broadcasted_iota(jnp.int32, (rows, rows), 1)
    same_causal = (ti >= tj) & (ti // seq_len == tj // seq_len)
    low = lax.broadcasted_iota(jnp.int32, (rows, LANES), 1) < HEAD
    row_seq = lax.broadcasted_iota(jnp.int32, (rows, N_STATE), 0) // seq_len

    def per_seq_cols(m):
        return jnp.concatenate([jnp.where(row_seq == q, m, 0.0) for q in range(nseq_t)], axis=1).astype(BF16)

    for g in range(N_GROUPS):
        gsl = slice(g * GROUP_W, (g + 1) * GROUP_W)
        bg_f = bcv[:, g * N_STATE:(g + 1) * N_STATE]
        cg_f = bcv[:, (N_GROUPS + g) * N_STATE:(N_GROUPS + g + 1) * N_STATE]
        cb = lax.dot_general(cg_f.astype(BF16), bg_f.astype(BF16), _NT, preferred_element_type=F32)
        h_cat = jnp.concatenate([h0_ref[q, gsl, :] for q in range(nseq_t)], axis=1).astype(BF16)
        y_off = lax.dot_general(per_seq_cols(cg_f), h_cat, _NT, preferred_element_type=F32) * ecx[:, gsl]
        st = lax.dot_general(xdtw[:, gsl], per_seq_cols(bg_f), _TN, preferred_element_type=F32)
        y_pairs = []
        for j in range(GROUP_W // LANES):
            h_lo = g * (GROUP_W // HEAD) + 2 * j
            ms = []
            for h in (h_lo, h_lo + 1):
                seg = cum[:, h:h + 1] - cum_t[h:h + 1, :]
                decay = jnp.exp(jnp.where(same_causal, seg, -jnp.inf))
                ms.append((cb * decay).astype(BF16))
            slab = xdt[:, h_lo * HEAD:(h_lo + 2) * HEAD]
            rhs = jnp.concatenate([jnp.where(low, slab, 0.0), jnp.where(low, 0.0, slab)], axis=0).astype(BF16)
            y_pairs.append(jnp.dot(jnp.concatenate(ms, axis=1), rhs, preferred_element_type=F32))
            for h in (h_lo, h_lo + 1):
                hsl = slice(h * HEAD, (h + 1) * HEAD)
                hg = slice((h % HEADS_PER_GROUP) * HEAD, (h % HEADS_PER_GROUP + 1) * HEAD)
                for q in range(nseq_t):
                    dec = jnp.exp(jnp.broadcast_to(tot[q * seq_len:q * seq_len + 1, h:h + 1], (HEAD, N_STATE)))
                    ho_ref[q, hsl, :] = h0_ref[q, hsl, :] * dec + st[hg, q * N_STATE:(q + 1) * N_STATE]
        y = jnp.concatenate(y_pairs, axis=1) + y_off + xs[:, gsl] * dsk_ref[:, gsl]
        y = y * _silu(z[:, gsl])
        y_ref[:, gsl] = y * lax.rsqrt(jnp.mean(y * y, axis=-1, keepdims=True) + EPS) * gn_ref[:, gsl]


def _ssd_short(p, conv_state, h0, prm, nseq, seq_len):
    rows = nseq * seq_len
    tr = SSD_SHORT_SEQS * seq_len
    bc_w = BC_W

    def dense(s, lo, hi):
        part = conv_state[:, CONV_W - 1 - s:, lo:hi]
        return jnp.pad(part, ((0, 0), (0, seq_len - s), (0, 0))).reshape(rows, hi - lo)

    tile = lambda w, col: pl.BlockSpec((tr, w), lambda i: (i, col))
    const = lambda shape: pl.BlockSpec(shape, lambda i: (0,) * len(shape))
    state = pl.BlockSpec((SSD_SHORT_SEQS, D, N_STATE), lambda i: (i, 0, 0))
    return pl.pallas_call(
        functools.partial(_ssd_short_body, seq_len=seq_len),
        out_shape=(jax.ShapeDtypeStruct((rows, D), F32), jax.ShapeDtypeStruct((nseq, D, N_STATE), F32)),
        grid=(nseq // SSD_SHORT_SEQS,),
        in_specs=[
            tile(D, COL_Z // D), tile(D, COL_X // D), tile(bc_w, COL_BC // bc_w), tile(LANES, COL_DT // LANES),
            tile(D, 0), tile(D, 0), tile(D, 0), tile(bc_w, 0), tile(bc_w, 0), tile(bc_w, 0), state,
            const((CONV_W, D)), const((CONV_W, bc_w)), const((1, D)), const((1, bc_w)),
            const((1, LANES)), const((1, LANES)), const((1, D)), const((1, D)),
        ],
        out_specs=(tile(D, 0), state),
        compiler_params=_cparams(("parallel",), 48),
        name="ssd_short",
    )(p, p, p, p, dense(1, 0, D), dense(2, 0, D), dense(3, 0, D), dense(1, D, D + bc_w), dense(2, D, D + bc_w),
      dense(3, D, D + bc_w), h0, *prm)


def _rwkv_pre_body(rkv_ref, lora_ref, si_rkv_ref, si_lora_ref, mu_rkv_ref, mu_lora_ref,
                   w0_ref, w2_ref, a0_ref, a2_ref, g2_ref, kk_ref, ka_ref, rk_ref,
                   r_out, w_out, k_out, v_out, kk_out, b_out, bonus_out, g_out,
                   rkvbuf, lorabuf, *, q, log_decay, short_len):
    if short_len:
        def shifted(buf, cur_ref, si_ref, mu_ref):
            cur = cur_ref[...]
            first = lax.broadcasted_iota(jnp.int32, cur.shape, 0) % short_len == 0
            prev = jnp.where(first, si_ref[...], pltpu.roll(cur, 1, axis=0))
            return cur + mu_ref[...] * (prev - cur)
    else:
        c = pl.program_id(1)
        _carried_window(rkvbuf, rkv_ref, si_rkv_ref, c == 0, q)
        _carried_window(lorabuf, lora_ref, si_lora_ref, c == 0, q)

        def shifted(buf, cur_ref, si_ref, mu_ref):
            cur = cur_ref[...]
            prev = buf[pl.ds(SUBLANES - 1, q), :]
            return cur + mu_ref[...] * (prev - cur)

    ps = shifted(rkvbuf, rkv_ref, si_rkv_ref, mu_rkv_ref)
    lo_in = shifted(lorabuf, lora_ref, si_lora_ref, mu_lora_ref)
    if not short_len:
        _advance_window(rkvbuf, q)
        _advance_window(lorabuf, q)
    r = ps[:, 0:D]
    k = ps[:, D:2 * D]
    v = ps[:, 2 * D:3 * D]

    lw = jnp.dot(jnp.tanh(lo_in).astype(BF16), w2_ref[...], preferred_element_type=F32)
    la = jnp.dot(lo_in.astype(BF16), a2_ref[...], preferred_element_type=F32)
    g = jnp.dot(jax.nn.sigmoid(lo_in).astype(BF16), g2_ref[...], preferred_element_type=F32)
    wlog = -_softplus(-(w0_ref[...] + lw)) - 0.5
    log_w = -jnp.exp(wlog)
    a = jax.nn.sigmoid(a0_ref[...] + la)
    kkf = k * kk_ref[...]
    norm = jnp.maximum(jnp.sqrt(_head_allsum(kkf * kkf)), 1e-12)
    kk = kkf / norm
    k2 = k * (1.0 + (a - 1.0) * ka_ref[...])
    bonus = _head_allsum(r * k2 * rk_ref[...]) * v
    r_out[...] = r
    w_out[...] = log_w if log_decay else jnp.exp(log_w)
    k_out[...] = k2
    v_out[...] = v
    kk_out[...] = kk
    b_out[...] = kk * a
    bonus_out[...] = bonus
    g_out[...] = g


def _rwkv_pre(p, shift_state, prm, nseq, seq_len, log_decay):
    short_len = seq_len if seq_len < CHUNK else 0
    q = CHUNK if short_len else min(CHUNK, seq_len)
    nc = 1 if short_len else seq_len // q
    ntile = nseq * seq_len // q // nc
    const = lambda shape: pl.BlockSpec(shape, lambda s, c: (0,) * len(shape))
    tile = pl.BlockSpec((q, D), lambda s, c: (s * nc + c, 0))
    if short_len:
        dense = lambda a: jnp.pad(a, ((0, 0), (0, seq_len - 1), (0, 0))).reshape(nseq * seq_len, a.shape[-1])
        si = (dense(shift_state[:, :, :3 * D]), dense(shift_state[:, :, 3 * D:]))
        si_specs = [pl.BlockSpec((q, 3 * D), lambda s, c: (s, 0)), pl.BlockSpec((q, LORA_W), lambda s, c: (s, 0))]
    else:
        si = (_pad_front_rows(shift_state[:, :, :3 * D]), _pad_front_rows(shift_state[:, :, 3 * D:]))
        si_specs = [pl.BlockSpec((1, SUBLANES, 3 * D), lambda s, c: (s, 0, 0)),
                    pl.BlockSpec((1, SUBLANES, LORA_W), lambda s, c: (s, 0, 0))]
    sds = jax.ShapeDtypeStruct((nseq * seq_len, D), F32)
    return pl.pallas_call(
        functools.partial(_rwkv_pre_body, q=q, log_decay=log_decay, short_len=short_len),
        out_shape=(sds,) * 8,
        grid=(ntile, nc),
        in_specs=[
            pl.BlockSpec((q, 3 * D), lambda s, c: (s * nc + c, 0)),
            pl.BlockSpec((q, LORA_W), lambda s, c: (s * nc + c, COL_LORA // LORA_W)),
            *si_specs,
            const((1, 3 * D)), const((1, LORA_W)),
            const((1, D)), const((LORA_W, D)), const((1, D)), const((LORA_W, D)), const((LORA_W, D)),
            const((1, D)), const((1, D)), const((1, D)),
        ],
        out_specs=(tile,) * 8,
        scratch_shapes=[pltpu.VMEM((CHUNK + SUBLANES, 3 * D), F32), pltpu.VMEM((CHUNK + SUBLANES, LORA_W), F32)],
        compiler_params=_cparams(("parallel", "arbitrary"), 48),
        name="rwkv_pre",
    )(p, p, *si, *prm)


def _rwkv_out(o, bonus, g, ln_w, ln_b):
    mean = _head_allsum(o) * (1.0 / HEAD)
    cen = o - mean
    var = _head_allsum(cen * cen) * (1.0 / HEAD)
    return (cen * lax.rsqrt(var + GN_EPS) * ln_w + ln_b + bonus) * g


def _split_bf16(x):
    hi = x.astype(BF16)
    return hi, (x - hi.astype(F32)).astype(BF16)


_NN = (((1,), (0,)), ((), ()))
_NT = (((1,), (1,)), ((), ()))
_TN = (((0,), (0,)), ((), ()))


def _solve_unit_lower(n, rhs, lower_left):
    rows, width = rhs.shape
    half = rows // 2
    nblk, ncol = rows // SUBLANES, width // LANES
    tiles = lambda a, r0, r1: [[a[SUBLANES * i:SUBLANES * (i + 1), LANES * j:LANES * (j + 1)] for j in range(ncol)]
                               for i in range(r0 // SUBLANES, r1 // SUBLANES)]
    nb = tiles(n, 0, rows)
    xb = tiles(rhs, 0, rows)
    low = lax.broadcasted_iota(jnp.int32, (SUBLANES, LANES), 1) < HEAD

    def substitute(lo, hi):
        for s in range(lo, hi - 1):
            i0, r0 = divmod(s, SUBLANES)
            idx = jnp.where(low, s, HEAD + s)
            for j in range(ncol):
                row = xb[i0][j][r0:r0 + 1, :]
                for i in range(i0 if r0 < SUBLANES - 1 else i0 + 1, hi // SUBLANES):
                    xb[i][j] = xb[i][j] - jnp.take_along_axis(nb[i][j], idx, axis=1) * row

    join = lambda blocks: jnp.concatenate([jnp.concatenate(xr, axis=1) for xr in blocks], axis=0)
    substitute(0, half)
    corr = tiles(lower_left(join(xb[:half // SUBLANES])), 0, half)
    for i in range(half // SUBLANES):
        for j in range(ncol):
            xb[half // SUBLANES + i][j] = xb[half // SUBLANES + i][j] - corr[i][j]
    substitute(half, rows)
    return join(xb)


def _wkv_tile(r, lw, k, v, kk, b, st):
    rows, width = r.shape
    nh = WKV_LANES // HEAD
    groups = [slice(g * WKV_LANES, (g + 1) * WKV_LANES) for g in range(width // WKV_LANES)]
    cl = _cumsum_rows(lw)
    cl_last = cl[rows - 1:rows, :]
    p_inv = jnp.exp(-cl)
    p_end = jnp.exp(cl_last - cl)
    x2h, x2l = _split_bf16(jnp.concatenate([kk * jnp.exp(cl - lw), r * jnp.exp(cl)], axis=0))
    k_hat = k * p_inv
    b_hat = b * p_inv
    k_end = k * p_end
    b_end = -(b * p_end)

    bd_r = lax.broadcasted_iota(jnp.int32, (nh * rows, WKV_LANES), 0) // rows
    bd_c = lax.broadcasted_iota(jnp.int32, (nh * rows, WKV_LANES), 1) // HEAD
    bd_mask = jnp.where(bd_r == bd_c, 1.0, 0.0).astype(BF16)

    def per_head_rows(y):
        return [jnp.concatenate([part] * nh, axis=0) * bd_mask for part in _split_bf16(y)]

    def dot3(ah, al, bh, bl, dims):
        dg = lambda x, y: lax.dot_general(x, y, dims, preferred_element_type=F32)
        return dg(ah, bh) + (dg(ah, bl) + dg(al, bh))

    t_i = lax.broadcasted_iota(jnp.int32, (rows, WKV_LANES), 0)
    lane = lax.broadcasted_iota(jnp.int32, (rows, WKV_LANES), 1)
    s_i = lane & (HEAD - 1)
    strict = t_i > s_i
    incl = t_i >= s_i
    head_of_lane = lane // HEAD

    a_kb, a_rb, base, o_part = [], [], [], []
    for gs in groups:
        ak = dot3(x2h[:, gs], x2l[:, gs], *per_head_rows(k_hat[:, gs]), _NT)
        ab = dot3(x2h[:, gs], x2l[:, gs], *per_head_rows(b_hat[:, gs]), _NT)
        a_k = jnp.concatenate([jnp.where(strict, ak[0:rows], 0.0), jnp.where(incl, ak[rows:], 0.0)], axis=0)
        xs = dot3(x2h[:, gs], x2l[:, gs], *per_head_rows(st[:, gs]), _NN)
        av = dot3(*_split_bf16(a_k), *per_head_rows(v[:, gs]), _NN)
        a_kb.append(jnp.where(strict, ab[0:rows], 0.0))
        a_rb.append(jnp.where(incl, ab[rows:], 0.0))
        base.append(xs[0:rows] + av[0:rows])
        o_part.append(xs[rows:] + av[rows:])
    half = rows // 2
    top_cols = (lax.broadcasted_iota(jnp.int32, (half, WKV_LANES), 1) & (HEAD - 1)) < half

    def lower_left(x_top):
        x_pad = jnp.concatenate([x_top, jnp.zeros_like(x_top)], axis=0)
        parts = []
        for g, gs in enumerate(groups):
            lhs = jnp.where(top_cols, a_kb[g][half:], 0.0)
            parts.append(dot3(*_split_bf16(lhs), *per_head_rows(x_pad[:, gs]), _NN))
        return jnp.concatenate(parts, axis=1)

    sa = _solve_unit_lower(jnp.concatenate(a_kb, axis=1), jnp.concatenate(base, axis=1), lower_left)

    o, st_new = [], []
    for g, gs in enumerate(groups):
        o.append(o_part[g] - dot3(*_split_bf16(a_rb[g]), *per_head_rows(sa[:, gs]), _NN))
        eye = jnp.where(t_i == s_i, jnp.exp(cl_last[:, gs]), 0.0)
        lhs = jnp.concatenate([k_end[:, gs], b_end[:, gs], eye], axis=0)
        rhs = jnp.concatenate([v[:, gs], sa[:, gs], st[:, gs]], axis=0)
        full = dot3(*_split_bf16(lhs), *_split_bf16(rhs), _TN)
        acc = jnp.where(head_of_lane == 0, full[0:HEAD, :], 0.0)
        for h in range(1, nh):
            acc = acc + jnp.where(head_of_lane == h, full[h * HEAD:(h + 1) * HEAD, :], 0.0)
        st_new.append(acc)
    return jnp.concatenate(o, axis=1), jnp.concatenate(st_new, axis=1)


def _wkv_long_body(r_ref, lw_ref, k_ref, v_ref, kk_ref, b_ref, bonus_ref, g_ref, lnw_ref, lnb_ref,
                   y_ref, so_ref, st_s, *, nc):
    c = pl.program_id(2)

    @pl.when(c == 0)
    def _():
        st_s[...] = jnp.zeros_like(st_s)

    o, st_new = _wkv_tile(r_ref[...], lw_ref[...], k_ref[...], v_ref[...], kk_ref[...], b_ref[...], st_s[...])
    y_ref[...] = _rwkv_out(o, bonus_ref[...], g_ref[...], lnw_ref[...], lnb_ref[...])
    st_s[...] = st_new

    @pl.when(c == nc - 1)
    def _():
        for j in range(WKV_GROUPS * WKV_LANES // LANES):
            js = slice(j * LANES, (j + 1) * LANES)
            sq = jnp.concatenate([st_s[:, js], jnp.zeros((LANES - HEAD, LANES), F32)], axis=0)
            so_ref[0, js, :] = sq.T[:, 0:HEAD]


def _wkv_long(r, lw, k, v, kk, b, bonus, g, ln_w, ln_b, nseq, seq_len):
    nc = seq_len // WKV_CHUNK
    width = WKV_GROUPS * WKV_LANES
    tile = pl.BlockSpec((WKV_CHUNK, width), lambda s, hg, c: (s * nc + c, hg))
    vec = pl.BlockSpec((1, width), lambda s, hg, c: (0, hg))
    y, s_last = pl.pallas_call(
        functools.partial(_wkv_long_body, nc=nc),
        out_shape=(jax.ShapeDtypeStruct((nseq * seq_len, D), F32),
                   jax.ShapeDtypeStruct((nseq, D, HEAD), F32)),
        grid=(nseq, D // width, nc),
        in_specs=[tile] * 8 + [vec, vec],
        out_specs=(tile, pl.BlockSpec((1, width, HEAD), lambda s, hg, c: (s, hg, 0))),
        scratch_shapes=[pltpu.VMEM((HEAD, width), F32)],
        compiler_params=_cparams(("parallel", "parallel", "arbitrary"), 32),
        name="wkv_long",
    )(r, lw, k, v, kk, b, bonus, g, ln_w, ln_b)
    return y, s_last


def _wkv_short_body(r_ref, w_ref, k_ref, v_ref, kk_ref, b_ref, s0_ref, o_ref, so_ref, *, steps):
    def per_v_group(i, carry):
        vis = [i * WKV_SHORT_ROWS + u for u in range(WKV_SHORT_ROWS)]
        ss = [s0_ref[0, vi] for vi in vis]
        for t in range(steps):
            kk_t, w_t, b_t, k_t, r_t = kk_ref[t, 0], w_ref[t, 0], b_ref[t, 0], k_ref[t, 0], r_ref[t, 0]
            for u, vi in enumerate(vis):
                vrow = v_ref[t, 0, pl.ds(vi, 1), :]
                sa = jnp.sum(ss[u] * kk_t, axis=0, keepdims=True)
                ss[u] = ss[u] * w_t - b_t * sa + k_t * vrow
                o_ref[t, 0, pl.ds(vi, 1), :] = jnp.sum(ss[u] * r_t, axis=0, keepdims=True)
        for u, vi in enumerate(vis):
            so_ref[0, vi] = ss[u]
        return carry

    lax.fori_loop(0, HEAD // WKV_SHORT_ROWS, per_v_group, 0)


def _wkv_short(r, w, k, v, kk, b, s0, nseq, seq_len):
    vec = pl.BlockSpec((seq_len, 1, HEAD, nseq), lambda h: (0, h, 0, 0))
    st = pl.BlockSpec((1, HEAD, HEAD, nseq), lambda h: (h, 0, 0, 0))
    return pl.pallas_call(
        functools.partial(_wkv_short_body, steps=seq_len),
        out_shape=(jax.ShapeDtypeStruct((seq_len, N_HEADS, HEAD, nseq), F32),
                   jax.ShapeDtypeStruct((N_HEADS, HEAD, HEAD, nseq), F32)),
        grid=(N_HEADS,),
        in_specs=[vec] * 6 + [st],
        out_specs=(vec, st),
        compiler_params=_cparams(("parallel",), 32),
        name="wkv_short",
    )(r, w, k, v, kk, b, s0)


def _rwkv_post_body(o_ref, bonus_ref, g_ref, lnw_ref, lnb_ref, y_ref):
    y_ref[...] = _rwkv_out(o_ref[...], bonus_ref[...], g_ref[...], lnw_ref[...], lnb_ref[...])


def _rwkv_post(o, bonus, g, ln_w, ln_b):
    m = o.shape[0]
    tile = pl.BlockSpec((TM_POST, D), lambda i: (i, 0))
    const = pl.BlockSpec((1, D), lambda i: (0, 0))
    return pl.pallas_call(
        _rwkv_post_body,
        out_shape=jax.ShapeDtypeStruct((m, D), F32),
        grid=(m // TM_POST,),
        in_specs=[tile, tile, tile, const, const],
        out_specs=tile,
        compiler_params=_cparams(("parallel",), 32),
        name="rwkv_post",
    )(o, bonus, g, ln_w, ln_b)


def _lru_coeffs(xc, wga_ref, bga_ref, wgx_ref, bgx_ref, lam_ref):
    ra, rx = [], []
    for blk in range(LRU_BLOCKS):
        xh = xc[:, blk * LRU_BLK:(blk + 1) * LRU_BLK].astype(BF16)
        ra.append(jnp.dot(xh, wga_ref[blk], preferred_element_type=F32))
        rx.append(jnp.dot(xh, wgx_ref[blk], preferred_element_type=F32))
    rg = jax.nn.sigmoid(jnp.concatenate(ra, axis=1) + bga_ref[...])
    ig = jax.nn.sigmoid(jnp.concatenate(rx, axis=1) + bgx_ref[...])
    log_a = -LRU_C * rg * _softplus(-lam_ref[...])
    return jnp.exp(log_a), jnp.sqrt(1.0 - jnp.exp(2.0 * log_a)) * (ig * xc)


def _lru_body(gate_ref, x_ref, ci_ref, h0_ref, cw_ref, cb_ref, wga_ref, bga_ref, wgx_ref, bgx_ref, lam_ref,
              y_ref, ho_ref, xbuf, a_s, b_s, h_s, hc_s, *, q, nc):
    c = pl.program_id(1)
    _carried_window(xbuf, x_ref, ci_ref, c == 0, q)

    @pl.when(c == 0)
    def _():
        hc_s[...] = h0_ref[0]

    xc = _causal_conv(xbuf, cw_ref, cb_ref, q)
    _advance_window(xbuf, q)
    a_s[...], b_s[...] = _lru_coeffs(xc, wga_ref, bga_ref, wgx_ref, bgx_ref, lam_ref)

    def step(t, h):
        h = a_s[pl.ds(t, 1), :] * h + b_s[pl.ds(t, 1), :]
        h_s[pl.ds(t, 1), :] = h
        return h

    h = lax.fori_loop(0, q, step, hc_s[...])
    hc_s[...] = h
    y_ref[...] = h_s[...] * jax.nn.gelu(gate_ref[...])

    @pl.when(c == nc - 1)
    def _():
        ho_ref[0] = h


def _lru(p, conv_init, h0, prm, nseq, seq_len):
    q = min(CHUNK, seq_len)
    nc = seq_len // q
    per_seq = lambda shape: pl.BlockSpec((1,) + shape, lambda s, c: (s,) + (0,) * len(shape))
    const = lambda shape: pl.BlockSpec(shape, lambda s, c: (0,) * len(shape))
    return pl.pallas_call(
        functools.partial(_lru_body, q=q, nc=nc),
        out_shape=(jax.ShapeDtypeStruct((nseq * seq_len, D), F32),
                   jax.ShapeDtypeStruct((nseq, 1, D), F32)),
        grid=(nseq, nc),
        in_specs=[
            pl.BlockSpec((q, D), lambda s, c: (s * nc + c, 0)),
            pl.BlockSpec((q, D), lambda s, c: (s * nc + c, 1)),
            per_seq((SUBLANES, D)), per_seq((1, D)),
            const((CONV_W, D)), const((1, D)),
            const((LRU_BLOCKS, LRU_BLK, LRU_BLK)), const((1, D)),
            const((LRU_BLOCKS, LRU_BLK, LRU_BLK)), const((1, D)), const((1, D)),
        ],
        out_specs=(pl.BlockSpec((q, D), lambda s, c: (s * nc + c, 0)), per_seq((1, D))),
        scratch_shapes=[pltpu.VMEM((CHUNK + SUBLANES, D), F32), pltpu.VMEM((q, D), F32), pltpu.VMEM((q, D), F32),
                        pltpu.VMEM((q, D), F32), pltpu.VMEM((1, D), F32)],
        compiler_params=_cparams(("parallel", "arbitrary"), 32),
        name="lru",
    )(p, p, conv_init, h0, *prm)


def _lru_short_body(gate_ref, x_ref, p1_ref, p2_ref, p3_ref, h0_ref, cw_ref, cb_ref, wga_ref, bga_ref, wgx_ref,
                    bgx_ref, lam_ref, y_ref, h_ref, *, seq_len):
    cur = x_ref[...]
    t = lax.broadcasted_iota(jnp.int32, cur.shape, 0) % seq_len
    prev = lambda s, ref: jnp.where(t >= s, pltpu.roll(cur, s, axis=0), ref[...])
    xc = (cb_ref[...] + cw_ref[0:1, :] * prev(3, p3_ref) + cw_ref[1:2, :] * prev(2, p2_ref)
          + cw_ref[2:3, :] * prev(1, p1_ref) + cw_ref[3:4, :] * cur)
    a, b = _lru_coeffs(xc, wga_ref, bga_ref, wgx_ref, bgx_ref, lam_ref)
    s = 1
    while s < seq_len:
        keep = t >= s
        b = jnp.where(keep, a * pltpu.roll(b, s, axis=0) + b, b)
        a = jnp.where(keep, a * pltpu.roll(a, s, axis=0), a)
        s *= 2
    h = a * h0_ref[...] + b
    h_ref[...] = h
    y_ref[...] = h * jax.nn.gelu(gate_ref[...])


def _lru_short(p, conv_state, h0, prm, nseq, seq_len):
    rows = nseq * seq_len
    dense = lambda s: jnp.pad(conv_state[:, CONV_W - 1 - s:, :], ((0, 0), (0, seq_len - s), (0, 0))).reshape(rows, D)
    h0_rows = jnp.repeat(h0, seq_len, axis=0)
    tile = lambda col: pl.BlockSpec((CHUNK, D), lambda i: (i, col))
    const = lambda shape: pl.BlockSpec(shape, lambda i: (0,) * len(shape))
    sds = jax.ShapeDtypeStruct((rows, D), F32)
    return pl.pallas_call(
        functools.partial(_lru_short_body, seq_len=seq_len),
        out_shape=(sds, sds),
        grid=(rows // CHUNK,),
        in_specs=[tile(0), tile(1), tile(0), tile(0), tile(0), tile(0),
                  const((CONV_W, D)), const((1, D)),
                  const((LRU_BLOCKS, LRU_BLK, LRU_BLK)), const((1, D)),
                  const((LRU_BLOCKS, LRU_BLK, LRU_BLK)), const((1, D)), const((1, D))],
        out_specs=(tile(0), tile(0)),
        compiler_params=_cparams(("parallel",), 32),
        name="lru_short",
    )(p, p, dense(1), dense(2), dense(3), h0_rows, *prm)


def _pad_front_rows(buf):
    return jnp.pad(buf, ((0, 0), (SUBLANES - buf.shape[1], 0), (0, 0)))


def _row2(v):
    return v.reshape(1, -1)


def kernel(x_prompt, x_sample, state_ssm_a, state_conv_a, state_wkv_b, state_shift_b, state_lru_c, state_conv_c, norm_gain, w_ffn_in, w_ffn_out, w_in_ab, conv_w_a, conv_b_a, dt_bias_a, a_log_a, d_skip_a, gnorm_a, mu_b, w0_b, w2_b, a0_b, a2_b, g2_b, k_k_b, k_a_b, r_k_b, ln_w_b, ln_b_b, w_out_ab, w_in_c, conv_w_c, conv_b_c, w_gate_a_c, b_gate_a_c, w_gate_x_c, b_gate_x_c, lambda_c, w_out_c, final_norm_gain):
    w_gu, w_dn = _ffn_weights(w_ffn_in, w_ffn_out)

    in_a = D + (D + BC_W) + N_HEADS
    bc_w = BC_W
    segments = (
        (in_a, 3 * D),
        (0, D),
        (D, D),
        (2 * D, BC_W),
        (in_a + 3 * D, LORA_W),
        (in_a - N_HEADS, N_HEADS),
    )
    w_proj0 = _reorder_cast_cols(w_in_ab[0], segments, PROJ_W)
    w_proj1 = w_in_c[0].astype(BF16)
    w_out0 = w_out_ab[0].astype(BF16)
    w_out1 = w_out_c[0].astype(BF16)

    pad_lanes = lambda v: jnp.pad(v.reshape(1, -1), ((0, 0), (0, LANES - v.shape[-1])))
    rep_head = lambda v: jnp.repeat(v, HEAD).reshape(1, D)
    ssd_prm = (conv_w_a[0][:, :D], conv_w_a[0][:, D:], _row2(conv_b_a[0][:D]), _row2(conv_b_a[0][D:]),
               pad_lanes(dt_bias_a[0]), pad_lanes(a_log_a[0]), rep_head(d_skip_a[0]), _row2(gnorm_a[0]))
    lora_rows = lambda w, lo: jnp.pad(w, ((lo, LORA_W - lo - w.shape[0]), (0, 0))).astype(BF16)
    mu = mu_b[0]
    rwkv_prm = (_row2(mu[:3 * D]), _row2(mu[3 * D:]),
                _row2(w0_b[0]), lora_rows(w2_b[0], 0), _row2(a0_b[0]), lora_rows(a2_b[0], 64),
                lora_rows(g2_b[0], 128), _row2(k_k_b[0]), _row2(k_a_b[0]), _row2(r_k_b[0]))
    lru_prm = (conv_w_c[0], _row2(conv_b_c[0]), w_gate_a_c[0].astype(BF16), _row2(b_gate_a_c[0]),
               w_gate_x_c[0].astype(BF16), _row2(b_gate_x_c[0]), _row2(lambda_c[0]))

    def trunk(x3, ssm0, conva0, wkv0, shift0, lru0, convc0):
        nseq, seq_len, _ = x3.shape
        x = x3.reshape(nseq * seq_len, D)
        tail = lambda arr, n: arr.reshape(nseq, seq_len, arr.shape[-1])[:, seq_len - n:, :]

        x = _ffn(x, _row2(norm_gain[0, 0]), w_gu, w_dn, (0, 0))
        p = _proj(x, _row2(norm_gain[0, 1]), w_proj0)
        if seq_len >= CHUNK:
            ya, ssm_n = _ssd(p, _pad_front_rows(conva0[:, :, :D]), _pad_front_rows(conva0[:, :, D:]),
                             ssm0.reshape(nseq, D, N_STATE), ssd_prm, nseq, seq_len)
        else:
            ya, ssm_n = _ssd_short(p, conva0, ssm0.reshape(nseq, D, N_STATE), ssd_prm, nseq, seq_len)
        p_tail = tail(p, CONV_W - 1)
        conva_n = jnp.concatenate([p_tail[:, :, COL_X:COL_X + D], p_tail[:, :, COL_BC:COL_BC + bc_w]], axis=-1)
        shift_n = jnp.concatenate([p_tail[:, -1:, :3 * D], p_tail[:, -1:, COL_LORA:COL_LORA + LORA_W]], axis=-1)

        r, w, k, v, kk, b, bonus, g = _rwkv_pre(p, shift0, rwkv_prm, nseq, seq_len, log_decay=wkv0 is None)
        if wkv0 is None:
            yb, wkv_n = _wkv_long(r, w, k, v, kk, b, bonus, g, _row2(ln_w_b[0]), _row2(ln_b_b[0]), nseq, seq_len)
            wkv_n = wkv_n.reshape(nseq, N_HEADS, HEAD, HEAD)
        else:
            to_lanes = lambda a: a.reshape(nseq, seq_len, N_HEADS, HEAD).transpose(1, 2, 3, 0)
            o, wkv_n = _wkv_short(*[to_lanes(a) for a in (r, w, k, v, kk, b)], wkv0.transpose(1, 2, 3, 0),
                                  nseq, seq_len)
            o = o.transpose(3, 0, 1, 2).reshape(nseq * seq_len, D)
            wkv_n = wkv_n.transpose(3, 0, 1, 2)
            yb = _rwkv_post(o, bonus, g, _row2(ln_w_b[0]), _row2(ln_b_b[0]))
        x = _out_proj(x, (ya, yb), (w_out0[:D], w_out0[D:]))
        x = _ffn(x, _row2(norm_gain[0, 2]), w_gu, w_dn, (0, 1))

        x = _ffn(x, _row2(norm_gain[1, 0]), w_gu, w_dn, (1, 0))
        pc = _proj(x, _row2(norm_gain[1, 1]), w_proj1)
        if seq_len >= CHUNK:
            yc, lru_n = _lru(pc, _pad_front_rows(convc0), lru0.reshape(nseq, 1, D), lru_prm, nseq, seq_len)
        else:
            yc, h_rows = _lru_short(pc, convc0, lru0, lru_prm, nseq, seq_len)
            lru_n = tail(h_rows, 1)
        convc_n = tail(pc, CONV_W - 1)[:, :, D:]
        x = _out_proj(x, (yc,), (w_out1,))
        y = _ffn(x, _row2(norm_gain[1, 2]), w_gu, w_dn, (1, 1), final_gain=_row2(final_norm_gain))

        return (y.reshape(nseq, seq_len, D), ssm_n.reshape(1, nseq, N_HEADS, HEAD, N_STATE), conva_n[None],
                wkv_n[None], shift_n[None], lru_n.reshape(1, nseq, D), convc_n[None])

    bp = x_prompt.shape[0]
    zeros = lambda s: jnp.zeros((bp,) + s.shape[2:], F32)
    outs_p = trunk(x_prompt, zeros(state_ssm_a), zeros(state_conv_a), None, zeros(state_shift_b),
                   zeros(state_lru_c), zeros(state_conv_c))
    outs_s = trunk(x_sample, state_ssm_a[0], state_conv_a[0], state_wkv_b[0], state_shift_b[0],
                   state_lru_c[0], state_conv_c[0])
    return (outs_p[0], outs_s[0]) + outs_p[1:] + outs_s[1:]
```

```python
import functools

import jax
import jax.numpy as jnp
from jax import lax
from jax.experimental import pallas as pl
from jax.experimental.pallas import tpu as pltpu

F32 = jnp.float32
BF16 = jnp.bfloat16

D = 2048
D_FF = 5504
D_FF_PAD = 5632
HEAD = 64
N_HEADS = 32
N_GROUPS = 4
GROUP_W = D // N_GROUPS
N_STATE = 128
HEADS_PER_GROUP = N_HEADS // N_GROUPS
BC_W = 2 * N_GROUPS * N_STATE
CONV_W = 4
LORA_W = 256
LRU_BLOCKS = 8
LRU_BLK = D // LRU_BLOCKS
LRU_C = 8.0
EPS = 1e-6
GN_EPS = 64e-5
SUBLANES = 8
LANES = 128
CHUNK = 128
WKV_CHUNK = 64
WKV_LANES = 256
WKV_GROUPS = 4
WKV_TILE_CHUNKS = 4
WKV_SHORT_ROWS = 4
SSD_SHORT_SEQS = 8

COL_RKV = 0
COL_Z = 3 * D
COL_X = 4 * D
COL_BC = 5 * D
COL_LORA = 5 * D + 2 * N_GROUPS * N_STATE
COL_DT = COL_LORA + LORA_W
PROJ_W = COL_DT + 256

TM_FFN = 512
TF_FFN = 512
TM_PROJ = 1024
TN_PROJ = 512
TM_OUT = 512
TN_OUT = 1024
TM_POST = 256
W_IN_ROWS = 64


def _cparams(sem, vmem_mib):
    return pltpu.CompilerParams(dimension_semantics=sem, vmem_limit_bytes=vmem_mib * 1024 * 1024)


def _softplus(x):
    return jnp.maximum(x, 0.0) + jnp.log(1.0 + jnp.exp(-jnp.abs(x)))


def _silu(x):
    return x * jax.nn.sigmoid(x)


def _rms(x, gain):
    ms = jnp.mean(x * x, axis=-1, keepdims=True)
    return x * lax.rsqrt(ms + EPS) * gain


def _ffn_body(x_ref, g_ref, wg_ref, wu_ref, wo_ref, fg_ref, o_ref, xn_ref, acc_ref, *, nf, final):
    f = pl.program_id(1)

    @pl.when(f == 0)
    def _():
        xn_ref[...] = _rms(x_ref[...], g_ref[...]).astype(BF16)
        acc_ref[...] = jnp.zeros_like(acc_ref)

    xn = xn_ref[...]
    gate = jnp.dot(xn, wg_ref[...], preferred_element_type=F32)
    up = jnp.dot(xn, wu_ref[...], preferred_element_type=F32)
    h = (_silu(gate) * up).astype(BF16)
    acc_ref[...] += jnp.dot(h, wo_ref[...], preferred_element_type=F32)

    @pl.when(f == nf - 1)
    def _():
        y = x_ref[...] + 0.5 * acc_ref[...]
        if final:
            y = _rms(y, fg_ref[...])
        o_ref[...] = y


def _ffn(x, gain, w_gu, w_dn, which, final_gain=None):
    m = x.shape[0]
    li, si = which
    nf = D_FF_PAD // TF_FFN
    final = final_gain is not None
    fg = final_gain if final else gain
    return pl.pallas_call(
        functools.partial(_ffn_body, nf=nf, final=final),
        out_shape=jax.ShapeDtypeStruct((m, D), F32),
        grid=(m // TM_FFN, nf),
        in_specs=[
            pl.BlockSpec((TM_FFN, D), lambda i, f: (i, 0)),
            pl.BlockSpec((1, D), lambda i, f: (0, 0)),
            pl.BlockSpec((None, None, D, TF_FFN), lambda i, f: (li, si, 0, f)),
            pl.BlockSpec((None, None, D, TF_FFN), lambda i, f: (li, si, 0, f + nf)),
            pl.BlockSpec((None, None, TF_FFN, D), lambda i, f: (li, si, f, 0)),
            pl.BlockSpec((1, D), lambda i, f: (0, 0)),
        ],
        out_specs=pl.BlockSpec((TM_FFN, D), lambda i, f: (i, 0)),
        scratch_shapes=[pltpu.VMEM((TM_FFN, D), BF16), pltpu.VMEM((TM_FFN, D), F32)],
        compiler_params=_cparams(("parallel", "arbitrary"), 48),
        name="ffn",
    )(x, gain, w_gu, w_gu, w_dn, fg)


def _proj_body(x_ref, g_ref, w_ref, o_ref, xn_ref):
    @pl.when(pl.program_id(1) == 0)
    def _():
        xn_ref[...] = _rms(x_ref[...], g_ref[...]).astype(BF16)

    o_ref[...] = jnp.dot(xn_ref[...], w_ref[...], preferred_element_type=F32)


def _proj(x, gain, w):
    m = x.shape[0]
    n = w.shape[1]
    return pl.pallas_call(
        _proj_body,
        out_shape=jax.ShapeDtypeStruct((m, n), F32),
        grid=(m // TM_PROJ, n // TN_PROJ),
        in_specs=[
            pl.BlockSpec((TM_PROJ, D), lambda i, j: (i, 0)),
            pl.BlockSpec((1, D), lambda i, j: (0, 0)),
            pl.BlockSpec((D, TN_PROJ), lambda i, j: (0, j)),
        ],
        out_specs=pl.BlockSpec((TM_PROJ, TN_PROJ), lambda i, j: (i, j)),
        scratch_shapes=[pltpu.VMEM((TM_PROJ, D), BF16)],
        compiler_params=_cparams(("parallel", "arbitrary"), 40),
        name="proj",
    )(x, gain, w)


def _out2_body(res_ref, ya_ref, yb_ref, wa_ref, wb_ref, o_ref):
    acc = jnp.dot(ya_ref[...].astype(BF16), wa_ref[...], preferred_element_type=F32)
    acc = acc + jnp.dot(yb_ref[...].astype(BF16), wb_ref[...], preferred_element_type=F32)
    o_ref[...] = res_ref[...] + acc


def _out1_body(res_ref, y_ref, w_ref, o_ref):
    o_ref[...] = res_ref[...] + jnp.dot(y_ref[...].astype(BF16), w_ref[...], preferred_element_type=F32)


def _out_proj(res, ys, ws):
    m = res.shape[0]
    body = _out2_body if len(ys) == 2 else _out1_body
    y_spec = pl.BlockSpec((TM_OUT, D), lambda j, i: (i, 0))
    w_spec = pl.BlockSpec((D, TN_OUT), lambda j, i: (0, j))
    r_spec = pl.BlockSpec((TM_OUT, TN_OUT), lambda j, i: (i, j))
    return pl.pallas_call(
        body,
        out_shape=jax.ShapeDtypeStruct((m, D), F32),
        grid=(D // TN_OUT, m // TM_OUT),
        in_specs=[r_spec] + [y_spec] * len(ys) + [w_spec] * len(ws),
        out_specs=r_spec,
        compiler_params=_cparams(("arbitrary", "arbitrary"), 48),
        name="out_proj",
    )(res, *ys, *ws)


def _cast_pad_body(w_ref, o_ref, *, axis, zero_blocks):
    c = pl.program_id(axis)
    is_pad = functools.reduce(jnp.logical_or, [c == z for z in zero_blocks])
    o_ref[...] = jnp.where(is_pad, 0.0, w_ref[...]).astype(BF16)


def _cast_pad_halves_body(w_ref, o_ref):
    rows = o_ref.shape[0]
    zeros = jnp.zeros((rows, D_FF_PAD - D_FF), BF16)
    for half in range(2):
        o_ref[:, half * D_FF_PAD:half * D_FF_PAD + D_FF] = w_ref[:, half * D_FF:(half + 1) * D_FF].astype(BF16)
        o_ref[:, half * D_FF_PAD + D_FF:(half + 1) * D_FF_PAD] = zeros


def _reorder_cast_body(w_ref, o_ref, *, segments):
    col = 0
    for src, width in segments:
        o_ref[:, col:col + width] = w_ref[:, src:src + width].astype(BF16)
        col += width
    o_ref[:, col:] = jnp.zeros((o_ref.shape[0], o_ref.shape[1] - col), BF16)


def _reorder_cast_cols(w, segments, out_cols):
    _, rows, cols = w.shape
    return pl.pallas_call(
        functools.partial(_reorder_cast_body, segments=segments),
        out_shape=jax.ShapeDtypeStruct((rows, out_cols), BF16),
        grid=(rows // W_IN_ROWS,),
        in_specs=[pl.BlockSpec((None, W_IN_ROWS, cols), lambda r: (0, r, 0))],
        out_specs=pl.BlockSpec((W_IN_ROWS, out_cols), lambda r: (r, 0)),
        compiler_params=_cparams(("parallel",), 32),
        name="reorder_cast",
    )(w)


def _ffn_weights(w_in, w_out):
    n_layers, n_slots = w_in.shape[:2]
    nb = D_FF // LANES
    nbp = D_FF_PAD // LANES
    w_gu = pl.pallas_call(
        _cast_pad_halves_body,
        out_shape=jax.ShapeDtypeStruct((n_layers, n_slots, D, 2 * D_FF_PAD), BF16),
        grid=(n_layers * n_slots, D // W_IN_ROWS),
        in_specs=[pl.BlockSpec((None, None, W_IN_ROWS, 2 * D_FF), lambda q, r: (q // n_slots, q % n_slots, r, 0))],
        out_specs=pl.BlockSpec((None, None, W_IN_ROWS, 2 * D_FF_PAD), lambda q, r: (q // n_slots, q % n_slots, r, 0)),
        compiler_params=_cparams(("parallel", "parallel"), 32),
        name="ffn_w_in",
    )(w_in)
    w_dn = pl.pallas_call(
        functools.partial(_cast_pad_body, axis=1, zero_blocks=tuple(range(nb, nbp))),
        out_shape=jax.ShapeDtypeStruct((n_layers, n_slots, D_FF_PAD, D), BF16),
        grid=(n_layers * n_slots, nbp),
        in_specs=[pl.BlockSpec((None, None, LANES, D),
                               lambda q, r: (q // n_slots, q % n_slots, jnp.minimum(r, nb - 1), 0))],
        out_specs=pl.BlockSpec((None, None, LANES, D), lambda q, r: (q // n_slots, q % n_slots, r, 0)),
        compiler_params=_cparams(("parallel", "parallel"), 32),
        name="ffn_w_out",
    )(w_out)
    return w_gu, w_dn


def _carried_window(buf, cur_ref, init_ref, first, q):
    @pl.when(first)
    def _():
        buf[0:SUBLANES, :] = init_ref[0]

    buf[SUBLANES:SUBLANES + q, :] = cur_ref[...]


def _advance_window(buf, q):
    tail = buf[q:q + SUBLANES, :]
    buf[0:SUBLANES, :] = tail


def _causal_conv(buf, cw_ref, cb_ref, q):
    acc = cb_ref[...] + cw_ref[0:1, :] * buf[pl.ds(SUBLANES - CONV_W + 1, q), :]
    for k in range(1, CONV_W):
        acc = acc + cw_ref[k:k + 1, :] * buf[pl.ds(SUBLANES - CONV_W + 1 + k, q), :]
    return acc


def _cumsum_rows(x):
    n = x.shape[0]
    row = lax.broadcasted_iota(jnp.int32, x.shape, 0)
    s = 1
    while s < n:
        x = x + jnp.where(row >= s, pltpu.roll(x, s, axis=0), 0.0)
        s *= 2
    return x


def _expand_heads(a, rows):
    lane = lax.broadcasted_iota(jnp.int32, (rows, LANES), 1)
    low = lane < HEAD
    pieces = []
    for j in range(N_HEADS // 2):
        e0 = jnp.broadcast_to(a[:, 2 * j:2 * j + 1], (rows, LANES))
        e1 = jnp.broadcast_to(a[:, 2 * j + 1:2 * j + 2], (rows, LANES))
        pieces.append(jnp.where(low, e0, e1))
    return jnp.concatenate(pieces, axis=1)


def _head_allsum(x):
    width = x.shape[-1]
    blk = 4 * HEAD
    r = lax.broadcasted_iota(jnp.int32, (blk, blk), 0) // HEAD
    c = lax.broadcasted_iota(jnp.int32, (blk, blk), 1) // HEAD
    ones_bd = jnp.where(r == c, 1.0, 0.0).astype(BF16)
    hi = x.astype(BF16)
    rem = x - hi.astype(F32)
    mid = rem.astype(BF16)
    lo = (rem - mid.astype(F32)).astype(BF16)
    dot = lambda p: jnp.dot(p, ones_bd, preferred_element_type=F32)
    cols = []
    for j in range(width // blk):
        sl = slice(j * blk, (j + 1) * blk)
        cols.append(dot(hi[:, sl]) + (dot(mid[:, sl]) + dot(lo[:, sl])))
    return jnp.concatenate(cols, axis=1)


def _ssd_body(z_ref, x_ref, bc_ref, dt_ref, cix_ref, cibc_ref, h0_ref,
              cwx_ref, cwbc_ref, cbx_ref, cbbc_ref, dtb_ref, alog_ref, dsk_ref, gn_ref,
              y_ref, ho_ref, xbuf, bcbuf, h_s, *, nc):
    c = pl.program_id(1)
    q = CHUNK
    _carried_window(xbuf, x_ref, cix_ref, c == 0, q)
    _carried_window(bcbuf, bc_ref, cibc_ref, c == 0, q)

    @pl.when(c == 0)
    def _():
        h_s[...] = h0_ref[0]

    xs = _silu(_causal_conv(xbuf, cwx_ref, cbx_ref, CHUNK))
    bcv = _silu(_causal_conv(bcbuf, cwbc_ref, cbbc_ref, CHUNK))
    _advance_window(xbuf, q)
    _advance_window(bcbuf, q)
    z = z_ref[...]
    dt = _softplus(dt_ref[...] + dtb_ref[...])
    a_head = -jnp.exp(alog_ref[...])
    cum = _cumsum_rows(dt * a_head)
    cum_last = cum[CHUNK - 1:CHUNK, :]
    cum_t = cum.T
    dtx = _expand_heads(dt, CHUNK)
    ecx = _expand_heads(jnp.exp(cum), CHUNK)
    tlx = _expand_heads(jnp.exp(cum_last - cum), CHUNK)
    xdt = xs * dtx
    xdtw = (xdt * tlx).astype(BF16)

    ti = lax.broadcasted_iota(jnp.int32, (CHUNK, CHUNK), 0)
    tj = lax.broadcasted_iota(jnp.int32, (CHUNK, CHUNK), 1)
    causal = ti >= tj
    low = tj < HEAD
    nt = (((1,), (1,)), ((), ()))
    tn = (((0,), (0,)), ((), ()))

    for g in range(N_GROUPS):
        gsl = slice(g * GROUP_W, (g + 1) * GROUP_W)
        bg = bcv[:, g * N_STATE:(g + 1) * N_STATE].astype(BF16)
        cg = bcv[:, (N_GROUPS + g) * N_STATE:(N_GROUPS + g + 1) * N_STATE].astype(BF16)
        cb = lax.dot_general(cg, bg, nt, preferred_element_type=F32)
        hg = h_s[gsl, :]
        y_off = lax.dot_general(cg, hg.astype(BF16), nt, preferred_element_type=F32) * ecx[:, gsl]
        st = lax.dot_general(xdtw[:, gsl], bg, tn, preferred_element_type=F32)
        y_pairs = []
        for j in range(GROUP_W // LANES):
            h0 = g * (GROUP_W // HEAD) + 2 * j
            ms = []
            for h in (h0, h0 + 1):
                seg = cum[:, h:h + 1] - cum_t[h:h + 1, :]
                decay = jnp.exp(jnp.where(causal, seg, -jnp.inf))
                ms.append((cb * decay).astype(BF16))
            psl = slice(h0 * HEAD, (h0 + 2) * HEAD)
            slab = xdt[:, psl]
            rhs = jnp.concatenate([jnp.where(low, slab, 0.0), jnp.where(low, 0.0, slab)], axis=0).astype(BF16)
            y_pairs.append(jnp.dot(jnp.concatenate(ms, axis=1), rhs, preferred_element_type=F32))
            for h in (h0, h0 + 1):
                hsl = slice(h * HEAD, (h + 1) * HEAD)
                dec = jnp.exp(jnp.broadcast_to(cum_last[:, h:h + 1], (HEAD, N_STATE)))
                hg = slice((h % HEADS_PER_GROUP) * HEAD, (h % HEADS_PER_GROUP + 1) * HEAD)
                h_s[hsl, :] = h_s[hsl, :] * dec + st[hg, :]
        y = jnp.concatenate(y_pairs, axis=1) + y_off + xs[:, gsl] * dsk_ref[:, gsl]
        y = y * _silu(z[:, gsl])
        y_ref[:, gsl] = y * lax.rsqrt(jnp.mean(y * y, axis=-1, keepdims=True) + EPS) * gn_ref[:, gsl]

    @pl.when(c == nc - 1)
    def _():
        ho_ref[0] = h_s[...]


def _ssd(p, conv_x, conv_bc, h0, prm, nseq, seq_len):
    q = CHUNK
    nc = seq_len // q
    rows = lambda w, col: pl.BlockSpec((q, w), lambda s, c: (s * nc + c, col))
    per_seq = lambda shape: pl.BlockSpec((1,) + shape, lambda s, c: (s,) + (0,) * len(shape))
    const = lambda shape: pl.BlockSpec(shape, lambda s, c: (0,) * len(shape))
    y, h_last = pl.pallas_call(
        functools.partial(_ssd_body, nc=nc),
        out_shape=(jax.ShapeDtypeStruct((nseq * seq_len, D), F32),
                   jax.ShapeDtypeStruct((nseq, D, N_STATE), F32)),
        grid=(nseq, nc),
        in_specs=[
            rows(D, COL_Z // D), rows(D, COL_X // D), rows(BC_W, COL_BC // BC_W),
            rows(LANES, COL_DT // LANES),
            per_seq((SUBLANES, D)), per_seq((SUBLANES, BC_W)), per_seq((D, N_STATE)),
            const((CONV_W, D)), const((CONV_W, BC_W)), const((1, D)), const((1, BC_W)),
            const((1, LANES)), const((1, LANES)), const((1, D)), const((1, D)),
        ],
        out_specs=(pl.BlockSpec((q, D), lambda s, c: (s * nc + c, 0)), per_seq((D, N_STATE))),
        scratch_shapes=[
            pltpu.VMEM((CHUNK + SUBLANES, D), F32), pltpu.VMEM((CHUNK + SUBLANES, BC_W), F32),
            pltpu.VMEM((D, N_STATE), F32),
        ],
        compiler_params=_cparams(("parallel", "arbitrary"), 40),
        name="ssd",
    )(p, p, p, p, conv_x, conv_bc, h0, *prm)
    return y, h_last


def _ssd_short_body(z_ref, x_ref, bc_ref, dt_ref, px1_ref, px2_ref, px3_ref, pb1_ref, pb2_ref, pb3_ref, h0_ref,
                    cwx_ref, cwbc_ref, cbx_ref, cbbc_ref, dtb_ref, alog_ref, dsk_ref, gn_ref,
                    y_ref, ho_ref, *, seq_len):
    rows = x_ref.shape[0]
    nseq_t = rows // seq_len
    t = lax.broadcasted_iota(jnp.int32, (rows, LANES), 0) % seq_len

    def conv(cur_ref, p1_ref, p2_ref, p3_ref, cw_ref, cb_ref):
        cur = cur_ref[...]
        tt = lax.broadcasted_iota(jnp.int32, cur.shape, 0) % seq_len
        prev = lambda s, ref: jnp.where(tt >= s, pltpu.roll(cur, s, axis=0), ref[...])
        return (cb_ref[...] + cw_ref[0:1, :] * prev(3, p3_ref) + cw_ref[1:2, :] * prev(2, p2_ref)
                + cw_ref[2:3, :] * prev(1, p1_ref) + cw_ref[3:4, :] * cur)

    xs = _silu(conv(x_ref, px1_ref, px2_ref, px3_ref, cwx_ref, cbx_ref))
    bcv = _silu(conv(bc_ref, pb1_ref, pb2_ref, pb3_ref, cwbc_ref, cbbc_ref))
    z = z_ref[...]
    dt = _softplus(dt_ref[...] + dtb_ref[...])
    cum = dt * -jnp.exp(alog_ref[...])
    s = 1
    while s < seq_len:
        cum = cum + jnp.where(t >= s, pltpu.roll(cum, s, axis=0), 0.0)
        s *= 2
    tot = cum
    s = 1
    while s < seq_len:
        tot = jnp.where(t + s < seq_len, pltpu.roll(tot, rows - s, axis=0), tot)
        s *= 2
    cum_sq = cum if rows == LANES else jnp.concatenate([cum, jnp.zeros((LANES - rows, LANES), F32)], axis=0)
    cum_t = cum_sq.T[:, 0:rows]
    dtx = _expand_heads(dt, rows)
    ecx = _expand_heads(jnp.exp(cum), rows)
    tlx = _expand_heads(jnp.exp(tot - cum), rows)
    xdt = xs * dtx
    xdtw = (xdt * tlx).astype(BF16)

    ti = lax.broadcasted_iota(jnp.int32, (rows, rows), 0)
    tj = lax.broadcasted_iota(jnp.int32, (rows, rows), 1)
    same_causal = (ti >= tj) & (ti // seq_len == tj // seq_len)
    low = lax.broadcasted_iota(jnp.int32, (rows, LANES), 1) < HEAD
    row_seq = lax.broadcasted_iota(jnp.int32, (rows, N_STATE), 0) // seq_len

    def per_seq_cols(m):
        return jnp.concatenate([jnp.where(row_seq == q, m, 0.0) for q in range(nseq_t)], axis=1).astype(BF16)

    for g in range(N_GROUPS):
        gsl = slice(g * GROUP_W, (g + 1) * GROUP_W)
        bg_f = bcv[:, g * N_STATE:(g + 1) * N_STATE]
        cg_f = bcv[:, (N_GROUPS + g) * N_STATE:(N_GROUPS + g + 1) * N_STATE]
        cb = lax.dot_general(cg_f.astype(BF16), bg_f.astype(BF16), _NT, preferred_element_type=F32)
        h_cat = jnp.concatenate([h0_ref[q, gsl, :] for q in range(nseq_t)], axis=1).astype(BF16)
        y_off = lax.dot_general(per_seq_cols(cg_f), h_cat, _NT, preferred_element_type=F32) * ecx[:, gsl]
        st = lax.dot_general(xdtw[:, gsl], per_seq_cols(bg_f), _TN, preferred_element_type=F32)
        y_pairs = []
        for j in range(GROUP_W // LANES):
            h_lo = g * (GROUP_W // HEAD) + 2 * j
            ms = []
            for h in (h_lo, h_lo + 1):
                seg = cum[:, h:h + 1] - cum_t[h:h + 1, :]
                decay = jnp.exp(jnp.where(same_causal, seg, -jnp.inf))
                ms.append((cb * decay).astype(BF16))
            slab = xdt[:, h_lo * HEAD:(h_lo + 2) * HEAD]
            rhs = jnp.concatenate([jnp.where(low, slab, 0.0), jnp.where(low, 0.0, slab)], axis=0).astype(BF16)
            y_pairs.append(jnp.dot(jnp.concatenate(ms, axis=1), rhs, preferred_element_type=F32))
            for h in (h_lo, h_lo + 1):
                hsl = slice(h * HEAD, (h + 1) * HEAD)
                hg = slice((h % HEADS_PER_GROUP) * HEAD, (h % HEADS_PER_GROUP + 1) * HEAD)
                for q in range(nseq_t):
                    dec = jnp.exp(jnp.broadcast_to(tot[q * seq_len:q * seq_len + 1, h:h + 1], (HEAD, N_STATE)))
                    ho_ref[q, hsl, :] = h0_ref[q, hsl, :] * dec + st[hg, q * N_STATE:(q + 1) * N_STATE]
        y = jnp.concatenate(y_pairs, axis=1) + y_off + xs[:, gsl] * dsk_ref[:, gsl]
        y = y * _silu(z[:, gsl])
        y_ref[:, gsl] = y * lax.rsqrt(jnp.mean(y * y, axis=-1, keepdims=True) + EPS) * gn_ref[:, gsl]


def _ssd_short(p, conv_state, h0, prm, nseq, seq_len):
    rows = nseq * seq_len
    tr = SSD_SHORT_SEQS * seq_len
    bc_w = BC_W

    def dense(s, lo, hi):
        part = conv_state[:, CONV_W - 1 - s:, lo:hi]
        return jnp.pad(part, ((0, 0), (0, seq_len - s), (0, 0))).reshape(rows, hi - lo)

    tile = lambda w, col: pl.BlockSpec((tr, w), lambda i: (i, col))
    const = lambda shape: pl.BlockSpec(shape, lambda i: (0,) * len(shape))
    state = pl.BlockSpec((SSD_SHORT_SEQS, D, N_STATE), lambda i: (i, 0, 0))
    return pl.pallas_call(
        functools.partial(_ssd_short_body, seq_len=seq_len),
        out_shape=(jax.ShapeDtypeStruct((rows, D), F32), jax.ShapeDtypeStruct((nseq, D, N_STATE), F32)),
        grid=(nseq // SSD_SHORT_SEQS,),
        in_specs=[
            tile(D, COL_Z // D), tile(D, COL_X // D), tile(bc_w, COL_BC // bc_w), tile(LANES, COL_DT // LANES),
            tile(D, 0), tile(D, 0), tile(D, 0), tile(bc_w, 0), tile(bc_w, 0), tile(bc_w, 0), state,
            const((CONV_W, D)), const((CONV_W, bc_w)), const((1, D)), const((1, bc_w)),
            const((1, LANES)), const((1, LANES)), const((1, D)), const((1, D)),
        ],
        out_specs=(tile(D, 0), state),
        compiler_params=_cparams(("parallel",), 48),
        name="ssd_short",
    )(p, p, p, p, dense(1, 0, D), dense(2, 0, D), dense(3, 0, D), dense(1, D, D + bc_w), dense(2, D, D + bc_w),
      dense(3, D, D + bc_w), h0, *prm)


def _rwkv_pre_body(rkv_ref, lora_ref, si_rkv_ref, si_lora_ref, mu_rkv_ref, mu_lora_ref,
                   w0_ref, w2_ref, a0_ref, a2_ref, g2_ref, kk_ref, ka_ref, rk_ref,
                   r_out, w_out, k_out, v_out, kk_out, b_out, bonus_out, g_out,
                   rkvbuf, lorabuf, *, q, log_decay, short_len):
    if short_len:
        def shifted(buf, cur_ref, si_ref, mu_ref):
            cur = cur_ref[...]
            first = lax.broadcasted_iota(jnp.int32, cur.shape, 0) % short_len == 0
            prev = jnp.where(first, si_ref[...], pltpu.roll(cur, 1, axis=0))
            return cur + mu_ref[...] * (prev - cur)
    else:
        c = pl.program_id(1)
        _carried_window(rkvbuf, rkv_ref, si_rkv_ref, c == 0, q)
        _carried_window(lorabuf, lora_ref, si_lora_ref, c == 0, q)

        def shifted(buf, cur_ref, si_ref, mu_ref):
            cur = cur_ref[...]
            prev = buf[pl.ds(SUBLANES - 1, q), :]
            return cur + mu_ref[...] * (prev - cur)

    ps = shifted(rkvbuf, rkv_ref, si_rkv_ref, mu_rkv_ref)
    lo_in = shifted(lorabuf, lora_ref, si_lora_ref, mu_lora_ref)
    if not short_len:
        _advance_window(rkvbuf, q)
        _advance_window(lorabuf, q)
    r = ps[:, 0:D]
    k = ps[:, D:2 * D]
    v = ps[:, 2 * D:3 * D]

    lw = jnp.dot(jnp.tanh(lo_in).astype(BF16), w2_ref[...], preferred_element_type=F32)
    la = jnp.dot(lo_in.astype(BF16), a2_ref[...], preferred_element_type=F32)
    g = jnp.dot(jax.nn.sigmoid(lo_in).astype(BF16), g2_ref[...], preferred_element_type=F32)
    wlog = -_softplus(-(w0_ref[...] + lw)) - 0.5
    log_w = -jnp.exp(wlog)
    a = jax.nn.sigmoid(a0_ref[...] + la)
    kkf = k * kk_ref[...]
    norm = jnp.maximum(jnp.sqrt(_head_allsum(kkf * kkf)), 1e-12)
    kk = kkf / norm
    k2 = k * (1.0 + (a - 1.0) * ka_ref[...])
    bonus = _head_allsum(r * k2 * rk_ref[...]) * v
    r_out[...] = r
    w_out[...] = log_w if log_decay else jnp.exp(log_w)
    k_out[...] = k2
    v_out[...] = v
    kk_out[...] = kk
    b_out[...] = kk * a
    bonus_out[...] = bonus
    g_out[...] = g


def _rwkv_pre(p, shift_state, prm, nseq, seq_len, log_decay):
    short_len = seq_len if seq_len < CHUNK else 0
    q = CHUNK if short_len else min(CHUNK, seq_len)
    nc = 1 if short_len else seq_len // q
    ntile = nseq * seq_len // q // nc
    const = lambda shape: pl.BlockSpec(shape, lambda s, c: (0,) * len(shape))
    tile = pl.BlockSpec((q, D), lambda s, c: (s * nc + c, 0))
    if short_len:
        dense = lambda a: jnp.pad(a, ((0, 0), (0, seq_len - 1), (0, 0))).reshape(nseq * seq_len, a.shape[-1])
        si = (dense(shift_state[:, :, :3 * D]), dense(shift_state[:, :, 3 * D:]))
        si_specs = [pl.BlockSpec((q, 3 * D), lambda s, c: (s, 0)), pl.BlockSpec((q, LORA_W), lambda s, c: (s, 0))]
    else:
        si = (_pad_front_rows(shift_state[:, :, :3 * D]), _pad_front_rows(shift_state[:, :, 3 * D:]))
        si_specs = [pl.BlockSpec((1, SUBLANES, 3 * D), lambda s, c: (s, 0, 0)),
                    pl.BlockSpec((1, SUBLANES, LORA_W), lambda s, c: (s, 0, 0))]
    sds = jax.ShapeDtypeStruct((nseq * seq_len, D), F32)
    return pl.pallas_call(
        functools.partial(_rwkv_pre_body, q=q, log_decay=log_decay, short_len=short_len),
        out_shape=(sds,) * 8,
        grid=(ntile, nc),
        in_specs=[
            pl.BlockSpec((q, 3 * D), lambda s, c: (s * nc + c, 0)),
            pl.BlockSpec((q, LORA_W), lambda s, c: (s * nc + c, COL_LORA // LORA_W)),
            *si_specs,
            const((1, 3 * D)), const((1, LORA_W)),
            const((1, D)), const((LORA_W, D)), const((1, D)), const((LORA_W, D)), const((LORA_W, D)),
            const((1, D)), const((1, D)), const((1, D)),
        ],
        out_specs=(tile,) * 8,
        scratch_shapes=[pltpu.VMEM((CHUNK + SUBLANES, 3 * D), F32), pltpu.VMEM((CHUNK + SUBLANES, LORA_W), F32)],
        compiler_params=_cparams(("parallel", "arbitrary"), 48),
        name="rwkv_pre",
    )(p, p, *si, *prm)


def _rwkv_out(o, bonus, g, ln_w, ln_b):
    mean = _head_allsum(o) * (1.0 / HEAD)
    cen = o - mean
    var = _head_allsum(cen * cen) * (1.0 / HEAD)
    return (cen * lax.rsqrt(var + GN_EPS) * ln_w + ln_b + bonus) * g


def _split_bf16(x):
    hi = x.astype(BF16)
    return hi, (x - hi.astype(F32)).astype(BF16)


_NN = (((1,), (0,)), ((), ()))
_NT = (((1,), (1,)), ((), ()))
_TN = (((0,), (0,)), ((), ()))


def _solve_unit_lower(n, rhs, lower_left):
    rows, width = rhs.shape
    half = rows // 2
    nblk, ncol = rows // SUBLANES, width // LANES
    tiles = lambda a, r0, r1: [[a[SUBLANES * i:SUBLANES * (i + 1), LANES * j:LANES * (j + 1)] for j in range(ncol)]
                               for i in range(r0 // SUBLANES, r1 // SUBLANES)]
    nb = tiles(n, 0, rows)
    xb = tiles(rhs, 0, rows)
    low = lax.broadcasted_iota(jnp.int32, (SUBLANES, LANES), 1) < HEAD

    def substitute(lo, hi):
        for s in range(lo, hi - 1):
            i0, r0 = divmod(s, SUBLANES)
            idx = jnp.where(low, s, HEAD + s)
            for j in range(ncol):
                row = xb[i0][j][r0:r0 + 1, :]
                for i in range(i0 if r0 < SUBLANES - 1 else i0 + 1, hi // SUBLANES):
                    xb[i][j] = xb[i][j] - jnp.take_along_axis(nb[i][j], idx, axis=1) * row

    join = lambda blocks: jnp.concatenate([jnp.concatenate(xr, axis=1) for xr in blocks], axis=0)
    substitute(0, half)
    corr = tiles(lower_left(join(xb[:half // SUBLANES])), 0, half)
    for i in range(half // SUBLANES):
        for j in range(ncol):
            xb[half // SUBLANES + i][j] = xb[half // SUBLANES + i][j] - corr[i][j]
    substitute(half, rows)
    return join(xb)


def _wkv_chunk(r, lw, k, v, kk, b):
    rows, width = r.shape
    nh = WKV_LANES // HEAD
    groups = [slice(g * WKV_LANES, (g + 1) * WKV_LANES) for g in range(width // WKV_LANES)]
    cl = _cumsum_rows(lw)
    cl_last = cl[rows - 1:rows, :]
    p_inv = jnp.exp(-cl)
    p_end = jnp.exp(cl_last - cl)
    x2h, x2l = _split_bf16(jnp.concatenate([kk * jnp.exp(cl - lw), r * jnp.exp(cl)], axis=0))
    k_hat = k * p_inv
    b_hat = b * p_inv
    k_end = k * p_end
    b_end = -(b * p_end)

    bd_r = lax.broadcasted_iota(jnp.int32, (nh * rows, WKV_LANES), 0) // rows
    bd_c = lax.broadcasted_iota(jnp.int32, (nh * rows, WKV_LANES), 1) // HEAD
    bd_mask = jnp.where(bd_r == bd_c, 1.0, 0.0).astype(BF16)

    def per_head_rows(y):
        return [jnp.concatenate([part] * nh, axis=0) * bd_mask for part in _split_bf16(y)]

    def dot3(ah, al, bh, bl, dims):
        dg = lambda x, y: lax.dot_general(x, y, dims, preferred_element_type=F32)
        return dg(ah, bh) + (dg(ah, bl) + dg(al, bh))

    t_i = lax.broadcasted_iota(jnp.int32, (rows, WKV_LANES), 0)
    lane = lax.broadcasted_iota(jnp.int32, (rows, WKV_LANES), 1)
    s_i = lane & (HEAD - 1)
    strict = t_i > s_i
    incl = t_i >= s_i
    head_of_lane = lane // HEAD

    a_kb, a_rb, av = [], [], []
    for gs in groups:
        ak = dot3(x2h[:, gs], x2l[:, gs], *per_head_rows(k_hat[:, gs]), _NT)
        ab = dot3(x2h[:, gs], x2l[:, gs], *per_head_rows(b_hat[:, gs]), _NT)
        a_k = jnp.concatenate([jnp.where(strict, ak[0:rows], 0.0), jnp.where(incl, ak[rows:], 0.0)], axis=0)
        av.append(dot3(*_split_bf16(a_k), *per_head_rows(v[:, gs]), _NN))
        a_kb.append(jnp.where(strict, ab[0:rows], 0.0))
        a_rb.append(jnp.where(incl, ab[rows:], 0.0))
    a_kb_all = jnp.concatenate(a_kb, axis=1)
    half = rows // 2
    top_cols = (lax.broadcasted_iota(jnp.int32, (half, WKV_LANES), 1) & (HEAD - 1)) < half

    def lower_left(x_top):
        x_pad = jnp.concatenate([x_top, jnp.zeros_like(x_top)], axis=0)
        parts = []
        for g, gs in enumerate(groups):
            lhs = jnp.where(top_cols, a_kb[g][half:], 0.0)
            parts.append(dot3(*_split_bf16(lhs), *per_head_rows(x_pad[:, gs]), _NN))
        return jnp.concatenate(parts, axis=1)

    upd_lhs = [_split_bf16(jnp.concatenate([k_end[:, gs], b_end[:, gs],
                                            jnp.where(t_i == s_i, jnp.exp(cl_last[:, gs]), 0.0)], axis=0))
               for gs in groups]

    def advance(st):
        xs = [dot3(x2h[:, gs], x2l[:, gs], *per_head_rows(st[:, gs]), _NN) for gs in groups]
        base = jnp.concatenate([xs[g][0:rows] + av[g][0:rows] for g in range(len(groups))], axis=1)
        sa = _solve_unit_lower(a_kb_all, base, lower_left)
        o, st_new = [], []
        for g, gs in enumerate(groups):
            o.append(xs[g][rows:] + av[g][rows:] - dot3(*_split_bf16(a_rb[g]), *per_head_rows(sa[:, gs]), _NN))
            rhs = jnp.concatenate([v[:, gs], sa[:, gs], st[:, gs]], axis=0)
            full = dot3(*upd_lhs[g], *_split_bf16(rhs), _TN)
            acc = jnp.where(head_of_lane == 0, full[0:HEAD, :], 0.0)
            for h in range(1, nh):
                acc = acc + jnp.where(head_of_lane == h, full[h * HEAD:(h + 1) * HEAD, :], 0.0)
            st_new.append(acc)
        return jnp.concatenate(o, axis=1), jnp.concatenate(st_new, axis=1)

    return advance


def _wkv_long_body(r_ref, lw_ref, k_ref, v_ref, kk_ref, b_ref, bonus_ref, g_ref, lnw_ref, lnb_ref,
                   y_ref, so_ref, st_s, *, nc):
    c = pl.program_id(2)

    @pl.when(c == 0)
    def _():
        st_s[...] = jnp.zeros_like(st_s)

    chunks = [slice(i * WKV_CHUNK, (i + 1) * WKV_CHUNK) for i in range(r_ref.shape[0] // WKV_CHUNK)]
    advances = [_wkv_chunk(r_ref[cs, :], lw_ref[cs, :], k_ref[cs, :], v_ref[cs, :], kk_ref[cs, :], b_ref[cs, :])
                for cs in chunks]
    st = st_s[...]
    for cs, advance in zip(chunks, advances):
        o, st = advance(st)
        y_ref[cs, :] = _rwkv_out(o, bonus_ref[cs, :], g_ref[cs, :], lnw_ref[...], lnb_ref[...])
    st_s[...] = st

    @pl.when(c == nc - 1)
    def _():
        for j in range(WKV_GROUPS * WKV_LANES // LANES):
            js = slice(j * LANES, (j + 1) * LANES)
            sq = jnp.concatenate([st_s[:, js], jnp.zeros((LANES - HEAD, LANES), F32)], axis=0)
            so_ref[0, js, :] = sq.T[:, 0:HEAD]


def _wkv_long(r, lw, k, v, kk, b, bonus, g, ln_w, ln_b, nseq, seq_len):
    tile_rows = WKV_TILE_CHUNKS * WKV_CHUNK
    nc = seq_len // tile_rows
    width = WKV_GROUPS * WKV_LANES
    tile = pl.BlockSpec((tile_rows, width), lambda s, hg, c: (s * nc + c, hg))
    vec = pl.BlockSpec((1, width), lambda s, hg, c: (0, hg))
    y, s_last = pl.pallas_call(
        functools.partial(_wkv_long_body, nc=nc),
        out_shape=(jax.ShapeDtypeStruct((nseq * seq_len, D), F32),
                   jax.ShapeDtypeStruct((nseq, D, HEAD), F32)),
        grid=(nseq, D // width, nc),
        in_specs=[tile] * 8 + [vec, vec],
        out_specs=(tile, pl.BlockSpec((1, width, HEAD), lambda s, hg, c: (s, hg, 0))),
        scratch_shapes=[pltpu.VMEM((HEAD, width), F32)],
        compiler_params=_cparams(("parallel", "parallel", "arbitrary"), 48),
        name="wkv_long",
    )(r, lw, k, v, kk, b, bonus, g, ln_w, ln_b)
    return y, s_last


def _wkv_short_body(r_ref, w_ref, k_ref, v_ref, kk_ref, b_ref, s0_ref, o_ref, so_ref, *, steps):
    def per_v_group(i, carry):
        vis = [i * WKV_SHORT_ROWS + u for u in range(WKV_SHORT_ROWS)]
        ss = [s0_ref[0, vi] for vi in vis]
        for t in range(steps):
            kk_t, w_t, b_t, k_t, r_t = kk_ref[t, 0], w_ref[t, 0], b_ref[t, 0], k_ref[t, 0], r_ref[t, 0]
            for u, vi in enumerate(vis):
                vrow = v_ref[t, 0, pl.ds(vi, 1), :]
                sa = jnp.sum(ss[u] * kk_t, axis=0, keepdims=True)
                ss[u] = ss[u] * w_t - b_t * sa + k_t * vrow
                o_ref[t, 0, pl.ds(vi, 1), :] = jnp.sum(ss[u] * r_t, axis=0, keepdims=True)
        for u, vi in enumerate(vis):
            so_ref[0, vi] = ss[u]
        return carry

    lax.fori_loop(0, HEAD // WKV_SHORT_ROWS, per_v_group, 0)


def _wkv_short(r, w, k, v, kk, b, s0, nseq, seq_len):
    vec = pl.BlockSpec((seq_len, 1, HEAD, nseq), lambda h: (0, h, 0, 0))
    st = pl.BlockSpec((1, HEAD, HEAD, nseq), lambda h: (h, 0, 0, 0))
    return pl.pallas_call(
        functools.partial(_wkv_short_body, steps=seq_len),
        out_shape=(jax.ShapeDtypeStruct((seq_len, N_HEADS, HEAD, nseq), F32),
                   jax.ShapeDtypeStruct((N_HEADS, HEAD, HEAD, nseq), F32)),
        grid=(N_HEADS,),
        in_specs=[vec] * 6 + [st],
        out_specs=(vec, st),
        compiler_params=_cparams(("parallel",), 32),
        name="wkv_short",
    )(r, w, k, v, kk, b, s0)


def _rwkv_post_body(o_ref, bonus_ref, g_ref, lnw_ref, lnb_ref, y_ref):
    y_ref[...] = _rwkv_out(o_ref[...], bonus_ref[...], g_ref[...], lnw_ref[...], lnb_ref[...])


def _rwkv_post(o, bonus, g, ln_w, ln_b):
    m = o.shape[0]
    tile = pl.BlockSpec((TM_POST, D), lambda i: (i, 0))
    const = pl.BlockSpec((1, D), lambda i: (0, 0))
    return pl.pallas_call(
        _rwkv_post_body,
        out_shape=jax.ShapeDtypeStruct((m, D), F32),
        grid=(m // TM_POST,),
        in_specs=[tile, tile, tile, const, const],
        out_specs=tile,
        compiler_params=_cparams(("parallel",), 32),
        name="rwkv_post",
    )(o, bonus, g, ln_w, ln_b)


def _lru_coeffs(xc, wga_ref, bga_ref, wgx_ref, bgx_ref, lam_ref):
    ra, rx = [], []
    for blk in range(LRU_BLOCKS):
        xh = xc[:, blk * LRU_BLK:(blk + 1) * LRU_BLK].astype(BF16)
        ra.append(jnp.dot(xh, wga_ref[blk], preferred_element_type=F32))
        rx.append(jnp.dot(xh, wgx_ref[blk], preferred_element_type=F32))
    rg = jax.nn.sigmoid(jnp.concatenate(ra, axis=1) + bga_ref[...])
    ig = jax.nn.sigmoid(jnp.concatenate(rx, axis=1) + bgx_ref[...])
    log_a = -LRU_C * rg * _softplus(-lam_ref[...])
    return jnp.exp(log_a), jnp.sqrt(1.0 - jnp.exp(2.0 * log_a)) * (ig * xc)


def _lru_body(gate_ref, x_ref, ci_ref, h0_ref, cw_ref, cb_ref, wga_ref, bga_ref, wgx_ref, bgx_ref, lam_ref,
              y_ref, ho_ref, xbuf, a_s, b_s, h_s, hc_s, *, q, nc):
    c = pl.program_id(1)
    _carried_window(xbuf, x_ref, ci_ref, c == 0, q)

    @pl.when(c == 0)
    def _():
        hc_s[...] = h0_ref[0]

    xc = _causal_conv(xbuf, cw_ref, cb_ref, q)
    _advance_window(xbuf, q)
    a_s[...], b_s[...] = _lru_coeffs(xc, wga_ref, bga_ref, wgx_ref, bgx_ref, lam_ref)

    def step(t, h):
        h = a_s[pl.ds(t, 1), :] * h + b_s[pl.ds(t, 1), :]
        h_s[pl.ds(t, 1), :] = h
        return h

    h = lax.fori_loop(0, q, step, hc_s[...])
    hc_s[...] = h
    y_ref[...] = h_s[...] * jax.nn.gelu(gate_ref[...])

    @pl.when(c == nc - 1)
    def _():
        ho_ref[0] = h


def _lru(p, conv_init, h0, prm, nseq, seq_len):
    q = min(CHUNK, seq_len)
    nc = seq_len // q
    per_seq = lambda shape: pl.BlockSpec((1,) + shape, lambda s, c: (s,) + (0,) * len(shape))
    const = lambda shape: pl.BlockSpec(shape, lambda s, c: (0,) * len(shape))
    return pl.pallas_call(
        functools.partial(_lru_body, q=q, nc=nc),
        out_shape=(jax.ShapeDtypeStruct((nseq * seq_len, D), F32),
                   jax.ShapeDtypeStruct((nseq, 1, D), F32)),
        grid=(nseq, nc),
        in_specs=[
            pl.BlockSpec((q, D), lambda s, c: (s * nc + c, 0)),
            pl.BlockSpec((q, D), lambda s, c: (s * nc + c, 1)),
            per_seq((SUBLANES, D)), per_seq((1, D)),
            const((CONV_W, D)), const((1, D)),
            const((LRU_BLOCKS, LRU_BLK, LRU_BLK)), const((1, D)),
            const((LRU_BLOCKS, LRU_BLK, LRU_BLK)), const((1, D)), const((1, D)),
        ],
        out_specs=(pl.BlockSpec((q, D), lambda s, c: (s * nc + c, 0)), per_seq((1, D))),
        scratch_shapes=[pltpu.VMEM((CHUNK + SUBLANES, D), F32), pltpu.VMEM((q, D), F32), pltpu.VMEM((q, D), F32),
                        pltpu.VMEM((q, D), F32), pltpu.VMEM((1, D), F32)],
        compiler_params=_cparams(("parallel", "arbitrary"), 32),
        name="lru",
    )(p, p, conv_init, h0, *prm)


def _lru_short_body(gate_ref, x_ref, p1_ref, p2_ref, p3_ref, h0_ref, cw_ref, cb_ref, wga_ref, bga_ref, wgx_ref,
                    bgx_ref, lam_ref, y_ref, h_ref, *, seq_len):
    cur = x_ref[...]
    t = lax.broadcasted_iota(jnp.int32, cur.shape, 0) % seq_len
    prev = lambda s, ref: jnp.where(t >= s, pltpu.roll(cur, s, axis=0), ref[...])
    xc = (cb_ref[...] + cw_ref[0:1, :] * prev(3, p3_ref) + cw_ref[1:2, :] * prev(2, p2_ref)
          + cw_ref[2:3, :] * prev(1, p1_ref) + cw_ref[3:4, :] * cur)
    a, b = _lru_coeffs(xc, wga_ref, bga_ref, wgx_ref, bgx_ref, lam_ref)
    s = 1
    while s < seq_len:
        keep = t >= s
        b = jnp.where(keep, a * pltpu.roll(b, s, axis=0) + b, b)
        a = jnp.where(keep, a * pltpu.roll(a, s, axis=0), a)
        s *= 2
    h = a * h0_ref[...] + b
    h_ref[...] = h
    y_ref[...] = h * jax.nn.gelu(gate_ref[...])


def _lru_short(p, conv_state, h0, prm, nseq, seq_len):
    rows = nseq * seq_len
    dense = lambda s: jnp.pad(conv_state[:, CONV_W - 1 - s:, :], ((0, 0), (0, seq_len - s), (0, 0))).reshape(rows, D)
    h0_rows = jnp.repeat(h0, seq_len, axis=0)
    tile = lambda col: pl.BlockSpec((CHUNK, D), lambda i: (i, col))
    const = lambda shape: pl.BlockSpec(shape, lambda i: (0,) * len(shape))
    sds = jax.ShapeDtypeStruct((rows, D), F32)
    return pl.pallas_call(
        functools.partial(_lru_short_body, seq_len=seq_len),
        out_shape=(sds, sds),
        grid=(rows // CHUNK,),
        in_specs=[tile(0), tile(1), tile(0), tile(0), tile(0), tile(0),
                  const((CONV_W, D)), const((1, D)),
                  const((LRU_BLOCKS, LRU_BLK, LRU_BLK)), const((1, D)),
                  const((LRU_BLOCKS, LRU_BLK, LRU_BLK)), const((1, D)), const((1, D))],
        out_specs=(tile(0), tile(0)),
        compiler_params=_cparams(("parallel",), 32),
        name="lru_short",
    )(p, p, dense(1), dense(2), dense(3), h0_rows, *prm)


def _pad_front_rows(buf):
    return jnp.pad(buf, ((0, 0), (SUBLANES - buf.shape[1], 0), (0, 0)))


def _row2(v):
    return v.reshape(1, -1)


def kernel(x_prompt, x_sample, state_ssm_a, state_conv_a, state_wkv_b, state_shift_b, state_lru_c, state_conv_c, norm_gain, w_ffn_in, w_ffn_out, w_in_ab, conv_w_a, conv_b_a, dt_bias_a, a_log_a, d_skip_a, gnorm_a, mu_b, w0_b, w2_b, a0_b, a2_b, g2_b, k_k_b, k_a_b, r_k_b, ln_w_b, ln_b_b, w_out_ab, w_in_c, conv_w_c, conv_b_c, w_gate_a_c, b_gate_a_c, w_gate_x_c, b_gate_x_c, lambda_c, w_out_c, final_norm_gain):
    w_gu, w_dn = _ffn_weights(w_ffn_in, w_ffn_out)

    in_a = D + (D + BC_W) + N_HEADS
    bc_w = BC_W
    segments = (
        (in_a, 3 * D),
        (0, D),
        (D, D),
        (2 * D, BC_W),
        (in_a + 3 * D, LORA_W),
        (in_a - N_HEADS, N_HEADS),
    )
    w_proj0 = _reorder_cast_cols(w_in_ab, segments, PROJ_W)
    w_proj1 = w_in_c[0].astype(BF16)
    w_out0 = w_out_ab[0].astype(BF16)
    w_out1 = w_out_c[0].astype(BF16)

    pad_lanes = lambda v: jnp.pad(v.reshape(1, -1), ((0, 0), (0, LANES - v.shape[-1])))
    rep_head = lambda v: jnp.repeat(v, HEAD).reshape(1, D)
    ssd_prm = (conv_w_a[0][:, :D], conv_w_a[0][:, D:], _row2(conv_b_a[0][:D]), _row2(conv_b_a[0][D:]),
               pad_lanes(dt_bias_a[0]), pad_lanes(a_log_a[0]), rep_head(d_skip_a[0]), _row2(gnorm_a[0]))
    lora_rows = lambda w, lo: jnp.pad(w, ((lo, LORA_W - lo - w.shape[0]), (0, 0))).astype(BF16)
    mu = mu_b[0]
    rwkv_prm = (_row2(mu[:3 * D]), _row2(mu[3 * D:]),
                _row2(w0_b[0]), lora_rows(w2_b[0], 0), _row2(a0_b[0]), lora_rows(a2_b[0], 64),
                lora_rows(g2_b[0], 128), _row2(k_k_b[0]), _row2(k_a_b[0]), _row2(r_k_b[0]))
    lru_prm = (conv_w_c[0], _row2(conv_b_c[0]), w_gate_a_c[0].astype(BF16), _row2(b_gate_a_c[0]),
               w_gate_x_c[0].astype(BF16), _row2(b_gate_x_c[0]), _row2(lambda_c[0]))

    def trunk(x3, ssm0, conva0, wkv0, shift0, lru0, convc0):
        nseq, seq_len, _ = x3.shape
        x = x3.reshape(nseq * seq_len, D)
        tail = lambda arr, n: arr.reshape(nseq, seq_len, arr.shape[-1])[:, seq_len - n:, :]

        x = _ffn(x, _row2(norm_gain[0, 0]), w_gu, w_dn, (0, 0))
        p = _proj(x, _row2(norm_gain[0, 1]), w_proj0)
        if seq_len >= CHUNK:
            ya, ssm_n = _ssd(p, _pad_front_rows(conva0[:, :, :D]), _pad_front_rows(conva0[:, :, D:]),
                             ssm0.reshape(nseq, D, N_STATE), ssd_prm, nseq, seq_len)
        else:
            ya, ssm_n = _ssd_short(p, conva0, ssm0.reshape(nseq, D, N_STATE), ssd_prm, nseq, seq_len)
        p_tail = tail(p, CONV_W - 1)
        conva_n = jnp.concatenate([p_tail[:, :, COL_X:COL_X + D], p_tail[:, :, COL_BC:COL_BC + bc_w]], axis=-1)
        shift_n = jnp.concatenate([p_tail[:, -1:, :3 * D], p_tail[:, -1:, COL_LORA:COL_LORA + LORA_W]], axis=-1)

        r, w, k, v, kk, b, bonus, g = _rwkv_pre(p, shift0, rwkv_prm, nseq, seq_len, log_decay=wkv0 is None)
        if wkv0 is None:
            yb, wkv_n = _wkv_long(r, w, k, v, kk, b, bonus, g, _row2(ln_w_b[0]), _row2(ln_b_b[0]), nseq, seq_len)
            wkv_n = wkv_n.reshape(nseq, N_HEADS, HEAD, HEAD)
        else:
            to_lanes = lambda a: a.reshape(nseq, seq_len, N_HEADS, HEAD).transpose(1, 2, 3, 0)
            o, wkv_n = _wkv_short(*[to_lanes(a) for a in (r, w, k, v, kk, b)], wkv0.transpose(1, 2, 3, 0),
                                  nseq, seq_len)
            o = o.transpose(3, 0, 1, 2).reshape(nseq * seq_len, D)
            wkv_n = wkv_n.transpose(3, 0, 1, 2)
            yb = _rwkv_post(o, bonus, g, _row2(ln_w_b[0]), _row2(ln_b_b[0]))
        x = _out_proj(x, (ya, yb), (w_out0[:D], w_out0[D:]))
        x = _ffn(x, _row2(norm_gain[0, 2]), w_gu, w_dn, (0, 1))

        x = _ffn(x, _row2(norm_gain[1, 0]), w_gu, w_dn, (1, 0))
        pc = _proj(x, _row2(norm_gain[1, 1]), w_proj1)
        if seq_len >= CHUNK:
            yc, lru_n = _lru(pc, _pad_front_rows(convc0), lru0.reshape(nseq, 1, D), lru_prm, nseq, seq_len)
        else:
            yc, h_rows = _lru_short(pc, convc0, lru0, lru_prm, nseq, seq_len)
            lru_n = tail(h_rows, 1)
        convc_n = tail(pc, CONV_W - 1)[:, :, D:]
        x = _out_proj(x, (yc,), (w_out1,))
        y = _ffn(x, _row2(norm_gain[1, 2]), w_gu, w_dn, (1, 1), final_gain=_row2(final_norm_gain))

        return (y.reshape(nseq, seq_len, D), ssm_n.reshape(1, nseq, N_HEADS, HEAD, N_STATE), conva_n[None],
                wkv_n[None], shift_n[None], lru_n.reshape(1, nseq, D), convc_n[None])

    bp = x_prompt.shape[0]
    zeros = lambda s: jnp.zeros((bp,) + s.shape[2:], F32)
    outs_p = trunk(x_prompt, zeros(state_ssm_a), zeros(state_conv_a), None, zeros(state_shift_b),
                   zeros(state_lru_c), zeros(state_conv_c))
    outs_s = trunk(x_sample, state_ssm_a[0], state_conv_a[0], state_wkv_b[0], state_shift_b[0],
                   state_lru_c[0], state_conv_c[0])
    return (outs_p[0], outs_s[0]) + outs_p[1:] + outs_s[1:]
```

```python
import functools

import jax
import jax.numpy as jnp
from jax import lax
from jax.experimental import pallas as pl
from jax.experimental.pallas import tpu as pltpu

F32 = jnp.float32
BF16 = jnp.bfloat16

D = 2048
D_FF = 5504
D_FF_PAD = 5632
HEAD = 64
N_HEADS = 32
N_GROUPS = 4
GROUP_W = D // N_GROUPS
N_STATE = 128
HEADS_PER_GROUP = N_HEADS // N_GROUPS
BC_W = 2 * N_GROUPS * N_STATE
CONV_W = 4
LORA_W = 256
LRU_BLOCKS = 8
LRU_BLK = D // LRU_BLOCKS
LRU_C = 8.0
EPS = 1e-6
GN_EPS = 64e-5
SUBLANES = 8
LANES = 128
CHUNK = 128
WKV_CHUNK = 64
WKV_LANES = 256
WKV_GROUPS = 4
WKV_TILE_CHUNKS = 4
WKV_SHORT_ROWS = 4
SSD_SHORT_SEQS = 8

COL_RKV = 0
COL_Z = 3 * D
COL_X = 4 * D
COL_BC = 5 * D
COL_LORA = 5 * D + 2 * N_GROUPS * N_STATE
COL_DT = COL_LORA + LORA_W
PROJ_W = COL_DT + 256

TM_FFN = 512
TF_FFN = 512
TM_PROJ = 1024
TN_PROJ = 512
TM_OUT = 512
TN_OUT = 1024
TM_POST = 256
W_IN_ROWS = 64
W_OUT_COLS = 256


def _cparams(sem, vmem_mib):
    return pltpu.CompilerParams(dimension_semantics=sem, vmem_limit_bytes=vmem_mib * 1024 * 1024)


def _softplus(x):
    return jnp.maximum(x, 0.0) + jnp.log(1.0 + jnp.exp(-jnp.abs(x)))


def _silu(x):
    return x * jax.nn.sigmoid(x)


def _rms(x, gain):
    ms = jnp.mean(x * x, axis=-1, keepdims=True)
    return x * lax.rsqrt(ms + EPS) * gain


def _ffn_body(x_ref, g_ref, wg_ref, wu_ref, wo_ref, fg_ref, o_ref, xn_ref, acc_ref, *, nf, final):
    f = pl.program_id(1)

    @pl.when(f == 0)
    def _():
        xn_ref[...] = _rms(x_ref[...], g_ref[...]).astype(BF16)
        acc_ref[...] = jnp.zeros_like(acc_ref)

    xn = xn_ref[...]
    gate = jnp.dot(xn, wg_ref[...], preferred_element_type=F32)
    up = jnp.dot(xn, wu_ref[...], preferred_element_type=F32)
    h = (_silu(gate) * up).astype(BF16)
    acc_ref[...] += jnp.dot(h, wo_ref[...], preferred_element_type=F32)

    @pl.when(f == nf - 1)
    def _():
        y = x_ref[...] + 0.5 * acc_ref[...]
        if final:
            y = _rms(y, fg_ref[...])
        o_ref[...] = y


def _ffn(x, gain, w_gu, w_dn, which, final_gain=None):
    m = x.shape[0]
    li, si = which
    nf = D_FF_PAD // TF_FFN
    final = final_gain is not None
    fg = final_gain if final else gain
    return pl.pallas_call(
        functools.partial(_ffn_body, nf=nf, final=final),
        out_shape=jax.ShapeDtypeStruct((m, D), F32),
        grid=(m // TM_FFN, nf),
        in_specs=[
            pl.BlockSpec((TM_FFN, D), lambda i, f: (i, 0)),
            pl.BlockSpec((1, D), lambda i, f: (0, 0)),
            pl.BlockSpec((None, None, D, TF_FFN), lambda i, f: (li, si, 0, f)),
            pl.BlockSpec((None, None, D, TF_FFN), lambda i, f: (li, si, 0, f + nf)),
            pl.BlockSpec((None, None, TF_FFN, D), lambda i, f: (li, si, f, 0)),
            pl.BlockSpec((1, D), lambda i, f: (0, 0)),
        ],
        out_specs=pl.BlockSpec((TM_FFN, D), lambda i, f: (i, 0)),
        scratch_shapes=[pltpu.VMEM((TM_FFN, D), BF16), pltpu.VMEM((TM_FFN, D), F32)],
        compiler_params=_cparams(("parallel", "arbitrary"), 48),
        name="ffn",
    )(x, gain, w_gu, w_gu, w_dn, fg)


def _proj_body(x_ref, g_ref, w_ref, o_ref, xn_ref):
    @pl.when(pl.program_id(1) == 0)
    def _():
        xn_ref[...] = _rms(x_ref[...], g_ref[...]).astype(BF16)

    o_ref[...] = jnp.dot(xn_ref[...], w_ref[...], preferred_element_type=F32)


def _proj(x, gain, w):
    m = x.shape[0]
    n = w.shape[1]
    return pl.pallas_call(
        _proj_body,
        out_shape=jax.ShapeDtypeStruct((m, n), F32),
        grid=(m // TM_PROJ, n // TN_PROJ),
        in_specs=[
            pl.BlockSpec((TM_PROJ, D), lambda i, j: (i, 0)),
            pl.BlockSpec((1, D), lambda i, j: (0, 0)),
            pl.BlockSpec((D, TN_PROJ), lambda i, j: (0, j)),
        ],
        out_specs=pl.BlockSpec((TM_PROJ, TN_PROJ), lambda i, j: (i, j)),
        scratch_shapes=[pltpu.VMEM((TM_PROJ, D), BF16)],
        compiler_params=_cparams(("parallel", "arbitrary"), 40),
        name="proj",
    )(x, gain, w)


def _out2_body(res_ref, ya_ref, yb_ref, wa_ref, wb_ref, o_ref):
    acc = jnp.dot(ya_ref[...].astype(BF16), wa_ref[...], preferred_element_type=F32)
    acc = acc + jnp.dot(yb_ref[...].astype(BF16), wb_ref[...], preferred_element_type=F32)
    o_ref[...] = res_ref[...] + acc


def _out1_body(res_ref, y_ref, w_ref, o_ref):
    o_ref[...] = res_ref[...] + jnp.dot(y_ref[...].astype(BF16), w_ref[...], preferred_element_type=F32)


def _out_proj(res, ys, ws):
    m = res.shape[0]
    body = _out2_body if len(ys) == 2 else _out1_body
    y_spec = pl.BlockSpec((TM_OUT, D), lambda j, i: (i, 0))
    w_spec = pl.BlockSpec((D, TN_OUT), lambda j, i: (0, j))
    r_spec = pl.BlockSpec((TM_OUT, TN_OUT), lambda j, i: (i, j))
    return pl.pallas_call(
        body,
        out_shape=jax.ShapeDtypeStruct((m, D), F32),
        grid=(D // TN_OUT, m // TM_OUT),
        in_specs=[r_spec] + [y_spec] * len(ys) + [w_spec] * len(ws),
        out_specs=r_spec,
        compiler_params=_cparams(("arbitrary", "arbitrary"), 48),
        name="out_proj",
    )(res, *ys, *ws)


def _cast_pad_rows_body(w_ref, o_ref):
    o_ref[0:D_FF, :] = w_ref[...].astype(BF16)
    o_ref[D_FF:, :] = jnp.zeros((D_FF_PAD - D_FF, o_ref.shape[1]), BF16)


def _cast_pad_halves_body(w_ref, o_ref):
    rows = o_ref.shape[0]
    zeros = jnp.zeros((rows, D_FF_PAD - D_FF), BF16)
    for half in range(2):
        o_ref[:, half * D_FF_PAD:half * D_FF_PAD + D_FF] = w_ref[:, half * D_FF:(half + 1) * D_FF].astype(BF16)
        o_ref[:, half * D_FF_PAD + D_FF:(half + 1) * D_FF_PAD] = zeros


def _reorder_cast_body(w_ref, o_ref, *, segments):
    col = 0
    for src, width in segments:
        o_ref[:, col:col + width] = w_ref[:, src:src + width].astype(BF16)
        col += width
    o_ref[:, col:] = jnp.zeros((o_ref.shape[0], o_ref.shape[1] - col), BF16)


def _reorder_cast_cols(w, segments, out_cols):
    _, rows, cols = w.shape
    return pl.pallas_call(
        functools.partial(_reorder_cast_body, segments=segments),
        out_shape=jax.ShapeDtypeStruct((rows, out_cols), BF16),
        grid=(rows // W_IN_ROWS,),
        in_specs=[pl.BlockSpec((None, W_IN_ROWS, cols), lambda r: (0, r, 0))],
        out_specs=pl.BlockSpec((W_IN_ROWS, out_cols), lambda r: (r, 0)),
        compiler_params=_cparams(("parallel",), 32),
        name="reorder_cast",
    )(w)


def _ffn_weights(w_in, w_out):
    n_layers, n_slots = w_in.shape[:2]
    w_gu = pl.pallas_call(
        _cast_pad_halves_body,
        out_shape=jax.ShapeDtypeStruct((n_layers, n_slots, D, 2 * D_FF_PAD), BF16),
        grid=(n_layers * n_slots, D // W_IN_ROWS),
        in_specs=[pl.BlockSpec((None, None, W_IN_ROWS, 2 * D_FF), lambda q, r: (q // n_slots, q % n_slots, r, 0))],
        out_specs=pl.BlockSpec((None, None, W_IN_ROWS, 2 * D_FF_PAD), lambda q, r: (q // n_slots, q % n_slots, r, 0)),
        compiler_params=_cparams(("parallel", "parallel"), 32),
        name="ffn_w_in",
    )(w_in)
    w_dn = pl.pallas_call(
        _cast_pad_rows_body,
        out_shape=jax.ShapeDtypeStruct((n_layers, n_slots, D_FF_PAD, D), BF16),
        grid=(n_layers * n_slots, D // W_OUT_COLS),
        in_specs=[pl.BlockSpec((None, None, D_FF, W_OUT_COLS), lambda q, c: (q // n_slots, q % n_slots, 0, c))],
        out_specs=pl.BlockSpec((None, None, D_FF_PAD, W_OUT_COLS), lambda q, c: (q // n_slots, q % n_slots, 0, c)),
        compiler_params=_cparams(("parallel", "parallel"), 32),
        name="ffn_w_out",
    )(w_out)
    return w_gu, w_dn


def _carried_window(buf, cur_ref, init_ref, first, q):
    @pl.when(first)
    def _():
        buf[0:SUBLANES, :] = init_ref[0]

    buf[SUBLANES:SUBLANES + q, :] = cur_ref[...]


def _advance_window(buf, q):
    tail = buf[q:q + SUBLANES, :]
    buf[0:SUBLANES, :] = tail


def _causal_conv(buf, cw_ref, cb_ref, q):
    acc = cb_ref[...] + cw_ref[0:1, :] * buf[pl.ds(SUBLANES - CONV_W + 1, q), :]
    for k in range(1, CONV_W):
        acc = acc + cw_ref[k:k + 1, :] * buf[pl.ds(SUBLANES - CONV_W + 1 + k, q), :]
    return acc


def _cumsum_rows(x):
    n = x.shape[0]
    row = lax.broadcasted_iota(jnp.int32, x.shape, 0)
    s = 1
    while s < n:
        x = x + jnp.where(row >= s, pltpu.roll(x, s, axis=0), 0.0)
        s *= 2
    return x


def _expand_heads(a, rows):
    lane = lax.broadcasted_iota(jnp.int32, (rows, LANES), 1)
    low = lane < HEAD
    pieces = []
    for j in range(N_HEADS // 2):
        e0 = jnp.broadcast_to(a[:, 2 * j:2 * j + 1], (rows, LANES))
        e1 = jnp.broadcast_to(a[:, 2 * j + 1:2 * j + 2], (rows, LANES))
        pieces.append(jnp.where(low, e0, e1))
    return jnp.concatenate(pieces, axis=1)


def _head_allsum(x):
    width = x.shape[-1]
    blk = 4 * HEAD
    r = lax.broadcasted_iota(jnp.int32, (blk, blk), 0) // HEAD
    c = lax.broadcasted_iota(jnp.int32, (blk, blk), 1) // HEAD
    ones_bd = jnp.where(r == c, 1.0, 0.0).astype(BF16)
    hi = x.astype(BF16)
    rem = x - hi.astype(F32)
    mid = rem.astype(BF16)
    lo = (rem - mid.astype(F32)).astype(BF16)
    dot = lambda p: jnp.dot(p, ones_bd, preferred_element_type=F32)
    cols = []
    for j in range(width // blk):
        sl = slice(j * blk, (j + 1) * blk)
        cols.append(dot(hi[:, sl]) + (dot(mid[:, sl]) + dot(lo[:, sl])))
    return jnp.concatenate(cols, axis=1)


def _ssd_body(z_ref, x_ref, bc_ref, dt_ref, cix_ref, cibc_ref, h0_ref,
              cwx_ref, cwbc_ref, cbx_ref, cbbc_ref, dtb_ref, alog_ref, dsk_ref, gn_ref,
              y_ref, ho_ref, xbuf, bcbuf, h_s, *, nc):
    c = pl.program_id(1)
    q = CHUNK
    _carried_window(xbuf, x_ref, cix_ref, c == 0, q)
    _carried_window(bcbuf, bc_ref, cibc_ref, c == 0, q)

    @pl.when(c == 0)
    def _():
        h_s[...] = h0_ref[0]

    xs = _silu(_causal_conv(xbuf, cwx_ref, cbx_ref, CHUNK))
    bcv = _silu(_causal_conv(bcbuf, cwbc_ref, cbbc_ref, CHUNK))
    _advance_window(xbuf, q)
    _advance_window(bcbuf, q)
    z = z_ref[...]
    dt = _softplus(dt_ref[...] + dtb_ref[...])
    a_head = -jnp.exp(alog_ref[...])
    cum = _cumsum_rows(dt * a_head)
    cum_last = cum[CHUNK - 1:CHUNK, :]
    cum_t = cum.T
    dtx = _expand_heads(dt, CHUNK)
    ecx = _expand_heads(jnp.exp(cum), CHUNK)
    tlx = _expand_heads(jnp.exp(cum_last - cum), CHUNK)
    xdt = xs * dtx
    xdtw = (xdt * tlx).astype(BF16)

    ti = lax.broadcasted_iota(jnp.int32, (CHUNK, CHUNK), 0)
    tj = lax.broadcasted_iota(jnp.int32, (CHUNK, CHUNK), 1)
    causal = ti >= tj
    low = tj < HEAD
    nt = (((1,), (1,)), ((), ()))
    tn = (((0,), (0,)), ((), ()))

    for g in range(N_GROUPS):
        gsl = slice(g * GROUP_W, (g + 1) * GROUP_W)
        bg = bcv[:, g * N_STATE:(g + 1) * N_STATE].astype(BF16)
        cg = bcv[:, (N_GROUPS + g) * N_STATE:(N_GROUPS + g + 1) * N_STATE].astype(BF16)
        cb = lax.dot_general(cg, bg, nt, preferred_element_type=F32)
        hg = h_s[gsl, :]
        y_off = lax.dot_general(cg, hg.astype(BF16), nt, preferred_element_type=F32) * ecx[:, gsl]
        st = lax.dot_general(xdtw[:, gsl], bg, tn, preferred_element_type=F32)
        y_pairs = []
        for j in range(GROUP_W // LANES):
            h0 = g * (GROUP_W // HEAD) + 2 * j
            ms = []
            for h in (h0, h0 + 1):
                seg = cum[:, h:h + 1] - cum_t[h:h + 1, :]
                decay = jnp.exp(jnp.where(causal, seg, -jnp.inf))
                ms.append((cb * decay).astype(BF16))
            psl = slice(h0 * HEAD, (h0 + 2) * HEAD)
            slab = xdt[:, psl]
            rhs = jnp.concatenate([jnp.where(low, slab, 0.0), jnp.where(low, 0.0, slab)], axis=0).astype(BF16)
            y_pairs.append(jnp.dot(jnp.concatenate(ms, axis=1), rhs, preferred_element_type=F32))
            for h in (h0, h0 + 1):
                hsl = slice(h * HEAD, (h + 1) * HEAD)
                dec = jnp.exp(jnp.broadcast_to(cum_last[:, h:h + 1], (HEAD, N_STATE)))
                hg = slice((h % HEADS_PER_GROUP) * HEAD, (h % HEADS_PER_GROUP + 1) * HEAD)
                h_s[hsl, :] = h_s[hsl, :] * dec + st[hg, :]
        y = jnp.concatenate(y_pairs, axis=1) + y_off + xs[:, gsl] * dsk_ref[:, gsl]
        y = y * _silu(z[:, gsl])
        y_ref[:, gsl] = y * lax.rsqrt(jnp.mean(y * y, axis=-1, keepdims=True) + EPS) * gn_ref[:, gsl]

    @pl.when(c == nc - 1)
    def _():
        ho_ref[0] = h_s[...]


def _ssd(p, conv_x, conv_bc, h0, prm, nseq, seq_len):
    q = CHUNK
    nc = seq_len // q
    rows = lambda w, col: pl.BlockSpec((q, w), lambda s, c: (s * nc + c, col))
    per_seq = lambda shape: pl.BlockSpec((1,) + shape, lambda s, c: (s,) + (0,) * len(shape))
    const = lambda shape: pl.BlockSpec(shape, lambda s, c: (0,) * len(shape))
    y, h_last = pl.pallas_call(
        functools.partial(_ssd_body, nc=nc),
        out_shape=(jax.ShapeDtypeStruct((nseq * seq_len, D), F32),
                   jax.ShapeDtypeStruct((nseq, D, N_STATE), F32)),
        grid=(nseq, nc),
        in_specs=[
            rows(D, COL_Z // D), rows(D, COL_X // D), rows(BC_W, COL_BC // BC_W),
            rows(LANES, COL_DT // LANES),
            per_seq((SUBLANES, D)), per_seq((SUBLANES, BC_W)), per_seq((D, N_STATE)),
            const((CONV_W, D)), const((CONV_W, BC_W)), const((1, D)), const((1, BC_W)),
            const((1, LANES)), const((1, LANES)), const((1, D)), const((1, D)),
        ],
        out_specs=(pl.BlockSpec((q, D), lambda s, c: (s * nc + c, 0)), per_seq((D, N_STATE))),
        scratch_shapes=[
            pltpu.VMEM((CHUNK + SUBLANES, D), F32), pltpu.VMEM((CHUNK + SUBLANES, BC_W), F32),
            pltpu.VMEM((D, N_STATE), F32),
        ],
        compiler_params=_cparams(("parallel", "arbitrary"), 40),
        name="ssd",
    )(p, p, p, p, conv_x, conv_bc, h0, *prm)
    return y, h_last


def _ssd_short_body(z_ref, x_ref, bc_ref, dt_ref, px1_ref, px2_ref, px3_ref, pb1_ref, pb2_ref, pb3_ref, h0_ref,
                    cwx_ref, cwbc_ref, cbx_ref, cbbc_ref, dtb_ref, alog_ref, dsk_ref, gn_ref,
                    y_ref, ho_ref, *, seq_len):
    rows = x_ref.shape[0]
    nseq_t = rows // seq_len
    t = lax.broadcasted_iota(jnp.int32, (rows, LANES), 0) % seq_len

    def conv(cur_ref, p1_ref, p2_ref, p3_ref, cw_ref, cb_ref):
        cur = cur_ref[...]
        tt = lax.broadcasted_iota(jnp.int32, cur.shape, 0) % seq_len
        prev = lambda s, ref: jnp.where(tt >= s, pltpu.roll(cur, s, axis=0), ref[...])
        return (cb_ref[...] + cw_ref[0:1, :] * prev(3, p3_ref) + cw_ref[1:2, :] * prev(2, p2_ref)
                + cw_ref[2:3, :] * prev(1, p1_ref) + cw_ref[3:4, :] * cur)

    xs = _silu(conv(x_ref, px1_ref, px2_ref, px3_ref, cwx_ref, cbx_ref))
    bcv = _silu(conv(bc_ref, pb1_ref, pb2_ref, pb3_ref, cwbc_ref, cbbc_ref))
    z = z_ref[...]
    dt = _softplus(dt_ref[...] + dtb_ref[...])
    cum = dt * -jnp.exp(alog_ref[...])
    s = 1
    while s < seq_len:
        cum = cum + jnp.where(t >= s, pltpu.roll(cum, s, axis=0), 0.0)
        s *= 2
    tot = cum
    s = 1
    while s < seq_len:
        tot = jnp.where(t + s < seq_len, pltpu.roll(tot, rows - s, axis=0), tot)
        s *= 2
    cum_sq = cum if rows == LANES else jnp.concatenate([cum, jnp.zeros((LANES - rows, LANES), F32)], axis=0)
    cum_t = cum_sq.T[:, 0:rows]
    dtx = _expand_heads(dt, rows)
    ecx = _expand_heads(jnp.exp(cum), rows)
    tlx = _expand_heads(jnp.exp(tot - cum), rows)
    xdt = xs * dtx
    xdtw = (xdt * tlx).astype(BF16)

    ti = lax.broadcasted_iota(jnp.int32, (rows, rows), 0)
    tj = lax.broadcasted_iota(jnp.int32, (rows, rows), 1)
    same_causal = (ti >= tj) & (ti // seq_len == tj // seq_len)
    low = lax.broadcasted_iota(jnp.int32, (rows, LANES), 1) < HEAD
    row_seq = lax.broadcasted_iota(jnp.int32, (rows, N_STATE), 0) // seq_len

    def per_seq_cols(m):
        return jnp.concatenate([jnp.where(row_seq == q, m, 0.0) for q in range(nseq_t)], axis=1).astype(BF16)

    for g in range(N_GROUPS):
        gsl = slice(g * GROUP_W, (g + 1) * GROUP_W)
        bg_f = bcv[:, g * N_STATE:(g + 1) * N_STATE]
        cg_f = bcv[:, (N_GROUPS + g) * N_STATE:(N_GROUPS + g + 1) * N_STATE]
        cb = lax.dot_general(cg_f.astype(BF16), bg_f.astype(BF16), _NT, preferred_element_type=F32)
        h_cat = jnp.concatenate([h0_ref[q, gsl, :] for q in range(nseq_t)], axis=1).astype(BF16)
        y_off = lax.dot_general(per_seq_cols(cg_f), h_cat, _NT, preferred_element_type=F32) * ecx[:, gsl]
        st = lax.dot_general(xdtw[:, gsl], per_seq_cols(bg_f), _TN, preferred_element_type=F32)
        y_pairs = []
        for j in range(GROUP_W // LANES):
            h_lo = g * (GROUP_W // HEAD) + 2 * j
            ms = []
            for h in (h_lo, h_lo + 1):
                seg = cum[:, h:h + 1] - cum_t[h:h + 1, :]
                decay = jnp.exp(jnp.where(same_causal, seg, -jnp.inf))
                ms.append((cb * decay).astype(BF16))
            slab = xdt[:, h_lo * HEAD:(h_lo + 2) * HEAD]
            rhs = jnp.concatenate([jnp.where(low, slab, 0.0), jnp.where(low, 0.0, slab)], axis=0).astype(BF16)
            y_pairs.append(jnp.dot(jnp.concatenate(ms, axis=1), rhs, preferred_element_type=F32))
            for h in (h_lo, h_lo + 1):
                hsl = slice(h * HEAD, (h + 1) * HEAD)
                hg = slice((h % HEADS_PER_GROUP) * HEAD, (h % HEADS_PER_GROUP + 1) * HEAD)
                for q in range(nseq_t):
                    dec = jnp.exp(jnp.broadcast_to(tot[q * seq_len:q * seq_len + 1, h:h + 1], (HEAD, N_STATE)))
                    ho_ref[q, hsl, :] = h0_ref[q, hsl, :] * dec + st[hg, q * N_STATE:(q + 1) * N_STATE]
        y = jnp.concatenate(y_pairs, axis=1) + y_off + xs[:, gsl] * dsk_ref[:, gsl]
        y = y * _silu(z[:, gsl])
        y_ref[:, gsl] = y * lax.rsqrt(jnp.mean(y * y, axis=-1, keepdims=True) + EPS) * gn_ref[:, gsl]


def _ssd_short(p, conv_state, h0, prm, nseq, seq_len):
    rows = nseq * seq_len
    tr = SSD_SHORT_SEQS * seq_len
    bc_w = BC_W

    def dense(s, lo, hi):
        part = conv_state[:, CONV_W - 1 - s:, lo:hi]
        return jnp.pad(part, ((0, 0), (0, seq_len - s), (0, 0))).reshape(rows, hi - lo)

    tile = lambda w, col: pl.BlockSpec((tr, w), lambda i: (i, col))
    const = lambda shape: pl.BlockSpec(shape, lambda i: (0,) * len(shape))
    state = pl.BlockSpec((SSD_SHORT_SEQS, D, N_STATE), lambda i: (i, 0, 0))
    return pl.pallas_call(
        functools.partial(_ssd_short_body, seq_len=seq_len),
        out_shape=(jax.ShapeDtypeStruct((rows, D), F32), jax.ShapeDtypeStruct((nseq, D, N_STATE), F32)),
        grid=(nseq // SSD_SHORT_SEQS,),
        in_specs=[
            tile(D, COL_Z // D), tile(D, COL_X // D), tile(bc_w, COL_BC // bc_w), tile(LANES, COL_DT // LANES),
            tile(D, 0), tile(D, 0), tile(D, 0), tile(bc_w, 0), tile(bc_w, 0), tile(bc_w, 0), state,
            const((CONV_W, D)), const((CONV_W, bc_w)), const((1, D)), const((1, bc_w)),
            const((1, LANES)), const((1, LANES)), const((1, D)), const((1, D)),
        ],
        out_specs=(tile(D, 0), state),
        compiler_params=_cparams(("parallel",), 48),
        name="ssd_short",
    )(p, p, p, p, dense(1, 0, D), dense(2, 0, D), dense(3, 0, D), dense(1, D, D + bc_w), dense(2, D, D + bc_w),
      dense(3, D, D + bc_w), h0, *prm)


def _rwkv_pre_body(rkv_ref, lora_ref, si_rkv_ref, si_lora_ref, mu_rkv_ref, mu_lora_ref,
                   w0_ref, w2_ref, a0_ref, a2_ref, g2_ref, kk_ref, ka_ref, rk_ref,
                   r_out, w_out, k_out, v_out, kk_out, b_out, bonus_out, g_out,
                   rkvbuf, lorabuf, *, q, log_decay, short_len):
    if short_len:
        def shifted(buf, cur_ref, si_ref, mu_ref):
            cur = cur_ref[...]
            first = lax.broadcasted_iota(jnp.int32, cur.shape, 0) % short_len == 0
            prev = jnp.where(first, si_ref[...], pltpu.roll(cur, 1, axis=0))
            return cur + mu_ref[...] * (prev - cur)
    else:
        c = pl.program_id(1)
        _carried_window(rkvbuf, rkv_ref, si_rkv_ref, c == 0, q)
        _carried_window(lorabuf, lora_ref, si_lora_ref, c == 0, q)

        def shifted(buf, cur_ref, si_ref, mu_ref):
            cur = cur_ref[...]
            prev = buf[pl.ds(SUBLANES - 1, q), :]
            return cur + mu_ref[...] * (prev - cur)

    ps = shifted(rkvbuf, rkv_ref, si_rkv_ref, mu_rkv_ref)
    lo_in = shifted(lorabuf, lora_ref, si_lora_ref, mu_lora_ref)
    if not short_len:
        _advance_window(rkvbuf, q)
        _advance_window(lorabuf, q)
    r = ps[:, 0:D]
    k = ps[:, D:2 * D]
    v = ps[:, 2 * D:3 * D]

    lw = jnp.dot(jnp.tanh(lo_in).astype(BF16), w2_ref[...], preferred_element_type=F32)
    la = jnp.dot(lo_in.astype(BF16), a2_ref[...], preferred_element_type=F32)
    g = jnp.dot(jax.nn.sigmoid(lo_in).astype(BF16), g2_ref[...], preferred_element_type=F32)
    wlog = -_softplus(-(w0_ref[...] + lw)) - 0.5
    log_w = -jnp.exp(wlog)
    a = jax.nn.sigmoid(a0_ref[...] + la)
    kkf = k * kk_ref[...]
    norm = jnp.maximum(jnp.sqrt(_head_allsum(kkf * kkf)), 1e-12)
    kk = kkf / norm
    k2 = k * (1.0 + (a - 1.0) * ka_ref[...])
    bonus = _head_allsum(r * k2 * rk_ref[...]) * v
    r_out[...] = r
    w_out[...] = log_w if log_decay else jnp.exp(log_w)
    k_out[...] = k2
    v_out[...] = v
    kk_out[...] = kk
    b_out[...] = kk * a
    bonus_out[...] = bonus
    g_out[...] = g


def _rwkv_pre(p, shift_state, prm, nseq, seq_len, log_decay):
    short_len = seq_len if seq_len < CHUNK else 0
    q = CHUNK if short_len else min(CHUNK, seq_len)
    nc = 1 if short_len else seq_len // q
    ntile = nseq * seq_len // q // nc
    const = lambda shape: pl.BlockSpec(shape, lambda s, c: (0,) * len(shape))
    tile = pl.BlockSpec((q, D), lambda s, c: (s * nc + c, 0))
    if short_len:
        dense = lambda a: jnp.pad(a, ((0, 0), (0, seq_len - 1), (0, 0))).reshape(nseq * seq_len, a.shape[-1])
        si = (dense(shift_state[:, :, :3 * D]), dense(shift_state[:, :, 3 * D:]))
        si_specs = [pl.BlockSpec((q, 3 * D), lambda s, c: (s, 0)), pl.BlockSpec((q, LORA_W), lambda s, c: (s, 0))]
    else:
        si = (_pad_front_rows(shift_state[:, :, :3 * D]), _pad_front_rows(shift_state[:, :, 3 * D:]))
        si_specs = [pl.BlockSpec((1, SUBLANES, 3 * D), lambda s, c: (s, 0, 0)),
                    pl.BlockSpec((1, SUBLANES, LORA_W), lambda s, c: (s, 0, 0))]
    sds = jax.ShapeDtypeStruct((nseq * seq_len, D), F32)
    return pl.pallas_call(
        functools.partial(_rwkv_pre_body, q=q, log_decay=log_decay, short_len=short_len),
        out_shape=(sds,) * 8,
        grid=(ntile, nc),
        in_specs=[
            pl.BlockSpec((q, 3 * D), lambda s, c: (s * nc + c, 0)),
            pl.BlockSpec((q, LORA_W), lambda s, c: (s * nc + c, COL_LORA // LORA_W)),
            *si_specs,
            const((1, 3 * D)), const((1, LORA_W)),
            const((1, D)), const((LORA_W, D)), const((1, D)), const((LORA_W, D)), const((LORA_W, D)),
            const((1, D)), const((1, D)), const((1, D)),
        ],
        out_specs=(tile,) * 8,
        scratch_shapes=[pltpu.VMEM((CHUNK + SUBLANES, 3 * D), F32), pltpu.VMEM((CHUNK + SUBLANES, LORA_W), F32)],
        compiler_params=_cparams(("parallel", "arbitrary"), 48),
        name="rwkv_pre",
    )(p, p, *si, *prm)


def _rwkv_out(o, bonus, g, ln_w, ln_b):
    mean = _head_allsum(o) * (1.0 / HEAD)
    cen = o - mean
    var = _head_allsum(cen * cen) * (1.0 / HEAD)
    return (cen * lax.rsqrt(var + GN_EPS) * ln_w + ln_b + bonus) * g


def _split_bf16(x):
    hi = x.astype(BF16)
    return hi, (x - hi.astype(F32)).astype(BF16)


_NN = (((1,), (0,)), ((), ()))
_NT = (((1,), (1,)), ((), ()))
_TN = (((0,), (0,)), ((), ()))


def _solve_unit_lower(n, rhs, lower_left):
    rows, width = rhs.shape
    half = rows // 2
    nblk, ncol = rows // SUBLANES, width // LANES
    tiles = lambda a, r0, r1: [[a[SUBLANES * i:SUBLANES * (i + 1), LANES * j:LANES * (j + 1)] for j in range(ncol)]
                               for i in range(r0 // SUBLANES, r1 // SUBLANES)]
    nb = tiles(n, 0, rows)
    xb = tiles(rhs, 0, rows)
    low = lax.broadcasted_iota(jnp.int32, (SUBLANES, LANES), 1) < HEAD

    def substitute(lo, hi):
        for s in range(lo, hi - 1):
            i0, r0 = divmod(s, SUBLANES)
            idx = jnp.where(low, s, HEAD + s)
            for j in range(ncol):
                row = xb[i0][j][r0:r0 + 1, :]
                for i in range(i0 if r0 < SUBLANES - 1 else i0 + 1, hi // SUBLANES):
                    xb[i][j] = xb[i][j] - jnp.take_along_axis(nb[i][j], idx, axis=1) * row

    join = lambda blocks: jnp.concatenate([jnp.concatenate(xr, axis=1) for xr in blocks], axis=0)
    substitute(0, half)
    corr = tiles(lower_left(join(xb[:half // SUBLANES])), 0, half)
    for i in range(half // SUBLANES):
        for j in range(ncol):
            xb[half // SUBLANES + i][j] = xb[half // SUBLANES + i][j] - corr[i][j]
    substitute(half, rows)
    return join(xb)


def _wkv_chunk(r, lw, k, v, kk, b):
    rows, width = r.shape
    nh = WKV_LANES // HEAD
    groups = [slice(g * WKV_LANES, (g + 1) * WKV_LANES) for g in range(width // WKV_LANES)]
    cl = _cumsum_rows(lw)
    cl_last = cl[rows - 1:rows, :]
    p_inv = jnp.exp(-cl)
    p_end = jnp.exp(cl_last - cl)
    x2h, x2l = _split_bf16(jnp.concatenate([kk * jnp.exp(cl - lw), r * jnp.exp(cl)], axis=0))
    k_hat = k * p_inv
    b_hat = b * p_inv
    k_end = k * p_end
    b_end = -(b * p_end)

    bd_r = lax.broadcasted_iota(jnp.int32, (nh * rows, WKV_LANES), 0) // rows
    bd_c = lax.broadcasted_iota(jnp.int32, (nh * rows, WKV_LANES), 1) // HEAD
    bd_mask = jnp.where(bd_r == bd_c, 1.0, 0.0).astype(BF16)

    def per_head_rows(y):
        return [jnp.concatenate([part] * nh, axis=0) * bd_mask for part in _split_bf16(y)]

    def dot3(ah, al, bh, bl, dims):
        dg = lambda x, y: lax.dot_general(x, y, dims, preferred_element_type=F32)
        return dg(ah, bh) + (dg(ah, bl) + dg(al, bh))

    t_i = lax.broadcasted_iota(jnp.int32, (rows, WKV_LANES), 0)
    lane = lax.broadcasted_iota(jnp.int32, (rows, WKV_LANES), 1)
    s_i = lane & (HEAD - 1)
    strict = t_i > s_i
    incl = t_i >= s_i
    head_of_lane = lane // HEAD

    a_kb, a_rb, av = [], [], []
    for gs in groups:
        ak = dot3(x2h[:, gs], x2l[:, gs], *per_head_rows(k_hat[:, gs]), _NT)
        ab = dot3(x2h[:, gs], x2l[:, gs], *per_head_rows(b_hat[:, gs]), _NT)
        a_k = jnp.concatenate([jnp.where(strict, ak[0:rows], 0.0), jnp.where(incl, ak[rows:], 0.0)], axis=0)
        av.append(dot3(*_split_bf16(a_k), *per_head_rows(v[:, gs]), _NN))
        a_kb.append(jnp.where(strict, ab[0:rows], 0.0))
        a_rb.append(jnp.where(incl, ab[rows:], 0.0))
    a_kb_all = jnp.concatenate(a_kb, axis=1)
    half = rows // 2
    top_cols = (lax.broadcasted_iota(jnp.int32, (half, WKV_LANES), 1) & (HEAD - 1)) < half

    def lower_left(x_top):
        x_pad = jnp.concatenate([x_top, jnp.zeros_like(x_top)], axis=0)
        parts = []
        for g, gs in enumerate(groups):
            lhs = jnp.where(top_cols, a_kb[g][half:], 0.0)
            parts.append(dot3(*_split_bf16(lhs), *per_head_rows(x_pad[:, gs]), _NN))
        return jnp.concatenate(parts, axis=1)

    upd_lhs = [_split_bf16(jnp.concatenate([k_end[:, gs], b_end[:, gs],
                                            jnp.where(t_i == s_i, jnp.exp(cl_last[:, gs]), 0.0)], axis=0))
               for gs in groups]

    def advance(st):
        xs = [dot3(x2h[:, gs], x2l[:, gs], *per_head_rows(st[:, gs]), _NN) for gs in groups]
        base = jnp.concatenate([xs[g][0:rows] + av[g][0:rows] for g in range(len(groups))], axis=1)
        sa = _solve_unit_lower(a_kb_all, base, lower_left)
        o, st_new = [], []
        for g, gs in enumerate(groups):
            o.append(xs[g][rows:] + av[g][rows:] - dot3(*_split_bf16(a_rb[g]), *per_head_rows(sa[:, gs]), _NN))
            rhs = jnp.concatenate([v[:, gs], sa[:, gs], st[:, gs]], axis=0)
            full = dot3(*upd_lhs[g], *_split_bf16(rhs), _TN)
            acc = jnp.where(head_of_lane == 0, full[0:HEAD, :], 0.0)
            for h in range(1, nh):
                acc = acc + jnp.where(head_of_lane == h, full[h * HEAD:(h + 1) * HEAD, :], 0.0)
            st_new.append(acc)
        return jnp.concatenate(o, axis=1), jnp.concatenate(st_new, axis=1)

    return advance


def _wkv_long_body(r_ref, lw_ref, k_ref, v_ref, kk_ref, b_ref, bonus_ref, g_ref, lnw_ref, lnb_ref,
                   y_ref, so_ref, st_s, *, nc):
    c = pl.program_id(2)

    @pl.when(c == 0)
    def _():
        st_s[...] = jnp.zeros_like(st_s)

    chunks = [slice(i * WKV_CHUNK, (i + 1) * WKV_CHUNK) for i in range(r_ref.shape[0] // WKV_CHUNK)]
    advances = [_wkv_chunk(r_ref[cs, :], lw_ref[cs, :], k_ref[cs, :], v_ref[cs, :], kk_ref[cs, :], b_ref[cs, :])
                for cs in chunks]
    st = st_s[...]
    for cs, advance in zip(chunks, advances):
        o, st = advance(st)
        y_ref[cs, :] = _rwkv_out(o, bonus_ref[cs, :], g_ref[cs, :], lnw_ref[...], lnb_ref[...])
    st_s[...] = st

    @pl.when(c == nc - 1)
    def _():
        for j in range(WKV_GROUPS * WKV_LANES // LANES):
            js = slice(j * LANES, (j + 1) * LANES)
            sq = jnp.concatenate([st_s[:, js], jnp.zeros((LANES - HEAD, LANES), F32)], axis=0)
            so_ref[0, js, :] = sq.T[:, 0:HEAD]


def _wkv_long(r, lw, k, v, kk, b, bonus, g, ln_w, ln_b, nseq, seq_len):
    tile_rows = WKV_TILE_CHUNKS * WKV_CHUNK
    nc = seq_len // tile_rows
    width = WKV_GROUPS * WKV_LANES
    tile = pl.BlockSpec((tile_rows, width), lambda s, hg, c: (s * nc + c, hg))
    vec = pl.BlockSpec((1, width), lambda s, hg, c: (0, hg))
    y, s_last = pl.pallas_call(
        functools.partial(_wkv_long_body, nc=nc),
        out_shape=(jax.ShapeDtypeStruct((nseq * seq_len, D), F32),
                   jax.ShapeDtypeStruct((nseq, D, HEAD), F32)),
        grid=(nseq, D // width, nc),
        in_specs=[tile] * 8 + [vec, vec],
        out_specs=(tile, pl.BlockSpec((1, width, HEAD), lambda s, hg, c: (s, hg, 0))),
        scratch_shapes=[pltpu.VMEM((HEAD, width), F32)],
        compiler_params=_cparams(("parallel", "parallel", "arbitrary"), 48),
        name="wkv_long",
    )(r, lw, k, v, kk, b, bonus, g, ln_w, ln_b)
    return y, s_last


def _wkv_short_body(r_ref, w_ref, k_ref, v_ref, kk_ref, b_ref, s0_ref, o_ref, so_ref, *, steps):
    def per_v_group(i, carry):
        vis = [i * WKV_SHORT_ROWS + u for u in range(WKV_SHORT_ROWS)]
        ss = [s0_ref[0, vi] for vi in vis]
        for t in range(steps):
            kk_t, w_t, b_t, k_t, r_t = kk_ref[t, 0], w_ref[t, 0], b_ref[t, 0], k_ref[t, 0], r_ref[t, 0]
            for u, vi in enumerate(vis):
                vrow = v_ref[t, 0, pl.ds(vi, 1), :]
                sa = jnp.sum(ss[u] * kk_t, axis=0, keepdims=True)
                ss[u] = ss[u] * w_t - b_t * sa + k_t * vrow
                o_ref[t, 0, pl.ds(vi, 1), :] = jnp.sum(ss[u] * r_t, axis=0, keepdims=True)
        for u, vi in enumerate(vis):
            so_ref[0, vi] = ss[u]
        return carry

    lax.fori_loop(0, HEAD // WKV_SHORT_ROWS, per_v_group, 0)


def _to_lanes_body(*refs, n_in, seq_len):
    nseq = refs[0].shape[0] // seq_len
    for src, dst in zip(refs[:n_in], refs[n_in:]):
        for t in range(seq_len):
            xt = src[pl.ds(t, nseq, stride=seq_len), :].T
            dst[t, 0] = xt[0:HEAD, :]
            dst[t, 1] = xt[HEAD:2 * HEAD, :]


def _to_lanes(arrays, nseq, seq_len):
    n = len(arrays)
    return pl.pallas_call(
        functools.partial(_to_lanes_body, n_in=n, seq_len=seq_len),
        out_shape=(jax.ShapeDtypeStruct((seq_len, N_HEADS, HEAD, nseq), F32),) * n,
        grid=(D // LANES,),
        in_specs=[pl.BlockSpec((nseq * seq_len, LANES), lambda hp: (0, hp))] * n,
        out_specs=(pl.BlockSpec((seq_len, 2, HEAD, nseq), lambda hp: (0, hp, 0, 0)),) * n,
        compiler_params=_cparams(("parallel",), 32),
        name="to_lanes",
    )(*arrays)


def _wkv_short(r, w, k, v, kk, b, s0, nseq, seq_len):
    vec = pl.BlockSpec((seq_len, 1, HEAD, nseq), lambda h: (0, h, 0, 0))
    st = pl.BlockSpec((1, HEAD, HEAD, nseq), lambda h: (h, 0, 0, 0))
    return pl.pallas_call(
        functools.partial(_wkv_short_body, steps=seq_len),
        out_shape=(jax.ShapeDtypeStruct((seq_len, N_HEADS, HEAD, nseq), F32),
                   jax.ShapeDtypeStruct((N_HEADS, HEAD, HEAD, nseq), F32)),
        grid=(N_HEADS,),
        in_specs=[vec] * 6 + [st],
        out_specs=(vec, st),
        compiler_params=_cparams(("parallel",), 32),
        name="wkv_short",
    )(r, w, k, v, kk, b, s0)


def _rwkv_post_body(o_ref, bonus_ref, g_ref, lnw_ref, lnb_ref, y_ref):
    y_ref[...] = _rwkv_out(o_ref[...], bonus_ref[...], g_ref[...], lnw_ref[...], lnb_ref[...])


def _rwkv_post(o, bonus, g, ln_w, ln_b):
    m = o.shape[0]
    tile = pl.BlockSpec((TM_POST, D), lambda i: (i, 0))
    const = pl.BlockSpec((1, D), lambda i: (0, 0))
    return pl.pallas_call(
        _rwkv_post_body,
        out_shape=jax.ShapeDtypeStruct((m, D), F32),
        grid=(m // TM_POST,),
        in_specs=[tile, tile, tile, const, const],
        out_specs=tile,
        compiler_params=_cparams(("parallel",), 32),
        name="rwkv_post",
    )(o, bonus, g, ln_w, ln_b)


def _lru_coeffs(xc, wga_ref, bga_ref, wgx_ref, bgx_ref, lam_ref):
    ra, rx = [], []
    for blk in range(LRU_BLOCKS):
        xh = xc[:, blk * LRU_BLK:(blk + 1) * LRU_BLK].astype(BF16)
        ra.append(jnp.dot(xh, wga_ref[blk], preferred_element_type=F32))
        rx.append(jnp.dot(xh, wgx_ref[blk], preferred_element_type=F32))
    rg = jax.nn.sigmoid(jnp.concatenate(ra, axis=1) + bga_ref[...])
    ig = jax.nn.sigmoid(jnp.concatenate(rx, axis=1) + bgx_ref[...])
    log_a = -LRU_C * rg * _softplus(-lam_ref[...])
    return jnp.exp(log_a), jnp.sqrt(1.0 - jnp.exp(2.0 * log_a)) * (ig * xc)


def _lru_body(gate_ref, x_ref, ci_ref, h0_ref, cw_ref, cb_ref, wga_ref, bga_ref, wgx_ref, bgx_ref, lam_ref,
              y_ref, ho_ref, xbuf, a_s, b_s, h_s, hc_s, *, q, nc):
    c = pl.program_id(1)
    _carried_window(xbuf, x_ref, ci_ref, c == 0, q)

    @pl.when(c == 0)
    def _():
        hc_s[...] = h0_ref[0]

    xc = _causal_conv(xbuf, cw_ref, cb_ref, q)
    _advance_window(xbuf, q)
    a_s[...], b_s[...] = _lru_coeffs(xc, wga_ref, bga_ref, wgx_ref, bgx_ref, lam_ref)

    def step(t, h):
        h = a_s[pl.ds(t, 1), :] * h + b_s[pl.ds(t, 1), :]
        h_s[pl.ds(t, 1), :] = h
        return h

    h = lax.fori_loop(0, q, step, hc_s[...])
    hc_s[...] = h
    y_ref[...] = h_s[...] * jax.nn.gelu(gate_ref[...])

    @pl.when(c == nc - 1)
    def _():
        ho_ref[0] = h


def _lru(p, conv_init, h0, prm, nseq, seq_len):
    q = min(CHUNK, seq_len)
    nc = seq_len // q
    per_seq = lambda shape: pl.BlockSpec((1,) + shape, lambda s, c: (s,) + (0,) * len(shape))
    const = lambda shape: pl.BlockSpec(shape, lambda s, c: (0,) * len(shape))
    return pl.pallas_call(
        functools.partial(_lru_body, q=q, nc=nc),
        out_shape=(jax.ShapeDtypeStruct((nseq * seq_len, D), F32),
                   jax.ShapeDtypeStruct((nseq, 1, D), F32)),
        grid=(nseq, nc),
        in_specs=[
            pl.BlockSpec((q, D), lambda s, c: (s * nc + c, 0)),
            pl.BlockSpec((q, D), lambda s, c: (s * nc + c, 1)),
            per_seq((SUBLANES, D)), per_seq((1, D)),
            const((CONV_W, D)), const((1, D)),
            const((LRU_BLOCKS, LRU_BLK, LRU_BLK)), const((1, D)),
            const((LRU_BLOCKS, LRU_BLK, LRU_BLK)), const((1, D)), const((1, D)),
        ],
        out_specs=(pl.BlockSpec((q, D), lambda s, c: (s * nc + c, 0)), per_seq((1, D))),
        scratch_shapes=[pltpu.VMEM((CHUNK + SUBLANES, D), F32), pltpu.VMEM((q, D), F32), pltpu.VMEM((q, D), F32),
                        pltpu.VMEM((q, D), F32), pltpu.VMEM((1, D), F32)],
        compiler_params=_cparams(("parallel", "arbitrary"), 32),
        name="lru",
    )(p, p, conv_init, h0, *prm)


def _lru_short_body(gate_ref, x_ref, p1_ref, p2_ref, p3_ref, h0_ref, cw_ref, cb_ref, wga_ref, bga_ref, wgx_ref,
                    bgx_ref, lam_ref, y_ref, h_ref, *, seq_len):
    cur = x_ref[...]
    t = lax.broadcasted_iota(jnp.int32, cur.shape, 0) % seq_len
    prev = lambda s, ref: jnp.where(t >= s, pltpu.roll(cur, s, axis=0), ref[...])
    xc = (cb_ref[...] + cw_ref[0:1, :] * prev(3, p3_ref) + cw_ref[1:2, :] * prev(2, p2_ref)
          + cw_ref[2:3, :] * prev(1, p1_ref) + cw_ref[3:4, :] * cur)
    a, b = _lru_coeffs(xc, wga_ref, bga_ref, wgx_ref, bgx_ref, lam_ref)
    s = 1
    while s < seq_len:
        keep = t >= s
        b = jnp.where(keep, a * pltpu.roll(b, s, axis=0) + b, b)
        a = jnp.where(keep, a * pltpu.roll(a, s, axis=0), a)
        s *= 2
    h = a * h0_ref[...] + b
    h_ref[...] = h
    y_ref[...] = h * jax.nn.gelu(gate_ref[...])


def _lru_short(p, conv_state, h0, prm, nseq, seq_len):
    rows = nseq * seq_len
    dense = lambda s: jnp.pad(conv_state[:, CONV_W - 1 - s:, :], ((0, 0), (0, seq_len - s), (0, 0))).reshape(rows, D)
    h0_rows = jnp.repeat(h0, seq_len, axis=0)
    tile = lambda col: pl.BlockSpec((CHUNK, D), lambda i: (i, col))
    const = lambda shape: pl.BlockSpec(shape, lambda i: (0,) * len(shape))
    sds = jax.ShapeDtypeStruct((rows, D), F32)
    return pl.pallas_call(
        functools.partial(_lru_short_body, seq_len=seq_len),
        out_shape=(sds, sds),
        grid=(rows // CHUNK,),
        in_specs=[tile(0), tile(1), tile(0), tile(0), tile(0), tile(0),
                  const((CONV_W, D)), const((1, D)),
                  const((LRU_BLOCKS, LRU_BLK, LRU_BLK)), const((1, D)),
                  const((LRU_BLOCKS, LRU_BLK, LRU_BLK)), const((1, D)), const((1, D))],
        out_specs=(tile(0), tile(0)),
        compiler_params=_cparams(("parallel",), 32),
        name="lru_short",
    )(p, p, dense(1), dense(2), dense(3), h0_rows, *prm)


def _pad_front_rows(buf):
    return jnp.pad(buf, ((0, 0), (SUBLANES - buf.shape[1], 0), (0, 0)))


def _row2(v):
    return v.reshape(1, -1)


def kernel(x_prompt, x_sample, state_ssm_a, state_conv_a, state_wkv_b, state_shift_b, state_lru_c, state_conv_c, norm_gain, w_ffn_in, w_ffn_out, w_in_ab, conv_w_a, conv_b_a, dt_bias_a, a_log_a, d_skip_a, gnorm_a, mu_b, w0_b, w2_b, a0_b, a2_b, g2_b, k_k_b, k_a_b, r_k_b, ln_w_b, ln_b_b, w_out_ab, w_in_c, conv_w_c, conv_b_c, w_gate_a_c, b_gate_a_c, w_gate_x_c, b_gate_x_c, lambda_c, w_out_c, final_norm_gain):
    w_gu, w_dn = _ffn_weights(w_ffn_in, w_ffn_out)

    in_a = D + (D + BC_W) + N_HEADS
    bc_w = BC_W
    segments = (
        (in_a, 3 * D),
        (0, D),
        (D, D),
        (2 * D, BC_W),
        (in_a + 3 * D, LORA_W),
        (in_a - N_HEADS, N_HEADS),
    )
    w_proj0 = _reorder_cast_cols(w_in_ab, segments, PROJ_W)
    w_proj1 = w_in_c[0].astype(BF16)
    w_out0 = w_out_ab[0].astype(BF16)
    w_out1 = w_out_c[0].astype(BF16)

    pad_lanes = lambda v: jnp.pad(v.reshape(1, -1), ((0, 0), (0, LANES - v.shape[-1])))
    rep_head = lambda v: jnp.repeat(v, HEAD).reshape(1, D)
    ssd_prm = (conv_w_a[0][:, :D], conv_w_a[0][:, D:], _row2(conv_b_a[0][:D]), _row2(conv_b_a[0][D:]),
               pad_lanes(dt_bias_a[0]), pad_lanes(a_log_a[0]), rep_head(d_skip_a[0]), _row2(gnorm_a[0]))
    lora_rows = lambda w, lo: jnp.pad(w, ((lo, LORA_W - lo - w.shape[0]), (0, 0))).astype(BF16)
    mu = mu_b[0]
    rwkv_prm = (_row2(mu[:3 * D]), _row2(mu[3 * D:]),
                _row2(w0_b[0]), lora_rows(w2_b[0], 0), _row2(a0_b[0]), lora_rows(a2_b[0], 64),
                lora_rows(g2_b[0], 128), _row2(k_k_b[0]), _row2(k_a_b[0]), _row2(r_k_b[0]))
    lru_prm = (conv_w_c[0], _row2(conv_b_c[0]), w_gate_a_c[0].astype(BF16), _row2(b_gate_a_c[0]),
               w_gate_x_c[0].astype(BF16), _row2(b_gate_x_c[0]), _row2(lambda_c[0]))

    def trunk(x3, ssm0, conva0, wkv0, shift0, lru0, convc0):
        nseq, seq_len, _ = x3.shape
        x = x3.reshape(nseq * seq_len, D)
        tail = lambda arr, n: arr.reshape(nseq, seq_len, arr.shape[-1])[:, seq_len - n:, :]

        x = _ffn(x, _row2(norm_gain[0, 0]), w_gu, w_dn, (0, 0))
        p = _proj(x, _row2(norm_gain[0, 1]), w_proj0)
        if seq_len >= CHUNK:
            ya, ssm_n = _ssd(p, _pad_front_rows(conva0[:, :, :D]), _pad_front_rows(conva0[:, :, D:]),
                             ssm0.reshape(nseq, D, N_STATE), ssd_prm, nseq, seq_len)
        else:
            ya, ssm_n = _ssd_short(p, conva0, ssm0.reshape(nseq, D, N_STATE), ssd_prm, nseq, seq_len)
        p_tail = tail(p, CONV_W - 1)
        conva_n = jnp.concatenate([p_tail[:, :, COL_X:COL_X + D], p_tail[:, :, COL_BC:COL_BC + bc_w]], axis=-1)
        shift_n = jnp.concatenate([p_tail[:, -1:, :3 * D], p_tail[:, -1:, COL_LORA:COL_LORA + LORA_W]], axis=-1)

        r, w, k, v, kk, b, bonus, g = _rwkv_pre(p, shift0, rwkv_prm, nseq, seq_len, log_decay=wkv0 is None)
        if wkv0 is None:
            yb, wkv_n = _wkv_long(r, w, k, v, kk, b, bonus, g, _row2(ln_w_b[0]), _row2(ln_b_b[0]), nseq, seq_len)
            wkv_n = wkv_n.reshape(nseq, N_HEADS, HEAD, HEAD)
        else:
            o, wkv_n = _wkv_short(*_to_lanes((r, w, k, v, kk, b), nseq, seq_len), wkv0.transpose(1, 2, 3, 0),
                                  nseq, seq_len)
            o = o.transpose(3, 0, 1, 2).reshape(nseq * seq_len, D)
            wkv_n = wkv_n.transpose(3, 0, 1, 2)
            yb = _rwkv_post(o, bonus, g, _row2(ln_w_b[0]), _row2(ln_b_b[0]))
        x = _out_proj(x, (ya, yb), (w_out0[:D], w_out0[D:]))
        x = _ffn(x, _row2(norm_gain[0, 2]), w_gu, w_dn, (0, 1))

        x = _ffn(x, _row2(norm_gain[1, 0]), w_gu, w_dn, (1, 0))
        pc = _proj(x, _row2(norm_gain[1, 1]), w_proj1)
        if seq_len >= CHUNK:
            yc, lru_n = _lru(pc, _pad_front_rows(convc0), lru0.reshape(nseq, 1, D), lru_prm, nseq, seq_len)
        else:
            yc, h_rows = _lru_short(pc, convc0, lru0, lru_prm, nseq, seq_len)
            lru_n = tail(h_rows, 1)
        convc_n = tail(pc, CONV_W - 1)[:, :, D:]
        x = _out_proj(x, (yc,), (w_out1,))
        y = _ffn(x, _row2(norm_gain[1, 2]), w_gu, w_dn, (1, 1), final_gain=_row2(final_norm_gain))

        return (y.reshape(nseq, seq_len, D), ssm_n.reshape(1, nseq, N_HEADS, HEAD, N_STATE), conva_n[None],
                wkv_n[None], shift_n[None], lru_n.reshape(1, nseq, D), convc_n[None])

    bp = x_prompt.shape[0]
    zeros = lambda s: jnp.zeros((bp,) + s.shape[2:], F32)
    outs_p = trunk(x_prompt, zeros(state_ssm_a), zeros(state_conv_a), None, zeros(state_shift_b),
                   zeros(state_lru_c), zeros(state_conv_c))
    outs_s = trunk(x_sample, state_ssm_a[0], state_conv_a[0], state_wkv_b[0], state_shift_b[0],
                   state_lru_c[0], state_conv_c[0])
    return (outs_p[0], outs_s[0]) + outs_p[1:] + outs_s[1:]
```

```python
import functools

import jax
import jax.numpy as jnp
from jax import lax
from jax.experimental import pallas as pl
from jax.experimental.pallas import tpu as pltpu

F32 = jnp.float32
BF16 = jnp.bfloat16

D = 2048
D_FF = 5504
D_FF_PAD = 5632
HEAD = 64
N_HEADS = 32
N_GROUPS = 4
GROUP_W = D // N_GROUPS
N_STATE = 128
HEADS_PER_GROUP = N_HEADS // N_GROUPS
BC_W = 2 * N_GROUPS * N_STATE
CONV_W = 4
LORA_W = 256
LRU_BLOCKS = 8
LRU_BLK = D // LRU_BLOCKS
LRU_C = 8.0
EPS = 1e-6
GN_EPS = 64e-5
SUBLANES = 8
LANES = 128
CHUNK = 128
WKV_CHUNK = 64
WKV_LANES = 256
WKV_GROUPS = 4
WKV_TILE_CHUNKS = 4
WKV_SHORT_ROWS = 4
SSD_SHORT_SEQS = 8
SSD_TILE_CHUNKS = 2

COL_RKV = 0
COL_Z = 3 * D
COL_X = 4 * D
COL_BC = 5 * D
COL_LORA = 5 * D + 2 * N_GROUPS * N_STATE
COL_DT = COL_LORA + LORA_W
PROJ_W = COL_DT + 256

TM_FFN = 512
TF_FFN = 512
TM_PROJ = 1024
TN_PROJ = 512
TM_OUT = 512
TN_OUT = 1024
TM_POST = 256
W_IN_ROWS = 64
W_OUT_COLS = 256


def _cparams(sem, vmem_mib):
    return pltpu.CompilerParams(dimension_semantics=sem, vmem_limit_bytes=vmem_mib * 1024 * 1024)


def _softplus(x):
    return jnp.maximum(x, 0.0) + jnp.log(1.0 + jnp.exp(-jnp.abs(x)))


def _silu(x):
    return x * jax.nn.sigmoid(x)


def _rms(x, gain):
    ms = jnp.mean(x * x, axis=-1, keepdims=True)
    return x * lax.rsqrt(ms + EPS) * gain


def _ffn_body(x_ref, g_ref, wg_ref, wu_ref, wo_ref, fg_ref, o_ref, xn_ref, acc_ref, *, nf, final):
    f = pl.program_id(1)

    @pl.when(f == 0)
    def _():
        xn_ref[...] = _rms(x_ref[...], g_ref[...]).astype(BF16)
        acc_ref[...] = jnp.zeros_like(acc_ref)

    xn = xn_ref[...]
    gate = jnp.dot(xn, wg_ref[...], preferred_element_type=F32)
    up = jnp.dot(xn, wu_ref[...], preferred_element_type=F32)
    h = (_silu(gate) * up).astype(BF16)
    acc_ref[...] += jnp.dot(h, wo_ref[...], preferred_element_type=F32)

    @pl.when(f == nf - 1)
    def _():
        y = x_ref[...] + 0.5 * acc_ref[...]
        if final:
            y = _rms(y, fg_ref[...])
        o_ref[...] = y


def _ffn(x, gain, w_gu, w_dn, which, final_gain=None):
    m = x.shape[0]
    li, si = which
    nf = D_FF_PAD // TF_FFN
    final = final_gain is not None
    fg = final_gain if final else gain
    return pl.pallas_call(
        functools.partial(_ffn_body, nf=nf, final=final),
        out_shape=jax.ShapeDtypeStruct((m, D), F32),
        grid=(m // TM_FFN, nf),
        in_specs=[
            pl.BlockSpec((TM_FFN, D), lambda i, f: (i, 0)),
            pl.BlockSpec((1, D), lambda i, f: (0, 0)),
            pl.BlockSpec((None, None, D, TF_FFN), lambda i, f: (li, si, 0, f)),
            pl.BlockSpec((None, None, D, TF_FFN), lambda i, f: (li, si, 0, f + nf)),
            pl.BlockSpec((None, None, TF_FFN, D), lambda i, f: (li, si, f, 0)),
            pl.BlockSpec((1, D), lambda i, f: (0, 0)),
        ],
        out_specs=pl.BlockSpec((TM_FFN, D), lambda i, f: (i, 0)),
        scratch_shapes=[pltpu.VMEM((TM_FFN, D), BF16), pltpu.VMEM((TM_FFN, D), F32)],
        compiler_params=_cparams(("parallel", "arbitrary"), 48),
        name="ffn",
    )(x, gain, w_gu, w_gu, w_dn, fg)


def _proj_body(x_ref, g_ref, w_ref, o_ref, xn_ref):
    @pl.when(pl.program_id(1) == 0)
    def _():
        xn_ref[...] = _rms(x_ref[...], g_ref[...]).astype(BF16)

    o_ref[...] = jnp.dot(xn_ref[...], w_ref[...], preferred_element_type=F32)


def _proj(x, gain, w):
    m = x.shape[0]
    n = w.shape[1]
    return pl.pallas_call(
        _proj_body,
        out_shape=jax.ShapeDtypeStruct((m, n), F32),
        grid=(m // TM_PROJ, n // TN_PROJ),
        in_specs=[
            pl.BlockSpec((TM_PROJ, D), lambda i, j: (i, 0)),
            pl.BlockSpec((1, D), lambda i, j: (0, 0)),
            pl.BlockSpec((D, TN_PROJ), lambda i, j: (0, j)),
        ],
        out_specs=pl.BlockSpec((TM_PROJ, TN_PROJ), lambda i, j: (i, j)),
        scratch_shapes=[pltpu.VMEM((TM_PROJ, D), BF16)],
        compiler_params=_cparams(("parallel", "arbitrary"), 40),
        name="proj",
    )(x, gain, w)


def _out2_body(res_ref, ya_ref, yb_ref, wa_ref, wb_ref, o_ref):
    acc = jnp.dot(ya_ref[...].astype(BF16), wa_ref[...], preferred_element_type=F32)
    acc = acc + jnp.dot(yb_ref[...].astype(BF16), wb_ref[...], preferred_element_type=F32)
    o_ref[...] = res_ref[...] + acc


def _out1_body(res_ref, y_ref, w_ref, o_ref):
    o_ref[...] = res_ref[...] + jnp.dot(y_ref[...].astype(BF16), w_ref[...], preferred_element_type=F32)


def _out_proj(res, ys, ws):
    m = res.shape[0]
    body = _out2_body if len(ys) == 2 else _out1_body
    y_spec = pl.BlockSpec((TM_OUT, D), lambda j, i: (i, 0))
    w_spec = pl.BlockSpec((D, TN_OUT), lambda j, i: (0, j))
    r_spec = pl.BlockSpec((TM_OUT, TN_OUT), lambda j, i: (i, j))
    return pl.pallas_call(
        body,
        out_shape=jax.ShapeDtypeStruct((m, D), F32),
        grid=(D // TN_OUT, m // TM_OUT),
        in_specs=[r_spec] + [y_spec] * len(ys) + [w_spec] * len(ws),
        out_specs=r_spec,
        compiler_params=_cparams(("arbitrary", "arbitrary"), 48),
        name="out_proj",
    )(res, *ys, *ws)


def _cast_pad_rows_body(w_ref, o_ref):
    o_ref[0:D_FF, :] = w_ref[...].astype(BF16)
    o_ref[D_FF:, :] = jnp.zeros((D_FF_PAD - D_FF, o_ref.shape[1]), BF16)


def _cast_pad_halves_body(w_ref, o_ref):
    rows = o_ref.shape[0]
    zeros = jnp.zeros((rows, D_FF_PAD - D_FF), BF16)
    for half in range(2):
        o_ref[:, half * D_FF_PAD:half * D_FF_PAD + D_FF] = w_ref[:, half * D_FF:(half + 1) * D_FF].astype(BF16)
        o_ref[:, half * D_FF_PAD + D_FF:(half + 1) * D_FF_PAD] = zeros


def _reorder_cast_body(w_ref, o_ref, *, segments):
    col = 0
    for src, width in segments:
        o_ref[:, col:col + width] = w_ref[:, src:src + width].astype(BF16)
        col += width
    o_ref[:, col:] = jnp.zeros((o_ref.shape[0], o_ref.shape[1] - col), BF16)


def _reorder_cast_cols(w, segments, out_cols):
    _, rows, cols = w.shape
    return pl.pallas_call(
        functools.partial(_reorder_cast_body, segments=segments),
        out_shape=jax.ShapeDtypeStruct((rows, out_cols), BF16),
        grid=(rows // W_IN_ROWS,),
        in_specs=[pl.BlockSpec((None, W_IN_ROWS, cols), lambda r: (0, r, 0))],
        out_specs=pl.BlockSpec((W_IN_ROWS, out_cols), lambda r: (r, 0)),
        compiler_params=_cparams(("parallel",), 32),
        name="reorder_cast",
    )(w)


def _ffn_weights(w_in, w_out):
    n_layers, n_slots = w_in.shape[:2]
    w_gu = pl.pallas_call(
        _cast_pad_halves_body,
        out_shape=jax.ShapeDtypeStruct((n_layers, n_slots, D, 2 * D_FF_PAD), BF16),
        grid=(n_layers * n_slots, D // W_IN_ROWS),
        in_specs=[pl.BlockSpec((None, None, W_IN_ROWS, 2 * D_FF), lambda q, r: (q // n_slots, q % n_slots, r, 0))],
        out_specs=pl.BlockSpec((None, None, W_IN_ROWS, 2 * D_FF_PAD), lambda q, r: (q // n_slots, q % n_slots, r, 0)),
        compiler_params=_cparams(("parallel", "parallel"), 32),
        name="ffn_w_in",
    )(w_in)
    w_dn = pl.pallas_call(
        _cast_pad_rows_body,
        out_shape=jax.ShapeDtypeStruct((n_layers, n_slots, D_FF_PAD, D), BF16),
        grid=(n_layers * n_slots, D // W_OUT_COLS),
        in_specs=[pl.BlockSpec((None, None, D_FF, W_OUT_COLS), lambda q, c: (q // n_slots, q % n_slots, 0, c))],
        out_specs=pl.BlockSpec((None, None, D_FF_PAD, W_OUT_COLS), lambda q, c: (q // n_slots, q % n_slots, 0, c)),
        compiler_params=_cparams(("parallel", "parallel"), 32),
        name="ffn_w_out",
    )(w_out)
    return w_gu, w_dn


def _carried_window(buf, cur_ref, init_ref, first, q):
    @pl.when(first)
    def _():
        buf[0:SUBLANES, :] = init_ref[0]

    buf[SUBLANES:SUBLANES + q, :] = cur_ref[...]


def _advance_window(buf, q):
    tail = buf[q:q + SUBLANES, :]
    buf[0:SUBLANES, :] = tail


def _causal_conv(buf, cw_ref, cb_ref, q, off=0):
    first = off + SUBLANES - CONV_W + 1
    acc = cb_ref[...] + cw_ref[0:1, :] * buf[pl.ds(first, q), :]
    for k in range(1, CONV_W):
        acc = acc + cw_ref[k:k + 1, :] * buf[pl.ds(first + k, q), :]
    return acc


def _cumsum_rows(x):
    n = x.shape[0]
    row = lax.broadcasted_iota(jnp.int32, x.shape, 0)
    s = 1
    while s < n:
        x = x + jnp.where(row >= s, pltpu.roll(x, s, axis=0), 0.0)
        s *= 2
    return x


def _expand_heads(a, rows):
    lane = lax.broadcasted_iota(jnp.int32, (rows, LANES), 1)
    low = lane < HEAD
    pieces = []
    for j in range(N_HEADS // 2):
        e0 = jnp.broadcast_to(a[:, 2 * j:2 * j + 1], (rows, LANES))
        e1 = jnp.broadcast_to(a[:, 2 * j + 1:2 * j + 2], (rows, LANES))
        pieces.append(jnp.where(low, e0, e1))
    return jnp.concatenate(pieces, axis=1)


def _head_allsum(x):
    width = x.shape[-1]
    blk = 4 * HEAD
    r = lax.broadcasted_iota(jnp.int32, (blk, blk), 0) // HEAD
    c = lax.broadcasted_iota(jnp.int32, (blk, blk), 1) // HEAD
    ones_bd = jnp.where(r == c, 1.0, 0.0).astype(BF16)
    hi = x.astype(BF16)
    rem = x - hi.astype(F32)
    mid = rem.astype(BF16)
    lo = (rem - mid.astype(F32)).astype(BF16)
    dot = lambda p: jnp.dot(p, ones_bd, preferred_element_type=F32)
    cols = []
    for j in range(width // blk):
        sl = slice(j * blk, (j + 1) * blk)
        cols.append(dot(hi[:, sl]) + (dot(mid[:, sl]) + dot(lo[:, sl])))
    return jnp.concatenate(cols, axis=1)


def _ssd_body(z_ref, x_ref, bc_ref, dt_ref, cix_ref, cibc_ref, h0_ref,
              cwx_ref, cwbc_ref, cbx_ref, cbbc_ref, dtb_ref, alog_ref, dsk_ref, gn_ref,
              y_ref, ho_ref, xbuf, bcbuf, h_s, *, nc):
    c = pl.program_id(1)
    rows = x_ref.shape[0]
    _carried_window(xbuf, x_ref, cix_ref, c == 0, rows)
    _carried_window(bcbuf, bc_ref, cibc_ref, c == 0, rows)

    @pl.when(c == 0)
    def _():
        h_s[...] = h0_ref[0]

    ti = lax.broadcasted_iota(jnp.int32, (CHUNK, CHUNK), 0)
    tj = lax.broadcasted_iota(jnp.int32, (CHUNK, CHUNK), 1)
    causal = ti >= tj
    low = tj < HEAD

    def prepare(off):
        xs = _silu(_causal_conv(xbuf, cwx_ref, cbx_ref, CHUNK, off))
        bcv = _silu(_causal_conv(bcbuf, cwbc_ref, cbbc_ref, CHUNK, off))
        dt = _softplus(dt_ref[off:off + CHUNK, :] + dtb_ref[...])
        cum = _cumsum_rows(dt * -jnp.exp(alog_ref[...]))
        cum_last = cum[CHUNK - 1:CHUNK, :]
        cum_t = cum.T
        xdt = xs * _expand_heads(dt, CHUNK)
        xdtw = (xdt * _expand_heads(jnp.exp(cum_last - cum), CHUNK)).astype(BF16)
        ecx = _expand_heads(jnp.exp(cum), CHUNK)
        per_group = []
        for g in range(N_GROUPS):
            gsl = slice(g * GROUP_W, (g + 1) * GROUP_W)
            bg = bcv[:, g * N_STATE:(g + 1) * N_STATE].astype(BF16)
            cg = bcv[:, (N_GROUPS + g) * N_STATE:(N_GROUPS + g + 1) * N_STATE].astype(BF16)
            cb = lax.dot_general(cg, bg, _NT, preferred_element_type=F32)
            st = lax.dot_general(xdtw[:, gsl], bg, _TN, preferred_element_type=F32)
            y_pairs = []
            for j in range(GROUP_W // LANES):
                h_lo = g * HEADS_PER_GROUP + 2 * j
                ms = []
                for h in (h_lo, h_lo + 1):
                    seg = cum[:, h:h + 1] - cum_t[h:h + 1, :]
                    ms.append((cb * jnp.exp(jnp.where(causal, seg, -jnp.inf))).astype(BF16))
                slab = xdt[:, h_lo * HEAD:(h_lo + 2) * HEAD]
                rhs = jnp.concatenate([jnp.where(low, slab, 0.0), jnp.where(low, 0.0, slab)], axis=0).astype(BF16)
                y_pairs.append(jnp.dot(jnp.concatenate(ms, axis=1), rhs, preferred_element_type=F32))
            y_in = jnp.concatenate(y_pairs, axis=1) + xs[:, gsl] * dsk_ref[:, gsl]
            per_group.append((cg, st, y_in, ecx[:, gsl]))
        return per_group, cum_last

    def finish(off, prepared):
        per_group, cum_last = prepared
        for g, (cg, st, y_in, ecg) in enumerate(per_group):
            gsl = slice(g * GROUP_W, (g + 1) * GROUP_W)
            y_off = lax.dot_general(cg, h_s[gsl, :].astype(BF16), _NT, preferred_element_type=F32) * ecg
            for h in range(g * HEADS_PER_GROUP, (g + 1) * HEADS_PER_GROUP):
                hsl = slice(h * HEAD, (h + 1) * HEAD)
                dec = jnp.exp(jnp.broadcast_to(cum_last[:, h:h + 1], (HEAD, N_STATE)))
                hg = slice((h % HEADS_PER_GROUP) * HEAD, (h % HEADS_PER_GROUP + 1) * HEAD)
                h_s[hsl, :] = h_s[hsl, :] * dec + st[hg, :]
            y = (y_in + y_off) * _silu(z_ref[off:off + CHUNK, gsl])
            y_ref[off:off + CHUNK, gsl] = y * lax.rsqrt(jnp.mean(y * y, axis=-1, keepdims=True) + EPS) * gn_ref[:, gsl]

    offsets = [i * CHUNK for i in range(rows // CHUNK)]
    prepared = [prepare(off) for off in offsets]
    _advance_window(xbuf, rows)
    _advance_window(bcbuf, rows)
    for off, prep in zip(offsets, prepared):
        finish(off, prep)

    @pl.when(c == nc - 1)
    def _():
        ho_ref[0] = h_s[...]


def _ssd(p, conv_x, conv_bc, h0, prm, nseq, seq_len):
    q = SSD_TILE_CHUNKS * CHUNK
    nc = seq_len // q
    rows = lambda w, col: pl.BlockSpec((q, w), lambda s, c: (s * nc + c, col))
    per_seq = lambda shape: pl.BlockSpec((1,) + shape, lambda s, c: (s,) + (0,) * len(shape))
    const = lambda shape: pl.BlockSpec(shape, lambda s, c: (0,) * len(shape))
    y, h_last = pl.pallas_call(
        functools.partial(_ssd_body, nc=nc),
        out_shape=(jax.ShapeDtypeStruct((nseq * seq_len, D), F32),
                   jax.ShapeDtypeStruct((nseq, D, N_STATE), F32)),
        grid=(nseq, nc),
        in_specs=[
            rows(D, COL_Z // D), rows(D, COL_X // D), rows(BC_W, COL_BC // BC_W),
            rows(LANES, COL_DT // LANES),
            per_seq((SUBLANES, D)), per_seq((SUBLANES, BC_W)), per_seq((D, N_STATE)),
            const((CONV_W, D)), const((CONV_W, BC_W)), const((1, D)), const((1, BC_W)),
            const((1, LANES)), const((1, LANES)), const((1, D)), const((1, D)),
        ],
        out_specs=(pl.BlockSpec((q, D), lambda s, c: (s * nc + c, 0)), per_seq((D, N_STATE))),
        scratch_shapes=[
            pltpu.VMEM((q + SUBLANES, D), F32), pltpu.VMEM((q + SUBLANES, BC_W), F32),
            pltpu.VMEM((D, N_STATE), F32),
        ],
        compiler_params=_cparams(("parallel", "arbitrary"), 40),
        name="ssd",
    )(p, p, p, p, conv_x, conv_bc, h0, *prm)
    return y, h_last


def _ssd_short_body(z_ref, x_ref, bc_ref, dt_ref, px1_ref, px2_ref, px3_ref, pb1_ref, pb2_ref, pb3_ref, h0_ref,
                    cwx_ref, cwbc_ref, cbx_ref, cbbc_ref, dtb_ref, alog_ref, dsk_ref, gn_ref,
                    y_ref, ho_ref, *, seq_len):
    rows = x_ref.shape[0]
    nseq_t = rows // seq_len
    t = lax.broadcasted_iota(jnp.int32, (rows, LANES), 0) % seq_len

    def conv(cur_ref, p1_ref, p2_ref, p3_ref, cw_ref, cb_ref):
        cur = cur_ref[...]
        tt = lax.broadcasted_iota(jnp.int32, cur.shape, 0) % seq_len
        prev = lambda s, ref: jnp.where(tt >= s, pltpu.roll(cur, s, axis=0), ref[...])
        return (cb_ref[...] + cw_ref[0:1, :] * prev(3, p3_ref) + cw_ref[1:2, :] * prev(2, p2_ref)
                + cw_ref[2:3, :] * prev(1, p1_ref) + cw_ref[3:4, :] * cur)

    xs = _silu(conv(x_ref, px1_ref, px2_ref, px3_ref, cwx_ref, cbx_ref))
    bcv = _silu(conv(bc_ref, pb1_ref, pb2_ref, pb3_ref, cwbc_ref, cbbc_ref))
    z = z_ref[...]
    dt = _softplus(dt_ref[...] + dtb_ref[...])
    cum = dt * -jnp.exp(alog_ref[...])
    s = 1
    while s < seq_len:
        cum = cum + jnp.where(t >= s, pltpu.roll(cum, s, axis=0), 0.0)
        s *= 2
    tot = cum
    s = 1
    while s < seq_len:
        tot = jnp.where(t + s < seq_len, pltpu.roll(tot, rows - s, axis=0), tot)
        s *= 2
    cum_sq = cum if rows == LANES else jnp.concatenate([cum, jnp.zeros((LANES - rows, LANES), F32)], axis=0)
    cum_t = cum_sq.T[:, 0:rows]
    dtx = _expand_heads(dt, rows)
    ecx = _expand_heads(jnp.exp(cum), rows)
    tlx = _expand_heads(jnp.exp(tot - cum), rows)
    xdt = xs * dtx
    xdtw = (xdt * tlx).astype(BF16)

    ti = lax.broadcasted_iota(jnp.int32, (rows, rows), 0)
    tj = lax.broadcasted_iota(jnp.int32, (rows, rows), 1)
    same_causal = (ti >= tj) & (ti // seq_len == tj // seq_len)
    low = lax.broadcasted_iota(jnp.int32, (rows, LANES), 1) < HEAD
    row_seq = lax.broadcasted_iota(jnp.int32, (rows, N_STATE), 0) // seq_len

    def per_seq_cols(m):
        return jnp.concatenate([jnp.where(row_seq == q, m, 0.0) for q in range(nseq_t)], axis=1).astype(BF16)

    for g in range(N_GROUPS):
        gsl = slice(g * GROUP_W, (g + 1) * GROUP_W)
        bg_f = bcv[:, g * N_STATE:(g + 1) * N_STATE]
        cg_f = bcv[:, (N_GROUPS + g) * N_STATE:(N_GROUPS + g + 1) * N_STATE]
        cb = lax.dot_general(cg_f.astype(BF16), bg_f.astype(BF16), _NT, preferred_element_type=F32)
        h_cat = jnp.concatenate([h0_ref[q, gsl, :] for q in range(nseq_t)], axis=1).astype(BF16)
        y_off = lax.dot_general(per_seq_cols(cg_f), h_cat, _NT, preferred_element_type=F32) * ecx[:, gsl]
        st = lax.dot_general(xdtw[:, gsl], per_seq_cols(bg_f), _TN, preferred_element_type=F32)
        y_pairs = []
        for j in range(GROUP_W // LANES):
            h_lo = g * (GROUP_W // HEAD) + 2 * j
            ms = []
            for h in (h_lo, h_lo + 1):
                seg = cum[:, h:h + 1] - cum_t[h:h + 1, :]
                decay = jnp.exp(jnp.where(same_causal, seg, -jnp.inf))
                ms.append((cb * decay).astype(BF16))
            slab = xdt[:, h_lo * HEAD:(h_lo + 2) * HEAD]
            rhs = jnp.concatenate([jnp.where(low, slab, 0.0), jnp.where(low, 0.0, slab)], axis=0).astype(BF16)
            y_pairs.append(jnp.dot(jnp.concatenate(ms, axis=1), rhs, preferred_element_type=F32))
            for h in (h_lo, h_lo + 1):
                hsl = slice(h * HEAD, (h + 1) * HEAD)
                hg = slice((h % HEADS_PER_GROUP) * HEAD, (h % HEADS_PER_GROUP + 1) * HEAD)
                for q in range(nseq_t):
                    dec = jnp.exp(jnp.broadcast_to(tot[q * seq_len:q * seq_len + 1, h:h + 1], (HEAD, N_STATE)))
                    ho_ref[q, hsl, :] = h0_ref[q, hsl, :] * dec + st[hg, q * N_STATE:(q + 1) * N_STATE]
        y = jnp.concatenate(y_pairs, axis=1) + y_off + xs[:, gsl] * dsk_ref[:, gsl]
        y = y * _silu(z[:, gsl])
        y_ref[:, gsl] = y * lax.rsqrt(jnp.mean(y * y, axis=-1, keepdims=True) + EPS) * gn_ref[:, gsl]


def _ssd_short(p, conv_state, h0, prm, nseq, seq_len):
    rows = nseq * seq_len
    tr = SSD_SHORT_SEQS * seq_len
    bc_w = BC_W

    def dense(s, lo, hi):
        part = conv_state[:, CONV_W - 1 - s:, lo:hi]
        return jnp.pad(part, ((0, 0), (0, seq_len - s), (0, 0))).reshape(rows, hi - lo)

    tile = lambda w, col: pl.BlockSpec((tr, w), lambda i: (i, col))
    const = lambda shape: pl.BlockSpec(shape, lambda i: (0,) * len(shape))
    state = pl.BlockSpec((SSD_SHORT_SEQS, D, N_STATE), lambda i: (i, 0, 0))
    return pl.pallas_call(
        functools.partial(_ssd_short_body, seq_len=seq_len),
        out_shape=(jax.ShapeDtypeStruct((rows, D), F32), jax.ShapeDtypeStruct((nseq, D, N_STATE), F32)),
        grid=(nseq // SSD_SHORT_SEQS,),
        in_specs=[
            tile(D, COL_Z // D), tile(D, COL_X // D), tile(bc_w, COL_BC // bc_w), tile(LANES, COL_DT // LANES),
            tile(D, 0), tile(D, 0), tile(D, 0), tile(bc_w, 0), tile(bc_w, 0), tile(bc_w, 0), state,
            const((CONV_W, D)), const((CONV_W, bc_w)), const((1, D)), const((1, bc_w)),
            const((1, LANES)), const((1, LANES)), const((1, D)), const((1, D)),
        ],
        out_specs=(tile(D, 0), state),
        compiler_params=_cparams(("parallel",), 48),
        name="ssd_short",
    )(p, p, p, p, dense(1, 0, D), dense(2, 0, D), dense(3, 0, D), dense(1, D, D + bc_w), dense(2, D, D + bc_w),
      dense(3, D, D + bc_w), h0, *prm)


def _rwkv_pre_body(rkv_ref, lora_ref, si_rkv_ref, si_lora_ref, mu_rkv_ref, mu_lora_ref,
                   w0_ref, w2_ref, a0_ref, a2_ref, g2_ref, kk_ref, ka_ref, rk_ref,
                   r_out, w_out, k_out, v_out, kk_out, b_out, bonus_out, g_out,
                   rkvbuf, lorabuf, *, q, log_decay, short_len):
    if short_len:
        def shifted(buf, cur_ref, si_ref, mu_ref):
            cur = cur_ref[...]
            first = lax.broadcasted_iota(jnp.int32, cur.shape, 0) % short_len == 0
            prev = jnp.where(first, si_ref[...], pltpu.roll(cur, 1, axis=0))
            return cur + mu_ref[...] * (prev - cur)
    else:
        c = pl.program_id(1)
        _carried_window(rkvbuf, rkv_ref, si_rkv_ref, c == 0, q)
        _carried_window(lorabuf, lora_ref, si_lora_ref, c == 0, q)

        def shifted(buf, cur_ref, si_ref, mu_ref):
            cur = cur_ref[...]
            prev = buf[pl.ds(SUBLANES - 1, q), :]
            return cur + mu_ref[...] * (prev - cur)

    ps = shifted(rkvbuf, rkv_ref, si_rkv_ref, mu_rkv_ref)
    lo_in = shifted(lorabuf, lora_ref, si_lora_ref, mu_lora_ref)
    if not short_len:
        _advance_window(rkvbuf, q)
        _advance_window(lorabuf, q)
    r = ps[:, 0:D]
    k = ps[:, D:2 * D]
    v = ps[:, 2 * D:3 * D]

    lw = jnp.dot(jnp.tanh(lo_in).astype(BF16), w2_ref[...], preferred_element_type=F32)
    la = jnp.dot(lo_in.astype(BF16), a2_ref[...], preferred_element_type=F32)
    g = jnp.dot(jax.nn.sigmoid(lo_in).astype(BF16), g2_ref[...], preferred_element_type=F32)
    wlog = -_softplus(-(w0_ref[...] + lw)) - 0.5
    log_w = -jnp.exp(wlog)
    a = jax.nn.sigmoid(a0_ref[...] + la)
    kkf = k * kk_ref[...]
    norm = jnp.maximum(jnp.sqrt(_head_allsum(kkf * kkf)), 1e-12)
    kk = kkf / norm
    k2 = k * (1.0 + (a - 1.0) * ka_ref[...])
    bonus = _head_allsum(r * k2 * rk_ref[...]) * v
    r_out[...] = r
    w_out[...] = log_w if log_decay else jnp.exp(log_w)
    k_out[...] = k2
    v_out[...] = v
    kk_out[...] = kk
    b_out[...] = kk * a
    bonus_out[...] = bonus
    g_out[...] = g


def _rwkv_pre(p, shift_state, prm, nseq, seq_len, log_decay):
    short_len = seq_len if seq_len < CHUNK else 0
    q = CHUNK if short_len else min(CHUNK, seq_len)
    nc = 1 if short_len else seq_len // q
    ntile = nseq * seq_len // q // nc
    const = lambda shape: pl.BlockSpec(shape, lambda s, c: (0,) * len(shape))
    tile = pl.BlockSpec((q, D), lambda s, c: (s * nc + c, 0))
    if short_len:
        dense = lambda a: jnp.pad(a, ((0, 0), (0, seq_len - 1), (0, 0))).reshape(nseq * seq_len, a.shape[-1])
        si = (dense(shift_state[:, :, :3 * D]), dense(shift_state[:, :, 3 * D:]))
        si_specs = [pl.BlockSpec((q, 3 * D), lambda s, c: (s, 0)), pl.BlockSpec((q, LORA_W), lambda s, c: (s, 0))]
    else:
        si = (_pad_front_rows(shift_state[:, :, :3 * D]), _pad_front_rows(shift_state[:, :, 3 * D:]))
        si_specs = [pl.BlockSpec((1, SUBLANES, 3 * D), lambda s, c: (s, 0, 0)),
                    pl.BlockSpec((1, SUBLANES, LORA_W), lambda s, c: (s, 0, 0))]
    sds = jax.ShapeDtypeStruct((nseq * seq_len, D), F32)
    return pl.pallas_call(
        functools.partial(_rwkv_pre_body, q=q, log_decay=log_decay, short_len=short_len),
        out_shape=(sds,) * 8,
        grid=(ntile, nc),
        in_specs=[
            pl.BlockSpec((q, 3 * D), lambda s, c: (s * nc + c, 0)),
            pl.BlockSpec((q, LORA_W), lambda s, c: (s * nc + c, COL_LORA // LORA_W)),
            *si_specs,
            const((1, 3 * D)), const((1, LORA_W)),
            const((1, D)), const((LORA_W, D)), const((1, D)), const((LORA_W, D)), const((LORA_W, D)),
            const((1, D)), const((1, D)), const((1, D)),
        ],
        out_specs=(tile,) * 8,
        scratch_shapes=[pltpu.VMEM((CHUNK + SUBLANES, 3 * D), F32), pltpu.VMEM((CHUNK + SUBLANES, LORA_W), F32)],
        compiler_params=_cparams(("parallel", "arbitrary"), 48),
        name="rwkv_pre",
    )(p, p, *si, *prm)


def _rwkv_out(o, bonus, g, ln_w, ln_b):
    mean = _head_allsum(o) * (1.0 / HEAD)
    cen = o - mean
    var = _head_allsum(cen * cen) * (1.0 / HEAD)
    return (cen * lax.rsqrt(var + GN_EPS) * ln_w + ln_b + bonus) * g


def _split_bf16(x):
    hi = x.astype(BF16)
    return hi, (x - hi.astype(F32)).astype(BF16)


_NN = (((1,), (0,)), ((), ()))
_NT = (((1,), (1,)), ((), ()))
_TN = (((0,), (0,)), ((), ()))


def _solve_unit_lower(n, rhs, lower_left):
    rows, width = rhs.shape
    half = rows // 2
    nblk, ncol = rows // SUBLANES, width // LANES
    tiles = lambda a, r0, r1: [[a[SUBLANES * i:SUBLANES * (i + 1), LANES * j:LANES * (j + 1)] for j in range(ncol)]
                               for i in range(r0 // SUBLANES, r1 // SUBLANES)]
    nb = tiles(n, 0, rows)
    xb = tiles(rhs, 0, rows)
    low = lax.broadcasted_iota(jnp.int32, (SUBLANES, LANES), 1) < HEAD

    def substitute(lo, hi):
        for s in range(lo, hi - 1):
            i0, r0 = divmod(s, SUBLANES)
            idx = jnp.where(low, s, HEAD + s)
            for j in range(ncol):
                row = xb[i0][j][r0:r0 + 1, :]
                for i in range(i0 if r0 < SUBLANES - 1 else i0 + 1, hi // SUBLANES):
                    xb[i][j] = xb[i][j] - jnp.take_along_axis(nb[i][j], idx, axis=1) * row

    join = lambda blocks: jnp.concatenate([jnp.concatenate(xr, axis=1) for xr in blocks], axis=0)
    substitute(0, half)
    corr = tiles(lower_left(join(xb[:half // SUBLANES])), 0, half)
    for i in range(half // SUBLANES):
        for j in range(ncol):
            xb[half // SUBLANES + i][j] = xb[half // SUBLANES + i][j] - corr[i][j]
    substitute(half, rows)
    return join(xb)


def _wkv_chunk(r, lw, k, v, kk, b):
    rows, width = r.shape
    nh = WKV_LANES // HEAD
    groups = [slice(g * WKV_LANES, (g + 1) * WKV_LANES) for g in range(width // WKV_LANES)]
    cl = _cumsum_rows(lw)
    cl_last = cl[rows - 1:rows, :]
    p_inv = jnp.exp(-cl)
    p_end = jnp.exp(cl_last - cl)
    x2h, x2l = _split_bf16(jnp.concatenate([kk * jnp.exp(cl - lw), r * jnp.exp(cl)], axis=0))
    k_hat = k * p_inv
    b_hat = b * p_inv
    k_end = k * p_end
    b_end = -(b * p_end)

    bd_r = lax.broadcasted_iota(jnp.int32, (nh * rows, WKV_LANES), 0) // rows
    bd_c = lax.broadcasted_iota(jnp.int32, (nh * rows, WKV_LANES), 1) // HEAD
    bd_mask = jnp.where(bd_r == bd_c, 1.0, 0.0).astype(BF16)

    def per_head_rows(y):
        return [jnp.concatenate([part] * nh, axis=0) * bd_mask for part in _split_bf16(y)]

    def dot3(ah, al, bh, bl, dims):
        dg = lambda x, y: lax.dot_general(x, y, dims, preferred_element_type=F32)
        return dg(ah, bh) + (dg(ah, bl) + dg(al, bh))

    t_i = lax.broadcasted_iota(jnp.int32, (rows, WKV_LANES), 0)
    lane = lax.broadcasted_iota(jnp.int32, (rows, WKV_LANES), 1)
    s_i = lane & (HEAD - 1)
    strict = t_i > s_i
    incl = t_i >= s_i
    head_of_lane = lane // HEAD

    a_kb, a_rb, av = [], [], []
    for gs in groups:
        ak = dot3(x2h[:, gs], x2l[:, gs], *per_head_rows(k_hat[:, gs]), _NT)
        ab = dot3(x2h[:, gs], x2l[:, gs], *per_head_rows(b_hat[:, gs]), _NT)
        a_k = jnp.concatenate([jnp.where(strict, ak[0:rows], 0.0), jnp.where(incl, ak[rows:], 0.0)], axis=0)
        av.append(dot3(*_split_bf16(a_k), *per_head_rows(v[:, gs]), _NN))
        a_kb.append(jnp.where(strict, ab[0:rows], 0.0))
        a_rb.append(jnp.where(incl, ab[rows:], 0.0))
    a_kb_all = jnp.concatenate(a_kb, axis=1)
    half = rows // 2
    top_cols = (lax.broadcasted_iota(jnp.int32, (half, WKV_LANES), 1) & (HEAD - 1)) < half

    def lower_left(x_top):
        x_pad = jnp.concatenate([x_top, jnp.zeros_like(x_top)], axis=0)
        parts = []
        for g, gs in enumerate(groups):
            lhs = jnp.where(top_cols, a_kb[g][half:], 0.0)
            parts.append(dot3(*_split_bf16(lhs), *per_head_rows(x_pad[:, gs]), _NN))
        return jnp.concatenate(parts, axis=1)

    upd_lhs = [_split_bf16(jnp.concatenate([k_end[:, gs], b_end[:, gs],
                                            jnp.where(t_i == s_i, jnp.exp(cl_last[:, gs]), 0.0)], axis=0))
               for gs in groups]

    def advance(st):
        xs = [dot3(x2h[:, gs], x2l[:, gs], *per_head_rows(st[:, gs]), _NN) for gs in groups]
        base = jnp.concatenate([xs[g][0:rows] + av[g][0:rows] for g in range(len(groups))], axis=1)
        sa = _solve_unit_lower(a_kb_all, base, lower_left)
        o, st_new = [], []
        for g, gs in enumerate(groups):
            o.append(xs[g][rows:] + av[g][rows:] - dot3(*_split_bf16(a_rb[g]), *per_head_rows(sa[:, gs]), _NN))
            rhs = jnp.concatenate([v[:, gs], sa[:, gs], st[:, gs]], axis=0)
            full = dot3(*upd_lhs[g], *_split_bf16(rhs), _TN)
            acc = jnp.where(head_of_lane == 0, full[0:HEAD, :], 0.0)
            for h in range(1, nh):
                acc = acc + jnp.where(head_of_lane == h, full[h * HEAD:(h + 1) * HEAD, :], 0.0)
            st_new.append(acc)
        return jnp.concatenate(o, axis=1), jnp.concatenate(st_new, axis=1)

    return advance


def _wkv_long_body(r_ref, lw_ref, k_ref, v_ref, kk_ref, b_ref, bonus_ref, g_ref, lnw_ref, lnb_ref,
                   y_ref, so_ref, st_s, *, nc):
    c = pl.program_id(2)

    @pl.when(c == 0)
    def _():
        st_s[...] = jnp.zeros_like(st_s)

    chunks = [slice(i * WKV_CHUNK, (i + 1) * WKV_CHUNK) for i in range(r_ref.shape[0] // WKV_CHUNK)]
    advances = [_wkv_chunk(r_ref[cs, :], lw_ref[cs, :], k_ref[cs, :], v_ref[cs, :], kk_ref[cs, :], b_ref[cs, :])
                for cs in chunks]
    st = st_s[...]
    for cs, advance in zip(chunks, advances):
        o, st = advance(st)
        y_ref[cs, :] = _rwkv_out(o, bonus_ref[cs, :], g_ref[cs, :], lnw_ref[...], lnb_ref[...])
    st_s[...] = st

    @pl.when(c == nc - 1)
    def _():
        for j in range(WKV_GROUPS * WKV_LANES // LANES):
            js = slice(j * LANES, (j + 1) * LANES)
            sq = jnp.concatenate([st_s[:, js], jnp.zeros((LANES - HEAD, LANES), F32)], axis=0)
            so_ref[0, js, :] = sq.T[:, 0:HEAD]


def _wkv_long(r, lw, k, v, kk, b, bonus, g, ln_w, ln_b, nseq, seq_len):
    tile_rows = WKV_TILE_CHUNKS * WKV_CHUNK
    nc = seq_len // tile_rows
    width = WKV_GROUPS * WKV_LANES
    tile = pl.BlockSpec((tile_rows, width), lambda s, hg, c: (s * nc + c, hg))
    vec = pl.BlockSpec((1, width), lambda s, hg, c: (0, hg))
    y, s_last = pl.pallas_call(
        functools.partial(_wkv_long_body, nc=nc),
        out_shape=(jax.ShapeDtypeStruct((nseq * seq_len, D), F32),
                   jax.ShapeDtypeStruct((nseq, D, HEAD), F32)),
        grid=(nseq, D // width, nc),
        in_specs=[tile] * 8 + [vec, vec],
        out_specs=(tile, pl.BlockSpec((1, width, HEAD), lambda s, hg, c: (s, hg, 0))),
        scratch_shapes=[pltpu.VMEM((HEAD, width), F32)],
        compiler_params=_cparams(("parallel", "parallel", "arbitrary"), 48),
        name="wkv_long",
    )(r, lw, k, v, kk, b, bonus, g, ln_w, ln_b)
    return y, s_last


def _wkv_short_body(r_ref, w_ref, k_ref, v_ref, kk_ref, b_ref, s0_ref, o_ref, so_ref, *, steps):
    def per_v_group(i, carry):
        vis = [i * WKV_SHORT_ROWS + u for u in range(WKV_SHORT_ROWS)]
        ss = [s0_ref[0, vi] for vi in vis]
        for t in range(steps):
            kk_t, w_t, b_t, k_t, r_t = kk_ref[t, 0], w_ref[t, 0], b_ref[t, 0], k_ref[t, 0], r_ref[t, 0]
            for u, vi in enumerate(vis):
                vrow = v_ref[t, 0, pl.ds(vi, 1), :]
                sa = jnp.sum(ss[u] * kk_t, axis=0, keepdims=True)
                ss[u] = ss[u] * w_t - b_t * sa + k_t * vrow
                o_ref[t, 0, pl.ds(vi, 1), :] = jnp.sum(ss[u] * r_t, axis=0, keepdims=True)
        for u, vi in enumerate(vis):
            so_ref[0, vi] = ss[u]
        return carry

    lax.fori_loop(0, HEAD // WKV_SHORT_ROWS, per_v_group, 0)


def _to_lanes_body(*refs, n_in, seq_len):
    nseq = refs[0].shape[0] // seq_len
    for src, dst in zip(refs[:n_in], refs[n_in:]):
        for t in range(seq_len):
            xt = src[pl.ds(t, nseq, stride=seq_len), :].T
            dst[t, 0] = xt[0:HEAD, :]
            dst[t, 1] = xt[HEAD:2 * HEAD, :]


def _to_lanes(arrays, nseq, seq_len):
    n = len(arrays)
    return pl.pallas_call(
        functools.partial(_to_lanes_body, n_in=n, seq_len=seq_len),
        out_shape=(jax.ShapeDtypeStruct((seq_len, N_HEADS, HEAD, nseq), F32),) * n,
        grid=(D // LANES,),
        in_specs=[pl.BlockSpec((nseq * seq_len, LANES), lambda hp: (0, hp))] * n,
        out_specs=(pl.BlockSpec((seq_len, 2, HEAD, nseq), lambda hp: (0, hp, 0, 0)),) * n,
        compiler_params=_cparams(("parallel",), 32),
        name="to_lanes",
    )(*arrays)


def _wkv_short(r, w, k, v, kk, b, s0, nseq, seq_len):
    vec = pl.BlockSpec((seq_len, 1, HEAD, nseq), lambda h: (0, h, 0, 0))
    st = pl.BlockSpec((1, HEAD, HEAD, nseq), lambda h: (h, 0, 0, 0))
    return pl.pallas_call(
        functools.partial(_wkv_short_body, steps=seq_len),
        out_shape=(jax.ShapeDtypeStruct((seq_len, N_HEADS, HEAD, nseq), F32),
                   jax.ShapeDtypeStruct((N_HEADS, HEAD, HEAD, nseq), F32)),
        grid=(N_HEADS,),
        in_specs=[vec] * 6 + [st],
        out_specs=(vec, st),
        compiler_params=_cparams(("parallel",), 32),
        name="wkv_short",
    )(r, w, k, v, kk, b, s0)


def _rwkv_post_body(o_ref, bonus_ref, g_ref, lnw_ref, lnb_ref, y_ref):
    y_ref[...] = _rwkv_out(o_ref[...], bonus_ref[...], g_ref[...], lnw_ref[...], lnb_ref[...])


def _rwkv_post(o, bonus, g, ln_w, ln_b):
    m = o.shape[0]
    tile = pl.BlockSpec((TM_POST, D), lambda i: (i, 0))
    const = pl.BlockSpec((1, D), lambda i: (0, 0))
    return pl.pallas_call(
        _rwkv_post_body,
        out_shape=jax.ShapeDtypeStruct((m, D), F32),
        grid=(m // TM_POST,),
        in_specs=[tile, tile, tile, const, const],
        out_specs=tile,
        compiler_params=_cparams(("parallel",), 32),
        name="rwkv_post",
    )(o, bonus, g, ln_w, ln_b)


def _lru_coeffs(xc, wga_ref, bga_ref, wgx_ref, bgx_ref, lam_ref):
    ra, rx = [], []
    for blk in range(LRU_BLOCKS):
        xh = xc[:, blk * LRU_BLK:(blk + 1) * LRU_BLK].astype(BF16)
        ra.append(jnp.dot(xh, wga_ref[blk], preferred_element_type=F32))
        rx.append(jnp.dot(xh, wgx_ref[blk], preferred_element_type=F32))
    rg = jax.nn.sigmoid(jnp.concatenate(ra, axis=1) + bga_ref[...])
    ig = jax.nn.sigmoid(jnp.concatenate(rx, axis=1) + bgx_ref[...])
    log_a = -LRU_C * rg * _softplus(-lam_ref[...])
    return jnp.exp(log_a), jnp.sqrt(1.0 - jnp.exp(2.0 * log_a)) * (ig * xc)


def _lru_body(gate_ref, x_ref, ci_ref, h0_ref, cw_ref, cb_ref, wga_ref, bga_ref, wgx_ref, bgx_ref, lam_ref,
              y_ref, ho_ref, xbuf, a_s, b_s, h_s, hc_s, *, q, nc):
    c = pl.program_id(1)
    _carried_window(xbuf, x_ref, ci_ref, c == 0, q)

    @pl.when(c == 0)
    def _():
        hc_s[...] = h0_ref[0]

    xc = _causal_conv(xbuf, cw_ref, cb_ref, q)
    _advance_window(xbuf, q)
    a_s[...], b_s[...] = _lru_coeffs(xc, wga_ref, bga_ref, wgx_ref, bgx_ref, lam_ref)

    def step(t, h):
        h = a_s[pl.ds(t, 1), :] * h + b_s[pl.ds(t, 1), :]
        h_s[pl.ds(t, 1), :] = h
        return h

    h = lax.fori_loop(0, q, step, hc_s[...])
    hc_s[...] = h
    y_ref[...] = h_s[...] * jax.nn.gelu(gate_ref[...])

    @pl.when(c == nc - 1)
    def _():
        ho_ref[0] = h


def _lru(p, conv_init, h0, prm, nseq, seq_len):
    q = min(CHUNK, seq_len)
    nc = seq_len // q
    per_seq = lambda shape: pl.BlockSpec((1,) + shape, lambda s, c: (s,) + (0,) * len(shape))
    const = lambda shape: pl.BlockSpec(shape, lambda s, c: (0,) * len(shape))
    return pl.pallas_call(
        functools.partial(_lru_body, q=q, nc=nc),
        out_shape=(jax.ShapeDtypeStruct((nseq * seq_len, D), F32),
                   jax.ShapeDtypeStruct((nseq, 1, D), F32)),
        grid=(nseq, nc),
        in_specs=[
            pl.BlockSpec((q, D), lambda s, c: (s * nc + c, 0)),
            pl.BlockSpec((q, D), lambda s, c: (s * nc + c, 1)),
            per_seq((SUBLANES, D)), per_seq((1, D)),
            const((CONV_W, D)), const((1, D)),
            const((LRU_BLOCKS, LRU_BLK, LRU_BLK)), const((1, D)),
            const((LRU_BLOCKS, LRU_BLK, LRU_BLK)), const((1, D)), const((1, D)),
        ],
        out_specs=(pl.BlockSpec((q, D), lambda s, c: (s * nc + c, 0)), per_seq((1, D))),
        scratch_shapes=[pltpu.VMEM((CHUNK + SUBLANES, D), F32), pltpu.VMEM((q, D), F32), pltpu.VMEM((q, D), F32),
                        pltpu.VMEM((q, D), F32), pltpu.VMEM((1, D), F32)],
        compiler_params=_cparams(("parallel", "arbitrary"), 32),
        name="lru",
    )(p, p, conv_init, h0, *prm)


def _lru_short_body(gate_ref, x_ref, p1_ref, p2_ref, p3_ref, h0_ref, cw_ref, cb_ref, wga_ref, bga_ref, wgx_ref,
                    bgx_ref, lam_ref, y_ref, h_ref, *, seq_len):
    cur = x_ref[...]
    t = lax.broadcasted_iota(jnp.int32, cur.shape, 0) % seq_len
    prev = lambda s, ref: jnp.where(t >= s, pltpu.roll(cur, s, axis=0), ref[...])
    xc = (cb_ref[...] + cw_ref[0:1, :] * prev(3, p3_ref) + cw_ref[1:2, :] * prev(2, p2_ref)
          + cw_ref[2:3, :] * prev(1, p1_ref) + cw_ref[3:4, :] * cur)
    a, b = _lru_coeffs(xc, wga_ref, bga_ref, wgx_ref, bgx_ref, lam_ref)
    s = 1
    while s < seq_len:
        keep = t >= s
        b = jnp.where(keep, a * pltpu.roll(b, s, axis=0) + b, b)
        a = jnp.where(keep, a * pltpu.roll(a, s, axis=0), a)
        s *= 2
    h = a * h0_ref[...] + b
    h_ref[...] = h
    y_ref[...] = h * jax.nn.gelu(gate_ref[...])


def _lru_short(p, conv_state, h0, prm, nseq, seq_len):
    rows = nseq * seq_len
    dense = lambda s: jnp.pad(conv_state[:, CONV_W - 1 - s:, :], ((0, 0), (0, seq_len - s), (0, 0))).reshape(rows, D)
    h0_rows = jnp.repeat(h0, seq_len, axis=0)
    tile = lambda col: pl.BlockSpec((CHUNK, D), lambda i: (i, col))
    const = lambda shape: pl.BlockSpec(shape, lambda i: (0,) * len(shape))
    sds = jax.ShapeDtypeStruct((rows, D), F32)
    return pl.pallas_call(
        functools.partial(_lru_short_body, seq_len=seq_len),
        out_shape=(sds, sds),
        grid=(rows // CHUNK,),
        in_specs=[tile(0), tile(1), tile(0), tile(0), tile(0), tile(0),
                  const((CONV_W, D)), const((1, D)),
                  const((LRU_BLOCKS, LRU_BLK, LRU_BLK)), const((1, D)),
                  const((LRU_BLOCKS, LRU_BLK, LRU_BLK)), const((1, D)), const((1, D))],
        out_specs=(tile(0), tile(0)),
        compiler_params=_cparams(("parallel",), 32),
        name="lru_short",
    )(p, p, dense(1), dense(2), dense(3), h0_rows, *prm)


def _pad_front_rows(buf):
    return jnp.pad(buf, ((0, 0), (SUBLANES - buf.shape[1], 0), (0, 0)))


def _row2(v):
    return v.reshape(1, -1)


def kernel(x_prompt, x_sample, state_ssm_a, state_conv_a, state_wkv_b, state_shift_b, state_lru_c, state_conv_c, norm_gain, w_ffn_in, w_ffn_out, w_in_ab, conv_w_a, conv_b_a, dt_bias_a, a_log_a, d_skip_a, gnorm_a, mu_b, w0_b, w2_b, a0_b, a2_b, g2_b, k_k_b, k_a_b, r_k_b, ln_w_b, ln_b_b, w_out_ab, w_in_c, conv_w_c, conv_b_c, w_gate_a_c, b_gate_a_c, w_gate_x_c, b_gate_x_c, lambda_c, w_out_c, final_norm_gain):
    w_gu, w_dn = _ffn_weights(w_ffn_in, w_ffn_out)

    in_a = D + (D + BC_W) + N_HEADS
    bc_w = BC_W
    segments = (
        (in_a, 3 * D),
        (0, D),
        (D, D),
        (2 * D, BC_W),
        (in_a + 3 * D, LORA_W),
        (in_a - N_HEADS, N_HEADS),
    )
    w_proj0 = _reorder_cast_cols(w_in_ab, segments, PROJ_W)
    w_proj1 = w_in_c[0].astype(BF16)
    w_out0 = w_out_ab[0].astype(BF16)
    w_out1 = w_out_c[0].astype(BF16)

    pad_lanes = lambda v: jnp.pad(v.reshape(1, -1), ((0, 0), (0, LANES - v.shape[-1])))
    rep_head = lambda v: jnp.repeat(v, HEAD).reshape(1, D)
    ssd_prm = (conv_w_a[0][:, :D], conv_w_a[0][:, D:], _row2(conv_b_a[0][:D]), _row2(conv_b_a[0][D:]),
               pad_lanes(dt_bias_a[0]), pad_lanes(a_log_a[0]), rep_head(d_skip_a[0]), _row2(gnorm_a[0]))
    lora_rows = lambda w, lo: jnp.pad(w, ((lo, LORA_W - lo - w.shape[0]), (0, 0))).astype(BF16)
    mu = mu_b[0]
    rwkv_prm = (_row2(mu[:3 * D]), _row2(mu[3 * D:]),
                _row2(w0_b[0]), lora_rows(w2_b[0], 0), _row2(a0_b[0]), lora_rows(a2_b[0], 64),
                lora_rows(g2_b[0], 128), _row2(k_k_b[0]), _row2(k_a_b[0]), _row2(r_k_b[0]))
    lru_prm = (conv_w_c[0], _row2(conv_b_c[0]), w_gate_a_c[0].astype(BF16), _row2(b_gate_a_c[0]),
               w_gate_x_c[0].astype(BF16), _row2(b_gate_x_c[0]), _row2(lambda_c[0]))

    def trunk(x3, ssm0, conva0, wkv0, shift0, lru0, convc0):
        nseq, seq_len, _ = x3.shape
        x = x3.reshape(nseq * seq_len, D)
        tail = lambda arr, n: arr.reshape(nseq, seq_len, arr.shape[-1])[:, seq_len - n:, :]

        x = _ffn(x, _row2(norm_gain[0, 0]), w_gu, w_dn, (0, 0))
        p = _proj(x, _row2(norm_gain[0, 1]), w_proj0)
        if seq_len >= CHUNK:
            ya, ssm_n = _ssd(p, _pad_front_rows(conva0[:, :, :D]), _pad_front_rows(conva0[:, :, D:]),
                             ssm0.reshape(nseq, D, N_STATE), ssd_prm, nseq, seq_len)
        else:
            ya, ssm_n = _ssd_short(p, conva0, ssm0.reshape(nseq, D, N_STATE), ssd_prm, nseq, seq_len)
        p_tail = tail(p, CONV_W - 1)
        conva_n = jnp.concatenate([p_tail[:, :, COL_X:COL_X + D], p_tail[:, :, COL_BC:COL_BC + bc_w]], axis=-1)
        shift_n = jnp.concatenate([p_tail[:, -1:, :3 * D], p_tail[:, -1:, COL_LORA:COL_LORA + LORA_W]], axis=-1)

        r, w, k, v, kk, b, bonus, g = _rwkv_pre(p, shift0, rwkv_prm, nseq, seq_len, log_decay=wkv0 is None)
        if wkv0 is None:
            yb, wkv_n = _wkv_long(r, w, k, v, kk, b, bonus, g, _row2(ln_w_b[0]), _row2(ln_b_b[0]), nseq, seq_len)
            wkv_n = wkv_n.reshape(nseq, N_HEADS, HEAD, HEAD)
        else:
            o, wkv_n = _wkv_short(*_to_lanes((r, w, k, v, kk, b), nseq, seq_len), wkv0.transpose(1, 2, 3, 0),
                                  nseq, seq_len)
            o = o.transpose(3, 0, 1, 2).reshape(nseq * seq_len, D)
            wkv_n = wkv_n.transpose(3, 0, 1, 2)
            yb = _rwkv_post(o, bonus, g, _row2(ln_w_b[0]), _row2(ln_b_b[0]))
        x = _out_proj(x, (ya, yb), (w_out0[:D], w_out0[D:]))
        x = _ffn(x, _row2(norm_gain[0, 2]), w_gu, w_dn, (0, 1))

        x = _ffn(x, _row2(norm_gain[1, 0]), w_gu, w_dn, (1, 0))
        pc = _proj(x, _row2(norm_gain[1, 1]), w_proj1)
        if seq_len >= CHUNK:
            yc, lru_n = _lru(pc, _pad_front_rows(convc0), lru0.reshape(nseq, 1, D), lru_prm, nseq, seq_len)
        else:
            yc, h_rows = _lru_short(pc, convc0, lru0, lru_prm, nseq, seq_len)
            lru_n = tail(h_rows, 1)
        convc_n = tail(pc, CONV_W - 1)[:, :, D:]
        x = _out_proj(x, (yc,), (w_out1,))
        y = _ffn(x, _row2(norm_gain[1, 2]), w_gu, w_dn, (1, 1), final_gain=_row2(final_norm_gain))

        return (y.reshape(nseq, seq_len, D), ssm_n.reshape(1, nseq, N_HEADS, HEAD, N_STATE), conva_n[None],
                wkv_n[None], shift_n[None], lru_n.reshape(1, nseq, D), convc_n[None])

    bp = x_prompt.shape[0]
    zeros = lambda s: jnp.zeros((bp,) + s.shape[2:], F32)
    outs_p = trunk(x_prompt, zeros(state_ssm_a), zeros(state_conv_a), None, zeros(state_shift_b),
                   zeros(state_lru_c), zeros(state_conv_c))
    outs_s = trunk(x_sample, state_ssm_a[0], state_conv_a[0], state_wkv_b[0], state_shift_b[0],
                   state_lru_c[0], state_conv_c[0])
    return (outs_p[0], outs_s[0]) + outs_p[1:] + outs_s[1:]
```

```python
import functools

import jax
import jax.numpy as jnp
from jax import lax
from jax.experimental import pallas as pl
from jax.experimental.pallas import tpu as pltpu

F32 = jnp.float32
BF16 = jnp.bfloat16

D = 2048
D_FF = 5504
D_FF_PAD = 5632
HEAD = 64
N_HEADS = 32
N_GROUPS = 4
GROUP_W = D // N_GROUPS
N_STATE = 128
HEADS_PER_GROUP = N_HEADS // N_GROUPS
BC_W = 2 * N_GROUPS * N_STATE
CONV_W = 4
LORA_W = 256
LRU_BLOCKS = 8
LRU_BLK = D // LRU_BLOCKS
LRU_C = 8.0
EPS = 1e-6
GN_EPS = 64e-5
SUBLANES = 8
LANES = 128
CHUNK = 128
WKV_CHUNK = 64
WKV_LANES = 256
WKV_GROUPS = 4
WKV_TILE_CHUNKS = 4
WKV_SHORT_ROWS = 4
SSD_SHORT_SEQS = 8
SSD_TILE_CHUNKS = 2
LRU_TILE_ROWS = 256

COL_RKV = 0
COL_Z = 3 * D
COL_X = 4 * D
COL_BC = 5 * D
COL_LORA = 5 * D + 2 * N_GROUPS * N_STATE
COL_DT = COL_LORA + LORA_W
PROJ_W = COL_DT + 256

TM_FFN = 512
TF_FFN = 512
TM_PROJ = 1024
TN_PROJ = 512
TM_OUT = 512
TN_OUT = 1024
TM_POST = 256
W_IN_ROWS = 64
W_OUT_COLS = 256


def _cparams(sem, vmem_mib):
    return pltpu.CompilerParams(dimension_semantics=sem, vmem_limit_bytes=vmem_mib * 1024 * 1024)


def _softplus(x):
    return jnp.maximum(x, 0.0) + jnp.log(1.0 + jnp.exp(-jnp.abs(x)))


def _silu(x):
    return x * jax.nn.sigmoid(x)


def _rms(x, gain):
    ms = jnp.mean(x * x, axis=-1, keepdims=True)
    return x * lax.rsqrt(ms + EPS) * gain


def _ffn_body(x_ref, g_ref, wg_ref, wu_ref, wo_ref, fg_ref, o_ref, xn_ref, acc_ref, *, nf, final):
    f = pl.program_id(1)

    @pl.when(f == 0)
    def _():
        xn_ref[...] = _rms(x_ref[...], g_ref[...]).astype(BF16)
        acc_ref[...] = jnp.zeros_like(acc_ref)

    xn = xn_ref[...]
    gate = jnp.dot(xn, wg_ref[...], preferred_element_type=F32)
    up = jnp.dot(xn, wu_ref[...], preferred_element_type=F32)
    h = (_silu(gate) * up).astype(BF16)
    acc_ref[...] += jnp.dot(h, wo_ref[...], preferred_element_type=F32)

    @pl.when(f == nf - 1)
    def _():
        y = x_ref[...] + 0.5 * acc_ref[...]
        if final:
            y = _rms(y, fg_ref[...])
        o_ref[...] = y


def _ffn(x, gain, w_gu, w_dn, which, final_gain=None):
    m = x.shape[0]
    li, si = which
    nf = D_FF_PAD // TF_FFN
    final = final_gain is not None
    fg = final_gain if final else gain
    return pl.pallas_call(
        functools.partial(_ffn_body, nf=nf, final=final),
        out_shape=jax.ShapeDtypeStruct((m, D), F32),
        grid=(m // TM_FFN, nf),
        in_specs=[
            pl.BlockSpec((TM_FFN, D), lambda i, f: (i, 0)),
            pl.BlockSpec((1, D), lambda i, f: (0, 0)),
            pl.BlockSpec((None, None, D, TF_FFN), lambda i, f: (li, si, 0, f)),
            pl.BlockSpec((None, None, D, TF_FFN), lambda i, f: (li, si, 0, f + nf)),
            pl.BlockSpec((None, None, TF_FFN, D), lambda i, f: (li, si, f, 0)),
            pl.BlockSpec((1, D), lambda i, f: (0, 0)),
        ],
        out_specs=pl.BlockSpec((TM_FFN, D), lambda i, f: (i, 0)),
        scratch_shapes=[pltpu.VMEM((TM_FFN, D), BF16), pltpu.VMEM((TM_FFN, D), F32)],
        compiler_params=_cparams(("parallel", "arbitrary"), 48),
        name="ffn",
    )(x, gain, w_gu, w_gu, w_dn, fg)


def _proj_body(x_ref, g_ref, w_ref, o_ref, xn_ref):
    @pl.when(pl.program_id(1) == 0)
    def _():
        xn_ref[...] = _rms(x_ref[...], g_ref[...]).astype(BF16)

    o_ref[...] = jnp.dot(xn_ref[...], w_ref[...], preferred_element_type=F32)


def _proj(x, gain, w):
    m = x.shape[0]
    n = w.shape[1]
    return pl.pallas_call(
        _proj_body,
        out_shape=jax.ShapeDtypeStruct((m, n), F32),
        grid=(m // TM_PROJ, n // TN_PROJ),
        in_specs=[
            pl.BlockSpec((TM_PROJ, D), lambda i, j: (i, 0)),
            pl.BlockSpec((1, D), lambda i, j: (0, 0)),
            pl.BlockSpec((D, TN_PROJ), lambda i, j: (0, j)),
        ],
        out_specs=pl.BlockSpec((TM_PROJ, TN_PROJ), lambda i, j: (i, j)),
        scratch_shapes=[pltpu.VMEM((TM_PROJ, D), BF16)],
        compiler_params=_cparams(("parallel", "arbitrary"), 40),
        name="proj",
    )(x, gain, w)


def _out2_body(res_ref, ya_ref, yb_ref, wa_ref, wb_ref, o_ref):
    acc = jnp.dot(ya_ref[...].astype(BF16), wa_ref[...], preferred_element_type=F32)
    acc = acc + jnp.dot(yb_ref[...].astype(BF16), wb_ref[...], preferred_element_type=F32)
    o_ref[...] = res_ref[...] + acc


def _out1_body(res_ref, y_ref, w_ref, o_ref):
    o_ref[...] = res_ref[...] + jnp.dot(y_ref[...].astype(BF16), w_ref[...], preferred_element_type=F32)


def _out_proj(res, ys, ws):
    m = res.shape[0]
    body = _out2_body if len(ys) == 2 else _out1_body
    y_spec = pl.BlockSpec((TM_OUT, D), lambda j, i: (i, 0))
    w_spec = pl.BlockSpec((D, TN_OUT), lambda j, i: (0, j))
    r_spec = pl.BlockSpec((TM_OUT, TN_OUT), lambda j, i: (i, j))
    return pl.pallas_call(
        body,
        out_shape=jax.ShapeDtypeStruct((m, D), F32),
        grid=(D // TN_OUT, m // TM_OUT),
        in_specs=[r_spec] + [y_spec] * len(ys) + [w_spec] * len(ws),
        out_specs=r_spec,
        compiler_params=_cparams(("arbitrary", "arbitrary"), 48),
        name="out_proj",
    )(res, *ys, *ws)


def _cast_pad_rows_body(w_ref, o_ref):
    o_ref[0:D_FF, :] = w_ref[...].astype(BF16)
    o_ref[D_FF:, :] = jnp.zeros((D_FF_PAD - D_FF, o_ref.shape[1]), BF16)


def _cast_pad_halves_body(w_ref, o_ref):
    rows = o_ref.shape[0]
    zeros = jnp.zeros((rows, D_FF_PAD - D_FF), BF16)
    for half in range(2):
        o_ref[:, half * D_FF_PAD:half * D_FF_PAD + D_FF] = w_ref[:, half * D_FF:(half + 1) * D_FF].astype(BF16)
        o_ref[:, half * D_FF_PAD + D_FF:(half + 1) * D_FF_PAD] = zeros


def _reorder_cast_body(w_ref, o_ref, *, segments):
    col = 0
    for src, width in segments:
        o_ref[:, col:col + width] = w_ref[:, src:src + width].astype(BF16)
        col += width
    o_ref[:, col:] = jnp.zeros((o_ref.shape[0], o_ref.shape[1] - col), BF16)


def _reorder_cast_cols(w, segments, out_cols):
    _, rows, cols = w.shape
    return pl.pallas_call(
        functools.partial(_reorder_cast_body, segments=segments),
        out_shape=jax.ShapeDtypeStruct((rows, out_cols), BF16),
        grid=(rows // W_IN_ROWS,),
        in_specs=[pl.BlockSpec((None, W_IN_ROWS, cols), lambda r: (0, r, 0))],
        out_specs=pl.BlockSpec((W_IN_ROWS, out_cols), lambda r: (r, 0)),
        compiler_params=_cparams(("parallel",), 32),
        name="reorder_cast",
    )(w)


def _ffn_weights(w_in, w_out):
    n_layers, n_slots = w_in.shape[:2]
    w_gu = pl.pallas_call(
        _cast_pad_halves_body,
        out_shape=jax.ShapeDtypeStruct((n_layers, n_slots, D, 2 * D_FF_PAD), BF16),
        grid=(n_layers * n_slots, D // W_IN_ROWS),
        in_specs=[pl.BlockSpec((None, None, W_IN_ROWS, 2 * D_FF), lambda q, r: (q // n_slots, q % n_slots, r, 0))],
        out_specs=pl.BlockSpec((None, None, W_IN_ROWS, 2 * D_FF_PAD), lambda q, r: (q // n_slots, q % n_slots, r, 0)),
        compiler_params=_cparams(("parallel", "parallel"), 32),
        name="ffn_w_in",
    )(w_in)
    w_dn = pl.pallas_call(
        _cast_pad_rows_body,
        out_shape=jax.ShapeDtypeStruct((n_layers, n_slots, D_FF_PAD, D), BF16),
        grid=(n_layers * n_slots, D // W_OUT_COLS),
        in_specs=[pl.BlockSpec((None, None, D_FF, W_OUT_COLS), lambda q, c: (q // n_slots, q % n_slots, 0, c))],
        out_specs=pl.BlockSpec((None, None, D_FF_PAD, W_OUT_COLS), lambda q, c: (q // n_slots, q % n_slots, 0, c)),
        compiler_params=_cparams(("parallel", "parallel"), 32),
        name="ffn_w_out",
    )(w_out)
    return w_gu, w_dn


def _carried_window(buf, cur_ref, init_ref, first, q):
    @pl.when(first)
    def _():
        buf[0:SUBLANES, :] = init_ref[0]

    buf[SUBLANES:SUBLANES + q, :] = cur_ref[...]


def _advance_window(buf, q):
    tail = buf[q:q + SUBLANES, :]
    buf[0:SUBLANES, :] = tail


def _causal_conv(buf, cw_ref, cb_ref, q, off=0):
    first = off + SUBLANES - CONV_W + 1
    acc = cb_ref[...] + cw_ref[0:1, :] * buf[pl.ds(first, q), :]
    for k in range(1, CONV_W):
        acc = acc + cw_ref[k:k + 1, :] * buf[pl.ds(first + k, q), :]
    return acc


def _cumsum_rows(x):
    n = x.shape[0]
    row = lax.broadcasted_iota(jnp.int32, x.shape, 0)
    s = 1
    while s < n:
        x = x + jnp.where(row >= s, pltpu.roll(x, s, axis=0), 0.0)
        s *= 2
    return x


def _expand_heads(a, rows):
    lane = lax.broadcasted_iota(jnp.int32, (rows, LANES), 1)
    low = lane < HEAD
    pieces = []
    for j in range(N_HEADS // 2):
        e0 = jnp.broadcast_to(a[:, 2 * j:2 * j + 1], (rows, LANES))
        e1 = jnp.broadcast_to(a[:, 2 * j + 1:2 * j + 2], (rows, LANES))
        pieces.append(jnp.where(low, e0, e1))
    return jnp.concatenate(pieces, axis=1)


def _head_allsum(x):
    width = x.shape[-1]
    blk = 4 * HEAD
    r = lax.broadcasted_iota(jnp.int32, (blk, blk), 0) // HEAD
    c = lax.broadcasted_iota(jnp.int32, (blk, blk), 1) // HEAD
    ones_bd = jnp.where(r == c, 1.0, 0.0).astype(BF16)
    hi = x.astype(BF16)
    rem = x - hi.astype(F32)
    mid = rem.astype(BF16)
    lo = (rem - mid.astype(F32)).astype(BF16)
    dot = lambda p: jnp.dot(p, ones_bd, preferred_element_type=F32)
    cols = []
    for j in range(width // blk):
        sl = slice(j * blk, (j + 1) * blk)
        cols.append(dot(hi[:, sl]) + (dot(mid[:, sl]) + dot(lo[:, sl])))
    return jnp.concatenate(cols, axis=1)


def _ssd_body(z_ref, x_ref, bc_ref, dt_ref, cix_ref, cibc_ref, h0_ref,
              cwx_ref, cwbc_ref, cbx_ref, cbbc_ref, dtb_ref, alog_ref, dsk_ref, gn_ref,
              y_ref, ho_ref, xbuf, bcbuf, h_s, *, nc):
    c = pl.program_id(1)
    rows = x_ref.shape[0]
    _carried_window(xbuf, x_ref, cix_ref, c == 0, rows)
    _carried_window(bcbuf, bc_ref, cibc_ref, c == 0, rows)

    @pl.when(c == 0)
    def _():
        h_s[...] = h0_ref[0]

    ti = lax.broadcasted_iota(jnp.int32, (CHUNK, CHUNK), 0)
    tj = lax.broadcasted_iota(jnp.int32, (CHUNK, CHUNK), 1)
    causal = ti >= tj
    low = tj < HEAD

    def prepare(off):
        xs = _silu(_causal_conv(xbuf, cwx_ref, cbx_ref, CHUNK, off))
        bcv = _silu(_causal_conv(bcbuf, cwbc_ref, cbbc_ref, CHUNK, off))
        dt = _softplus(dt_ref[off:off + CHUNK, :] + dtb_ref[...])
        cum = _cumsum_rows(dt * -jnp.exp(alog_ref[...]))
        cum_last = cum[CHUNK - 1:CHUNK, :]
        cum_t = cum.T
        xdt = xs * _expand_heads(dt, CHUNK)
        xdtw = (xdt * _expand_heads(jnp.exp(cum_last - cum), CHUNK)).astype(BF16)
        ecx = _expand_heads(jnp.exp(cum), CHUNK)
        per_group = []
        for g in range(N_GROUPS):
            gsl = slice(g * GROUP_W, (g + 1) * GROUP_W)
            bg = bcv[:, g * N_STATE:(g + 1) * N_STATE].astype(BF16)
            cg = bcv[:, (N_GROUPS + g) * N_STATE:(N_GROUPS + g + 1) * N_STATE].astype(BF16)
            cb = lax.dot_general(cg, bg, _NT, preferred_element_type=F32)
            st = lax.dot_general(xdtw[:, gsl], bg, _TN, preferred_element_type=F32)
            y_pairs = []
            for j in range(GROUP_W // LANES):
                h_lo = g * HEADS_PER_GROUP + 2 * j
                ms = []
                for h in (h_lo, h_lo + 1):
                    seg = cum[:, h:h + 1] - cum_t[h:h + 1, :]
                    ms.append((cb * jnp.exp(jnp.where(causal, seg, -jnp.inf))).astype(BF16))
                slab = xdt[:, h_lo * HEAD:(h_lo + 2) * HEAD]
                rhs = jnp.concatenate([jnp.where(low, slab, 0.0), jnp.where(low, 0.0, slab)], axis=0).astype(BF16)
                y_pairs.append(jnp.dot(jnp.concatenate(ms, axis=1), rhs, preferred_element_type=F32))
            y_in = jnp.concatenate(y_pairs, axis=1) + xs[:, gsl] * dsk_ref[:, gsl]
            per_group.append((cg, st, y_in, ecx[:, gsl]))
        return per_group, cum_last

    def finish(off, prepared):
        per_group, cum_last = prepared
        for g, (cg, st, y_in, ecg) in enumerate(per_group):
            gsl = slice(g * GROUP_W, (g + 1) * GROUP_W)
            y_off = lax.dot_general(cg, h_s[gsl, :].astype(BF16), _NT, preferred_element_type=F32) * ecg
            for h in range(g * HEADS_PER_GROUP, (g + 1) * HEADS_PER_GROUP):
                hsl = slice(h * HEAD, (h + 1) * HEAD)
                dec = jnp.exp(jnp.broadcast_to(cum_last[:, h:h + 1], (HEAD, N_STATE)))
                hg = slice((h % HEADS_PER_GROUP) * HEAD, (h % HEADS_PER_GROUP + 1) * HEAD)
                h_s[hsl, :] = h_s[hsl, :] * dec + st[hg, :]
            y = (y_in + y_off) * _silu(z_ref[off:off + CHUNK, gsl])
            y_ref[off:off + CHUNK, gsl] = y * lax.rsqrt(jnp.mean(y * y, axis=-1, keepdims=True) + EPS) * gn_ref[:, gsl]

    offsets = [i * CHUNK for i in range(rows // CHUNK)]
    prepared = [prepare(off) for off in offsets]
    _advance_window(xbuf, rows)
    _advance_window(bcbuf, rows)
    for off, prep in zip(offsets, prepared):
        finish(off, prep)

    @pl.when(c == nc - 1)
    def _():
        ho_ref[0] = h_s[...]


def _ssd(p, conv_x, conv_bc, h0, prm, nseq, seq_len):
    q = SSD_TILE_CHUNKS * CHUNK
    nc = seq_len // q
    rows = lambda w, col: pl.BlockSpec((q, w), lambda s, c: (s * nc + c, col))
    per_seq = lambda shape: pl.BlockSpec((1,) + shape, lambda s, c: (s,) + (0,) * len(shape))
    const = lambda shape: pl.BlockSpec(shape, lambda s, c: (0,) * len(shape))
    y, h_last = pl.pallas_call(
        functools.partial(_ssd_body, nc=nc),
        out_shape=(jax.ShapeDtypeStruct((nseq * seq_len, D), F32),
                   jax.ShapeDtypeStruct((nseq, D, N_STATE), F32)),
        grid=(nseq, nc),
        in_specs=[
            rows(D, COL_Z // D), rows(D, COL_X // D), rows(BC_W, COL_BC // BC_W),
            rows(LANES, COL_DT // LANES),
            per_seq((SUBLANES, D)), per_seq((SUBLANES, BC_W)), per_seq((D, N_STATE)),
            const((CONV_W, D)), const((CONV_W, BC_W)), const((1, D)), const((1, BC_W)),
            const((1, LANES)), const((1, LANES)), const((1, D)), const((1, D)),
        ],
        out_specs=(pl.BlockSpec((q, D), lambda s, c: (s * nc + c, 0)), per_seq((D, N_STATE))),
        scratch_shapes=[
            pltpu.VMEM((q + SUBLANES, D), F32), pltpu.VMEM((q + SUBLANES, BC_W), F32),
            pltpu.VMEM((D, N_STATE), F32),
        ],
        compiler_params=_cparams(("parallel", "arbitrary"), 40),
        name="ssd",
    )(p, p, p, p, conv_x, conv_bc, h0, *prm)
    return y, h_last


def _ssd_short_body(z_ref, x_ref, bc_ref, dt_ref, px1_ref, px2_ref, px3_ref, pb1_ref, pb2_ref, pb3_ref, h0_ref,
                    cwx_ref, cwbc_ref, cbx_ref, cbbc_ref, dtb_ref, alog_ref, dsk_ref, gn_ref,
                    y_ref, ho_ref, *, seq_len):
    rows = x_ref.shape[0]
    nseq_t = rows // seq_len
    t = lax.broadcasted_iota(jnp.int32, (rows, LANES), 0) % seq_len

    def conv(cur_ref, p1_ref, p2_ref, p3_ref, cw_ref, cb_ref):
        cur = cur_ref[...]
        tt = lax.broadcasted_iota(jnp.int32, cur.shape, 0) % seq_len
        prev = lambda s, ref: jnp.where(tt >= s, pltpu.roll(cur, s, axis=0), ref[...])
        return (cb_ref[...] + cw_ref[0:1, :] * prev(3, p3_ref) + cw_ref[1:2, :] * prev(2, p2_ref)
                + cw_ref[2:3, :] * prev(1, p1_ref) + cw_ref[3:4, :] * cur)

    xs = _silu(conv(x_ref, px1_ref, px2_ref, px3_ref, cwx_ref, cbx_ref))
    bcv = _silu(conv(bc_ref, pb1_ref, pb2_ref, pb3_ref, cwbc_ref, cbbc_ref))
    z = z_ref[...]
    dt = _softplus(dt_ref[...] + dtb_ref[...])
    cum = dt * -jnp.exp(alog_ref[...])
    s = 1
    while s < seq_len:
        cum = cum + jnp.where(t >= s, pltpu.roll(cum, s, axis=0), 0.0)
        s *= 2
    tot = cum
    s = 1
    while s < seq_len:
        tot = jnp.where(t + s < seq_len, pltpu.roll(tot, rows - s, axis=0), tot)
        s *= 2
    cum_sq = cum if rows == LANES else jnp.concatenate([cum, jnp.zeros((LANES - rows, LANES), F32)], axis=0)
    cum_t = cum_sq.T[:, 0:rows]
    dtx = _expand_heads(dt, rows)
    ecx = _expand_heads(jnp.exp(cum), rows)
    tlx = _expand_heads(jnp.exp(tot - cum), rows)
    xdt = xs * dtx
    xdtw = (xdt * tlx).astype(BF16)

    ti = lax.broadcasted_iota(jnp.int32, (rows, rows), 0)
    tj = lax.broadcasted_iota(jnp.int32, (rows, rows), 1)
    same_causal = (ti >= tj) & (ti // seq_len == tj // seq_len)
    low = lax.broadcasted_iota(jnp.int32, (rows, LANES), 1) < HEAD
    row_seq = lax.broadcasted_iota(jnp.int32, (rows, N_STATE), 0) // seq_len

    def per_seq_cols(m):
        return jnp.concatenate([jnp.where(row_seq == q, m, 0.0) for q in range(nseq_t)], axis=1).astype(BF16)

    for g in range(N_GROUPS):
        gsl = slice(g * GROUP_W, (g + 1) * GROUP_W)
        bg_f = bcv[:, g * N_STATE:(g + 1) * N_STATE]
        cg_f = bcv[:, (N_GROUPS + g) * N_STATE:(N_GROUPS + g + 1) * N_STATE]
        cb = lax.dot_general(cg_f.astype(BF16), bg_f.astype(BF16), _NT, preferred_element_type=F32)
        h_cat = jnp.concatenate([h0_ref[q, gsl, :] for q in range(nseq_t)], axis=1).astype(BF16)
        y_off = lax.dot_general(per_seq_cols(cg_f), h_cat, _NT, preferred_element_type=F32) * ecx[:, gsl]
        st = lax.dot_general(xdtw[:, gsl], per_seq_cols(bg_f), _TN, preferred_element_type=F32)
        y_pairs = []
        for j in range(GROUP_W // LANES):
            h_lo = g * (GROUP_W // HEAD) + 2 * j
            ms = []
            for h in (h_lo, h_lo + 1):
                seg = cum[:, h:h + 1] - cum_t[h:h + 1, :]
                decay = jnp.exp(jnp.where(same_causal, seg, -jnp.inf))
                ms.append((cb * decay).astype(BF16))
            slab = xdt[:, h_lo * HEAD:(h_lo + 2) * HEAD]
            rhs = jnp.concatenate([jnp.where(low, slab, 0.0), jnp.where(low, 0.0, slab)], axis=0).astype(BF16)
            y_pairs.append(jnp.dot(jnp.concatenate(ms, axis=1), rhs, preferred_element_type=F32))
            for h in (h_lo, h_lo + 1):
                hsl = slice(h * HEAD, (h + 1) * HEAD)
                hg = slice((h % HEADS_PER_GROUP) * HEAD, (h % HEADS_PER_GROUP + 1) * HEAD)
                for q in range(nseq_t):
                    dec = jnp.exp(jnp.broadcast_to(tot[q * seq_len:q * seq_len + 1, h:h + 1], (HEAD, N_STATE)))
                    ho_ref[q, hsl, :] = h0_ref[q, hsl, :] * dec + st[hg, q * N_STATE:(q + 1) * N_STATE]
        y = jnp.concatenate(y_pairs, axis=1) + y_off + xs[:, gsl] * dsk_ref[:, gsl]
        y = y * _silu(z[:, gsl])
        y_ref[:, gsl] = y * lax.rsqrt(jnp.mean(y * y, axis=-1, keepdims=True) + EPS) * gn_ref[:, gsl]


def _ssd_short(p, conv_state, h0, prm, nseq, seq_len):
    rows = nseq * seq_len
    tr = SSD_SHORT_SEQS * seq_len
    bc_w = BC_W

    def dense(s, lo, hi):
        part = conv_state[:, CONV_W - 1 - s:, lo:hi]
        return jnp.pad(part, ((0, 0), (0, seq_len - s), (0, 0))).reshape(rows, hi - lo)

    tile = lambda w, col: pl.BlockSpec((tr, w), lambda i: (i, col))
    const = lambda shape: pl.BlockSpec(shape, lambda i: (0,) * len(shape))
    state = pl.BlockSpec((SSD_SHORT_SEQS, D, N_STATE), lambda i: (i, 0, 0))
    return pl.pallas_call(
        functools.partial(_ssd_short_body, seq_len=seq_len),
        out_shape=(jax.ShapeDtypeStruct((rows, D), F32), jax.ShapeDtypeStruct((nseq, D, N_STATE), F32)),
        grid=(nseq // SSD_SHORT_SEQS,),
        in_specs=[
            tile(D, COL_Z // D), tile(D, COL_X // D), tile(bc_w, COL_BC // bc_w), tile(LANES, COL_DT // LANES),
            tile(D, 0), tile(D, 0), tile(D, 0), tile(bc_w, 0), tile(bc_w, 0), tile(bc_w, 0), state,
            const((CONV_W, D)), const((CONV_W, bc_w)), const((1, D)), const((1, bc_w)),
            const((1, LANES)), const((1, LANES)), const((1, D)), const((1, D)),
        ],
        out_specs=(tile(D, 0), state),
        compiler_params=_cparams(("parallel",), 48),
        name="ssd_short",
    )(p, p, p, p, dense(1, 0, D), dense(2, 0, D), dense(3, 0, D), dense(1, D, D + bc_w), dense(2, D, D + bc_w),
      dense(3, D, D + bc_w), h0, *prm)


def _rwkv_pre_body(rkv_ref, lora_ref, si_rkv_ref, si_lora_ref, mu_rkv_ref, mu_lora_ref,
                   w0_ref, w2_ref, a0_ref, a2_ref, g2_ref, kk_ref, ka_ref, rk_ref,
                   r_out, w_out, k_out, v_out, kk_out, b_out, bonus_out, g_out,
                   rkvbuf, lorabuf, *, q, log_decay, short_len):
    if short_len:
        def shifted(buf, cur_ref, si_ref, mu_ref):
            cur = cur_ref[...]
            first = lax.broadcasted_iota(jnp.int32, cur.shape, 0) % short_len == 0
            prev = jnp.where(first, si_ref[...], pltpu.roll(cur, 1, axis=0))
            return cur + mu_ref[...] * (prev - cur)
    else:
        c = pl.program_id(1)
        _carried_window(rkvbuf, rkv_ref, si_rkv_ref, c == 0, q)
        _carried_window(lorabuf, lora_ref, si_lora_ref, c == 0, q)

        def shifted(buf, cur_ref, si_ref, mu_ref):
            cur = cur_ref[...]
            prev = buf[pl.ds(SUBLANES - 1, q), :]
            return cur + mu_ref[...] * (prev - cur)

    ps = shifted(rkvbuf, rkv_ref, si_rkv_ref, mu_rkv_ref)
    lo_in = shifted(lorabuf, lora_ref, si_lora_ref, mu_lora_ref)
    if not short_len:
        _advance_window(rkvbuf, q)
        _advance_window(lorabuf, q)
    r = ps[:, 0:D]
    k = ps[:, D:2 * D]
    v = ps[:, 2 * D:3 * D]

    lw = jnp.dot(jnp.tanh(lo_in).astype(BF16), w2_ref[...], preferred_element_type=F32)
    la = jnp.dot(lo_in.astype(BF16), a2_ref[...], preferred_element_type=F32)
    g = jnp.dot(jax.nn.sigmoid(lo_in).astype(BF16), g2_ref[...], preferred_element_type=F32)
    wlog = -_softplus(-(w0_ref[...] + lw)) - 0.5
    log_w = -jnp.exp(wlog)
    a = jax.nn.sigmoid(a0_ref[...] + la)
    kkf = k * kk_ref[...]
    norm = jnp.maximum(jnp.sqrt(_head_allsum(kkf * kkf)), 1e-12)
    kk = kkf / norm
    k2 = k * (1.0 + (a - 1.0) * ka_ref[...])
    bonus = _head_allsum(r * k2 * rk_ref[...]) * v
    r_out[...] = r
    w_out[...] = log_w if log_decay else jnp.exp(log_w)
    k_out[...] = k2
    v_out[...] = v
    kk_out[...] = kk
    b_out[...] = kk * a
    bonus_out[...] = bonus
    g_out[...] = g


def _rwkv_pre(p, shift_state, prm, nseq, seq_len, log_decay):
    short_len = seq_len if seq_len < CHUNK else 0
    q = CHUNK if short_len else min(CHUNK, seq_len)
    nc = 1 if short_len else seq_len // q
    ntile = nseq * seq_len // q // nc
    const = lambda shape: pl.BlockSpec(shape, lambda s, c: (0,) * len(shape))
    tile = pl.BlockSpec((q, D), lambda s, c: (s * nc + c, 0))
    if short_len:
        dense = lambda a: jnp.pad(a, ((0, 0), (0, seq_len - 1), (0, 0))).reshape(nseq * seq_len, a.shape[-1])
        si = (dense(shift_state[:, :, :3 * D]), dense(shift_state[:, :, 3 * D:]))
        si_specs = [pl.BlockSpec((q, 3 * D), lambda s, c: (s, 0)), pl.BlockSpec((q, LORA_W), lambda s, c: (s, 0))]
    else:
        si = (_pad_front_rows(shift_state[:, :, :3 * D]), _pad_front_rows(shift_state[:, :, 3 * D:]))
        si_specs = [pl.BlockSpec((1, SUBLANES, 3 * D), lambda s, c: (s, 0, 0)),
                    pl.BlockSpec((1, SUBLANES, LORA_W), lambda s, c: (s, 0, 0))]
    sds = jax.ShapeDtypeStruct((nseq * seq_len, D), F32)
    return pl.pallas_call(
        functools.partial(_rwkv_pre_body, q=q, log_decay=log_decay, short_len=short_len),
        out_shape=(sds,) * 8,
        grid=(ntile, nc),
        in_specs=[
            pl.BlockSpec((q, 3 * D), lambda s, c: (s * nc + c, COL_RKV // (3 * D))),
            pl.BlockSpec((q, LORA_W), lambda s, c: (s * nc + c, COL_LORA // LORA_W)),
            *si_specs,
            const((1, 3 * D)), const((1, LORA_W)),
            const((1, D)), const((LORA_W, D)), const((1, D)), const((LORA_W, D)), const((LORA_W, D)),
            const((1, D)), const((1, D)), const((1, D)),
        ],
        out_specs=(tile,) * 8,
        scratch_shapes=[pltpu.VMEM((CHUNK + SUBLANES, 3 * D), F32), pltpu.VMEM((CHUNK + SUBLANES, LORA_W), F32)],
        compiler_params=_cparams(("parallel", "arbitrary"), 48),
        name="rwkv_pre",
    )(p, p, *si, *prm)


def _rwkv_out(o, bonus, g, ln_w, ln_b):
    mean = _head_allsum(o) * (1.0 / HEAD)
    cen = o - mean
    var = _head_allsum(cen * cen) * (1.0 / HEAD)
    return (cen * lax.rsqrt(var + GN_EPS) * ln_w + ln_b + bonus) * g


def _split_bf16(x):
    hi = x.astype(BF16)
    return hi, (x - hi.astype(F32)).astype(BF16)


_NN = (((1,), (0,)), ((), ()))
_NT = (((1,), (1,)), ((), ()))
_TN = (((0,), (0,)), ((), ()))


def _solve_unit_lower(n, rhs, lower_left):
    rows, width = rhs.shape
    half = rows // 2
    nblk, ncol = rows // SUBLANES, width // LANES
    tiles = lambda a, r0, r1: [[a[SUBLANES * i:SUBLANES * (i + 1), LANES * j:LANES * (j + 1)] for j in range(ncol)]
                               for i in range(r0 // SUBLANES, r1 // SUBLANES)]
    nb = tiles(n, 0, rows)
    xb = tiles(rhs, 0, rows)
    low = lax.broadcasted_iota(jnp.int32, (SUBLANES, LANES), 1) < HEAD

    def substitute(lo, hi):
        for s in range(lo, hi - 1):
            i0, r0 = divmod(s, SUBLANES)
            idx = jnp.where(low, s, HEAD + s)
            for j in range(ncol):
                row = xb[i0][j][r0:r0 + 1, :]
                for i in range(i0 if r0 < SUBLANES - 1 else i0 + 1, hi // SUBLANES):
                    xb[i][j] = xb[i][j] - jnp.take_along_axis(nb[i][j], idx, axis=1) * row

    join = lambda blocks: jnp.concatenate([jnp.concatenate(xr, axis=1) for xr in blocks], axis=0)
    substitute(0, half)
    corr = tiles(lower_left(join(xb[:half // SUBLANES])), 0, half)
    for i in range(half // SUBLANES):
        for j in range(ncol):
            xb[half // SUBLANES + i][j] = xb[half // SUBLANES + i][j] - corr[i][j]
    substitute(half, rows)
    return join(xb)


def _wkv_chunk(r, lw, k, v, kk, b):
    rows, width = r.shape
    nh = WKV_LANES // HEAD
    groups = [slice(g * WKV_LANES, (g + 1) * WKV_LANES) for g in range(width // WKV_LANES)]
    cl = _cumsum_rows(lw)
    cl_last = cl[rows - 1:rows, :]
    p_inv = jnp.exp(-cl)
    p_end = jnp.exp(cl_last - cl)
    x2h, x2l = _split_bf16(jnp.concatenate([kk * jnp.exp(cl - lw), r * jnp.exp(cl)], axis=0))
    k_hat = k * p_inv
    b_hat = b * p_inv
    k_end = k * p_end
    b_end = -(b * p_end)

    bd_r = lax.broadcasted_iota(jnp.int32, (nh * rows, WKV_LANES), 0) // rows
    bd_c = lax.broadcasted_iota(jnp.int32, (nh * rows, WKV_LANES), 1) // HEAD
    bd_mask = jnp.where(bd_r == bd_c, 1.0, 0.0).astype(BF16)

    def per_head_rows(y):
        return [jnp.concatenate([part] * nh, axis=0) * bd_mask for part in _split_bf16(y)]

    def dot3(ah, al, bh, bl, dims):
        dg = lambda x, y: lax.dot_general(x, y, dims, preferred_element_type=F32)
        return dg(ah, bh) + (dg(ah, bl) + dg(al, bh))

    t_i = lax.broadcasted_iota(jnp.int32, (rows, WKV_LANES), 0)
    lane = lax.broadcasted_iota(jnp.int32, (rows, WKV_LANES), 1)
    s_i = lane & (HEAD - 1)
    strict = t_i > s_i
    incl = t_i >= s_i
    head_of_lane = lane // HEAD

    a_kb, a_rb, av = [], [], []
    for gs in groups:
        ak = dot3(x2h[:, gs], x2l[:, gs], *per_head_rows(k_hat[:, gs]), _NT)
        ab = dot3(x2h[:, gs], x2l[:, gs], *per_head_rows(b_hat[:, gs]), _NT)
        a_k = jnp.concatenate([jnp.where(strict, ak[0:rows], 0.0), jnp.where(incl, ak[rows:], 0.0)], axis=0)
        av.append(dot3(*_split_bf16(a_k), *per_head_rows(v[:, gs]), _NN))
        a_kb.append(jnp.where(strict, ab[0:rows], 0.0))
        a_rb.append(jnp.where(incl, ab[rows:], 0.0))
    a_kb_all = jnp.concatenate(a_kb, axis=1)
    half = rows // 2
    top_cols = (lax.broadcasted_iota(jnp.int32, (half, WKV_LANES), 1) & (HEAD - 1)) < half

    def lower_left(x_top):
        x_pad = jnp.concatenate([x_top, jnp.zeros_like(x_top)], axis=0)
        parts = []
        for g, gs in enumerate(groups):
            lhs = jnp.where(top_cols, a_kb[g][half:], 0.0)
            parts.append(dot3(*_split_bf16(lhs), *per_head_rows(x_pad[:, gs]), _NN))
        return jnp.concatenate(parts, axis=1)

    upd_lhs = [_split_bf16(jnp.concatenate([k_end[:, gs], b_end[:, gs],
                                            jnp.where(t_i == s_i, jnp.exp(cl_last[:, gs]), 0.0)], axis=0))
               for gs in groups]

    def advance(st):
        xs = [dot3(x2h[:, gs], x2l[:, gs], *per_head_rows(st[:, gs]), _NN) for gs in groups]
        base = jnp.concatenate([xs[g][0:rows] + av[g][0:rows] for g in range(len(groups))], axis=1)
        sa = _solve_unit_lower(a_kb_all, base, lower_left)
        o, st_new = [], []
        for g, gs in enumerate(groups):
            o.append(xs[g][rows:] + av[g][rows:] - dot3(*_split_bf16(a_rb[g]), *per_head_rows(sa[:, gs]), _NN))
            rhs = jnp.concatenate([v[:, gs], sa[:, gs], st[:, gs]], axis=0)
            full = dot3(*upd_lhs[g], *_split_bf16(rhs), _TN)
            acc = jnp.where(head_of_lane == 0, full[0:HEAD, :], 0.0)
            for h in range(1, nh):
                acc = acc + jnp.where(head_of_lane == h, full[h * HEAD:(h + 1) * HEAD, :], 0.0)
            st_new.append(acc)
        return jnp.concatenate(o, axis=1), jnp.concatenate(st_new, axis=1)

    return advance


def _wkv_long_body(r_ref, lw_ref, k_ref, v_ref, kk_ref, b_ref, bonus_ref, g_ref, lnw_ref, lnb_ref,
                   y_ref, so_ref, st_s, *, nc):
    c = pl.program_id(2)

    @pl.when(c == 0)
    def _():
        st_s[...] = jnp.zeros_like(st_s)

    chunks = [slice(i * WKV_CHUNK, (i + 1) * WKV_CHUNK) for i in range(r_ref.shape[0] // WKV_CHUNK)]
    advances = [_wkv_chunk(r_ref[cs, :], lw_ref[cs, :], k_ref[cs, :], v_ref[cs, :], kk_ref[cs, :], b_ref[cs, :])
                for cs in chunks]
    st = st_s[...]
    for cs, advance in zip(chunks, advances):
        o, st = advance(st)
        y_ref[cs, :] = _rwkv_out(o, bonus_ref[cs, :], g_ref[cs, :], lnw_ref[...], lnb_ref[...])
    st_s[...] = st

    @pl.when(c == nc - 1)
    def _():
        for j in range(WKV_GROUPS * WKV_LANES // LANES):
            js = slice(j * LANES, (j + 1) * LANES)
            sq = jnp.concatenate([st_s[:, js], jnp.zeros((LANES - HEAD, LANES), F32)], axis=0)
            so_ref[0, js, :] = sq.T[:, 0:HEAD]


def _wkv_long(r, lw, k, v, kk, b, bonus, g, ln_w, ln_b, nseq, seq_len):
    tile_rows = WKV_TILE_CHUNKS * WKV_CHUNK
    nc = seq_len // tile_rows
    width = WKV_GROUPS * WKV_LANES
    tile = pl.BlockSpec((tile_rows, width), lambda s, hg, c: (s * nc + c, hg))
    vec = pl.BlockSpec((1, width), lambda s, hg, c: (0, hg))
    y, s_last = pl.pallas_call(
        functools.partial(_wkv_long_body, nc=nc),
        out_shape=(jax.ShapeDtypeStruct((nseq * seq_len, D), F32),
                   jax.ShapeDtypeStruct((nseq, D, HEAD), F32)),
        grid=(nseq, D // width, nc),
        in_specs=[tile] * 8 + [vec, vec],
        out_specs=(tile, pl.BlockSpec((1, width, HEAD), lambda s, hg, c: (s, hg, 0))),
        scratch_shapes=[pltpu.VMEM((HEAD, width), F32)],
        compiler_params=_cparams(("parallel", "parallel", "arbitrary"), 48),
        name="wkv_long",
    )(r, lw, k, v, kk, b, bonus, g, ln_w, ln_b)
    return y, s_last


def _wkv_short_body(r_ref, w_ref, k_ref, v_ref, kk_ref, b_ref, s0_ref, o_ref, so_ref, *, steps):
    def per_v_group(i, carry):
        vis = [i * WKV_SHORT_ROWS + u for u in range(WKV_SHORT_ROWS)]
        ss = [s0_ref[0, vi] for vi in vis]
        for t in range(steps):
            kk_t, w_t, b_t, k_t, r_t = kk_ref[t, 0], w_ref[t, 0], b_ref[t, 0], k_ref[t, 0], r_ref[t, 0]
            for u, vi in enumerate(vis):
                vrow = v_ref[t, 0, pl.ds(vi, 1), :]
                sa = jnp.sum(ss[u] * kk_t, axis=0, keepdims=True)
                ss[u] = ss[u] * w_t - b_t * sa + k_t * vrow
                o_ref[t, 0, pl.ds(vi, 1), :] = jnp.sum(ss[u] * r_t, axis=0, keepdims=True)
        for u, vi in enumerate(vis):
            so_ref[0, vi] = ss[u]
        return carry

    lax.fori_loop(0, HEAD // WKV_SHORT_ROWS, per_v_group, 0)


def _to_lanes_body(*refs, n_in, seq_len):
    nseq = refs[0].shape[0] // seq_len
    for src, dst in zip(refs[:n_in], refs[n_in:]):
        for t in range(seq_len):
            xt = src[pl.ds(t, nseq, stride=seq_len), :].T
            dst[t, 0] = xt[0:HEAD, :]
            dst[t, 1] = xt[HEAD:2 * HEAD, :]


def _to_lanes(arrays, nseq, seq_len):
    n = len(arrays)
    return pl.pallas_call(
        functools.partial(_to_lanes_body, n_in=n, seq_len=seq_len),
        out_shape=(jax.ShapeDtypeStruct((seq_len, N_HEADS, HEAD, nseq), F32),) * n,
        grid=(D // LANES,),
        in_specs=[pl.BlockSpec((nseq * seq_len, LANES), lambda hp: (0, hp))] * n,
        out_specs=(pl.BlockSpec((seq_len, 2, HEAD, nseq), lambda hp: (0, hp, 0, 0)),) * n,
        compiler_params=_cparams(("parallel",), 32),
        name="to_lanes",
    )(*arrays)


def _wkv_short(r, w, k, v, kk, b, s0, nseq, seq_len):
    vec = pl.BlockSpec((seq_len, 1, HEAD, nseq), lambda h: (0, h, 0, 0))
    st = pl.BlockSpec((1, HEAD, HEAD, nseq), lambda h: (h, 0, 0, 0))
    return pl.pallas_call(
        functools.partial(_wkv_short_body, steps=seq_len),
        out_shape=(jax.ShapeDtypeStruct((seq_len, N_HEADS, HEAD, nseq), F32),
                   jax.ShapeDtypeStruct((N_HEADS, HEAD, HEAD, nseq), F32)),
        grid=(N_HEADS,),
        in_specs=[vec] * 6 + [st],
        out_specs=(vec, st),
        compiler_params=_cparams(("parallel",), 32),
        name="wkv_short",
    )(r, w, k, v, kk, b, s0)


def _rwkv_post_body(o_ref, bonus_ref, g_ref, lnw_ref, lnb_ref, y_ref):
    y_ref[...] = _rwkv_out(o_ref[...], bonus_ref[...], g_ref[...], lnw_ref[...], lnb_ref[...])


def _rwkv_post(o, bonus, g, ln_w, ln_b):
    m = o.shape[0]
    tile = pl.BlockSpec((TM_POST, D), lambda i: (i, 0))
    const = pl.BlockSpec((1, D), lambda i: (0, 0))
    return pl.pallas_call(
        _rwkv_post_body,
        out_shape=jax.ShapeDtypeStruct((m, D), F32),
        grid=(m // TM_POST,),
        in_specs=[tile, tile, tile, const, const],
        out_specs=tile,
        compiler_params=_cparams(("parallel",), 32),
        name="rwkv_post",
    )(o, bonus, g, ln_w, ln_b)


def _lru_coeffs(xc, wga_ref, bga_ref, wgx_ref, bgx_ref, lam_ref):
    ra, rx = [], []
    for blk in range(LRU_BLOCKS):
        xh = xc[:, blk * LRU_BLK:(blk + 1) * LRU_BLK].astype(BF16)
        ra.append(jnp.dot(xh, wga_ref[blk], preferred_element_type=F32))
        rx.append(jnp.dot(xh, wgx_ref[blk], preferred_element_type=F32))
    rg = jax.nn.sigmoid(jnp.concatenate(ra, axis=1) + bga_ref[...])
    ig = jax.nn.sigmoid(jnp.concatenate(rx, axis=1) + bgx_ref[...])
    log_a = -LRU_C * rg * _softplus(-lam_ref[...])
    return jnp.exp(log_a), jnp.sqrt(1.0 - jnp.exp(2.0 * log_a)) * (ig * xc)


def _lru_body(gate_ref, x_ref, ci_ref, h0_ref, cw_ref, cb_ref, wga_ref, bga_ref, wgx_ref, bgx_ref, lam_ref,
              y_ref, ho_ref, xbuf, a_s, b_s, h_s, hc_s, *, q, nc):
    c = pl.program_id(1)
    _carried_window(xbuf, x_ref, ci_ref, c == 0, q)

    @pl.when(c == 0)
    def _():
        hc_s[...] = h0_ref[0]

    xc = _causal_conv(xbuf, cw_ref, cb_ref, q)
    _advance_window(xbuf, q)
    a_s[...], b_s[...] = _lru_coeffs(xc, wga_ref, bga_ref, wgx_ref, bgx_ref, lam_ref)

    def step(t, h):
        h = a_s[pl.ds(t, 1), :] * h + b_s[pl.ds(t, 1), :]
        h_s[pl.ds(t, 1), :] = h
        return h

    h = lax.fori_loop(0, q, step, hc_s[...])
    hc_s[...] = h
    y_ref[...] = h_s[...] * jax.nn.gelu(gate_ref[...])

    @pl.when(c == nc - 1)
    def _():
        ho_ref[0] = h


def _lru(p, conv_init, h0, prm, nseq, seq_len):
    q = LRU_TILE_ROWS
    nc = seq_len // q
    per_seq = lambda shape: pl.BlockSpec((1,) + shape, lambda s, c: (s,) + (0,) * len(shape))
    const = lambda shape: pl.BlockSpec(shape, lambda s, c: (0,) * len(shape))
    return pl.pallas_call(
        functools.partial(_lru_body, q=q, nc=nc),
        out_shape=(jax.ShapeDtypeStruct((nseq * seq_len, D), F32),
                   jax.ShapeDtypeStruct((nseq, 1, D), F32)),
        grid=(nseq, nc),
        in_specs=[
            pl.BlockSpec((q, D), lambda s, c: (s * nc + c, 0)),
            pl.BlockSpec((q, D), lambda s, c: (s * nc + c, 1)),
            per_seq((SUBLANES, D)), per_seq((1, D)),
            const((CONV_W, D)), const((1, D)),
            const((LRU_BLOCKS, LRU_BLK, LRU_BLK)), const((1, D)),
            const((LRU_BLOCKS, LRU_BLK, LRU_BLK)), const((1, D)), const((1, D)),
        ],
        out_specs=(pl.BlockSpec((q, D), lambda s, c: (s * nc + c, 0)), per_seq((1, D))),
        scratch_shapes=[pltpu.VMEM((q + SUBLANES, D), F32), pltpu.VMEM((q, D), F32), pltpu.VMEM((q, D), F32),
                        pltpu.VMEM((q, D), F32), pltpu.VMEM((1, D), F32)],
        compiler_params=_cparams(("parallel", "arbitrary"), 32),
        name="lru",
    )(p, p, conv_init, h0, *prm)


def _lru_short_body(gate_ref, x_ref, p1_ref, p2_ref, p3_ref, h0_ref, cw_ref, cb_ref, wga_ref, bga_ref, wgx_ref,
                    bgx_ref, lam_ref, y_ref, h_ref, *, seq_len):
    cur = x_ref[...]
    t = lax.broadcasted_iota(jnp.int32, cur.shape, 0) % seq_len
    prev = lambda s, ref: jnp.where(t >= s, pltpu.roll(cur, s, axis=0), ref[...])
    xc = (cb_ref[...] + cw_ref[0:1, :] * prev(3, p3_ref) + cw_ref[1:2, :] * prev(2, p2_ref)
          + cw_ref[2:3, :] * prev(1, p1_ref) + cw_ref[3:4, :] * cur)
    a, b = _lru_coeffs(xc, wga_ref, bga_ref, wgx_ref, bgx_ref, lam_ref)
    s = 1
    while s < seq_len:
        keep = t >= s
        b = jnp.where(keep, a * pltpu.roll(b, s, axis=0) + b, b)
        a = jnp.where(keep, a * pltpu.roll(a, s, axis=0), a)
        s *= 2
    h = a * h0_ref[...] + b
    h_ref[...] = h
    y_ref[...] = h * jax.nn.gelu(gate_ref[...])


def _lru_short(p, conv_state, h0, prm, nseq, seq_len):
    rows = nseq * seq_len
    dense = lambda s: jnp.pad(conv_state[:, CONV_W - 1 - s:, :], ((0, 0), (0, seq_len - s), (0, 0))).reshape(rows, D)
    h0_rows = jnp.repeat(h0, seq_len, axis=0)
    tile = lambda col: pl.BlockSpec((CHUNK, D), lambda i: (i, col))
    const = lambda shape: pl.BlockSpec(shape, lambda i: (0,) * len(shape))
    sds = jax.ShapeDtypeStruct((rows, D), F32)
    return pl.pallas_call(
        functools.partial(_lru_short_body, seq_len=seq_len),
        out_shape=(sds, sds),
        grid=(rows // CHUNK,),
        in_specs=[tile(0), tile(1), tile(0), tile(0), tile(0), tile(0),
                  const((CONV_W, D)), const((1, D)),
                  const((LRU_BLOCKS, LRU_BLK, LRU_BLK)), const((1, D)),
                  const((LRU_BLOCKS, LRU_BLK, LRU_BLK)), const((1, D)), const((1, D))],
        out_specs=(tile(0), tile(0)),
        compiler_params=_cparams(("parallel",), 32),
        name="lru_short",
    )(p, p, dense(1), dense(2), dense(3), h0_rows, *prm)


def _pad_front_rows(buf):
    return jnp.pad(buf, ((0, 0), (SUBLANES - buf.shape[1], 0), (0, 0)))


def _row2(v):
    return v.reshape(1, -1)


def kernel(x_prompt, x_sample, state_ssm_a, state_conv_a, state_wkv_b, state_shift_b, state_lru_c, state_conv_c, norm_gain, w_ffn_in, w_ffn_out, w_in_ab, conv_w_a, conv_b_a, dt_bias_a, a_log_a, d_skip_a, gnorm_a, mu_b, w0_b, w2_b, a0_b, a2_b, g2_b, k_k_b, k_a_b, r_k_b, ln_w_b, ln_b_b, w_out_ab, w_in_c, conv_w_c, conv_b_c, w_gate_a_c, b_gate_a_c, w_gate_x_c, b_gate_x_c, lambda_c, w_out_c, final_norm_gain):
    w_gu, w_dn = _ffn_weights(w_ffn_in, w_ffn_out)

    in_a = D + (D + BC_W) + N_HEADS
    bc_w = BC_W
    segments = (
        (in_a, 3 * D),
        (0, D),
        (D, D),
        (2 * D, BC_W),
        (in_a + 3 * D, LORA_W),
        (in_a - N_HEADS, N_HEADS),
    )
    w_proj0 = _reorder_cast_cols(w_in_ab, segments, PROJ_W)
    w_proj1 = w_in_c[0].astype(BF16)
    w_out0 = w_out_ab[0].astype(BF16)
    w_out1 = w_out_c[0].astype(BF16)

    pad_lanes = lambda v: jnp.pad(v.reshape(1, -1), ((0, 0), (0, LANES - v.shape[-1])))
    rep_head = lambda v: jnp.repeat(v, HEAD).reshape(1, D)
    ssd_prm = (conv_w_a[0][:, :D], conv_w_a[0][:, D:], _row2(conv_b_a[0][:D]), _row2(conv_b_a[0][D:]),
               pad_lanes(dt_bias_a[0]), pad_lanes(a_log_a[0]), rep_head(d_skip_a[0]), _row2(gnorm_a[0]))
    lora_rows = lambda w, lo: jnp.pad(w, ((lo, LORA_W - lo - w.shape[0]), (0, 0))).astype(BF16)
    mu = mu_b[0]
    rwkv_prm = (_row2(mu[:3 * D]), _row2(mu[3 * D:]),
                _row2(w0_b[0]), lora_rows(w2_b[0], 0), _row2(a0_b[0]), lora_rows(a2_b[0], 64),
                lora_rows(g2_b[0], 128), _row2(k_k_b[0]), _row2(k_a_b[0]), _row2(r_k_b[0]))
    lru_prm = (conv_w_c[0], _row2(conv_b_c[0]), w_gate_a_c[0].astype(BF16), _row2(b_gate_a_c[0]),
               w_gate_x_c[0].astype(BF16), _row2(b_gate_x_c[0]), _row2(lambda_c[0]))

    def trunk(x3, ssm0, conva0, wkv0, shift0, lru0, convc0):
        nseq, seq_len, _ = x3.shape
        x = x3.reshape(nseq * seq_len, D)
        tail = lambda arr, n: arr.reshape(nseq, seq_len, arr.shape[-1])[:, seq_len - n:, :]

        x = _ffn(x, _row2(norm_gain[0, 0]), w_gu, w_dn, (0, 0))
        p = _proj(x, _row2(norm_gain[0, 1]), w_proj0)
        if seq_len >= CHUNK:
            ya, ssm_n = _ssd(p, _pad_front_rows(conva0[:, :, :D]), _pad_front_rows(conva0[:, :, D:]),
                             ssm0.reshape(nseq, D, N_STATE), ssd_prm, nseq, seq_len)
        else:
            ya, ssm_n = _ssd_short(p, conva0, ssm0.reshape(nseq, D, N_STATE), ssd_prm, nseq, seq_len)
        p_tail = tail(p, CONV_W - 1)
        conva_n = jnp.concatenate([p_tail[:, :, COL_X:COL_X + D], p_tail[:, :, COL_BC:COL_BC + bc_w]], axis=-1)
        shift_n = jnp.concatenate([p_tail[:, -1:, :3 * D], p_tail[:, -1:, COL_LORA:COL_LORA + LORA_W]], axis=-1)

        r, w, k, v, kk, b, bonus, g = _rwkv_pre(p, shift0, rwkv_prm, nseq, seq_len, log_decay=wkv0 is None)
        if wkv0 is None:
            yb, wkv_n = _wkv_long(r, w, k, v, kk, b, bonus, g, _row2(ln_w_b[0]), _row2(ln_b_b[0]), nseq, seq_len)
            wkv_n = wkv_n.reshape(nseq, N_HEADS, HEAD, HEAD)
        else:
            o, wkv_n = _wkv_short(*_to_lanes((r, w, k, v, kk, b), nseq, seq_len), wkv0.transpose(1, 2, 3, 0),
                                  nseq, seq_len)
            o = o.transpose(3, 0, 1, 2).reshape(nseq * seq_len, D)
            wkv_n = wkv_n.transpose(3, 0, 1, 2)
            yb = _rwkv_post(o, bonus, g, _row2(ln_w_b[0]), _row2(ln_b_b[0]))
        x = _out_proj(x, (ya, yb), (w_out0[:D], w_out0[D:]))
        x = _ffn(x, _row2(norm_gain[0, 2]), w_gu, w_dn, (0, 1))

        x = _ffn(x, _row2(norm_gain[1, 0]), w_gu, w_dn, (1, 0))
        pc = _proj(x, _row2(norm_gain[1, 1]), w_proj1)
        if seq_len >= CHUNK:
            yc, lru_n = _lru(pc, _pad_front_rows(convc0), lru0.reshape(nseq, 1, D), lru_prm, nseq, seq_len)
        else:
            yc, h_rows = _lru_short(pc, convc0, lru0, lru_prm, nseq, seq_len)
            lru_n = tail(h_rows, 1)
        convc_n = tail(pc, CONV_W - 1)[:, :, D:]
        x = _out_proj(x, (yc,), (w_out1,))
        y = _ffn(x, _row2(norm_gain[1, 2]), w_gu, w_dn, (1, 1), final_gain=_row2(final_norm_gain))

        return (y.reshape(nseq, seq_len, D), ssm_n.reshape(1, nseq, N_HEADS, HEAD, N_STATE), conva_n[None],
                wkv_n[None], shift_n[None], lru_n.reshape(1, nseq, D), convc_n[None])

    bp = x_prompt.shape[0]
    zeros = lambda s: jnp.zeros((bp,) + s.shape[2:], F32)
    outs_p = trunk(x_prompt, zeros(state_ssm_a), zeros(state_conv_a), None, zeros(state_shift_b),
                   zeros(state_lru_c), zeros(state_conv_c))
    outs_s = trunk(x_sample, state_ssm_a[0], state_conv_a[0], state_wkv_b[0], state_shift_b[0],
                   state_lru_c[0], state_conv_c[0])
    return (outs_p[0], outs_s[0]) + outs_p[1:] + outs_s[1:]
```

```python
import functools

import jax
import jax.numpy as jnp
from jax import lax
from jax.experimental import pallas as pl
from jax.experimental.pallas import tpu as pltpu

F32 = jnp.float32
BF16 = jnp.bfloat16

D = 2048
D_FF = 5504
D_FF_PAD = 5632
HEAD = 64
N_HEADS = 32
N_GROUPS = 4
GROUP_W = D // N_GROUPS
N_STATE = 128
HEADS_PER_GROUP = N_HEADS // N_GROUPS
BC_W = 2 * N_GROUPS * N_STATE
CONV_W = 4
LORA_W = 256
LRU_BLOCKS = 8
LRU_BLK = D // LRU_BLOCKS
LRU_C = 8.0
EPS = 1e-6
GN_EPS = 64e-5
SUBLANES = 8
LANES = 128
CHUNK = 128
WKV_CHUNK = 64
WKV_LANES = 256
WKV_GROUPS = 4
WKV_TILE_CHUNKS = 4
WKV_SHORT_ROWS = 4
SSD_SHORT_SEQS = 8
SSD_TILE_CHUNKS = 2
LRU_TILE_ROWS = 256

COL_RKV = 0
COL_Z = 3 * D
COL_X = 4 * D
COL_BC = 5 * D
COL_LORA = 5 * D + 2 * N_GROUPS * N_STATE
COL_DT = COL_LORA + LORA_W
PROJ_W = COL_DT + 256

TM_FFN = 512
TF_FFN = 512
TM_PROJ = 1024
TN_PROJ = 512
TM_OUT = 512
TN_OUT = 1024
TM_POST = 256
W_IN_ROWS = 64
W_OUT_COLS = 256


def _cparams(sem, vmem_mib):
    return pltpu.CompilerParams(dimension_semantics=sem, vmem_limit_bytes=vmem_mib * 1024 * 1024)


def _softplus(x):
    return jnp.maximum(x, 0.0) + jnp.log(1.0 + jnp.exp(-jnp.abs(x)))


def _silu(x):
    return x * jax.nn.sigmoid(x)


def _rms(x, gain):
    ms = jnp.mean(x * x, axis=-1, keepdims=True)
    return x * lax.rsqrt(ms + EPS) * gain


def _ffn_body(x_ref, g_ref, wg_ref, wu_ref, wo_ref, fg_ref, o_ref, xn_ref, acc_ref, *, nf, final):
    f = pl.program_id(1)

    @pl.when(f == 0)
    def _():
        xn_ref[...] = _rms(x_ref[...], g_ref[...]).astype(BF16)
        acc_ref[...] = jnp.zeros_like(acc_ref)

    xn = xn_ref[...]
    gate = jnp.dot(xn, wg_ref[...], preferred_element_type=F32)
    up = jnp.dot(xn, wu_ref[...], preferred_element_type=F32)
    h = (_silu(gate) * up).astype(BF16)
    acc_ref[...] += jnp.dot(h, wo_ref[...], preferred_element_type=F32)

    @pl.when(f == nf - 1)
    def _():
        y = x_ref[...] + 0.5 * acc_ref[...]
        if final:
            y = _rms(y, fg_ref[...])
        o_ref[...] = y


def _ffn(x, gain, w_gu, w_dn, which, final_gain=None):
    m = x.shape[0]
    li, si = which
    nf = D_FF_PAD // TF_FFN
    final = final_gain is not None
    fg = final_gain if final else gain
    return pl.pallas_call(
        functools.partial(_ffn_body, nf=nf, final=final),
        out_shape=jax.ShapeDtypeStruct((m, D), F32),
        grid=(m // TM_FFN, nf),
        in_specs=[
            pl.BlockSpec((TM_FFN, D), lambda i, f: (i, 0)),
            pl.BlockSpec((1, D), lambda i, f: (0, 0)),
            pl.BlockSpec((None, None, D, TF_FFN), lambda i, f: (li, si, 0, f)),
            pl.BlockSpec((None, None, D, TF_FFN), lambda i, f: (li, si, 0, f + nf)),
            pl.BlockSpec((None, None, TF_FFN, D), lambda i, f: (li, si, f, 0)),
            pl.BlockSpec((1, D), lambda i, f: (0, 0)),
        ],
        out_specs=pl.BlockSpec((TM_FFN, D), lambda i, f: (i, 0)),
        scratch_shapes=[pltpu.VMEM((TM_FFN, D), BF16), pltpu.VMEM((TM_FFN, D), F32)],
        compiler_params=_cparams(("parallel", "arbitrary"), 48),
        name="ffn",
    )(x, gain, w_gu, w_gu, w_dn, fg)


def _proj_body(x_ref, g_ref, w_ref, o_ref, xn_ref):
    @pl.when(pl.program_id(1) == 0)
    def _():
        xn_ref[...] = _rms(x_ref[...], g_ref[...]).astype(BF16)

    o_ref[...] = jnp.dot(xn_ref[...], w_ref[...], preferred_element_type=F32)


def _proj(x, gain, w):
    m = x.shape[0]
    n = w.shape[1]
    tn = 2 * TN_PROJ if n % (2 * TN_PROJ) == 0 else TN_PROJ
    return pl.pallas_call(
        _proj_body,
        out_shape=jax.ShapeDtypeStruct((m, n), F32),
        grid=(m // TM_PROJ, n // tn),
        in_specs=[
            pl.BlockSpec((TM_PROJ, D), lambda i, j: (i, 0)),
            pl.BlockSpec((1, D), lambda i, j: (0, 0)),
            pl.BlockSpec((D, tn), lambda i, j: (0, j)),
        ],
        out_specs=pl.BlockSpec((TM_PROJ, tn), lambda i, j: (i, j)),
        scratch_shapes=[pltpu.VMEM((TM_PROJ, D), BF16)],
        compiler_params=_cparams(("parallel", "arbitrary"), 40),
        name="proj",
    )(x, gain, w)


def _out2_body(res_ref, ya_ref, yb_ref, wa_ref, wb_ref, o_ref):
    acc = jnp.dot(ya_ref[...].astype(BF16), wa_ref[...], preferred_element_type=F32)
    acc = acc + jnp.dot(yb_ref[...].astype(BF16), wb_ref[...], preferred_element_type=F32)
    o_ref[...] = res_ref[...] + acc


def _out1_body(res_ref, y_ref, w_ref, o_ref):
    o_ref[...] = res_ref[...] + jnp.dot(y_ref[...].astype(BF16), w_ref[...], preferred_element_type=F32)


def _out_proj(res, ys, ws):
    m = res.shape[0]
    body = _out2_body if len(ys) == 2 else _out1_body
    y_spec = pl.BlockSpec((TM_OUT, D), lambda j, i: (i, 0))
    w_spec = pl.BlockSpec((D, TN_OUT), lambda j, i: (0, j))
    r_spec = pl.BlockSpec((TM_OUT, TN_OUT), lambda j, i: (i, j))
    return pl.pallas_call(
        body,
        out_shape=jax.ShapeDtypeStruct((m, D), F32),
        grid=(D // TN_OUT, m // TM_OUT),
        in_specs=[r_spec] + [y_spec] * len(ys) + [w_spec] * len(ws),
        out_specs=r_spec,
        compiler_params=_cparams(("arbitrary", "arbitrary"), 48),
        name="out_proj",
    )(res, *ys, *ws)


def _cast_pad_rows_body(w_ref, o_ref):
    o_ref[0:D_FF, :] = w_ref[...].astype(BF16)
    o_ref[D_FF:, :] = jnp.zeros((D_FF_PAD - D_FF, o_ref.shape[1]), BF16)


def _cast_pad_halves_body(w_ref, o_ref):
    rows = o_ref.shape[0]
    zeros = jnp.zeros((rows, D_FF_PAD - D_FF), BF16)
    for half in range(2):
        o_ref[:, half * D_FF_PAD:half * D_FF_PAD + D_FF] = w_ref[:, half * D_FF:(half + 1) * D_FF].astype(BF16)
        o_ref[:, half * D_FF_PAD + D_FF:(half + 1) * D_FF_PAD] = zeros


def _reorder_cast_body(w_ref, o_ref, *, segments):
    col = 0
    for src, width in segments:
        o_ref[:, col:col + width] = w_ref[:, src:src + width].astype(BF16)
        col += width
    o_ref[:, col:] = jnp.zeros((o_ref.shape[0], o_ref.shape[1] - col), BF16)


def _reorder_cast_cols(w, segments, out_cols):
    _, rows, cols = w.shape
    return pl.pallas_call(
        functools.partial(_reorder_cast_body, segments=segments),
        out_shape=jax.ShapeDtypeStruct((rows, out_cols), BF16),
        grid=(rows // W_IN_ROWS,),
        in_specs=[pl.BlockSpec((None, W_IN_ROWS, cols), lambda r: (0, r, 0))],
        out_specs=pl.BlockSpec((W_IN_ROWS, out_cols), lambda r: (r, 0)),
        compiler_params=_cparams(("parallel",), 32),
        name="reorder_cast",
    )(w)


def _ffn_weights(w_in, w_out):
    n_layers, n_slots = w_in.shape[:2]
    w_gu = pl.pallas_call(
        _cast_pad_halves_body,
        out_shape=jax.ShapeDtypeStruct((n_layers, n_slots, D, 2 * D_FF_PAD), BF16),
        grid=(n_layers * n_slots, D // W_IN_ROWS),
        in_specs=[pl.BlockSpec((None, None, W_IN_ROWS, 2 * D_FF), lambda q, r: (q // n_slots, q % n_slots, r, 0))],
        out_specs=pl.BlockSpec((None, None, W_IN_ROWS, 2 * D_FF_PAD), lambda q, r: (q // n_slots, q % n_slots, r, 0)),
        compiler_params=_cparams(("parallel", "parallel"), 32),
        name="ffn_w_in",
    )(w_in)
    w_dn = pl.pallas_call(
        _cast_pad_rows_body,
        out_shape=jax.ShapeDtypeStruct((n_layers, n_slots, D_FF_PAD, D), BF16),
        grid=(n_layers * n_slots, D // W_OUT_COLS),
        in_specs=[pl.BlockSpec((None, None, D_FF, W_OUT_COLS), lambda q, c: (q // n_slots, q % n_slots, 0, c))],
        out_specs=pl.BlockSpec((None, None, D_FF_PAD, W_OUT_COLS), lambda q, c: (q // n_slots, q % n_slots, 0, c)),
        compiler_params=_cparams(("parallel", "parallel"), 32),
        name="ffn_w_out",
    )(w_out)
    return w_gu, w_dn


def _carried_window(buf, cur_ref, init_ref, first, q):
    @pl.when(first)
    def _():
        buf[0:SUBLANES, :] = init_ref[0]

    buf[SUBLANES:SUBLANES + q, :] = cur_ref[...]


def _advance_window(buf, q):
    tail = buf[q:q + SUBLANES, :]
    buf[0:SUBLANES, :] = tail


def _causal_conv(buf, cw_ref, cb_ref, q, off=0):
    first = off + SUBLANES - CONV_W + 1
    acc = cb_ref[...] + cw_ref[0:1, :] * buf[pl.ds(first, q), :]
    for k in range(1, CONV_W):
        acc = acc + cw_ref[k:k + 1, :] * buf[pl.ds(first + k, q), :]
    return acc


def _cumsum_rows(x):
    n = x.shape[0]
    row = lax.broadcasted_iota(jnp.int32, x.shape, 0)
    s = 1
    while s < n:
        x = x + jnp.where(row >= s, pltpu.roll(x, s, axis=0), 0.0)
        s *= 2
    return x


def _expand_heads(a, rows):
    lane = lax.broadcasted_iota(jnp.int32, (rows, LANES), 1)
    low = lane < HEAD
    pieces = []
    for j in range(N_HEADS // 2):
        e0 = jnp.broadcast_to(a[:, 2 * j:2 * j + 1], (rows, LANES))
        e1 = jnp.broadcast_to(a[:, 2 * j + 1:2 * j + 2], (rows, LANES))
        pieces.append(jnp.where(low, e0, e1))
    return jnp.concatenate(pieces, axis=1)


def _head_allsum(x):
    width = x.shape[-1]
    blk = 4 * HEAD
    r = lax.broadcasted_iota(jnp.int32, (blk, blk), 0) // HEAD
    c = lax.broadcasted_iota(jnp.int32, (blk, blk), 1) // HEAD
    ones_bd = jnp.where(r == c, 1.0, 0.0).astype(BF16)
    hi = x.astype(BF16)
    rem = x - hi.astype(F32)
    mid = rem.astype(BF16)
    lo = (rem - mid.astype(F32)).astype(BF16)
    dot = lambda p: jnp.dot(p, ones_bd, preferred_element_type=F32)
    cols = []
    for j in range(width // blk):
        sl = slice(j * blk, (j + 1) * blk)
        cols.append(dot(hi[:, sl]) + (dot(mid[:, sl]) + dot(lo[:, sl])))
    return jnp.concatenate(cols, axis=1)


def _ssd_body(z_ref, x_ref, bc_ref, dt_ref, cix_ref, cibc_ref, h0_ref,
              cwx_ref, cwbc_ref, cbx_ref, cbbc_ref, dtb_ref, alog_ref, dsk_ref, gn_ref,
              y_ref, ho_ref, xbuf, bcbuf, h_s, *, nc):
    c = pl.program_id(1)
    rows = x_ref.shape[0]
    _carried_window(xbuf, x_ref, cix_ref, c == 0, rows)
    _carried_window(bcbuf, bc_ref, cibc_ref, c == 0, rows)

    @pl.when(c == 0)
    def _():
        h_s[...] = h0_ref[0]

    ti = lax.broadcasted_iota(jnp.int32, (CHUNK, CHUNK), 0)
    tj = lax.broadcasted_iota(jnp.int32, (CHUNK, CHUNK), 1)
    causal = ti >= tj
    low = tj < HEAD

    def prepare(off):
        xs = _silu(_causal_conv(xbuf, cwx_ref, cbx_ref, CHUNK, off))
        bcv = _silu(_causal_conv(bcbuf, cwbc_ref, cbbc_ref, CHUNK, off))
        dt = _softplus(dt_ref[off:off + CHUNK, :] + dtb_ref[...])
        cum = _cumsum_rows(dt * -jnp.exp(alog_ref[...]))
        cum_last = cum[CHUNK - 1:CHUNK, :]
        cum_t = cum.T
        xdt = xs * _expand_heads(dt, CHUNK)
        xdtw = (xdt * _expand_heads(jnp.exp(cum_last - cum), CHUNK)).astype(BF16)
        ecx = _expand_heads(jnp.exp(cum), CHUNK)
        per_group = []
        for g in range(N_GROUPS):
            gsl = slice(g * GROUP_W, (g + 1) * GROUP_W)
            bg = bcv[:, g * N_STATE:(g + 1) * N_STATE].astype(BF16)
            cg = bcv[:, (N_GROUPS + g) * N_STATE:(N_GROUPS + g + 1) * N_STATE].astype(BF16)
            cb = lax.dot_general(cg, bg, _NT, preferred_element_type=F32)
            st = lax.dot_general(xdtw[:, gsl], bg, _TN, preferred_element_type=F32)
            y_pairs = []
            for j in range(GROUP_W // LANES):
                h_lo = g * HEADS_PER_GROUP + 2 * j
                ms = []
                for h in (h_lo, h_lo + 1):
                    seg = cum[:, h:h + 1] - cum_t[h:h + 1, :]
                    ms.append((cb * jnp.exp(jnp.where(causal, seg, -jnp.inf))).astype(BF16))
                slab = xdt[:, h_lo * HEAD:(h_lo + 2) * HEAD]
                rhs = jnp.concatenate([jnp.where(low, slab, 0.0), jnp.where(low, 0.0, slab)], axis=0).astype(BF16)
                y_pairs.append(jnp.dot(jnp.concatenate(ms, axis=1), rhs, preferred_element_type=F32))
            y_in = jnp.concatenate(y_pairs, axis=1) + xs[:, gsl] * dsk_ref[:, gsl]
            per_group.append((cg, st, y_in, ecx[:, gsl]))
        return per_group, cum_last

    def finish(off, prepared):
        per_group, cum_last = prepared
        for g, (cg, st, y_in, ecg) in enumerate(per_group):
            gsl = slice(g * GROUP_W, (g + 1) * GROUP_W)
            y_off = lax.dot_general(cg, h_s[gsl, :].astype(BF16), _NT, preferred_element_type=F32) * ecg
            for h in range(g * HEADS_PER_GROUP, (g + 1) * HEADS_PER_GROUP):
                hsl = slice(h * HEAD, (h + 1) * HEAD)
                dec = jnp.exp(jnp.broadcast_to(cum_last[:, h:h + 1], (HEAD, N_STATE)))
                hg = slice((h % HEADS_PER_GROUP) * HEAD, (h % HEADS_PER_GROUP + 1) * HEAD)
                h_s[hsl, :] = h_s[hsl, :] * dec + st[hg, :]
            y = (y_in + y_off) * _silu(z_ref[off:off + CHUNK, gsl])
            y_ref[off:off + CHUNK, gsl] = y * lax.rsqrt(jnp.mean(y * y, axis=-1, keepdims=True) + EPS) * gn_ref[:, gsl]

    offsets = [i * CHUNK for i in range(rows // CHUNK)]
    prepared = [prepare(off) for off in offsets]
    _advance_window(xbuf, rows)
    _advance_window(bcbuf, rows)
    for off, prep in zip(offsets, prepared):
        finish(off, prep)

    @pl.when(c == nc - 1)
    def _():
        ho_ref[0] = h_s[...]


def _ssd(p, conv_x, conv_bc, h0, prm, nseq, seq_len):
    q = SSD_TILE_CHUNKS * CHUNK
    nc = seq_len // q
    rows = lambda w, col: pl.BlockSpec((q, w), lambda s, c: (s * nc + c, col))
    per_seq = lambda shape: pl.BlockSpec((1,) + shape, lambda s, c: (s,) + (0,) * len(shape))
    const = lambda shape: pl.BlockSpec(shape, lambda s, c: (0,) * len(shape))
    y, h_last = pl.pallas_call(
        functools.partial(_ssd_body, nc=nc),
        out_shape=(jax.ShapeDtypeStruct((nseq * seq_len, D), F32),
                   jax.ShapeDtypeStruct((nseq, D, N_STATE), F32)),
        grid=(nseq, nc),
        in_specs=[
            rows(D, COL_Z // D), rows(D, COL_X // D), rows(BC_W, COL_BC // BC_W),
            rows(LANES, COL_DT // LANES),
            per_seq((SUBLANES, D)), per_seq((SUBLANES, BC_W)), per_seq((D, N_STATE)),
            const((CONV_W, D)), const((CONV_W, BC_W)), const((1, D)), const((1, BC_W)),
            const((1, LANES)), const((1, LANES)), const((1, D)), const((1, D)),
        ],
        out_specs=(pl.BlockSpec((q, D), lambda s, c: (s * nc + c, 0)), per_seq((D, N_STATE))),
        scratch_shapes=[
            pltpu.VMEM((q + SUBLANES, D), F32), pltpu.VMEM((q + SUBLANES, BC_W), F32),
            pltpu.VMEM((D, N_STATE), F32),
        ],
        compiler_params=_cparams(("parallel", "arbitrary"), 40),
        name="ssd",
    )(p, p, p, p, conv_x, conv_bc, h0, *prm)
    return y, h_last


def _ssd_short_body(z_ref, x_ref, bc_ref, dt_ref, px1_ref, px2_ref, px3_ref, pb1_ref, pb2_ref, pb3_ref, h0_ref,
                    cwx_ref, cwbc_ref, cbx_ref, cbbc_ref, dtb_ref, alog_ref, dsk_ref, gn_ref,
                    y_ref, ho_ref, *, seq_len):
    rows = x_ref.shape[0]
    nseq_t = rows // seq_len
    t = lax.broadcasted_iota(jnp.int32, (rows, LANES), 0) % seq_len

    def conv(cur_ref, p1_ref, p2_ref, p3_ref, cw_ref, cb_ref):
        cur = cur_ref[...]
        tt = lax.broadcasted_iota(jnp.int32, cur.shape, 0) % seq_len
        prev = lambda s, ref: jnp.where(tt >= s, pltpu.roll(cur, s, axis=0), ref[...])
        return (cb_ref[...] + cw_ref[0:1, :] * prev(3, p3_ref) + cw_ref[1:2, :] * prev(2, p2_ref)
                + cw_ref[2:3, :] * prev(1, p1_ref) + cw_ref[3:4, :] * cur)

    xs = _silu(conv(x_ref, px1_ref, px2_ref, px3_ref, cwx_ref, cbx_ref))
    bcv = _silu(conv(bc_ref, pb1_ref, pb2_ref, pb3_ref, cwbc_ref, cbbc_ref))
    z = z_ref[...]
    dt = _softplus(dt_ref[...] + dtb_ref[...])
    cum = dt * -jnp.exp(alog_ref[...])
    s = 1
    while s < seq_len:
        cum = cum + jnp.where(t >= s, pltpu.roll(cum, s, axis=0), 0.0)
        s *= 2
    tot = cum
    s = 1
    while s < seq_len:
        tot = jnp.where(t + s < seq_len, pltpu.roll(tot, rows - s, axis=0), tot)
        s *= 2
    cum_sq = cum if rows == LANES else jnp.concatenate([cum, jnp.zeros((LANES - rows, LANES), F32)], axis=0)
    cum_t = cum_sq.T[:, 0:rows]
    dtx = _expand_heads(dt, rows)
    ecx = _expand_heads(jnp.exp(cum), rows)
    tlx = _expand_heads(jnp.exp(tot - cum), rows)
    xdt = xs * dtx
    xdtw = (xdt * tlx).astype(BF16)

    ti = lax.broadcasted_iota(jnp.int32, (rows, rows), 0)
    tj = lax.broadcasted_iota(jnp.int32, (rows, rows), 1)
    same_causal = (ti >= tj) & (ti // seq_len == tj // seq_len)
    low = lax.broadcasted_iota(jnp.int32, (rows, LANES), 1) < HEAD
    row_seq = lax.broadcasted_iota(jnp.int32, (rows, N_STATE), 0) // seq_len

    def per_seq_cols(m):
        return jnp.concatenate([jnp.where(row_seq == q, m, 0.0) for q in range(nseq_t)], axis=1).astype(BF16)

    for g in range(N_GROUPS):
        gsl = slice(g * GROUP_W, (g + 1) * GROUP_W)
        bg_f = bcv[:, g * N_STATE:(g + 1) * N_STATE]
        cg_f = bcv[:, (N_GROUPS + g) * N_STATE:(N_GROUPS + g + 1) * N_STATE]
        cb = lax.dot_general(cg_f.astype(BF16), bg_f.astype(BF16), _NT, preferred_element_type=F32)
        h_cat = jnp.concatenate([h0_ref[q, gsl, :] for q in range(nseq_t)], axis=1).astype(BF16)
        y_off = lax.dot_general(per_seq_cols(cg_f), h_cat, _NT, preferred_element_type=F32) * ecx[:, gsl]
        st = lax.dot_general(xdtw[:, gsl], per_seq_cols(bg_f), _TN, preferred_element_type=F32)
        y_pairs = []
        for j in range(GROUP_W // LANES):
            h_lo = g * (GROUP_W // HEAD) + 2 * j
            ms = []
            for h in (h_lo, h_lo + 1):
                seg = cum[:, h:h + 1] - cum_t[h:h + 1, :]
                decay = jnp.exp(jnp.where(same_causal, seg, -jnp.inf))
                ms.append((cb * decay).astype(BF16))
            slab = xdt[:, h_lo * HEAD:(h_lo + 2) * HEAD]
            rhs = jnp.concatenate([jnp.where(low, slab, 0.0), jnp.where(low, 0.0, slab)], axis=0).astype(BF16)
            y_pairs.append(jnp.dot(jnp.concatenate(ms, axis=1), rhs, preferred_element_type=F32))
            for h in (h_lo, h_lo + 1):
                hsl = slice(h * HEAD, (h + 1) * HEAD)
                hg = slice((h % HEADS_PER_GROUP) * HEAD, (h % HEADS_PER_GROUP + 1) * HEAD)
                for q in range(nseq_t):
                    dec = jnp.exp(jnp.broadcast_to(tot[q * seq_len:q * seq_len + 1, h:h + 1], (HEAD, N_STATE)))
                    ho_ref[q, hsl, :] = h0_ref[q, hsl, :] * dec + st[hg, q * N_STATE:(q + 1) * N_STATE]
        y = jnp.concatenate(y_pairs, axis=1) + y_off + xs[:, gsl] * dsk_ref[:, gsl]
        y = y * _silu(z[:, gsl])
        y_ref[:, gsl] = y * lax.rsqrt(jnp.mean(y * y, axis=-1, keepdims=True) + EPS) * gn_ref[:, gsl]


def _ssd_short(p, conv_state, h0, prm, nseq, seq_len):
    rows = nseq * seq_len
    tr = SSD_SHORT_SEQS * seq_len
    bc_w = BC_W

    def dense(s, lo, hi):
        part = conv_state[:, CONV_W - 1 - s:, lo:hi]
        return jnp.pad(part, ((0, 0), (0, seq_len - s), (0, 0))).reshape(rows, hi - lo)

    tile = lambda w, col: pl.BlockSpec((tr, w), lambda i: (i, col))
    const = lambda shape: pl.BlockSpec(shape, lambda i: (0,) * len(shape))
    state = pl.BlockSpec((SSD_SHORT_SEQS, D, N_STATE), lambda i: (i, 0, 0))
    return pl.pallas_call(
        functools.partial(_ssd_short_body, seq_len=seq_len),
        out_shape=(jax.ShapeDtypeStruct((rows, D), F32), jax.ShapeDtypeStruct((nseq, D, N_STATE), F32)),
        grid=(nseq // SSD_SHORT_SEQS,),
        in_specs=[
            tile(D, COL_Z // D), tile(D, COL_X // D), tile(bc_w, COL_BC // bc_w), tile(LANES, COL_DT // LANES),
            tile(D, 0), tile(D, 0), tile(D, 0), tile(bc_w, 0), tile(bc_w, 0), tile(bc_w, 0), state,
            const((CONV_W, D)), const((CONV_W, bc_w)), const((1, D)), const((1, bc_w)),
            const((1, LANES)), const((1, LANES)), const((1, D)), const((1, D)),
        ],
        out_specs=(tile(D, 0), state),
        compiler_params=_cparams(("parallel",), 48),
        name="ssd_short",
    )(p, p, p, p, dense(1, 0, D), dense(2, 0, D), dense(3, 0, D), dense(1, D, D + bc_w), dense(2, D, D + bc_w),
      dense(3, D, D + bc_w), h0, *prm)


def _rwkv_pre_body(rkv_ref, lora_ref, si_rkv_ref, si_lora_ref, mu_rkv_ref, mu_lora_ref,
                   w0_ref, w2_ref, a0_ref, a2_ref, g2_ref, kk_ref, ka_ref, rk_ref,
                   r_out, w_out, k_out, v_out, kk_out, b_out, bonus_out, g_out,
                   rkvbuf, lorabuf, *, q, log_decay, short_len):
    if short_len:
        def shifted(buf, cur_ref, si_ref, mu_ref):
            cur = cur_ref[...]
            first = lax.broadcasted_iota(jnp.int32, cur.shape, 0) % short_len == 0
            prev = jnp.where(first, si_ref[...], pltpu.roll(cur, 1, axis=0))
            return cur + mu_ref[...] * (prev - cur)
    else:
        c = pl.program_id(1)
        _carried_window(rkvbuf, rkv_ref, si_rkv_ref, c == 0, q)
        _carried_window(lorabuf, lora_ref, si_lora_ref, c == 0, q)

        def shifted(buf, cur_ref, si_ref, mu_ref):
            cur = cur_ref[...]
            prev = buf[pl.ds(SUBLANES - 1, q), :]
            return cur + mu_ref[...] * (prev - cur)

    ps = shifted(rkvbuf, rkv_ref, si_rkv_ref, mu_rkv_ref)
    lo_in = shifted(lorabuf, lora_ref, si_lora_ref, mu_lora_ref)
    if not short_len:
        _advance_window(rkvbuf, q)
        _advance_window(lorabuf, q)
    r = ps[:, 0:D]
    k = ps[:, D:2 * D]
    v = ps[:, 2 * D:3 * D]

    lw = jnp.dot(jnp.tanh(lo_in).astype(BF16), w2_ref[...], preferred_element_type=F32)
    la = jnp.dot(lo_in.astype(BF16), a2_ref[...], preferred_element_type=F32)
    g = jnp.dot(jax.nn.sigmoid(lo_in).astype(BF16), g2_ref[...], preferred_element_type=F32)
    wlog = -_softplus(-(w0_ref[...] + lw)) - 0.5
    log_w = -jnp.exp(wlog)
    a = jax.nn.sigmoid(a0_ref[...] + la)
    kkf = k * kk_ref[...]
    norm = jnp.maximum(jnp.sqrt(_head_allsum(kkf * kkf)), 1e-12)
    kk = kkf / norm
    k2 = k * (1.0 + (a - 1.0) * ka_ref[...])
    bonus = _head_allsum(r * k2 * rk_ref[...]) * v
    r_out[...] = r
    w_out[...] = log_w if log_decay else jnp.exp(log_w)
    k_out[...] = k2
    v_out[...] = v
    kk_out[...] = kk
    b_out[...] = kk * a
    bonus_out[...] = bonus
    g_out[...] = g


def _rwkv_pre(p, shift_state, prm, nseq, seq_len, log_decay):
    short_len = seq_len if seq_len < CHUNK else 0
    q = CHUNK if short_len else min(CHUNK, seq_len)
    nc = 1 if short_len else seq_len // q
    ntile = nseq * seq_len // q // nc
    const = lambda shape: pl.BlockSpec(shape, lambda s, c: (0,) * len(shape))
    tile = pl.BlockSpec((q, D), lambda s, c: (s * nc + c, 0))
    if short_len:
        dense = lambda a: jnp.pad(a, ((0, 0), (0, seq_len - 1), (0, 0))).reshape(nseq * seq_len, a.shape[-1])
        si = (dense(shift_state[:, :, :3 * D]), dense(shift_state[:, :, 3 * D:]))
        si_specs = [pl.BlockSpec((q, 3 * D), lambda s, c: (s, 0)), pl.BlockSpec((q, LORA_W), lambda s, c: (s, 0))]
    else:
        si = (_pad_front_rows(shift_state[:, :, :3 * D]), _pad_front_rows(shift_state[:, :, 3 * D:]))
        si_specs = [pl.BlockSpec((1, SUBLANES, 3 * D), lambda s, c: (s, 0, 0)),
                    pl.BlockSpec((1, SUBLANES, LORA_W), lambda s, c: (s, 0, 0))]
    sds = jax.ShapeDtypeStruct((nseq * seq_len, D), F32)
    return pl.pallas_call(
        functools.partial(_rwkv_pre_body, q=q, log_decay=log_decay, short_len=short_len),
        out_shape=(sds,) * 8,
        grid=(ntile, nc),
        in_specs=[
            pl.BlockSpec((q, 3 * D), lambda s, c: (s * nc + c, COL_RKV // (3 * D))),
            pl.BlockSpec((q, LORA_W), lambda s, c: (s * nc + c, COL_LORA // LORA_W)),
            *si_specs,
            const((1, 3 * D)), const((1, LORA_W)),
            const((1, D)), const((LORA_W, D)), const((1, D)), const((LORA_W, D)), const((LORA_W, D)),
            const((1, D)), const((1, D)), const((1, D)),
        ],
        out_specs=(tile,) * 8,
        scratch_shapes=[pltpu.VMEM((CHUNK + SUBLANES, 3 * D), F32), pltpu.VMEM((CHUNK + SUBLANES, LORA_W), F32)],
        compiler_params=_cparams(("parallel", "arbitrary"), 48),
        name="rwkv_pre",
    )(p, p, *si, *prm)


def _rwkv_out(o, bonus, g, ln_w, ln_b):
    mean = _head_allsum(o) * (1.0 / HEAD)
    cen = o - mean
    var = _head_allsum(cen * cen) * (1.0 / HEAD)
    return (cen * lax.rsqrt(var + GN_EPS) * ln_w + ln_b + bonus) * g


def _split_bf16(x):
    hi = x.astype(BF16)
    return hi, (x - hi.astype(F32)).astype(BF16)


_NN = (((1,), (0,)), ((), ()))
_NT = (((1,), (1,)), ((), ()))
_TN = (((0,), (0,)), ((), ()))


def _solve_unit_lower(n, rhs, lower_left):
    rows, width = rhs.shape
    half = rows // 2
    nblk, ncol = rows // SUBLANES, width // LANES
    tiles = lambda a, r0, r1: [[a[SUBLANES * i:SUBLANES * (i + 1), LANES * j:LANES * (j + 1)] for j in range(ncol)]
                               for i in range(r0 // SUBLANES, r1 // SUBLANES)]
    nb = tiles(n, 0, rows)
    xb = tiles(rhs, 0, rows)
    low = lax.broadcasted_iota(jnp.int32, (SUBLANES, LANES), 1) < HEAD

    def substitute(lo, hi):
        for s in range(lo, hi - 1):
            i0, r0 = divmod(s, SUBLANES)
            idx = jnp.where(low, s, HEAD + s)
            for j in range(ncol):
                row = xb[i0][j][r0:r0 + 1, :]
                for i in range(i0 if r0 < SUBLANES - 1 else i0 + 1, hi // SUBLANES):
                    xb[i][j] = xb[i][j] - jnp.take_along_axis(nb[i][j], idx, axis=1) * row

    join = lambda blocks: jnp.concatenate([jnp.concatenate(xr, axis=1) for xr in blocks], axis=0)
    substitute(0, half)
    corr = tiles(lower_left(join(xb[:half // SUBLANES])), 0, half)
    for i in range(half // SUBLANES):
        for j in range(ncol):
            xb[half // SUBLANES + i][j] = xb[half // SUBLANES + i][j] - corr[i][j]
    substitute(half, rows)
    return join(xb)


def _wkv_chunk(r, lw, k, v, kk, b):
    rows, width = r.shape
    nh = WKV_LANES // HEAD
    groups = [slice(g * WKV_LANES, (g + 1) * WKV_LANES) for g in range(width // WKV_LANES)]
    cl = _cumsum_rows(lw)
    cl_last = cl[rows - 1:rows, :]
    p_inv = jnp.exp(-cl)
    p_end = jnp.exp(cl_last - cl)
    x2h, x2l = _split_bf16(jnp.concatenate([kk * jnp.exp(cl - lw), r * jnp.exp(cl)], axis=0))
    k_hat = k * p_inv
    b_hat = b * p_inv
    k_end = k * p_end
    b_end = -(b * p_end)

    bd_r = lax.broadcasted_iota(jnp.int32, (nh * rows, WKV_LANES), 0) // rows
    bd_c = lax.broadcasted_iota(jnp.int32, (nh * rows, WKV_LANES), 1) // HEAD
    bd_mask = jnp.where(bd_r == bd_c, 1.0, 0.0).astype(BF16)

    def per_head_rows(y):
        return [jnp.concatenate([part] * nh, axis=0) * bd_mask for part in _split_bf16(y)]

    def dot3(ah, al, bh, bl, dims):
        dg = lambda x, y: lax.dot_general(x, y, dims, preferred_element_type=F32)
        return dg(ah, bh) + (dg(ah, bl) + dg(al, bh))

    t_i = lax.broadcasted_iota(jnp.int32, (rows, WKV_LANES), 0)
    lane = lax.broadcasted_iota(jnp.int32, (rows, WKV_LANES), 1)
    s_i = lane & (HEAD - 1)
    strict = t_i > s_i
    incl = t_i >= s_i
    head_of_lane = lane // HEAD

    a_kb, a_rb, av = [], [], []
    for gs in groups:
        ak = dot3(x2h[:, gs], x2l[:, gs], *per_head_rows(k_hat[:, gs]), _NT)
        ab = dot3(x2h[:, gs], x2l[:, gs], *per_head_rows(b_hat[:, gs]), _NT)
        a_k = jnp.concatenate([jnp.where(strict, ak[0:rows], 0.0), jnp.where(incl, ak[rows:], 0.0)], axis=0)
        av.append(dot3(*_split_bf16(a_k), *per_head_rows(v[:, gs]), _NN))
        a_kb.append(jnp.where(strict, ab[0:rows], 0.0))
        a_rb.append(jnp.where(incl, ab[rows:], 0.0))
    a_kb_all = jnp.concatenate(a_kb, axis=1)
    half = rows // 2
    top_cols = (lax.broadcasted_iota(jnp.int32, (half, WKV_LANES), 1) & (HEAD - 1)) < half

    def lower_left(x_top):
        x_pad = jnp.concatenate([x_top, jnp.zeros_like(x_top)], axis=0)
        parts = []
        for g, gs in enumerate(groups):
            lhs = jnp.where(top_cols, a_kb[g][half:], 0.0)
            parts.append(dot3(*_split_bf16(lhs), *per_head_rows(x_pad[:, gs]), _NN))
        return jnp.concatenate(parts, axis=1)

    upd_lhs = [_split_bf16(jnp.concatenate([k_end[:, gs], b_end[:, gs],
                                            jnp.where(t_i == s_i, jnp.exp(cl_last[:, gs]), 0.0)], axis=0))
               for gs in groups]

    def advance(st):
        xs = [dot3(x2h[:, gs], x2l[:, gs], *per_head_rows(st[:, gs]), _NN) for gs in groups]
        base = jnp.concatenate([xs[g][0:rows] + av[g][0:rows] for g in range(len(groups))], axis=1)
        sa = _solve_unit_lower(a_kb_all, base, lower_left)
        o, st_new = [], []
        for g, gs in enumerate(groups):
            o.append(xs[g][rows:] + av[g][rows:] - dot3(*_split_bf16(a_rb[g]), *per_head_rows(sa[:, gs]), _NN))
            rhs = jnp.concatenate([v[:, gs], sa[:, gs], st[:, gs]], axis=0)
            full = dot3(*upd_lhs[g], *_split_bf16(rhs), _TN)
            acc = jnp.where(head_of_lane == 0, full[0:HEAD, :], 0.0)
            for h in range(1, nh):
                acc = acc + jnp.where(head_of_lane == h, full[h * HEAD:(h + 1) * HEAD, :], 0.0)
            st_new.append(acc)
        return jnp.concatenate(o, axis=1), jnp.concatenate(st_new, axis=1)

    return advance


def _wkv_long_body(r_ref, lw_ref, k_ref, v_ref, kk_ref, b_ref, bonus_ref, g_ref, lnw_ref, lnb_ref,
                   y_ref, so_ref, st_s, *, nc):
    c = pl.program_id(2)

    @pl.when(c == 0)
    def _():
        st_s[...] = jnp.zeros_like(st_s)

    chunks = [slice(i * WKV_CHUNK, (i + 1) * WKV_CHUNK) for i in range(r_ref.shape[0] // WKV_CHUNK)]
    advances = [_wkv_chunk(r_ref[cs, :], lw_ref[cs, :], k_ref[cs, :], v_ref[cs, :], kk_ref[cs, :], b_ref[cs, :])
                for cs in chunks]
    st = st_s[...]
    for cs, advance in zip(chunks, advances):
        o, st = advance(st)
        y_ref[cs, :] = _rwkv_out(o, bonus_ref[cs, :], g_ref[cs, :], lnw_ref[...], lnb_ref[...])
    st_s[...] = st

    @pl.when(c == nc - 1)
    def _():
        for j in range(WKV_GROUPS * WKV_LANES // LANES):
            js = slice(j * LANES, (j + 1) * LANES)
            sq = jnp.concatenate([st_s[:, js], jnp.zeros((LANES - HEAD, LANES), F32)], axis=0)
            so_ref[0, js, :] = sq.T[:, 0:HEAD]


def _wkv_long(r, lw, k, v, kk, b, bonus, g, ln_w, ln_b, nseq, seq_len):
    tile_rows = WKV_TILE_CHUNKS * WKV_CHUNK
    nc = seq_len // tile_rows
    width = WKV_GROUPS * WKV_LANES
    tile = pl.BlockSpec((tile_rows, width), lambda s, hg, c: (s * nc + c, hg))
    vec = pl.BlockSpec((1, width), lambda s, hg, c: (0, hg))
    y, s_last = pl.pallas_call(
        functools.partial(_wkv_long_body, nc=nc),
        out_shape=(jax.ShapeDtypeStruct((nseq * seq_len, D), F32),
                   jax.ShapeDtypeStruct((nseq, D, HEAD), F32)),
        grid=(nseq, D // width, nc),
        in_specs=[tile] * 8 + [vec, vec],
        out_specs=(tile, pl.BlockSpec((1, width, HEAD), lambda s, hg, c: (s, hg, 0))),
        scratch_shapes=[pltpu.VMEM((HEAD, width), F32)],
        compiler_params=_cparams(("parallel", "parallel", "arbitrary"), 48),
        name="wkv_long",
    )(r, lw, k, v, kk, b, bonus, g, ln_w, ln_b)
    return y, s_last


def _wkv_short_body(r_ref, w_ref, k_ref, v_ref, kk_ref, b_ref, s0_ref, o_ref, so_ref, *, steps):
    def per_v_group(i, carry):
        vis = [i * WKV_SHORT_ROWS + u for u in range(WKV_SHORT_ROWS)]
        ss = [s0_ref[0, vi] for vi in vis]
        for t in range(steps):
            kk_t, w_t, b_t, k_t, r_t = kk_ref[t, 0], w_ref[t, 0], b_ref[t, 0], k_ref[t, 0], r_ref[t, 0]
            for u, vi in enumerate(vis):
                vrow = v_ref[t, 0, pl.ds(vi, 1), :]
                sa = jnp.sum(ss[u] * kk_t, axis=0, keepdims=True)
                ss[u] = ss[u] * w_t - b_t * sa + k_t * vrow
                o_ref[t, 0, pl.ds(vi, 1), :] = jnp.sum(ss[u] * r_t, axis=0, keepdims=True)
        for u, vi in enumerate(vis):
            so_ref[0, vi] = ss[u]
        return carry

    lax.fori_loop(0, HEAD // WKV_SHORT_ROWS, per_v_group, 0)


def _to_lanes_body(*refs, n_in, seq_len):
    nseq = refs[0].shape[0] // seq_len
    for src, dst in zip(refs[:n_in], refs[n_in:]):
        for t in range(seq_len):
            xt = src[pl.ds(t, nseq, stride=seq_len), :].T
            dst[t, 0] = xt[0:HEAD, :]
            dst[t, 1] = xt[HEAD:2 * HEAD, :]


def _to_lanes(arrays, nseq, seq_len):
    n = len(arrays)
    return pl.pallas_call(
        functools.partial(_to_lanes_body, n_in=n, seq_len=seq_len),
        out_shape=(jax.ShapeDtypeStruct((seq_len, N_HEADS, HEAD, nseq), F32),) * n,
        grid=(D // LANES,),
        in_specs=[pl.BlockSpec((nseq * seq_len, LANES), lambda hp: (0, hp))] * n,
        out_specs=(pl.BlockSpec((seq_len, 2, HEAD, nseq), lambda hp: (0, hp, 0, 0)),) * n,
        compiler_params=_cparams(("parallel",), 32),
        name="to_lanes",
    )(*arrays)


def _wkv_short(r, w, k, v, kk, b, s0, nseq, seq_len):
    vec = pl.BlockSpec((seq_len, 1, HEAD, nseq), lambda h: (0, h, 0, 0))
    st = pl.BlockSpec((1, HEAD, HEAD, nseq), lambda h: (h, 0, 0, 0))
    return pl.pallas_call(
        functools.partial(_wkv_short_body, steps=seq_len),
        out_shape=(jax.ShapeDtypeStruct((seq_len, N_HEADS, HEAD, nseq), F32),
                   jax.ShapeDtypeStruct((N_HEADS, HEAD, HEAD, nseq), F32)),
        grid=(N_HEADS,),
        in_specs=[vec] * 6 + [st],
        out_specs=(vec, st),
        compiler_params=_cparams(("parallel",), 32),
        name="wkv_short",
    )(r, w, k, v, kk, b, s0)


def _rwkv_post_body(o_ref, bonus_ref, g_ref, lnw_ref, lnb_ref, y_ref):
    y_ref[...] = _rwkv_out(o_ref[...], bonus_ref[...], g_ref[...], lnw_ref[...], lnb_ref[...])


def _rwkv_post(o, bonus, g, ln_w, ln_b):
    m = o.shape[0]
    tile = pl.BlockSpec((TM_POST, D), lambda i: (i, 0))
    const = pl.BlockSpec((1, D), lambda i: (0, 0))
    return pl.pallas_call(
        _rwkv_post_body,
        out_shape=jax.ShapeDtypeStruct((m, D), F32),
        grid=(m // TM_POST,),
        in_specs=[tile, tile, tile, const, const],
        out_specs=tile,
        compiler_params=_cparams(("parallel",), 32),
        name="rwkv_post",
    )(o, bonus, g, ln_w, ln_b)


def _lru_coeffs(xc, wga_ref, bga_ref, wgx_ref, bgx_ref, lam_ref):
    ra, rx = [], []
    for blk in range(LRU_BLOCKS):
        xh = xc[:, blk * LRU_BLK:(blk + 1) * LRU_BLK].astype(BF16)
        ra.append(jnp.dot(xh, wga_ref[blk], preferred_element_type=F32))
        rx.append(jnp.dot(xh, wgx_ref[blk], preferred_element_type=F32))
    rg = jax.nn.sigmoid(jnp.concatenate(ra, axis=1) + bga_ref[...])
    ig = jax.nn.sigmoid(jnp.concatenate(rx, axis=1) + bgx_ref[...])
    log_a = -LRU_C * rg * _softplus(-lam_ref[...])
    return jnp.exp(log_a), jnp.sqrt(1.0 - jnp.exp(2.0 * log_a)) * (ig * xc)


def _lru_body(gate_ref, x_ref, ci_ref, h0_ref, cw_ref, cb_ref, wga_ref, bga_ref, wgx_ref, bgx_ref, lam_ref,
              y_ref, ho_ref, xbuf, a_s, b_s, h_s, hc_s, *, q, nc):
    c = pl.program_id(1)
    _carried_window(xbuf, x_ref, ci_ref, c == 0, q)

    @pl.when(c == 0)
    def _():
        hc_s[...] = h0_ref[0]

    xc = _causal_conv(xbuf, cw_ref, cb_ref, q)
    _advance_window(xbuf, q)
    a_s[...], b_s[...] = _lru_coeffs(xc, wga_ref, bga_ref, wgx_ref, bgx_ref, lam_ref)

    def step(t, h):
        h = a_s[pl.ds(t, 1), :] * h + b_s[pl.ds(t, 1), :]
        h_s[pl.ds(t, 1), :] = h
        return h

    h = lax.fori_loop(0, q, step, hc_s[...])
    hc_s[...] = h
    y_ref[...] = h_s[...] * jax.nn.gelu(gate_ref[...])

    @pl.when(c == nc - 1)
    def _():
        ho_ref[0] = h


def _lru(p, conv_init, h0, prm, nseq, seq_len):
    q = LRU_TILE_ROWS
    nc = seq_len // q
    per_seq = lambda shape: pl.BlockSpec((1,) + shape, lambda s, c: (s,) + (0,) * len(shape))
    const = lambda shape: pl.BlockSpec(shape, lambda s, c: (0,) * len(shape))
    return pl.pallas_call(
        functools.partial(_lru_body, q=q, nc=nc),
        out_shape=(jax.ShapeDtypeStruct((nseq * seq_len, D), F32),
                   jax.ShapeDtypeStruct((nseq, 1, D), F32)),
        grid=(nseq, nc),
        in_specs=[
            pl.BlockSpec((q, D), lambda s, c: (s * nc + c, 0)),
            pl.BlockSpec((q, D), lambda s, c: (s * nc + c, 1)),
            per_seq((SUBLANES, D)), per_seq((1, D)),
            const((CONV_W, D)), const((1, D)),
            const((LRU_BLOCKS, LRU_BLK, LRU_BLK)), const((1, D)),
            const((LRU_BLOCKS, LRU_BLK, LRU_BLK)), const((1, D)), const((1, D)),
        ],
        out_specs=(pl.BlockSpec((q, D), lambda s, c: (s * nc + c, 0)), per_seq((1, D))),
        scratch_shapes=[pltpu.VMEM((q + SUBLANES, D), F32), pltpu.VMEM((q, D), F32), pltpu.VMEM((q, D), F32),
                        pltpu.VMEM((q, D), F32), pltpu.VMEM((1, D), F32)],
        compiler_params=_cparams(("parallel", "arbitrary"), 32),
        name="lru",
    )(p, p, conv_init, h0, *prm)


def _lru_short_body(gate_ref, x_ref, p1_ref, p2_ref, p3_ref, h0_ref, cw_ref, cb_ref, wga_ref, bga_ref, wgx_ref,
                    bgx_ref, lam_ref, y_ref, h_ref, *, seq_len):
    cur = x_ref[...]
    t = lax.broadcasted_iota(jnp.int32, cur.shape, 0) % seq_len
    prev = lambda s, ref: jnp.where(t >= s, pltpu.roll(cur, s, axis=0), ref[...])
    xc = (cb_ref[...] + cw_ref[0:1, :] * prev(3, p3_ref) + cw_ref[1:2, :] * prev(2, p2_ref)
          + cw_ref[2:3, :] * prev(1, p1_ref) + cw_ref[3:4, :] * cur)
    a, b = _lru_coeffs(xc, wga_ref, bga_ref, wgx_ref, bgx_ref, lam_ref)
    s = 1
    while s < seq_len:
        keep = t >= s
        b = jnp.where(keep, a * pltpu.roll(b, s, axis=0) + b, b)
        a = jnp.where(keep, a * pltpu.roll(a, s, axis=0), a)
        s *= 2
    h = a * h0_ref[...] + b
    h_ref[...] = h
    y_ref[...] = h * jax.nn.gelu(gate_ref[...])


def _lru_short(p, conv_state, h0, prm, nseq, seq_len):
    rows = nseq * seq_len
    dense = lambda s: jnp.pad(conv_state[:, CONV_W - 1 - s:, :], ((0, 0), (0, seq_len - s), (0, 0))).reshape(rows, D)
    h0_rows = jnp.repeat(h0, seq_len, axis=0)
    tile = lambda col: pl.BlockSpec((CHUNK, D), lambda i: (i, col))
    const = lambda shape: pl.BlockSpec(shape, lambda i: (0,) * len(shape))
    sds = jax.ShapeDtypeStruct((rows, D), F32)
    return pl.pallas_call(
        functools.partial(_lru_short_body, seq_len=seq_len),
        out_shape=(sds, sds),
        grid=(rows // CHUNK,),
        in_specs=[tile(0), tile(1), tile(0), tile(0), tile(0), tile(0),
                  const((CONV_W, D)), const((1, D)),
                  const((LRU_BLOCKS, LRU_BLK, LRU_BLK)), const((1, D)),
                  const((LRU_BLOCKS, LRU_BLK, LRU_BLK)), const((1, D)), const((1, D))],
        out_specs=(tile(0), tile(0)),
        compiler_params=_cparams(("parallel",), 32),
        name="lru_short",
    )(p, p, dense(1), dense(2), dense(3), h0_rows, *prm)


def _pad_front_rows(buf):
    return jnp.pad(buf, ((0, 0), (SUBLANES - buf.shape[1], 0), (0, 0)))


def _row2(v):
    return v.reshape(1, -1)


def kernel(x_prompt, x_sample, state_ssm_a, state_conv_a, state_wkv_b, state_shift_b, state_lru_c, state_conv_c, norm_gain, w_ffn_in, w_ffn_out, w_in_ab, conv_w_a, conv_b_a, dt_bias_a, a_log_a, d_skip_a, gnorm_a, mu_b, w0_b, w2_b, a0_b, a2_b, g2_b, k_k_b, k_a_b, r_k_b, ln_w_b, ln_b_b, w_out_ab, w_in_c, conv_w_c, conv_b_c, w_gate_a_c, b_gate_a_c, w_gate_x_c, b_gate_x_c, lambda_c, w_out_c, final_norm_gain):
    w_gu, w_dn = _ffn_weights(w_ffn_in, w_ffn_out)

    in_a = D + (D + BC_W) + N_HEADS
    bc_w = BC_W
    segments = (
        (in_a, 3 * D),
        (0, D),
        (D, D),
        (2 * D, BC_W),
        (in_a + 3 * D, LORA_W),
        (in_a - N_HEADS, N_HEADS),
    )
    w_proj0 = _reorder_cast_cols(w_in_ab, segments, PROJ_W)
    w_proj1 = w_in_c[0].astype(BF16)
    w_out0 = w_out_ab[0].astype(BF16)
    w_out1 = w_out_c[0].astype(BF16)

    pad_lanes = lambda v: jnp.pad(v.reshape(1, -1), ((0, 0), (0, LANES - v.shape[-1])))
    rep_head = lambda v: jnp.repeat(v, HEAD).reshape(1, D)
    ssd_prm = (conv_w_a[0][:, :D], conv_w_a[0][:, D:], _row2(conv_b_a[0][:D]), _row2(conv_b_a[0][D:]),
               pad_lanes(dt_bias_a[0]), pad_lanes(a_log_a[0]), rep_head(d_skip_a[0]), _row2(gnorm_a[0]))
    lora_rows = lambda w, lo: jnp.pad(w, ((lo, LORA_W - lo - w.shape[0]), (0, 0))).astype(BF16)
    mu = mu_b[0]
    rwkv_prm = (_row2(mu[:3 * D]), _row2(mu[3 * D:]),
                _row2(w0_b[0]), lora_rows(w2_b[0], 0), _row2(a0_b[0]), lora_rows(a2_b[0], 64),
                lora_rows(g2_b[0], 128), _row2(k_k_b[0]), _row2(k_a_b[0]), _row2(r_k_b[0]))
    lru_prm = (conv_w_c[0], _row2(conv_b_c[0]), w_gate_a_c[0].astype(BF16), _row2(b_gate_a_c[0]),
               w_gate_x_c[0].astype(BF16), _row2(b_gate_x_c[0]), _row2(lambda_c[0]))

    def trunk(x3, ssm0, conva0, wkv0, shift0, lru0, convc0):
        nseq, seq_len, _ = x3.shape
        x = x3.reshape(nseq * seq_len, D)
        tail = lambda arr, n: arr.reshape(nseq, seq_len, arr.shape[-1])[:, seq_len - n:, :]

        x = _ffn(x, _row2(norm_gain[0, 0]), w_gu, w_dn, (0, 0))
        p = _proj(x, _row2(norm_gain[0, 1]), w_proj0)
        if seq_len >= CHUNK:
            ya, ssm_n = _ssd(p, _pad_front_rows(conva0[:, :, :D]), _pad_front_rows(conva0[:, :, D:]),
                             ssm0.reshape(nseq, D, N_STATE), ssd_prm, nseq, seq_len)
        else:
            ya, ssm_n = _ssd_short(p, conva0, ssm0.reshape(nseq, D, N_STATE), ssd_prm, nseq, seq_len)
        p_tail = tail(p, CONV_W - 1)
        conva_n = jnp.concatenate([p_tail[:, :, COL_X:COL_X + D], p_tail[:, :, COL_BC:COL_BC + bc_w]], axis=-1)
        shift_n = jnp.concatenate([p_tail[:, -1:, :3 * D], p_tail[:, -1:, COL_LORA:COL_LORA + LORA_W]], axis=-1)

        r, w, k, v, kk, b, bonus, g = _rwkv_pre(p, shift0, rwkv_prm, nseq, seq_len, log_decay=wkv0 is None)
        if wkv0 is None:
            yb, wkv_n = _wkv_long(r, w, k, v, kk, b, bonus, g, _row2(ln_w_b[0]), _row2(ln_b_b[0]), nseq, seq_len)
            wkv_n = wkv_n.reshape(nseq, N_HEADS, HEAD, HEAD)
        else:
            o, wkv_n = _wkv_short(*_to_lanes((r, w, k, v, kk, b), nseq, seq_len), wkv0.transpose(1, 2, 3, 0),
                                  nseq, seq_len)
            o = o.transpose(3, 0, 1, 2).reshape(nseq * seq_len, D)
            wkv_n = wkv_n.transpose(3, 0, 1, 2)
            yb = _rwkv_post(o, bonus, g, _row2(ln_w_b[0]), _row2(ln_b_b[0]))
        x = _out_proj(x, (ya, yb), (w_out0[:D], w_out0[D:]))
        x = _ffn(x, _row2(norm_gain[0, 2]), w_gu, w_dn, (0, 1))

        x = _ffn(x, _row2(norm_gain[1, 0]), w_gu, w_dn, (1, 0))
        pc = _proj(x, _row2(norm_gain[1, 1]), w_proj1)
        if seq_len >= CHUNK:
            yc, lru_n = _lru(pc, _pad_front_rows(convc0), lru0.reshape(nseq, 1, D), lru_prm, nseq, seq_len)
        else:
            yc, h_rows = _lru_short(pc, convc0, lru0, lru_prm, nseq, seq_len)
            lru_n = tail(h_rows, 1)
        convc_n = tail(pc, CONV_W - 1)[:, :, D:]
        x = _out_proj(x, (yc,), (w_out1,))
        y = _ffn(x, _row2(norm_gain[1, 2]), w_gu, w_dn, (1, 1), final_gain=_row2(final_norm_gain))

        return (y.reshape(nseq, seq_len, D), ssm_n.reshape(1, nseq, N_HEADS, HEAD, N_STATE), conva_n[None],
                wkv_n[None], shift_n[None], lru_n.reshape(1, nseq, D), convc_n[None])

    bp = x_prompt.shape[0]
    zeros = lambda s: jnp.zeros((bp,) + s.shape[2:], F32)
    outs_p = trunk(x_prompt, zeros(state_ssm_a), zeros(state_conv_a), None, zeros(state_shift_b),
                   zeros(state_lru_c), zeros(state_conv_c))
    outs_s = trunk(x_sample, state_ssm_a[0], state_conv_a[0], state_wkv_b[0], state_shift_b[0],
                   state_lru_c[0], state_conv_c[0])
    return (outs_p[0], outs_s[0]) + outs_p[1:] + outs_s[1:]
```
